```python
import math
import jax, jax.numpy as jnp
from jax import lax
import numpy as np

D_MODEL = 1024
BATCH = 2
SEQ = 8192
DEPTH = 2
DEC_BATCH = 32
DEC_SEQ = 4
PAST_LEN = 8192
PAGE_SIZE = 128

HD = 64
N_SB = 8
N_DF = 4
N_NSA = 8
N_NSA_KV = 2
NSA_GROUP = N_NSA // N_NSA_KV
N_DSA = 8
N_IDX = 8
IDX_D = 64
N_MEM = 256
N_XH = 4
D_FF = 2816
N_EXPERTS = 8
TOP_K_EXPERTS = 2
MOE_D_FF = 2816
ROPE_THETA = 10000.0
Q_BLOCK = 128
CMP_LEN = 32
CMP_STRIDE = 16
CMP_HIDDEN = 128
SLC_BLOCK = 64
CMP_PER_SLC = SLC_BLOCK // CMP_STRIDE
N_SLC = 16
N_LOCAL = 2
WINDOW = 512
DSA_TOPK_MAX = 256
EPS = 1e-6
NEG = -1e30
FORCE = 1e9
SCALE = HD ** -0.5
IDX_SCALE = IDX_D ** -0.5
N_EVEN = (DEPTH + 1) // 2
N_ODD = DEPTH // 2
SB_W = N_SB * HD
DF_W = N_DF * 2 * HD
KV_W = N_NSA_KV * HD
EVEN_WIDTHS = (SB_W, SB_W, SB_W, DF_W, DF_W, DF_W)
ODD_WIDTHS = (N_NSA * HD, KV_W, KV_W, KV_W, KV_W, KV_W, KV_W, N_NSA * 3,
              N_DSA * HD, HD, HD, N_IDX * IDX_D, IDX_D, N_IDX)
EVEN_IN_W = sum(EVEN_WIDTHS)
ODD_IN_W = sum(ODD_WIDTHS)
MIX_W = SB_W + DF_W
X_W = N_XH * HD

kernel_name = 'hybrid_sbdiff_nsadsa_decoder_step'


def rms_norm(x, g):
    xf = x.astype(jnp.float32)
    y = xf * lax.rsqrt(jnp.mean(xf * xf, axis=-1, keepdims=True) + EPS)
    return (y * g.astype(jnp.float32)).astype(x.dtype)


def rope(x, pos):
    half = x.shape[-1] // 2
    inv = ROPE_THETA ** (-jnp.arange(half, dtype=jnp.float32) / half)
    ang = pos.astype(jnp.float32)[:, None] * inv[None, :]
    cos, sin = jnp.cos(ang)[:, None, :], jnp.sin(ang)[:, None, :]
    xf = x.astype(jnp.float32)
    x1, x2 = xf[..., :half], xf[..., half:]
    return jnp.concatenate([x1 * cos - x2 * sin, x2 * cos + x1 * sin], axis=-1).astype(x.dtype)


def split_cols(x, widths):
    return jnp.split(x, [int(c) for c in np.cumsum(widths)[:-1]], axis=-1)


def gather_pages(pool, page_table):
    g = pool[page_table]
    return g.reshape((g.shape[0], g.shape[1] * g.shape[2]) + g.shape[3:])


def over_query_blocks(fn, qs, q_pos):
    T = q_pos.shape[0]
    if T <= Q_BLOCK or T % Q_BLOCK:
        return fn(qs, q_pos)
    nb = T // Q_BLOCK

    def to_blocks(a):
        return jnp.moveaxis(a.reshape((a.shape[0], nb, Q_BLOCK) + a.shape[2:]), 1, 0)

    out = lax.map(lambda args: fn(args[0], args[1]),
                  (tuple(to_blocks(a) for a in qs), q_pos.reshape(nb, Q_BLOCK)))
    out = jnp.moveaxis(out, 0, 1)
    return out.reshape((out.shape[0], T) + out.shape[3:])


def masked_softmax(s, mask):
    s = jnp.where(mask, s.astype(jnp.float32), NEG)
    m = jnp.max(s, axis=-1, keepdims=True)
    p = jnp.exp(s - m) * mask
    return p / jnp.maximum(jnp.sum(p, axis=-1, keepdims=True), 1e-30)


def stick_breaking(q, k, v, q_pos, k_pos):
    z = jnp.einsum('bqhd,bkhd->bhqk', q, k).astype(jnp.float32) * SCALE
    vis = k_pos[None, :] < q_pos[:, None]
    log_1m = jnp.where(vis, jax.nn.log_sigmoid(-z), 0.0)
    tail = lax.cumsum(log_1m, axis=3, reverse=True) - log_1m
    a = jnp.where(vis, jnp.exp(jax.nn.log_sigmoid(z) + tail), 0.0)
    return jnp.einsum('bhqk,bkhd->bqhd', a.astype(v.dtype), v)


def diff_attention(q, k, v, q_pos, k_pos, lam):
    s = jnp.einsum('bqhid,bkhid->bihqk', q, k).astype(jnp.float32) * SCALE
    s = jnp.where(k_pos[None, :] <= q_pos[:, None], s, NEG)
    p = jax.nn.softmax(s, axis=-1)
    a = p[:, 0] - lam * p[:, 1]
    return jnp.einsum('bhqk,bkhe->bqhe', a.astype(v.dtype), v)


def compress(t, pe, w1, w2):
    B, Lp, G, d = t.shape
    n = Lp // CMP_STRIDE
    c = t.reshape(B, n, CMP_STRIDE, G, d)
    nxt = jnp.concatenate([c[:, 1:], jnp.zeros_like(c[:, :1])], axis=1)
    blk = jnp.concatenate([c, nxt], axis=2) + pe[:, None, :]
    flat = jnp.moveaxis(blk, 3, 2).reshape(B, n, G, CMP_LEN * d)
    return jax.nn.silu(flat @ w1) @ w2


def even_mixer(h, q_pos, past, w_in, w_out, qk_gain, lam_vec, subln_gain, lam_init):
    B, T, _ = h.shape
    q_sb, k_sb, v_sb, q_df, k_df, v_df = split_cols(h @ w_in, EVEN_WIDTHS)
    q_df = rope(rms_norm(q_df.reshape(B, T, 2 * N_DF, HD), qk_gain[0]), q_pos)
    k_df = rope(rms_norm(k_df.reshape(B, T, 2 * N_DF, HD), qk_gain[1]), q_pos)
    new_k = jnp.concatenate([k_sb, k_df.reshape(B, T, DF_W)], axis=-1)
    new_v = jnp.concatenate([v_sb, v_df], axis=-1)
    if past is None:
        all_k, all_v = new_k, new_v
    else:
        all_k = jnp.concatenate([past[0], new_k], axis=1)
        all_v = jnp.concatenate([past[1], new_v], axis=1)
    L = all_k.shape[1]
    k_pos = jnp.arange(L, dtype=jnp.int32)
    k_sb_all = all_k[..., :SB_W].reshape(B, L, N_SB, HD)
    v_sb_all = all_v[..., :SB_W].reshape(B, L, N_SB, HD)
    k_df_all = all_k[..., SB_W:].reshape(B, L, N_DF, 2, HD)
    v_df_all = all_v[..., SB_W:].reshape(B, L, N_DF, 2 * HD)
    lv = lam_vec.astype(jnp.float32)
    lam = jnp.exp(jnp.sum(lv[0] * lv[1])) - jnp.exp(jnp.sum(lv[2] * lv[3])) + lam_init

    def block(qs, qp):
        qsb, qdf = qs
        Tq = qp.shape[0]
        o_sb = stick_breaking(qsb, k_sb_all, v_sb_all, qp, k_pos)
        o_df = rms_norm(diff_attention(qdf, k_df_all, v_df_all, qp, k_pos, lam), subln_gain) * (1.0 - lam_init)
        return jnp.concatenate([o_sb.reshape(B, Tq, SB_W), o_df.reshape(B, Tq, DF_W)], axis=-1)

    o = over_query_blocks(block, (q_sb.reshape(B, T, N_SB, HD), q_df.reshape(B, T, N_DF, 2, HD)), q_pos)
    return o @ w_out, new_k, new_v


def odd_mixer(h, q_pos, past, w_in, w_out, nsa_gain, cmp_pe, cmp_w1, cmp_w2, dsa_gain):
    B, T, _ = h.shape
    (q_n, k_c, v_c, k_s, v_s, k_w, v_w, gate, q_d, k_d, v_d, q_i, k_i, w_i) = split_cols(h @ w_in, ODD_WIDTHS)

    def kvh(a):
        return a.reshape(B, T, N_NSA_KV, HD)

    q_n = rms_norm(q_n.reshape(B, T, N_NSA, HD), nsa_gain[0])
    q_r = rope(q_n, q_pos)
    k_s = rope(rms_norm(kvh(k_s), nsa_gain[2]), q_pos)
    k_w = rope(rms_norm(kvh(k_w), nsa_gain[3]), q_pos)
    gate = jax.nn.sigmoid(gate).reshape(B, T, N_NSA, 3)
    q_d = rope(rms_norm(q_d.reshape(B, T, N_DSA, HD), dsa_gain[0]), q_pos)
    k_d = rope(rms_norm(k_d.reshape(B, T, 1, HD), dsa_gain[1]), q_pos)[:, :, 0]
    k_i = rope(rms_norm(k_i.reshape(B, T, 1, IDX_D), dsa_gain[2]), q_pos)[:, :, 0]
    q_i = rope(q_i.reshape(B, T, N_IDX, IDX_D), q_pos)
    new_nsa = jnp.stack([kvh(k_c), kvh(v_c), k_s, kvh(v_s)], axis=2)
    new_dsa = jnp.stack([k_d, v_d, k_i], axis=2)
    new_win = jnp.stack([k_w, kvh(v_w)], axis=2)
    if past is None:
        all_nsa, all_dsa = new_nsa, new_dsa
        w_src = jnp.pad(new_win, ((0, 0), (WINDOW, 0), (0, 0), (0, 0), (0, 0)))

        def window_rows(qp):
            Tq = qp.shape[0]
            rows = lax.dynamic_slice_in_dim(w_src, qp[0], WINDOW + Tq, axis=1)
            return rows, qp[0] - WINDOW + jnp.arange(WINDOW + Tq, dtype=jnp.int32)

        new_win_state = new_win[:, -min(WINDOW, T):]
    else:
        all_nsa = jnp.concatenate([past[0], new_nsa], axis=1)
        all_dsa = jnp.concatenate([past[1], new_dsa], axis=1)
        wb = past[2].shape[1]
        w_all = jnp.concatenate([past[2], new_win], axis=1)
        w_pos_all = q_pos[0] - wb + jnp.arange(wb + T, dtype=jnp.int32)

        def window_rows(qp):
            return w_all, w_pos_all

        new_win_state = w_all[:, -wb:]
    L = all_nsa.shape[1]
    Lp = -(-L // SLC_BLOCK) * SLC_BLOCK
    nsa_p = jnp.pad(all_nsa, ((0, 0), (0, Lp - L), (0, 0), (0, 0), (0, 0)))
    kc_blk = rms_norm(compress(nsa_p[:, :, 0], cmp_pe[0], cmp_w1[0], cmp_w2[0]), nsa_gain[1])
    vc_blk = compress(nsa_p[:, :, 1], cmp_pe[1], cmp_w1[1], cmp_w2[1])
    n_cmp = Lp // CMP_STRIDE
    n_blk = Lp // SLC_BLOCK
    c_end = jnp.arange(n_cmp, dtype=jnp.int32) * CMP_STRIDE + (CMP_LEN - 1)
    blk_idx = jnp.arange(n_blk, dtype=jnp.int32)

    def to_sel_blocks(a):
        return a.reshape(B, n_blk, SLC_BLOCK, N_NSA_KV, HD).transpose(0, 3, 1, 2, 4)

    ks_blk, vs_blk = to_sel_blocks(nsa_p[:, :, 2]), to_sel_blocks(nsa_p[:, :, 3])
    n_sel = min(N_SLC, n_blk)
    k_pos = jnp.arange(L, dtype=jnp.int32)
    k_dsa, v_dsa, kidx = all_dsa[:, :, 0], all_dsa[:, :, 1], all_dsa[:, :, 2]
    n_top = min(DSA_TOPK_MAX, L // 4)
    bi = jnp.arange(B)[:, None, None, None]
    gi = jnp.arange(N_NSA_KV)[None, :, None, None]
    bq = jnp.arange(B)[:, None, None]

    def block(qs, qp):
        qn_b, qr_b, g_b, qd_b, qi_b, wi_b = qs
        Tq = qp.shape[0]
        qg = qn_b.reshape(B, Tq, N_NSA_KV, NSA_GROUP, HD)
        qrg = qr_b.reshape(B, Tq, N_NSA_KV, NSA_GROUP, HD)
        p_c = masked_softmax(jnp.einsum('bqgrd,bngd->bgrqn', qg, kc_blk) * SCALE, c_end[None, :] <= qp[:, None])
        o_c = jnp.einsum('bgrqn,bngd->bqgrd', p_c.astype(vc_blk.dtype), vc_blk)
        imp = p_c.sum(axis=2).reshape(B, N_NSA_KV, Tq, n_blk, CMP_PER_SLC).sum(axis=-1)
        q_blk = qp // SLC_BLOCK
        visible = blk_idx[None, :] * SLC_BLOCK <= qp[:, None]
        forced = (blk_idx[None, :] == 0) | ((blk_idx[None, :] <= q_blk[:, None]) & (blk_idx[None, :] > q_blk[:, None] - N_LOCAL))
        score = jnp.where(visible, jnp.where(forced, FORCE, imp), NEG)
        _, sel = lax.top_k(score, n_sel)
        k_sel, v_sel = ks_blk[bi, gi, sel], vs_blk[bi, gi, sel]
        tok_pos = sel[..., None] * SLC_BLOCK + jnp.arange(SLC_BLOCK, dtype=jnp.int32)
        m_s = (tok_pos <= qp[None, None, :, None, None]).reshape(B, N_NSA_KV, 1, Tq, n_sel * SLC_BLOCK)
        s_s = jnp.einsum('bqgrd,bgqjld->bgrqjl', qrg, k_sel).reshape(B, N_NSA_KV, NSA_GROUP, Tq, n_sel * SLC_BLOCK) * SCALE
        p_s = masked_softmax(s_s, m_s)
        o_s = jnp.einsum('bgrqm,bgqmd->bqgrd', p_s.astype(v_sel.dtype), v_sel.reshape(B, N_NSA_KV, Tq, n_sel * SLC_BLOCK, HD))
        rows, w_pos = window_rows(qp)
        m_w = (w_pos[None, :] <= qp[:, None]) & (qp[:, None] - w_pos[None, :] < WINDOW) & (w_pos[None, :] >= 0)
        p_w = masked_softmax(jnp.einsum('bqgrd,bkgd->bgrqk', qrg, rows[:, :, 0]) * SCALE, m_w)
        o_w = jnp.einsum('bgrqk,bkgd->bqgrd', p_w.astype(rows.dtype), rows[:, :, 1])
        g = g_b.reshape(B, Tq, N_NSA_KV, NSA_GROUP, 3)
        o_nsa = g[..., 0:1] * o_c + g[..., 1:2] * o_s + g[..., 2:3] * o_w
        sc = jnp.einsum('bqhd,bkd->bqhk', qi_b, kidx).astype(jnp.float32) * IDX_SCALE
        idx_score = jnp.einsum('bqh,bqhk->bqk', wi_b.astype(jnp.float32) * (N_IDX ** -0.5), jax.nn.relu(sc))
        idx_score = jnp.where(k_pos[None, :] <= qp[:, None], idx_score, NEG)
        _, top = lax.top_k(idx_score, n_top)
        k_top, v_top = k_dsa[bq, top], v_dsa[bq, top]
        p_d = masked_softmax(jnp.einsum('bqhd,bqjd->bhqj', qd_b, k_top) * SCALE, (top <= qp[None, :, None])[:, None])
        o_d = jnp.einsum('bhqj,bqjd->bqhd', p_d.astype(v_top.dtype), v_top)
        return jnp.concatenate([o_nsa.reshape(B, Tq, N_NSA * HD), o_d.reshape(B, Tq, N_DSA * HD)], axis=-1)

    o = over_query_blocks(block, (q_n, q_r, gate, q_d, q_i, w_i), q_pos)
    return o @ w_out, new_nsa, new_dsa, new_win_state


def memory_kv(mem, wk, wv, gk):
    B, M, _ = mem.shape
    k = rms_norm((mem @ wk).reshape(B, M, N_XH, HD), gk)
    v = (mem @ wv).reshape(B, M, N_XH, HD)
    return jnp.stack([k, v], axis=2)


def cross_attend(h, mem_kv, wq, wo, gq):
    B, T, _ = h.shape
    q = rms_norm((h @ wq).reshape(B, T, N_XH, HD), gq)
    s = jnp.einsum('bqhd,bmhd->bhqm', q, mem_kv[:, :, 0]).astype(jnp.float32) * SCALE
    p = jax.nn.softmax(s, axis=-1)
    o = jnp.einsum('bhqm,bmhd->bqhd', p.astype(h.dtype), mem_kv[:, :, 1])
    return o.reshape(B, T, X_W) @ wo


def swiglu(h, w1, w3, w2):
    return (jax.nn.silu(h @ w1) * (h @ w3)) @ w2


def moe(h, router, w1, w3, w2):
    logits = (h @ router).astype(jnp.float32)
    top_val, top_idx = lax.top_k(logits, TOP_K_EXPERTS)
    gates = jax.nn.softmax(top_val, axis=-1)
    dense_gate = jnp.sum(jax.nn.one_hot(top_idx, N_EXPERTS, dtype=jnp.float32) * gates[..., None], axis=-2)
    out = jnp.zeros_like(h)
    for e in range(N_EXPERTS):
        out = out + dense_gate[..., e:e + 1].astype(h.dtype) * swiglu(h, w1[e], w3[e], w2[e])
    return out


def setup_inputs(seed: int = 0) -> dict:
    key = jax.random.key(seed)
    keys = iter(jax.random.split(key, 48))

    def nrm(shape, scale=1.0):
        return jax.random.normal(next(keys), shape, jnp.float32) * scale

    def gain(shape):
        return 1.0 + 0.05 * nrm(shape)

    n_pages = PAST_LEN // PAGE_SIZE
    n_used = DEC_BATCH * n_pages
    n_pool = n_used + n_used // 4
    win_buf = min(WINDOW, PAST_LEN)
    page_table = jax.random.permutation(next(keys), n_pool)[:n_used].reshape(DEC_BATCH, n_pages).astype(jnp.int32)
    return {
        'x_prompt': nrm((BATCH, SEQ, D_MODEL)),
        'x_sample': nrm((DEC_BATCH, DEC_SEQ, D_MODEL)),
        'mem_prompt': nrm((BATCH, N_MEM, D_MODEL)),
        'cache_even_k': nrm((N_EVEN, n_pool, PAGE_SIZE, MIX_W)),
        'cache_even_v': nrm((N_EVEN, n_pool, PAGE_SIZE, MIX_W)),
        'cache_odd_nsa': nrm((N_ODD, n_pool, PAGE_SIZE, 4, N_NSA_KV, HD)),
        'cache_odd_dsa': nrm((N_ODD, n_pool, PAGE_SIZE, 3, HD)),
        'state_odd_win': nrm((N_ODD, DEC_BATCH, win_buf, 2, N_NSA_KV, HD)),
        'cache_mem': nrm((DEPTH, DEC_BATCH, N_MEM, 2, N_XH, HD)),
        'page_table': page_table,
        'norm_gain': gain((DEPTH, 3, D_MODEL)),
        'ev_w_in': nrm((N_EVEN, D_MODEL, EVEN_IN_W), D_MODEL ** -0.5),
        'ev_w_out': nrm((N_EVEN, MIX_W, D_MODEL), MIX_W ** -0.5),
        'df_qk_gain': gain((N_EVEN, 2, HD)),
        'df_lambda': nrm((N_EVEN, 4, HD), 0.1),
        'df_subln_gain': gain((N_EVEN, 2 * HD)),
        'ffn_w1': nrm((N_EVEN, D_MODEL, D_FF), D_MODEL ** -0.5),
        'ffn_w3': nrm((N_EVEN, D_MODEL, D_FF), D_MODEL ** -0.5),
        'ffn_w2': nrm((N_EVEN, D_FF, D_MODEL), D_FF ** -0.5),
        'od_w_in': nrm((N_ODD, D_MODEL, ODD_IN_W), D_MODEL ** -0.5),
        'od_w_out': nrm((N_ODD, MIX_W, D_MODEL), MIX_W ** -0.5),
        'nsa_qk_gain': gain((N_ODD, 4, HD)),
        'cmp_pe': nrm((N_ODD, 2, CMP_LEN, HD), 0.1),
        'cmp_w1': nrm((N_ODD, 2, CMP_LEN * HD, CMP_HIDDEN), (CMP_LEN * HD) ** -0.5),
        'cmp_w2': nrm((N_ODD, 2, CMP_HIDDEN, HD), CMP_HIDDEN ** -0.5),
        'dsa_qk_gain': gain((N_ODD, 3, HD)),
        'moe_router': nrm((N_ODD, D_MODEL, N_EXPERTS), D_MODEL ** -0.5),
        'moe_w1': nrm((N_ODD, N_EXPERTS, D_MODEL, MOE_D_FF), D_MODEL ** -0.5),
        'moe_w3': nrm((N_ODD, N_EXPERTS, D_MODEL, MOE_D_FF), D_MODEL ** -0.5),
        'moe_w2': nrm((N_ODD, N_EXPERTS, MOE_D_FF, D_MODEL), MOE_D_FF ** -0.5),
        'x_wq': nrm((DEPTH, D_MODEL, X_W), D_MODEL ** -0.5),
        'x_wk': nrm((DEPTH, D_MODEL, X_W), D_MODEL ** -0.5),
        'x_wv': nrm((DEPTH, D_MODEL, X_W), D_MODEL ** -0.5),
        'x_wo': nrm((DEPTH, X_W, D_MODEL), X_W ** -0.5),
        'x_qk_gain': gain((DEPTH, 2, HD)),
    }


def reference(x_prompt, x_sample, mem_prompt, cache_even_k, cache_even_v, cache_odd_nsa, cache_odd_dsa,
              state_odd_win, cache_mem, page_table, norm_gain, ev_w_in, ev_w_out, df_qk_gain, df_lambda,
              df_subln_gain, ffn_w1, ffn_w3, ffn_w2, od_w_in, od_w_out, nsa_qk_gain, cmp_pe, cmp_w1, cmp_w2,
              dsa_qk_gain, moe_router, moe_w1, moe_w3, moe_w2, x_wq, x_wk, x_wv, x_wo, x_qk_gain):

    def run(x, q_pos, mem_kvs, pasts):
        ek, ev, on, od, ow = [], [], [], [], []
        for layer in range(DEPTH):
            i = layer // 2
            h = rms_norm(x, norm_gain[layer, 0])
            if layer % 2 == 0:
                lam_init = 0.8 - 0.6 * math.exp(-0.3 * layer)
                mix, nk, nv = even_mixer(h, q_pos, pasts[layer], ev_w_in[i], ev_w_out[i], df_qk_gain[i],
                                         df_lambda[i], df_subln_gain[i], lam_init)
                ek.append(nk)
                ev.append(nv)
            else:
                mix, nn_rows, nd_rows, nw_state = odd_mixer(h, q_pos, pasts[layer], od_w_in[i], od_w_out[i],
                                                            nsa_qk_gain[i], cmp_pe[i], cmp_w1[i], cmp_w2[i], dsa_qk_gain[i])
                on.append(nn_rows)
                od.append(nd_rows)
                ow.append(nw_state)
            x = x + mix
            x = x + cross_attend(rms_norm(x, norm_gain[layer, 1]), mem_kvs[layer], x_wq[layer], x_wo[layer], x_qk_gain[layer, 0])
            hf = rms_norm(x, norm_gain[layer, 2])
            if layer % 2 == 0:
                x = x + swiglu(hf, ffn_w1[i], ffn_w3[i], ffn_w2[i])
            else:
                x = x + moe(hf, moe_router[i], moe_w1[i], moe_w3[i], moe_w2[i])
        return x, jnp.stack(ek), jnp.stack(ev), jnp.stack(on), jnp.stack(od), jnp.stack(ow)

    pos_p = jnp.arange(x_prompt.shape[1], dtype=jnp.int32)
    mem_p = [memory_kv(mem_prompt, x_wk[l], x_wv[l], x_qk_gain[l, 1]) for l in range(DEPTH)]
    y_prompt, p_even_k, p_even_v, p_odd_nsa, p_odd_dsa, p_odd_win = run(x_prompt, pos_p, mem_p, [None] * DEPTH)
    p_mem = jnp.stack(mem_p)

    n_past = page_table.shape[1] * cache_even_k.shape[2]
    pos_s = n_past + jnp.arange(x_sample.shape[1], dtype=jnp.int32)
    past_s = []
    for layer in range(DEPTH):
        i = layer // 2
        if layer % 2 == 0:
            past_s.append((gather_pages(cache_even_k[i], page_table), gather_pages(cache_even_v[i], page_table)))
        else:
            past_s.append((gather_pages(cache_odd_nsa[i], page_table), gather_pages(cache_odd_dsa[i], page_table),
                           state_odd_win[i]))
    y_sample, s_even_k, s_even_v, s_odd_nsa, s_odd_dsa, s_odd_win = run(
        x_sample, pos_s, [cache_mem[l] for l in range(DEPTH)], past_s)
    return (y_prompt, y_sample, p_even_k, p_even_v, p_odd_nsa, p_odd_dsa, p_odd_win, p_mem,
            s_even_k, s_even_v, s_odd_nsa, s_odd_dsa, s_odd_win)
```

```python
import functools
import math

import jax
import jax.numpy as jnp
import numpy as np
from jax import lax
from jax.experimental import pallas as pl
from jax.experimental.pallas import tpu as pltpu

F32 = jnp.float32
BF16 = jnp.bfloat16

HD = 64
N_SB = 8
N_DF = 4
N_NSA = 8
N_NSA_KV = 2
NSA_GROUP = N_NSA // N_NSA_KV
N_DSA = 8
N_IDX = 8
N_XH = 4
N_EXPERTS = 8
ROPE_THETA = 10000.0
CMP_LEN = 32
CMP_STRIDE = 16
SLC_BLOCK = 64
CMP_PER_SLC = SLC_BLOCK // CMP_STRIDE
N_SLC = 16
N_LOCAL = 2
WINDOW = 512
DSA_TOPK_MAX = 256
EPS = 1e-6
NEG = -1e30
FORCE = 1e9
SCALE = HD ** -0.5
IDX_SCALE = HD ** -0.5
SB_W = N_SB * HD
HD_SHIFT = 6
SLC_SHIFT = 6
DF_W = N_DF * 2 * HD

LANES = 128
SUBLANES = 8
PAGE = 128
VMEM_LIMIT = 52 * 1024 * 1024
INT_MIN = -2 ** 31

_NT = (((1,), (1,)), ((), ()))


def _cparams(sem):
    return pltpu.CompilerParams(dimension_semantics=sem, vmem_limit_bytes=VMEM_LIMIT)


def _dot(a, b):
    return jnp.dot(a, b, preferred_element_type=F32)


def _dot_nt(a, b):
    return lax.dot_general(a, b, _NT, preferred_element_type=F32)


def _split_dot(x, m_bf16):
    hi = x.astype(BF16)
    lo = (x - hi.astype(F32)).astype(BF16)
    return _dot(hi, m_bf16) + _dot(lo, m_bf16)


def _iota(shape, dim):
    return lax.broadcasted_iota(jnp.int32, shape, dim)


def _rms_rows(x, g):
    return x * lax.rsqrt(jnp.mean(x * x, axis=-1, keepdims=True) + EPS) * g


def _group_mean_matrix():
    r = _iota((LANES, LANES), 0) >> HD_SHIFT
    c = _iota((LANES, LANES), 1) >> HD_SHIFT
    return jnp.where(r == c, 1.0 / HD, 0.0).astype(BF16)


def _head_norm(y, g, gm):
    ms = _split_dot(y * y, gm)
    return y * lax.rsqrt(ms + EPS) * g


def _rope_slab(y, cos, sin):
    lane = _iota(y.shape, 1)
    first = (lane & (HD - 1)) < (HD // 2)
    swapped = jnp.where(first, pltpu.roll(y, LANES - HD // 2, 1), pltpu.roll(y, HD // 2, 1))
    return y * cos + swapped * sin


def _mm_kernel(*refs, n_rows, n_consts, lhs_fn, program, rope_idx, gains_idx, res_idx):
    rows = refs[:n_rows]
    consts = refs[n_rows:n_rows + n_consts]
    w_ref = refs[n_rows + n_consts]
    o_ref = refs[-1]
    lhs = lhs_fn(rows, consts).astype(BF16)
    gm = _group_mean_matrix() if gains_idx is not None else None
    for (src, width, gain_row, act, outs) in program:
        y_full = _dot(lhs, w_ref[:, src:src + width])
        for s in range(width // LANES):
            y = y_full[:, s * LANES:(s + 1) * LANES]
            if gain_row is not None:
                y = _head_norm(y, consts[gains_idx][gain_row:gain_row + 1, :], gm)
            if act == "sigmoid":
                y = 1.0 / (1.0 + jnp.exp(-y))
            for (dst, rope) in outs:
                z = y
                if rope:
                    z = _rope_slab(y, rows[rope_idx[0]][...], rows[rope_idx[1]][...])
                d0 = dst + s * LANES
                if res_idx is not None:
                    z = z + rows[res_idx][:, d0:d0 + LANES]
                o_ref[:, d0:d0 + LANES] = z


def _mm(rows, consts, w, lhs_fn, program, out_cols, *, tm, rope_idx=None, gains_idx=None, res_idx=None):
    n = rows[0].shape[0]
    assert n % tm == 0
    in_specs = [pl.BlockSpec((tm, r.shape[1]), lambda i: (i, 0)) for r in rows]
    in_specs += [pl.BlockSpec(c.shape, lambda i: (0, 0)) for c in consts]
    in_specs += [pl.BlockSpec(w.shape, lambda i: (0, 0))]
    kern = functools.partial(_mm_kernel, n_rows=len(rows), n_consts=len(consts), lhs_fn=lhs_fn,
                             program=program, rope_idx=rope_idx, gains_idx=gains_idx, res_idx=res_idx)
    return pl.pallas_call(
        kern,
        grid=(n // tm,),
        in_specs=in_specs,
        out_specs=pl.BlockSpec((tm, out_cols), lambda i: (i, 0)),
        out_shape=jax.ShapeDtypeStruct((n, out_cols), F32),
        compiler_params=_cparams(("parallel",)),
    )(*rows, *consts, w)


def _lhs_norm(rows, consts):
    return _rms_rows(rows[0][...], consts[0][...])


def _lhs_plain(rows, consts):
    return rows[0][...]


def _lhs_even(lam_init, rows, consts):
    o_sb, o1, o2 = rows[0][...], rows[1][...], rows[2][...]
    lv = consts[0][...]
    subln = consts[1][...]
    a = jnp.sum(jnp.sum(lv[0:1] * lv[1:2], axis=1, keepdims=True), axis=0, keepdims=True)
    b = jnp.sum(jnp.sum(lv[2:3] * lv[3:4], axis=1, keepdims=True), axis=0, keepdims=True)
    lam = jnp.exp(a) - jnp.exp(b) + lam_init
    parts = [o_sb]
    for h in range(N_DF):
        sl = slice(h * LANES, (h + 1) * LANES)
        d = o1[:, sl] - lam * o2[:, sl]
        parts.append(_rms_rows(d, subln) * (1.0 - lam_init))
    return jnp.concatenate(parts, axis=1)


def _lhs_odd(rows, consts):
    oc, os_, ow, od = rows[0][...], rows[1][...], rows[2][...], rows[3][...]
    g0, g1, g2 = rows[4][...], rows[5][...], rows[6][...]
    return jnp.concatenate([g0 * oc + g1 * os_ + g2 * ow, od], axis=1)


def _ffn_kernel(x_ref, g_ref, r_ref, w1_ref, w3_ref, w2_ref, o_ref, h_ref, acc_ref, gate_ref, *, routed):
    e = pl.program_id(1)
    f = pl.program_id(2)
    first = jnp.logical_and(e == 0, f == 0)
    last = jnp.logical_and(e == pl.num_programs(1) - 1, f == pl.num_programs(2) - 1)

    @pl.when(first)
    def _():
        x = x_ref[...]
        h = _rms_rows(x, g_ref[...])
        h_ref[...] = h.astype(BF16)
        acc_ref[...] = x
        if routed:
            logits = jnp.dot(h, r_ref[...], preferred_element_type=F32, precision=lax.Precision.HIGHEST)
            col = _iota(logits.shape, 1).astype(F32)
            logits = jnp.where(col < N_EXPERTS, logits, -jnp.inf)
            m1 = jnp.max(logits, axis=1, keepdims=True)
            i1 = jnp.min(jnp.where(logits == m1, col, 1e9), axis=1, keepdims=True)
            rest = jnp.where(col == i1, -jnp.inf, logits)
            m2 = jnp.max(rest, axis=1, keepdims=True)
            i2 = jnp.min(jnp.where(rest == m2, col, 1e9), axis=1, keepdims=True)
            e2 = jnp.exp(m2 - m1)
            g1 = 1.0 / (1.0 + e2)
            g2 = e2 / (1.0 + e2)
            gate_ref[...] = jnp.where(col == i1, g1, 0.0) + jnp.where(col == i2, g2, 0.0)

    def compute(gcol):
        h = h_ref[...]
        u = _dot(h, w1_ref[0])
        v = _dot(h, w3_ref[0])
        a = (u * (1.0 / (1.0 + jnp.exp(-u)))) * v
        y = _dot(a.astype(BF16), w2_ref[0])
        if gcol is not None:
            y = gcol * y
        acc_ref[...] += y

    if routed:
        col = _iota(gate_ref.shape, 1)
        gcol = jnp.sum(jnp.where(col == e, gate_ref[...], 0.0), axis=1, keepdims=True)
        active = jnp.max(gcol) > 0.0

        @pl.when(active)
        def _():
            compute(gcol)
    else:
        compute(None)

    @pl.when(last)
    def _():
        o_ref[...] = acc_ref[...]


def _ffn(x, g, router, w1, w3, w2, *, tm, tf, routed):
    n, d = x.shape
    ne, _, ff = w1.shape
    assert n % tm == 0 and ff % tf == 0
    kern = functools.partial(_ffn_kernel, routed=routed)
    return pl.pallas_call(
        kern,
        grid=(n // tm, ne, ff // tf),
        in_specs=[
            pl.BlockSpec((tm, d), lambda i, e, f: (i, 0)),
            pl.BlockSpec((1, d), lambda i, e, f: (0, 0)),
            pl.BlockSpec(router.shape, lambda i, e, f: (0, 0)),
            pl.BlockSpec((1, d, tf), lambda i, e, f: (e, 0, f)),
            pl.BlockSpec((1, d, tf), lambda i, e, f: (e, 0, f)),
            pl.BlockSpec((1, tf, d), lambda i, e, f: (e, f, 0)),
        ],
        out_specs=pl.BlockSpec((tm, d), lambda i, e, f: (i, 0)),
        out_shape=jax.ShapeDtypeStruct((n, d), F32),
        scratch_shapes=[pltpu.VMEM((tm, d), BF16), pltpu.VMEM((tm, d), F32), pltpu.VMEM((tm, LANES), F32)],
        compiler_params=_cparams(("parallel", "arbitrary", "arbitrary")),
    )(x, g, router, w1, w3, w2)


def _tile_rows(m, reps):
    return m if reps == 1 else jnp.concatenate([m] * reps, axis=0)


def _softmax_step(s, maskf, v, m, l, acc):
    keep = maskf > 0.5
    s = jnp.where(keep, s, NEG)
    m_new = jnp.maximum(m, jnp.max(s, axis=1, keepdims=True))
    p = jnp.where(keep, jnp.exp(s - m_new), 0.0)
    alpha = jnp.exp(m - m_new)
    l = alpha * l + jnp.sum(p, axis=1, keepdims=True)
    acc = alpha * acc + _dot(p.astype(BF16), v)
    return m_new, l, acc


def _flash_kernel(*refs, reps, tq, tk, q_off, k_off, mode, has_bm):
    if has_bm:
        q_ref, k_ref, v_ref, bm_ref, o_ref = refs
    else:
        q_ref, k_ref, v_ref, o_ref = refs
    i = pl.program_id(2)
    q = q_ref[0, 0, 0]
    rows = q.shape[0]
    dv = v_ref.shape[-1]
    n_kt = k_ref.shape[2] // tk
    q_lo = q_off + i * tq
    if mode == "full":
        lo, hi = 0, n_kt
    else:
        hi = jnp.minimum(lax.div(q_lo + tq - 1 - k_off, tk) + 1, n_kt)
        lo = 0
        if mode == "window":
            lo = lax.div(jnp.maximum(q_lo - (WINDOW - 1) - k_off, 0), tk)
    qpos = q_lo + _iota((tq, tk), 0)
    if has_bm:
        bm = bm_ref[0, 0].astype(BF16)
        nbp = bm.shape[1]

    def body(j, carry):
        m, l, acc = carry
        k0 = pl.multiple_of(j * tk, tk)
        k = k_ref[0, 0, pl.ds(k0, tk), :]
        v = v_ref[0, 0, pl.ds(k0, tk), :]
        s = _dot_nt(q, k)
        kidx = j * tk + _iota((tq, tk), 1)
        kpos = k_off + kidx
        if mode == "full":
            maskf = jnp.ones((tq, tk), F32)
        else:
            ok = kpos <= qpos
            if mode == "window":
                ok = jnp.logical_and(ok, qpos - kpos < WINDOW)
                ok = jnp.logical_and(ok, kpos >= 0)
            maskf = jnp.where(ok, 1.0, 0.0)
        if has_bm:
            blk = _iota((nbp, tk), 0)
            tok = (j * tk + _iota((nbp, tk), 1)) >> SLC_SHIFT
            expand = jnp.where(blk == tok, 1.0, 0.0).astype(BF16)
            maskf = maskf * _dot(bm, expand)
        return _softmax_step(s, _tile_rows(maskf, reps), v, m, l, acc)

    m0 = jnp.full((rows, 1), NEG, F32)
    l0 = jnp.zeros((rows, 1), F32)
    a0 = jnp.zeros((rows, dv), F32)
    m, l, acc = lax.fori_loop(lo, hi, body, (m0, l0, a0))
    o_ref[0, 0, 0] = acc / jnp.maximum(l, 1e-30)


def _flash(q, k, v, *, reps, tq, tk, q_off, k_off, mode, bm=None, v_map=None):
    b, hk, nq, rows, _ = q.shape
    lk = k.shape[2]
    dv = v.shape[3]
    assert rows == reps * tq and lk % tk == 0
    if v_map is None:
        v_map = lambda h: h
    in_specs = [
        pl.BlockSpec((1, 1, 1, rows, HD), lambda b_, h, i: (b_, h, i, 0, 0)),
        pl.BlockSpec((1, 1, lk, HD), lambda b_, h, i: (b_, h, 0, 0)),
        pl.BlockSpec((1, 1, lk, dv), lambda b_, h, i: (b_, v_map(h), 0, 0)),
    ]
    args = [q, k, v]
    if bm is not None:
        in_specs.append(pl.BlockSpec((1, 1, tq, bm.shape[3]), lambda b_, h, i: (b_, h, i, 0)))
        args.append(bm)
    kern = functools.partial(_flash_kernel, reps=reps, tq=tq, tk=tk, q_off=q_off, k_off=k_off, mode=mode,
                             has_bm=bm is not None)
    return pl.pallas_call(
        kern,
        grid=(b, hk, nq),
        in_specs=in_specs,
        out_specs=pl.BlockSpec((1, 1, 1, rows, dv), lambda b_, h, i: (b_, h, i, 0, 0)),
        out_shape=jax.ShapeDtypeStruct((b, hk, nq, rows, dv), F32),
        compiler_params=_cparams(("parallel", "parallel", "arbitrary")),
    )(*args)


def _sb_kernel(q_ref, k_ref, v_ref, o_ref, *, tq, tk, q_off):
    i = pl.program_id(2)
    q = q_ref[0, 0]
    q_lo = q_off + i * tq
    n_full = lax.div(q_lo, tk)
    later = jnp.where(_iota((tk, tk), 0) > _iota((tk, tk), 1), 1.0, 0.0).astype(BF16)

    def tile(j, carry, o, masked):
        k0 = pl.multiple_of(j * tk, tk)
        k = k_ref[0, 0, pl.ds(k0, tk), :]
        v = v_ref[0, 0, pl.ds(k0, tk), :]
        z = _dot_nt(q, k)
        sp = jnp.maximum(z, 0.0) + jnp.log1p(jnp.exp(-jnp.abs(z)))
        log_sig = z - sp
        if masked:
            vis = (j * tk + _iota((tq, tk), 1)) < (q_lo + _iota((tq, tk), 0))
            sp = jnp.where(vis, sp, 0.0)
        a = jnp.exp(log_sig - _split_dot(sp, later) - carry)
        if masked:
            a = jnp.where(vis, a, 0.0)
        o = o + _dot(a.astype(BF16), v)
        carry = carry + jnp.sum(sp, axis=1, keepdims=True)
        return carry, o

    carry, o = tile(n_full, jnp.zeros((tq, 1), F32), jnp.zeros((tq, HD), F32), True)

    def body(t, co):
        return tile(n_full - 1 - t, co[0], co[1], False)

    carry, o = lax.fori_loop(0, n_full, body, (carry, o))
    o_ref[0, 0] = o


def _sb(q, k, v, *, tq, tk, q_off):
    b, h, t, _ = q.shape
    lk = k.shape[2]
    assert t % tq == 0 and lk % tk == 0 and tk % tq == 0 and q_off % tk == 0
    kern = functools.partial(_sb_kernel, tq=tq, tk=tk, q_off=q_off)
    return pl.pallas_call(
        kern,
        grid=(b, h, t // tq),
        in_specs=[
            pl.BlockSpec((1, 1, tq, HD), lambda b_, h_, i: (b_, h_, i, 0)),
            pl.BlockSpec((1, 1, lk, HD), lambda b_, h_, i: (b_, h_, 0, 0)),
            pl.BlockSpec((1, 1, lk, HD), lambda b_, h_, i: (b_, h_, 0, 0)),
        ],
        out_specs=pl.BlockSpec((1, 1, tq, HD), lambda b_, h_, i: (b_, h_, i, 0)),
        out_shape=jax.ShapeDtypeStruct((b, h, t, HD), F32),
        compiler_params=_cparams(("parallel", "parallel", "arbitrary")),
    )(q, k, v)


def _compress_kernel(t_ref, w1_ref, pe_ref, w2_ref, g_ref, o_ref):
    kind = pl.program_id(1)
    half = CMP_STRIDE * HD
    c = t_ref[0, 0, 0].astype(BF16)
    n = c.shape[0]
    a1 = _dot(c, w1_ref[0, :half, :])
    a2 = _dot(c, w1_ref[0, half:, :])
    a2 = jnp.where(_iota(a2.shape, 0) < n - 1, pltpu.roll(a2, n - 1, 0), 0.0)
    pe = _dot(jnp.broadcast_to(pe_ref[0], (SUBLANES, CMP_LEN * HD)).astype(BF16), w1_ref[0])[0:1]
    hid = a1 + a2 + pe
    hid = hid * (1.0 / (1.0 + jnp.exp(-hid)))
    out = _dot(hid.astype(BF16), w2_ref[0])
    normed = _rms_rows(out, g_ref[...])
    o_ref[0, 0, 0] = jnp.where(kind == 0, normed, out)


def _compress(t, w1, pe_flat, w2, gain):
    b, _, g, n, width = t.shape
    return pl.pallas_call(
        _compress_kernel,
        grid=(b, 2, g),
        in_specs=[
            pl.BlockSpec((1, 1, 1, n, width), lambda b_, k, g_: (b_, k, g_, 0, 0)),
            pl.BlockSpec((1,) + w1.shape[1:], lambda b_, k, g_: (k, 0, 0)),
            pl.BlockSpec((1,) + pe_flat.shape[1:], lambda b_, k, g_: (k, 0, 0)),
            pl.BlockSpec((1,) + w2.shape[1:], lambda b_, k, g_: (k, 0, 0)),
            pl.BlockSpec(gain.shape, lambda b_, k, g_: (0, 0)),
        ],
        out_specs=pl.BlockSpec((1, 1, 1, n, HD), lambda b_, k, g_: (b_, k, g_, 0, 0)),
        out_shape=jax.ShapeDtypeStruct((b, 2, g, n, HD), F32),
        compiler_params=_cparams(("parallel", "arbitrary", "arbitrary")),
    )(t, w1, pe_flat, w2, gain)


def _nsa_cmp_kernel(q_ref, kc_ref, vc_ref, o_ref, sel_ref, *, tq, q_off, n_sel):
    i = pl.program_id(2)
    q = q_ref[0, 0, 0]
    kc = kc_ref[0, 0]
    vc = vc_ref[0, 0]
    ncol = kc.shape[0]
    nbp = ncol // CMP_PER_SLC
    q_lo = q_off + i * tq
    qpos = q_lo + _iota((tq, ncol), 0)
    col = _iota((tq, ncol), 1)
    jj = jnp.zeros_like(col)
    for u in range(1, CMP_PER_SLC):
        jj = jj + jnp.where(col >= u * nbp, 1, 0)
    c_end = (col - jj * nbp) * SLC_BLOCK + jj * CMP_STRIDE + (CMP_LEN - 1)
    maskf = _tile_rows(jnp.where(c_end <= qpos, 1.0, 0.0), NSA_GROUP)
    keep = maskf > 0.5
    s = jnp.where(keep, _dot_nt(q, kc), NEG)
    m = jnp.max(s, axis=1, keepdims=True)
    p = jnp.where(keep, jnp.exp(s - m), 0.0)
    p = p / jnp.maximum(jnp.sum(p, axis=1, keepdims=True), 1e-30)
    o_ref[0, 0, 0] = _dot(p.astype(BF16), vc)
    pg = p[0:tq]
    for r in range(1, NSA_GROUP):
        pg = pg + p[r * tq:(r + 1) * tq]
    imp = pg[:, 0:nbp]
    for j in range(1, CMP_PER_SLC):
        imp = imp + pg[:, j * nbp:(j + 1) * nbp]
    blk = _iota((tq, nbp), 1)
    qp = q_lo + _iota((tq, nbp), 0)
    q_blk = qp >> SLC_SHIFT
    visible = blk * SLC_BLOCK <= qp
    forced = jnp.logical_or(blk == 0, jnp.logical_and(blk <= q_blk, blk > q_blk - N_LOCAL))
    score = jnp.where(visible, jnp.where(forced, FORCE, imp), NEG)
    blkf = blk.astype(F32)
    sel = jnp.zeros((tq, nbp), F32)
    for _ in range(n_sel):
        top = jnp.max(score, axis=1, keepdims=True)
        idx = jnp.min(jnp.where(score == top, blkf, 1e9), axis=1, keepdims=True)
        pick = blkf == idx
        sel = jnp.where(pick, 1.0, sel)
        score = jnp.where(pick, -jnp.inf, score)
    sel_ref[0, 0] = sel


def _nsa_cmp(q, kc, vc, *, tq, q_off, n_sel):
    b, g, nq, rows, _ = q.shape
    ncol = kc.shape[2]
    nbp = ncol // CMP_PER_SLC
    kern = functools.partial(_nsa_cmp_kernel, tq=tq, q_off=q_off, n_sel=n_sel)
    return pl.pallas_call(
        kern,
        grid=(b, g, nq),
        in_specs=[
            pl.BlockSpec((1, 1, 1, rows, HD), lambda b_, g_, i: (b_, g_, i, 0, 0)),
            pl.BlockSpec((1, 1, ncol, HD), lambda b_, g_, i: (b_, g_, 0, 0)),
            pl.BlockSpec((1, 1, ncol, HD), lambda b_, g_, i: (b_, g_, 0, 0)),
        ],
        out_specs=[
            pl.BlockSpec((1, 1, 1, rows, HD), lambda b_, g_, i: (b_, g_, i, 0, 0)),
            pl.BlockSpec((1, 1, tq, nbp), lambda b_, g_, i: (b_, g_, i, 0)),
        ],
        out_shape=[
            jax.ShapeDtypeStruct((b, g, nq, rows, HD), F32),
            jax.ShapeDtypeStruct((b, g, nq * tq, nbp), F32),
        ],
        compiler_params=_cparams(("parallel", "parallel", "arbitrary")),
    )(q, kc, vc)


def _dsa_kernel(qi_ref, wi_ref, ki_ref, qd_ref, kd_ref, vd_ref, o_ref, key_ref, *, tq, tk, q_off, n_top):
    i = pl.program_id(1)
    qi = qi_ref[0, 0]
    qd = qd_ref[0, 0]
    w = wi_ref[0] * (N_IDX ** -0.5)
    n_kt = ki_ref.shape[1] // tk
    q_lo = q_off + i * tq
    hi = jnp.minimum(lax.div(q_lo + tq - 1, tk) + 1, n_kt)
    qpos = q_lo + _iota((tq, tk), 0)

    def visible(j):
        return (j * tk + _iota((tq, tk), 1)) <= qpos

    def score_tile(j, _):
        k0 = pl.multiple_of(j * tk, tk)
        sc = jnp.maximum(_dot_nt(qi, ki_ref[0, pl.ds(k0, tk), :]), 0.0)
        tot = w[:, 0:1] * sc[0:tq]
        for h in range(1, N_IDX):
            tot = tot + w[:, h:h + 1] * sc[h * tq:(h + 1) * tq]
        tot = jnp.where(visible(j), tot, NEG)
        bits = pltpu.bitcast(tot, jnp.int32)
        key = jnp.where(bits < 0, bits ^ jnp.int32(0x7FFFFFFF), bits)
        key_ref[:, pl.ds(k0, tk)] = jnp.where(tot == 0.0, 0, key)
        return 0

    lax.fori_loop(0, hi, score_tile, 0)

    def count_ge(c):
        def body(j, acc):
            blk = key_ref[:, pl.ds(pl.multiple_of(j * tk, tk), tk)]
            hit = jnp.where(blk >= c, 1.0, 0.0)
            part = hit[:, 0:LANES]
            for u in range(1, tk // LANES):
                part = part + hit[:, u * LANES:(u + 1) * LANES]
            return acc + part
        acc = lax.fori_loop(0, hi, body, jnp.zeros((tq, LANES), F32))
        return jnp.sum(acc, axis=1, keepdims=True)

    kf = float(n_top)
    tau = jnp.where(count_ge(jnp.zeros((tq, 1), jnp.int32)) >= kf, 0, INT_MIN).astype(jnp.int32)

    def bit_body(t, tau):
        cand = tau + jnp.left_shift(jnp.int32(1), 30 - t)
        return jnp.where(count_ge(cand) >= kf, cand, tau)

    tau = lax.fori_loop(0, 31, bit_body, tau)
    need = kf - count_ge(tau + 1)
    before = jnp.where(_iota((tk, tk), 0) < _iota((tk, tk), 1), 1.0, 0.0).astype(BF16)

    def attend(j, carry):
        m, l, acc, n_eq = carry
        k0 = pl.multiple_of(j * tk, tk)
        key = key_ref[:, pl.ds(k0, tk)]
        eqf = jnp.where(key == tau, 1.0, 0.0)
        rank = n_eq + _dot(eqf.astype(BF16), before)
        kept = jnp.where(key > tau, 1.0, jnp.where(rank < need, eqf, 0.0))
        maskf = jnp.where(visible(j), kept, 0.0)
        s = _dot_nt(qd, kd_ref[0, pl.ds(k0, tk), :])
        m, l, acc = _softmax_step(s, _tile_rows(maskf, N_DSA), vd_ref[0, pl.ds(k0, tk), :], m, l, acc)
        return m, l, acc, n_eq + jnp.sum(eqf, axis=1, keepdims=True)

    rows = qd.shape[0]
    init = (jnp.full((rows, 1), NEG, F32), jnp.zeros((rows, 1), F32), jnp.zeros((rows, HD), F32),
            jnp.zeros((tq, 1), F32))
    m, l, acc, _ = lax.fori_loop(0, hi, attend, init)
    o_ref[0, 0] = acc / jnp.maximum(l, 1e-30)


def _dsa(qi, wi, ki, qd, kd, vd, *, tq, tk, q_off, n_top):
    b, nq, rows, _ = qi.shape
    lk = ki.shape[1]
    assert lk % tk == 0
    kern = functools.partial(_dsa_kernel, tq=tq, tk=tk, q_off=q_off, n_top=n_top)
    qspec = pl.BlockSpec((1, 1, rows, HD), lambda b_, i: (b_, i, 0, 0))
    kspec = pl.BlockSpec((1, lk, HD), lambda b_, i: (b_, 0, 0))
    return pl.pallas_call(
        kern,
        grid=(b, nq),
        in_specs=[qspec, pl.BlockSpec((1, tq, N_IDX), lambda b_, i: (b_, i, 0)), kspec, qspec, kspec, kspec],
        out_specs=pl.BlockSpec((1, 1, rows, HD), lambda b_, i: (b_, i, 0, 0)),
        out_shape=jax.ShapeDtypeStruct((b, nq, rows, HD), F32),
        scratch_shapes=[pltpu.VMEM((tq, lk), jnp.int32)],
        compiler_params=_cparams(("parallel", "arbitrary")),
    )(qi, wi, ki, qd, kd, vd)


def _gather_kernel(tbl_ref, pool_ref, new_ref, o_ref, *, n_pages):
    p = pl.program_id(1)

    @pl.when(p < n_pages)
    def _():
        o_ref[...] = pool_ref[...]

    @pl.when(p >= n_pages)
    def _():
        o_ref[...] = new_ref[...]


def _page_gather(pool, table, new):
    b, n_pages = table.shape
    width = pool.shape[2]
    n_new = new.shape[1] // PAGE
    kern = functools.partial(_gather_kernel, n_pages=n_pages)
    grid_spec = pltpu.PrefetchScalarGridSpec(
        num_scalar_prefetch=1,
        grid=(b, n_pages + n_new),
        in_specs=[
            pl.BlockSpec((1, PAGE, width), lambda b_, p, t: (t[b_, jnp.minimum(p, n_pages - 1)], 0, 0)),
            pl.BlockSpec((1, PAGE, width), lambda b_, p, t: (b_, jnp.maximum(p - n_pages, 0), 0)),
        ],
        out_specs=pl.BlockSpec((1, PAGE, width), lambda b_, p, t: (b_, p, 0)),
    )
    return pl.pallas_call(
        kern,
        grid_spec=grid_spec,
        out_shape=jax.ShapeDtypeStruct((b, (n_pages + n_new) * PAGE, width), pool.dtype),
        compiler_params=_cparams(("parallel", "arbitrary")),
    )(table, pool, new)


def _rope_tables(pos):
    half = HD // 2
    inv = ROPE_THETA ** (-jnp.arange(half, dtype=F32) / half)
    ang = pos.astype(F32)[:, None] * inv[None, :]
    cos, sin = jnp.cos(ang), jnp.sin(ang)
    cos128 = jnp.tile(jnp.concatenate([cos, cos], axis=1), (1, LANES // HD))
    sin128 = jnp.tile(jnp.concatenate([-sin, sin], axis=1), (1, LANES // HD))
    return cos128, sin128


def _tile_gain(g):
    return jnp.tile(g.reshape(1, HD), (1, LANES // HD))


def _heads(a, b, t, h, d, scale=None):
    a = a.reshape(b, t, h, d)
    if scale is not None:
        a = a * scale
    return a.transpose(0, 2, 1, 3).astype(BF16)


def _stack_q(a, tq):
    b, hk, r, t, d = a.shape
    return a.reshape(b, hk, r, t // tq, tq, d).transpose(0, 1, 3, 2, 4, 5).reshape(b, hk, t // tq, r * tq, d)


def _unstack_q(a, r, tq):
    b, hk, nq, _, d = a.shape
    a = a.reshape(b, hk, nq, r, tq, d).transpose(0, 2, 4, 1, 3, 5)
    return a.reshape(b * nq * tq, hk * r * d)


def _pad_rows(a, rows):
    return jnp.pad(a, ((0, 0), (0, rows - a.shape[1])) + ((0, 0),) * (a.ndim - 2))


def _even_mixer(x2, b, t, q_off, cs, past, prm, cfg):
    n = b * t
    program = [
        (0, SB_W, None, None, [(0, False)]),
        (3 * SB_W, DF_W, 0, None, [(SB_W, True)]),
        (SB_W, SB_W, None, None, [(SB_W + DF_W, False)]),
        (3 * SB_W + DF_W, DF_W, 1, None, [(2 * SB_W + DF_W, True)]),
        (2 * SB_W, SB_W, None, None, [(2 * (SB_W + DF_W), False)]),
        (3 * SB_W + 2 * DF_W, DF_W, None, None, [(3 * SB_W + 2 * DF_W, False)]),
    ]
    gains = jnp.concatenate([_tile_gain(prm["df_qk_gain"][0]), _tile_gain(prm["df_qk_gain"][1])], axis=0)
    proj = _mm([x2, cs[0], cs[1]], [prm["g0"], gains], prm["ev_w_in"], _lhs_norm, program, 3 * (SB_W + DF_W),
               tm=cfg["tm"], rope_idx=(1, 2), gains_idx=1)
    mw = SB_W + DF_W
    new_k = proj[:, mw:2 * mw].reshape(b, t, mw)
    new_v = proj[:, 2 * mw:3 * mw].reshape(b, t, mw)
    if past is None:
        kbuf, vbuf = new_k, new_v
    else:
        pool_k, pool_v, table = past
        kbuf = _page_gather(pool_k, table, _pad_rows(new_k, cfg["new_rows"]))
        vbuf = _page_gather(pool_v, table, _pad_rows(new_v, cfg["new_rows"]))
    lk = kbuf.shape[1]
    kb2, vb2 = kbuf.reshape(b * lk, mw), vbuf.reshape(b * lk, mw)
    k_sb = _heads(kb2[:, :SB_W], b, lk, N_SB, HD)
    v_sb = _heads(vb2[:, :SB_W], b, lk, N_SB, HD)
    k_df = _heads(kb2[:, SB_W:], b, lk, 2 * N_DF, HD)
    v_df = _heads(vb2[:, SB_W:], b, lk, N_DF, 2 * HD)
    q_sb = _heads(proj[:, :SB_W], b, t, N_SB, HD, SCALE)
    q_df = _heads(proj[:, SB_W:mw], b, t, 2 * N_DF, HD, SCALE)
    tq = cfg["tq"]
    o_sb = _sb(q_sb, k_sb, v_sb, tq=tq, tk=cfg["tk_sb"], q_off=q_off)
    o_sb = o_sb.transpose(0, 2, 1, 3).reshape(n, SB_W)
    o_df = _flash(_stack_q(q_df[:, :, None], tq), k_df, v_df, reps=1, tq=tq, tk=cfg["tk"], q_off=q_off, k_off=0,
                  mode="causal", v_map=lambda h: h // 2)
    o_df = o_df.reshape(b, N_DF, 2, t, 2 * HD)
    o1 = o_df[:, :, 0].transpose(0, 2, 1, 3).reshape(n, DF_W)
    o2 = o_df[:, :, 1].transpose(0, 2, 1, 3).reshape(n, DF_W)
    lam_init = 0.8 - 0.6 * math.exp(-0.3 * prm["layer"])
    out = _mm([o_sb, o1, o2, x2], [prm["df_lambda"], prm["df_subln_gain"].reshape(1, 2 * HD)], prm["ev_w_out"],
              functools.partial(_lhs_even, lam_init), [(0, x2.shape[1], None, None, [(0, False)])], x2.shape[1],
              tm=cfg["tm"], res_idx=3)
    return out, new_k, new_v


def _odd_layout():
    widths = (N_NSA * HD, 128, 128, 128, 128, 128, 128, N_NSA * 3, N_DSA * HD, HD, HD, N_IDX * HD, HD, N_IDX)
    offs = np.concatenate([[0], np.cumsum(widths)])
    (q_n, k_c, v_c, k_s, v_s, k_w, v_w, gate, q_d, k_d, v_d, q_i, k_i, w_i) = [
        (int(offs[j]), int(offs[j + 1])) for j in range(len(widths))]
    pieces = [q_n, (k_c[0], v_w[1]), q_d, q_i, k_d, k_i, v_d, w_i, ("pad", HD - N_IDX), gate,
              ("pad", LANES - N_NSA * 3)]
    program = [
        (0, 512, 0, None, [(0, False), (512, True)]),
        (512, 128, None, None, [(1024, False)]),
        (640, 128, None, None, [(1152, False)]),
        (768, 128, 1, None, [(1280, True)]),
        (896, 128, None, None, [(1408, False)]),
        (1024, 128, 2, None, [(1536, True)]),
        (1152, 128, None, None, [(1664, False)]),
        (1280, 512, 3, None, [(1792, True)]),
        (1792, 512, None, None, [(2304, True)]),
        (2304, 128, 4, None, [(2816, True)]),
        (2432, 128, None, None, [(2944, False)]),
        (2560, 128, None, "sigmoid", [(3072, False)]),
    ]
    return pieces, program, 3200


def _permute_cols(w, pieces):
    cols = []
    for p in pieces:
        if p[0] == "pad":
            cols.append(jnp.zeros((w.shape[0], p[1]), w.dtype))
        else:
            cols.append(w[:, p[0]:p[1]])
    return jnp.concatenate(cols, axis=1)


def _odd_mixer(x2, b, t, t_real, q_off, cs, past, prm, cfg):
    n = b * t
    pieces, program, out_cols = _odd_layout()
    ng, dg = prm["nsa_qk_gain"], prm["dsa_qk_gain"]
    gains = jnp.concatenate([
        _tile_gain(ng[0]), _tile_gain(ng[2]), _tile_gain(ng[3]), _tile_gain(dg[0]),
        jnp.concatenate([dg[1], dg[2]]).reshape(1, LANES)], axis=0)
    w_in = _permute_cols(prm["od_w_in"], pieces)
    proj = _mm([x2, cs[0], cs[1]], [prm["g0"], gains], w_in, _lhs_norm, program, out_cols,
               tm=cfg["tm"], rope_idx=(1, 2), gains_idx=1)
    g = N_NSA_KV
    new_nsa = proj[:, 1024:1536].reshape(b, t, 4 * g * HD)
    new_win = proj[:, 1536:1792].reshape(b, t, 2, g, HD)
    new_dsa = jnp.concatenate([proj[:, 2816:2880], proj[:, 2944:3008], proj[:, 2880:2944]], axis=1).reshape(b, t, 3 * HD)
    w_i = proj[:, 3008:3008 + N_IDX].reshape(b, t, N_IDX)
    gate = proj[:, 3072:3072 + N_NSA * 3].reshape(n, N_NSA, 3)
    if past is None:
        nsa_buf, dsa_buf = new_nsa, new_dsa
        win_buf = new_win
        win_off = 0
        new_state = new_win[:, -min(WINDOW, t):]
        l_real = t
    else:
        pool_nsa, pool_dsa, state, table = past
        nsa_buf = _page_gather(pool_nsa, table, _pad_rows(new_nsa, cfg["new_rows"]))
        dsa_buf = _page_gather(pool_dsa, table, _pad_rows(new_dsa, cfg["new_rows"]))
        wb = state.shape[1]
        win_buf = _pad_rows(jnp.concatenate([state, new_win], axis=1), wb + cfg["tk_win"])
        win_off = q_off - wb
        new_state = jnp.concatenate([state, new_win[:, :t_real]], axis=1)[:, -wb:]
        l_real = table.shape[1] * PAGE + t_real
    lk = nsa_buf.shape[1]
    nsa5 = nsa_buf.reshape(b, lk, 4, g, HD)
    tq = cfg["tq"]

    n_cmp = lk // CMP_STRIDE
    t_c = nsa5[:, :, 0:2].transpose(0, 2, 3, 1, 4).reshape(b, 2, g, n_cmp, CMP_STRIDE * HD)
    pe_flat = prm["cmp_pe"].reshape(2, 1, CMP_LEN * HD)
    cmp = _compress(t_c, prm["cmp_w1"], pe_flat, prm["cmp_w2"], ng[1].reshape(1, HD))
    nb = n_cmp // CMP_PER_SLC
    nbp = -(-nb // LANES) * LANES
    cmp = cmp.reshape(b, 2, g, nb, CMP_PER_SLC, HD).transpose(0, 1, 2, 4, 3, 5)
    cmp = jnp.pad(cmp, ((0, 0),) * 4 + ((0, nbp - nb), (0, 0))).reshape(b, 2, g, CMP_PER_SLC * nbp, HD).astype(BF16)

    def group_q(cols):
        a = _heads(cols, b, t, N_NSA, HD, SCALE).reshape(b, g, NSA_GROUP, t, HD)
        return _stack_q(a, tq)

    q_n = group_q(proj[:, 0:512])
    q_r = group_q(proj[:, 512:1024])
    n_blk = -(-l_real // SLC_BLOCK)
    o_c, sel = _nsa_cmp(q_n, cmp[:, 0], cmp[:, 1], tq=tq, q_off=q_off, n_sel=min(N_SLC, n_blk))

    def kv_heads(a):
        return a.transpose(0, 2, 1, 3).astype(BF16)

    o_s = _flash(q_r, kv_heads(nsa5[:, :, 2]), kv_heads(nsa5[:, :, 3]), reps=NSA_GROUP, tq=tq, tk=cfg["tk"],
                 q_off=q_off, k_off=0, mode="causal", bm=sel)
    o_w = _flash(q_r, kv_heads(win_buf[:, :, 0]), kv_heads(win_buf[:, :, 1]), reps=NSA_GROUP, tq=tq,
                 tk=cfg["tk_win"], q_off=q_off, k_off=win_off, mode="window")

    dsa4 = dsa_buf.reshape(b, lk, 3, HD).astype(BF16)
    q_d = _stack_q(_heads(proj[:, 1792:2304], b, t, N_DSA, HD, SCALE)[:, None], tq)[:, 0]
    q_i = _stack_q(_heads(proj[:, 2304:2816], b, t, N_IDX, HD, IDX_SCALE)[:, None], tq)[:, 0]
    o_d = _dsa(q_i, w_i, dsa4[:, :, 2], q_d, dsa4[:, :, 0], dsa4[:, :, 1], tq=tq, tk=cfg["tk"], q_off=q_off,
               n_top=min(DSA_TOPK_MAX, l_real // 4))

    o_c, o_s, o_w = (_unstack_q(o, NSA_GROUP, tq) for o in (o_c, o_s, o_w))
    o_d = _unstack_q(o_d[:, None], N_DSA, tq)
    gfull = [jnp.repeat(gate[:, :, j], HD, axis=1) for j in range(3)]
    d = x2.shape[1]
    out = _mm([o_c, o_s, o_w, o_d] + gfull + [x2], [], prm["od_w_out"], _lhs_odd,
              [(0, d, None, None, [(0, False)])], d, tm=cfg["tm"], res_idx=7)
    return out, new_nsa, new_dsa, new_state


def _cross(x2, b, t, mem_k, mem_v, prm, cfg):
    d = x2.shape[1]
    xw = N_XH * HD
    q = _mm([x2], [prm["g1"], _tile_gain(prm["x_gq"])], prm["x_wq"], _lhs_norm,
            [(0, xw, 0, None, [(0, False)])], xw, tm=cfg["tm"], gains_idx=1)
    tq = cfg["tq"]
    qh = _stack_q(_heads(q, b, t, N_XH, HD, SCALE)[:, :, None], tq)
    o = _flash(qh, mem_k, mem_v, reps=1, tq=tq, tk=mem_k.shape[2], q_off=0, k_off=0, mode="full")
    o = _unstack_q(o, 1, tq)
    return _mm([o, x2], [], prm["x_wo"], _lhs_plain, [(0, d, None, None, [(0, False)])], d, tm=cfg["tm"], res_idx=1)


def _memory_kv(mem2, prm):
    xw = N_XH * HD
    w = jnp.concatenate([prm["x_wk"], prm["x_wv"]], axis=1)
    return _mm([mem2], [_tile_gain(prm["x_gk"])], w, _lhs_plain,
               [(0, xw, 0, None, [(0, False)]), (xw, xw, None, None, [(xw, False)])], 2 * xw,
               tm=min(256, mem2.shape[0]), gains_idx=0)


def _run_group(x, q_off, t_real, mem_kvs, pasts, layers, cfg):
    b, t, d = x.shape
    x2 = x.reshape(b * t, d)
    pos = q_off + jnp.arange(t, dtype=jnp.int32)
    cos, sin = _rope_tables(pos)
    cs = (jnp.tile(cos, (b, 1)), jnp.tile(sin, (b, 1)))
    outs = {}
    for li, prm in enumerate(layers):
        if li % 2 == 0:
            x2, nk, nv = _even_mixer(x2, b, t, q_off, cs, pasts[li], prm, cfg)
            outs["ek"], outs["ev"] = nk, nv
        else:
            x2, nn, nd, nw = _odd_mixer(x2, b, t, t_real, q_off, cs, pasts[li], prm, cfg)
            outs["on"], outs["od"], outs["ow"] = nn, nd, nw
        x2 = _cross(x2, b, t, mem_kvs[li][0], mem_kvs[li][1], prm, cfg)
        if li % 2 == 0:
            x2 = _ffn(x2, prm["g2"], prm["router"], prm["w1"], prm["w3"], prm["w2"], tm=cfg["tm_ffn"],
                      tf=cfg["tf"], routed=False)
        else:
            x2 = _ffn(x2, prm["g2"], prm["router"], prm["w1"], prm["w3"], prm["w2"], tm=cfg["tm_ffn"],
                      tf=cfg["tf"], routed=True)
    return x2.reshape(b, t, d), outs


def kernel(x_prompt, x_sample, mem_prompt, cache_even_k, cache_even_v, cache_odd_nsa, cache_odd_dsa, state_odd_win, cache_mem, page_table, norm_gain, ev_w_in, ev_w_out, df_qk_gain, df_lambda, df_subln_gain, ffn_w1, ffn_w3, ffn_w2, od_w_in, od_w_out, nsa_qk_gain, cmp_pe, cmp_w1, cmp_w2, dsa_qk_gain, moe_router, moe_w1, moe_w3, moe_w2, x_wq, x_wk, x_wv, x_wo, x_qk_gain):
    depth = norm_gain.shape[0]
    bp, tp, d = x_prompt.shape
    bs, ts, _ = x_sample.shape
    n_mem = mem_prompt.shape[1]
    xw = N_XH * HD

    layers = []
    for l in range(depth):
        i = l // 2
        prm = {
            "layer": l,
            "g0": norm_gain[l, 0].reshape(1, d), "g1": norm_gain[l, 1].reshape(1, d), "g2": norm_gain[l, 2].reshape(1, d),
            "x_wq": x_wq[l].astype(BF16), "x_wk": x_wk[l].astype(BF16), "x_wv": x_wv[l].astype(BF16),
            "x_wo": x_wo[l].astype(BF16), "x_gq": x_qk_gain[l, 0], "x_gk": x_qk_gain[l, 1],
        }
        if l % 2 == 0:
            prm.update({
                "ev_w_in": ev_w_in[i].astype(BF16), "ev_w_out": ev_w_out[i].astype(BF16),
                "df_qk_gain": df_qk_gain[i], "df_lambda": df_lambda[i], "df_subln_gain": df_subln_gain[i],
                "router": jnp.zeros((SUBLANES, LANES), F32),
                "w1": ffn_w1[i][None].astype(BF16), "w3": ffn_w3[i][None].astype(BF16), "w2": ffn_w2[i][None].astype(BF16),
            })
        else:
            prm.update({
                "od_w_in": od_w_in[i].astype(BF16), "od_w_out": od_w_out[i].astype(BF16),
                "nsa_qk_gain": nsa_qk_gain[i], "dsa_qk_gain": dsa_qk_gain[i],
                "cmp_pe": cmp_pe[i], "cmp_w1": cmp_w1[i].astype(BF16), "cmp_w2": cmp_w2[i].astype(BF16),
                "router": jnp.pad(moe_router[i], ((0, 0), (0, LANES - N_EXPERTS))),
                "w1": moe_w1[i].astype(BF16), "w3": moe_w3[i].astype(BF16), "w2": moe_w2[i].astype(BF16),
            })
        layers.append(prm)

    def mem_heads(kv, b):
        k = kv[:, :, 0].transpose(0, 2, 1, 3).astype(BF16)
        v = kv[:, :, 1].transpose(0, 2, 1, 3).astype(BF16)
        return k, v

    mem2 = mem_prompt.reshape(bp * n_mem, d)
    mem_p = [_memory_kv(mem2, layers[l]).reshape(bp, n_mem, 2, N_XH, HD) for l in range(depth)]
    ff = ffn_w1.shape[2]
    tf = ff // 2 if (ff // 2) % LANES == 0 else ff
    cfg_p = {"tm": 256, "tq": 128, "tk": 256, "tk_sb": 128, "tk_win": 128, "tm_ffn": 512, "tf": tf}
    cfg_p["tq"] = min(cfg_p["tq"], tp)
    y_prompt, op = _run_group(x_prompt, 0, tp, [mem_heads(m, bp) for m in mem_p], [None] * depth, layers, cfg_p)
    p_mem = jnp.stack(mem_p)

    n_past = page_table.shape[1] * cache_even_k.shape[2]
    ts_pad = -(-ts // SUBLANES) * SUBLANES
    xs = _pad_rows(x_sample, ts_pad)
    pasts = []
    for l in range(depth):
        i = l // 2
        if l % 2 == 0:
            pasts.append((cache_even_k[i], cache_even_v[i], page_table))
        else:
            pn = cache_odd_nsa[i]
            pd = cache_odd_dsa[i]
            pasts.append((pn.reshape(pn.shape[0], pn.shape[1], -1), pd.reshape(pd.shape[0], pd.shape[1], -1),
                          state_odd_win[i], page_table))
    cfg_s = {"tm": bs * ts_pad, "tq": ts_pad, "tk": 512, "tk_sb": 256, "tk_win": 128, "tm_ffn": bs * ts_pad,
             "tf": tf, "new_rows": 512}
    y_s, os_ = _run_group(xs, n_past, ts, [mem_heads(cache_mem[l], bs) for l in range(depth)], pasts, layers, cfg_s)
    y_sample = y_s[:, :ts]

    g = N_NSA_KV
    return (
        y_prompt, y_sample,
        op["ek"][None], op["ev"][None],
        op["on"].reshape(1, bp, tp, 4, g, HD), op["od"].reshape(1, bp, tp, 3, HD),
        op["ow"][None], p_mem,
        os_["ek"][:, :ts][None], os_["ev"][:, :ts][None],
        os_["on"][:, :ts].reshape(1, bs, ts, 4, g, HD), os_["od"][:, :ts].reshape(1, bs, ts, 3, HD),
        os_["ow"][None],
    )
```

```python
import functools
import math

import jax
import jax.numpy as jnp
import numpy as np
from jax import lax
from jax.experimental import pallas as pl
from jax.experimental.pallas import tpu as pltpu

F32 = jnp.float32
BF16 = jnp.bfloat16

HD = 64
N_SB = 8
N_DF = 4
N_NSA = 8
N_NSA_KV = 2
NSA_GROUP = N_NSA // N_NSA_KV
N_DSA = 8
N_IDX = 8
N_XH = 4
N_EXPERTS = 8
ROPE_THETA = 10000.0
CMP_LEN = 32
CMP_STRIDE = 16
SLC_BLOCK = 64
CMP_PER_SLC = SLC_BLOCK // CMP_STRIDE
N_SLC = 16
N_LOCAL = 2
WINDOW = 512
DSA_TOPK_MAX = 256
EPS = 1e-6
NEG = -1e30
FORCE = 1e9
SCALE = HD ** -0.5
IDX_SCALE = HD ** -0.5
SB_W = N_SB * HD
HD_SHIFT = 6
SLC_SHIFT = 6
DF_W = N_DF * 2 * HD

LANES = 128
SUBLANES = 8
PAGE = 128
VMEM_LIMIT = 52 * 1024 * 1024
INT_MIN = -2 ** 31

_NT = (((1,), (1,)), ((), ()))


def _cparams(sem):
    return pltpu.CompilerParams(dimension_semantics=sem, vmem_limit_bytes=VMEM_LIMIT)


def _dot(a, b):
    return jnp.dot(a, b, preferred_element_type=F32)


def _dot_nt(a, b):
    return lax.dot_general(a, b, _NT, preferred_element_type=F32)


def _split_dot(x, m_bf16):
    hi = x.astype(BF16)
    lo = (x - hi.astype(F32)).astype(BF16)
    return _dot(hi, m_bf16) + _dot(lo, m_bf16)


def _iota(shape, dim):
    return lax.broadcasted_iota(jnp.int32, shape, dim)


def _rms_rows(x, g):
    return x * lax.rsqrt(jnp.mean(x * x, axis=-1, keepdims=True) + EPS) * g


def _group_mean_matrix():
    r = _iota((LANES, LANES), 0) >> HD_SHIFT
    c = _iota((LANES, LANES), 1) >> HD_SHIFT
    return jnp.where(r == c, 1.0 / HD, 0.0).astype(BF16)


def _head_norm(y, g, gm):
    ms = _split_dot(y * y, gm)
    return y * lax.rsqrt(ms + EPS) * g


def _rope_slab(y, cos, sin):
    lane = _iota(y.shape, 1)
    first = (lane & (HD - 1)) < (HD // 2)
    swapped = jnp.where(first, pltpu.roll(y, LANES - HD // 2, 1), pltpu.roll(y, HD // 2, 1))
    return y * cos + swapped * sin


def _mm_kernel(*refs, n_rows, n_consts, lhs_fn, program, rope_idx, gains_idx, res_idx):
    rows = refs[:n_rows]
    consts = refs[n_rows:n_rows + n_consts]
    w_ref = refs[n_rows + n_consts]
    o_ref = refs[-1]
    lhs = lhs_fn(rows, consts).astype(BF16)
    gm = _group_mean_matrix() if gains_idx is not None else None
    for (src, width, gain_row, act, outs) in program:
        y_full = _dot(lhs, w_ref[:, src:src + width])
        for s in range(width // LANES):
            y = y_full[:, s * LANES:(s + 1) * LANES]
            if gain_row is not None:
                y = _head_norm(y, consts[gains_idx][gain_row:gain_row + 1, :], gm)
            if act == "sigmoid":
                y = 1.0 / (1.0 + jnp.exp(-y))
            for (dst, rope) in outs:
                z = y
                if rope:
                    z = _rope_slab(y, rows[rope_idx[0]][...], rows[rope_idx[1]][...])
                d0 = dst + s * LANES
                if res_idx is not None:
                    z = z + rows[res_idx][:, d0:d0 + LANES]
                o_ref[:, d0:d0 + LANES] = z


def _mm(rows, consts, w, lhs_fn, program, out_cols, *, tm, rope_idx=None, gains_idx=None, res_idx=None):
    n = rows[0].shape[0]
    assert n % tm == 0
    in_specs = [pl.BlockSpec((tm, r.shape[1]), lambda i: (i, 0)) for r in rows]
    in_specs += [pl.BlockSpec(c.shape, lambda i: (0, 0)) for c in consts]
    in_specs += [pl.BlockSpec(w.shape, lambda i: (0, 0))]
    kern = functools.partial(_mm_kernel, n_rows=len(rows), n_consts=len(consts), lhs_fn=lhs_fn,
                             program=program, rope_idx=rope_idx, gains_idx=gains_idx, res_idx=res_idx)
    return pl.pallas_call(
        kern,
        grid=(n // tm,),
        in_specs=in_specs,
        out_specs=pl.BlockSpec((tm, out_cols), lambda i: (i, 0)),
        out_shape=jax.ShapeDtypeStruct((n, out_cols), F32),
        compiler_params=_cparams(("parallel",)),
    )(*rows, *consts, w)


def _lhs_norm(rows, consts):
    return _rms_rows(rows[0][...], consts[0][...])


def _lhs_plain(rows, consts):
    return rows[0][...]


def _lhs_cat2(rows, consts):
    return jnp.concatenate([rows[0][...], rows[1][...]], axis=1)


def _lhs_odd(rows, consts):
    oc, os_, ow, od = rows[0][...], rows[1][...], rows[2][...], rows[3][...]
    g0, g1, g2 = rows[4][...], rows[5][...], rows[6][...]
    return jnp.concatenate([g0 * oc + g1 * os_ + g2 * ow, od], axis=1)


def _ffn_kernel(x_ref, g_ref, r_ref, w1_ref, w3_ref, w2_ref, o_ref, h_ref, acc_ref, gate_ref, *, routed):
    e = pl.program_id(1)
    f = pl.program_id(2)
    first = jnp.logical_and(e == 0, f == 0)
    last = jnp.logical_and(e == pl.num_programs(1) - 1, f == pl.num_programs(2) - 1)

    @pl.when(first)
    def _():
        x = x_ref[...]
        h = _rms_rows(x, g_ref[...])
        h_ref[...] = h.astype(BF16)
        acc_ref[...] = x
        if routed:
            logits = jnp.dot(h, r_ref[...], preferred_element_type=F32, precision=lax.Precision.HIGHEST)
            col = _iota(logits.shape, 1).astype(F32)
            logits = jnp.where(col < N_EXPERTS, logits, -jnp.inf)
            m1 = jnp.max(logits, axis=1, keepdims=True)
            i1 = jnp.min(jnp.where(logits == m1, col, 1e9), axis=1, keepdims=True)
            rest = jnp.where(col == i1, -jnp.inf, logits)
            m2 = jnp.max(rest, axis=1, keepdims=True)
            i2 = jnp.min(jnp.where(rest == m2, col, 1e9), axis=1, keepdims=True)
            e2 = jnp.exp(m2 - m1)
            g1 = 1.0 / (1.0 + e2)
            g2 = e2 / (1.0 + e2)
            gate_ref[...] = jnp.where(col == i1, g1, 0.0) + jnp.where(col == i2, g2, 0.0)

    def compute(gcol):
        h = h_ref[...]
        u = _dot(h, w1_ref[0])
        v = _dot(h, w3_ref[0])
        a = (u * (1.0 / (1.0 + jnp.exp(-u)))) * v
        y = _dot(a.astype(BF16), w2_ref[0])
        if gcol is not None:
            y = gcol * y
        acc_ref[...] += y

    if routed:
        col = _iota(gate_ref.shape, 1)
        gcol = jnp.sum(jnp.where(col == e, gate_ref[...], 0.0), axis=1, keepdims=True)
        active = jnp.max(gcol) > 0.0

        @pl.when(active)
        def _():
            compute(gcol)
    else:
        compute(None)

    @pl.when(last)
    def _():
        o_ref[...] = acc_ref[...]


def _ffn(x, g, router, w1, w3, w2, *, tm, tf, routed):
    n, d = x.shape
    ne, _, ff = w1.shape
    assert n % tm == 0 and ff % tf == 0
    kern = functools.partial(_ffn_kernel, routed=routed)
    return pl.pallas_call(
        kern,
        grid=(n // tm, ne, ff // tf),
        in_specs=[
            pl.BlockSpec((tm, d), lambda i, e, f: (i, 0)),
            pl.BlockSpec((1, d), lambda i, e, f: (0, 0)),
            pl.BlockSpec(router.shape, lambda i, e, f: (0, 0)),
            pl.BlockSpec((1, d, tf), lambda i, e, f: (e, 0, f)),
            pl.BlockSpec((1, d, tf), lambda i, e, f: (e, 0, f)),
            pl.BlockSpec((1, tf, d), lambda i, e, f: (e, f, 0)),
        ],
        out_specs=pl.BlockSpec((tm, d), lambda i, e, f: (i, 0)),
        out_shape=jax.ShapeDtypeStruct((n, d), F32),
        scratch_shapes=[pltpu.VMEM((tm, d), BF16), pltpu.VMEM((tm, d), F32), pltpu.VMEM((tm, LANES), F32)],
        compiler_params=_cparams(("parallel", "arbitrary", "arbitrary")),
    )(x, g, router, w1, w3, w2)


def _tile_rows(m, reps):
    return m if reps == 1 else jnp.concatenate([m] * reps, axis=0)


def _softmax_step(s, maskf, v, m, l, acc):
    keep = maskf > 0.5
    s = jnp.where(keep, s, NEG)
    m_new = jnp.maximum(m, jnp.max(s, axis=1, keepdims=True))
    p = jnp.where(keep, jnp.exp(s - m_new), 0.0)
    alpha = jnp.exp(m - m_new)
    l = alpha * l + jnp.sum(p, axis=1, keepdims=True)
    acc = alpha * acc + _dot(p.astype(BF16), v)
    return m_new, l, acc


def _flash_kernel(*refs, reps, tq, tk, q_off, k_off, mode, has_bm):
    if has_bm:
        q_ref, k_ref, v_ref, bm_ref, o_ref = refs
    else:
        q_ref, k_ref, v_ref, o_ref = refs
    i = pl.program_id(2)
    q = q_ref[0, 0, 0]
    rows = q.shape[0]
    dv = v_ref.shape[-1]
    n_kt = k_ref.shape[2] // tk
    q_lo = q_off + i * tq
    if mode == "full":
        lo, hi = 0, n_kt
    else:
        hi = jnp.minimum(lax.div(q_lo + tq - 1 - k_off, tk) + 1, n_kt)
        lo = 0
        if mode == "window":
            lo = lax.div(jnp.maximum(q_lo - (WINDOW - 1) - k_off, 0), tk)
    qpos = q_lo + _iota((tq, tk), 0)
    if has_bm:
        bm = bm_ref[0, 0].astype(BF16)
        nbp = bm.shape[1]

    def body(j, carry):
        m, l, acc = carry
        k0 = pl.multiple_of(j * tk, tk)
        k = k_ref[0, 0, pl.ds(k0, tk), :]
        v = v_ref[0, 0, pl.ds(k0, tk), :]
        s = _dot_nt(q, k)
        kidx = j * tk + _iota((tq, tk), 1)
        kpos = k_off + kidx
        if mode == "full":
            maskf = jnp.ones((tq, tk), F32)
        else:
            ok = kpos <= qpos
            if mode == "window":
                ok = jnp.logical_and(ok, qpos - kpos < WINDOW)
                ok = jnp.logical_and(ok, kpos >= 0)
            maskf = jnp.where(ok, 1.0, 0.0)
        if has_bm:
            blk = _iota((nbp, tk), 0)
            tok = (j * tk + _iota((nbp, tk), 1)) >> SLC_SHIFT
            expand = jnp.where(blk == tok, 1.0, 0.0).astype(BF16)
            maskf = maskf * _dot(bm, expand)
        return _softmax_step(s, _tile_rows(maskf, reps), v, m, l, acc)

    m0 = jnp.full((rows, 1), NEG, F32)
    l0 = jnp.zeros((rows, 1), F32)
    a0 = jnp.zeros((rows, dv), F32)
    m, l, acc = lax.fori_loop(lo, hi, body, (m0, l0, a0))
    o_ref[0, 0, 0] = acc / jnp.maximum(l, 1e-30)


def _flash(q, k, v, *, reps, tq, tk, q_off, k_off, mode, bm=None, v_map=None):
    b, hk, nq, rows, _ = q.shape
    lk = k.shape[2]
    dv = v.shape[3]
    assert rows == reps * tq and lk % tk == 0
    if v_map is None:
        v_map = lambda h: h
    in_specs = [
        pl.BlockSpec((1, 1, 1, rows, HD), lambda b_, h, i: (b_, h, i, 0, 0)),
        pl.BlockSpec((1, 1, lk, HD), lambda b_, h, i: (b_, h, 0, 0)),
        pl.BlockSpec((1, 1, lk, dv), lambda b_, h, i: (b_, v_map(h), 0, 0)),
    ]
    args = [q, k, v]
    if bm is not None:
        in_specs.append(pl.BlockSpec((1, 1, tq, bm.shape[3]), lambda b_, h, i: (b_, h, i, 0)))
        args.append(bm)
    kern = functools.partial(_flash_kernel, reps=reps, tq=tq, tk=tk, q_off=q_off, k_off=k_off, mode=mode,
                             has_bm=bm is not None)
    return pl.pallas_call(
        kern,
        grid=(b, hk, nq),
        in_specs=in_specs,
        out_specs=pl.BlockSpec((1, 1, 1, rows, dv), lambda b_, h, i: (b_, h, i, 0, 0)),
        out_shape=jax.ShapeDtypeStruct((b, hk, nq, rows, dv), F32),
        compiler_params=_cparams(("parallel", "parallel", "arbitrary")),
    )(*args)


def _softplus(z):
    return jnp.maximum(z, 0.0) + jnp.log(1.0 + jnp.exp(-jnp.abs(z)))


def _later_matrix(n):
    return jnp.where(_iota((n, n), 0) > _iota((n, n), 1), 1.0, 0.0).astype(BF16)


def _sb_kernel(q_ref, k_ref, v_ref, o_ref, *, t):
    i = pl.program_id(2)
    lane = _iota((t, LANES), 1)
    qf = q_ref[0] * SCALE
    qs = (jnp.where(lane < HD, qf, 0.0).astype(BF16), jnp.where(lane >= HD, qf, 0.0).astype(BF16))
    later = _later_matrix(t)

    def tile(j, carries, o, masked):
        k0 = pl.multiple_of(j * t, t)
        k = k_ref[0, pl.ds(k0, t), :].astype(BF16)
        v = v_ref[0, pl.ds(k0, t), :].astype(BF16)
        outs, new_carries = [], []
        for q, carry in zip(qs, carries):
            z = _dot_nt(q, k)
            sp = _softplus(z)
            log_sig = z - sp
            if masked:
                vis = _iota((t, t), 1) < _iota((t, t), 0)
                sp = jnp.where(vis, sp, 0.0)
            a = jnp.exp(log_sig - _dot(sp.astype(BF16), later) - carry)
            if masked:
                a = jnp.where(vis, a, 0.0)
            outs.append(_dot(a.astype(BF16), v))
            new_carries.append(carry + jnp.sum(sp, axis=1, keepdims=True))
        return tuple(new_carries), o + jnp.where(lane < HD, outs[0], outs[1])

    zero = jnp.zeros((t, 1), F32)
    carries, o = tile(i, (zero, zero), jnp.zeros((t, LANES), F32), True)

    def body(s, co):
        return tile(i - 1 - s, co[0], co[1], False)

    carries, o = lax.fori_loop(0, i, body, (carries, o))
    o_ref[0] = o


def _sb(proj3, *, t, q_slab, k_slab, v_slab, n_slabs):
    b, tt, _ = proj3.shape
    assert tt % t == 0
    kern = functools.partial(_sb_kernel, t=t)
    return pl.pallas_call(
        kern,
        grid=(b, n_slabs, tt // t),
        in_specs=[
            pl.BlockSpec((1, t, LANES), lambda b_, p, i: (b_, i, q_slab + p)),
            pl.BlockSpec((1, tt, LANES), lambda b_, p, i: (b_, 0, k_slab + p)),
            pl.BlockSpec((1, tt, LANES), lambda b_, p, i: (b_, 0, v_slab + p)),
        ],
        out_specs=pl.BlockSpec((1, t, LANES), lambda b_, p, i: (b_, i, p)),
        out_shape=jax.ShapeDtypeStruct((b, tt, n_slabs * LANES), F32),
        compiler_params=_cparams(("parallel", "parallel", "arbitrary")),
        name="sb_prompt",
    )(proj3, proj3, proj3)


def _lam(lv, lam_init):
    a = jnp.sum(jnp.sum(lv[0:1] * lv[1:2], axis=1, keepdims=True), axis=0, keepdims=True)
    b = jnp.sum(jnp.sum(lv[2:3] * lv[3:4], axis=1, keepdims=True), axis=0, keepdims=True)
    return jnp.exp(a) - jnp.exp(b) + lam_init


def _df_kernel(lv_ref, g_ref, q_ref, k_ref, v_ref, o_ref, *, t, lam_init):
    i = pl.program_id(2)
    lane = _iota((t, LANES), 1)
    qf = q_ref[0] * SCALE
    qs = (jnp.where(lane < HD, qf, 0.0).astype(BF16), jnp.where(lane >= HD, qf, 0.0).astype(BF16))

    def tile(j, state, masked):
        k0 = pl.multiple_of(j * t, t)
        k = k_ref[0, pl.ds(k0, t), :].astype(BF16)
        v = v_ref[0, pl.ds(k0, t), :].astype(BF16)
        new = []
        for q, (m, l, acc) in zip(qs, state):
            s = _dot_nt(q, k)
            if masked:
                s = jnp.where(_iota((t, t), 1) <= _iota((t, t), 0), s, NEG)
            m_new = jnp.maximum(m, jnp.max(s, axis=1, keepdims=True))
            p = jnp.exp(s - m_new)
            alpha = jnp.exp(m - m_new)
            new.append((m_new, alpha * l + jnp.sum(p, axis=1, keepdims=True), alpha * acc + _dot(p.astype(BF16), v)))
        return tuple(new)

    init = (jnp.full((t, 1), NEG, F32), jnp.zeros((t, 1), F32), jnp.zeros((t, LANES), F32))
    state = lax.fori_loop(0, i, lambda j, st: tile(j, st, False), (init, init))
    (_, l0, a0), (_, l1, a1) = tile(i, state, True)
    d = a0 / l0 - _lam(lv_ref[...], lam_init) * (a1 / l1)
    o_ref[0] = _rms_rows(d, g_ref[...]) * (1.0 - lam_init)


def _df(proj3, lv, subln, *, t, q_slab, k_slab, v_slab, n_slabs, lam_init):
    b, tt, _ = proj3.shape
    assert tt % t == 0
    kern = functools.partial(_df_kernel, t=t, lam_init=lam_init)
    return pl.pallas_call(
        kern,
        grid=(b, n_slabs, tt // t),
        in_specs=[
            pl.BlockSpec(lv.shape, lambda b_, p, i: (0, 0)),
            pl.BlockSpec(subln.shape, lambda b_, p, i: (0, 0)),
            pl.BlockSpec((1, t, LANES), lambda b_, p, i: (b_, i, q_slab + p)),
            pl.BlockSpec((1, tt, LANES), lambda b_, p, i: (b_, 0, k_slab + p)),
            pl.BlockSpec((1, tt, LANES), lambda b_, p, i: (b_, 0, v_slab + p)),
        ],
        out_specs=pl.BlockSpec((1, t, LANES), lambda b_, p, i: (b_, i, p)),
        out_shape=jax.ShapeDtypeStruct((b, tt, n_slabs * LANES), F32),
        compiler_params=_cparams(("parallel", "parallel", "arbitrary")),
        name="df_prompt",
    )(lv, subln, proj3, proj3, proj3)


def _dec_even_kernel(*refs, pp, ts, lam_init):
    tbl_ref, qt_ref = refs[0], refs[1]
    k_refs = refs[2:2 + pp]
    v_refs = refs[2 + pp:2 + 2 * pp]
    kn_ref, vn_ref, lv_ref, g_ref, o_ref, later_ref, st_ref, asb_ref, adf_ref = refs[2 + 2 * pp:]
    del tbl_ref
    s_id = pl.program_id(1)
    nk = pp * PAGE
    half = LANES // 2
    qt = qt_ref[0]

    def col_of(row):
        return jnp.transpose(jnp.broadcast_to(row, (SUBLANES, LANES)))[half:, 0:1]

    def tile(kt, vt, later, vis_sb, vis_df):
        carry, m, l = st_ref[0:1, :], st_ref[1:2, :], st_ref[2:3, :]
        zt = _dot(kt, qt)
        lane = _iota(zt.shape, 1)
        sp = _softplus(zt)
        log_sig = zt - sp
        s = zt
        if vis_sb is not None:
            sp = jnp.where(vis_sb, sp, 0.0)
            s = jnp.where(vis_df, s, NEG)
        a = jnp.exp(log_sig - _dot(later, sp.astype(BF16)) - carry)
        if vis_sb is not None:
            a = jnp.where(vis_sb, a, 0.0)
        m_new = jnp.maximum(m, jnp.max(s, axis=0, keepdims=True))
        p = jnp.exp(s - m_new)
        alpha = jnp.exp(m - m_new)
        st_ref[0:1, :] = carry + jnp.sum(sp, axis=0, keepdims=True)
        st_ref[1:2, :] = m_new
        st_ref[2:3, :] = alpha * l + jnp.sum(p, axis=0, keepdims=True)
        w = jnp.transpose(jnp.where(lane < half, a, p)).astype(BF16)
        asb_ref[...] += _dot(w[:half], vt[:, :SB_W])
        adf_ref[...] = col_of(alpha) * adf_ref[...] + _dot(w[half:], vt[:, SB_W:])

    @pl.when(s_id == 0)
    def _():
        later_ref[...] = jnp.where(_iota((nk, nk), 1) > _iota((nk, nk), 0), 1.0, 0.0).astype(BF16)
        st_ref[...] = jnp.where(_iota(st_ref.shape, 0) == 1, NEG, 0.0)
        asb_ref[...] = jnp.zeros_like(asb_ref)
        adf_ref[...] = jnp.zeros_like(adf_ref)
        key = _iota((PAGE, LANES), 0)
        tok = _iota((PAGE, LANES), 1) & (ts - 1)
        tile(kn_ref[0].astype(BF16), vn_ref[0].astype(BF16), later_ref[0:PAGE, 0:PAGE], key < tok, key <= tok)

    kt = jnp.concatenate([r[0].astype(BF16) for r in k_refs], axis=0)
    vt = jnp.concatenate([r[0].astype(BF16) for r in v_refs], axis=0)
    tile(kt, vt, later_ref[...], None, None)

    @pl.when(s_id == pl.num_programs(1) - 1)
    def _():
        row = _iota((half, SB_W), 0)
        lane = _iota((half, SB_W), 1)

        def fold(x):
            out = x[0:ts]
            for u in range(1, half // ts):
                out = out + x[u * ts:(u + 1) * ts]
            return out

        o_sb = fold(jnp.where((row >> 3) == (lane >> HD_SHIFT), asb_ref[...], 0.0))
        pn = adf_ref[...] / col_of(st_ref[2:3, :])
        same_head = (row >> 4) == (lane >> 7)
        o0 = fold(jnp.where(jnp.logical_and(same_head, ((row >> 3) & 1) == 0), pn, 0.0))
        o1 = fold(jnp.where(jnp.logical_and(same_head, ((row >> 3) & 1) == 1), pn, 0.0))
        d = o0 - _lam(lv_ref[...], lam_init) * o1
        parts = [o_sb]
        for h in range(N_DF):
            parts.append(_rms_rows(d[:, h * LANES:(h + 1) * LANES], g_ref[...]) * (1.0 - lam_init))
        o_ref[0] = jnp.concatenate(parts, axis=1)


def _dec_even(qt, pool_k, pool_v, table, k_new, v_new, lv, subln, *, pp, ts, lam_init):
    b, n_pages = table.shape
    width = pool_k.shape[2]
    assert n_pages % pp == 0 and ts == SUBLANES and k_new.shape[1] == PAGE
    n_steps = n_pages // pp

    def page_map(u):
        return lambda b_, s, t: (t[b_, n_pages - (s + 1) * pp + u], 0, 0)

    page_specs = [pl.BlockSpec((1, PAGE, width), page_map(u)) for u in range(pp)]
    new_spec = pl.BlockSpec((1, PAGE, width), lambda b_, s, t: (b_, 0, 0))
    grid_spec = pltpu.PrefetchScalarGridSpec(
        num_scalar_prefetch=1,
        grid=(b, n_steps),
        in_specs=[pl.BlockSpec((1,) + qt.shape[1:], lambda b_, s, t: (b_, 0, 0))] + page_specs + page_specs
        + [new_spec, new_spec, pl.BlockSpec(lv.shape, lambda b_, s, t: (0, 0)),
           pl.BlockSpec(subln.shape, lambda b_, s, t: (0, 0))],
        out_specs=pl.BlockSpec((1, ts, width), lambda b_, s, t: (b_, 0, 0)),
        scratch_shapes=[
            pltpu.VMEM((pp * PAGE, pp * PAGE), BF16),
            pltpu.VMEM((SUBLANES, LANES), F32),
            pltpu.VMEM((LANES // 2, SB_W), F32),
            pltpu.VMEM((LANES // 2, DF_W), F32),
        ],
    )
    kern = functools.partial(_dec_even_kernel, pp=pp, ts=ts, lam_init=lam_init)
    return pl.pallas_call(
        kern,
        grid_spec=grid_spec,
        out_shape=jax.ShapeDtypeStruct((b, ts, width), F32),
        compiler_params=_cparams(("parallel", "arbitrary")),
        name="dec_even",
    )(table, qt, *([pool_k] * pp), *([pool_v] * pp), k_new, v_new, lv, subln)


def _compress_kernel(t_ref, w1_ref, pe_ref, w2_ref, g_ref, o_ref):
    kind = pl.program_id(1)
    half = CMP_STRIDE * HD
    c = t_ref[0, 0, 0].astype(BF16)
    n = c.shape[0]
    a1 = _dot(c, w1_ref[0, :half, :])
    a2 = _dot(c, w1_ref[0, half:, :])
    a2 = jnp.where(_iota(a2.shape, 0) < n - 1, pltpu.roll(a2, n - 1, 0), 0.0)
    pe = _dot(jnp.broadcast_to(pe_ref[0], (SUBLANES, CMP_LEN * HD)).astype(BF16), w1_ref[0])[0:1]
    hid = a1 + a2 + pe
    hid = hid * (1.0 / (1.0 + jnp.exp(-hid)))
    out = _dot(hid.astype(BF16), w2_ref[0])
    normed = _rms_rows(out, g_ref[...])
    o_ref[0, 0, 0] = jnp.where(kind == 0, normed, out)


def _compress(t, w1, pe_flat, w2, gain):
    b, _, g, n, width = t.shape
    return pl.pallas_call(
        _compress_kernel,
        grid=(b, 2, g),
        in_specs=[
            pl.BlockSpec((1, 1, 1, n, width), lambda b_, k, g_: (b_, k, g_, 0, 0)),
            pl.BlockSpec((1,) + w1.shape[1:], lambda b_, k, g_: (k, 0, 0)),
            pl.BlockSpec((1,) + pe_flat.shape[1:], lambda b_, k, g_: (k, 0, 0)),
            pl.BlockSpec((1,) + w2.shape[1:], lambda b_, k, g_: (k, 0, 0)),
            pl.BlockSpec(gain.shape, lambda b_, k, g_: (0, 0)),
        ],
        out_specs=pl.BlockSpec((1, 1, 1, n, HD), lambda b_, k, g_: (b_, k, g_, 0, 0)),
        out_shape=jax.ShapeDtypeStruct((b, 2, g, n, HD), F32),
        compiler_params=_cparams(("parallel", "arbitrary", "arbitrary")),
    )(t, w1, pe_flat, w2, gain)


def _nsa_cmp_kernel(q_ref, kc_ref, vc_ref, o_ref, sel_ref, *, tq, q_off, n_sel):
    i = pl.program_id(2)
    q = q_ref[0, 0, 0]
    kc = kc_ref[0, 0]
    vc = vc_ref[0, 0]
    ncol = kc.shape[0]
    nbp = ncol // CMP_PER_SLC
    q_lo = q_off + i * tq
    qpos = q_lo + _iota((tq, ncol), 0)
    col = _iota((tq, ncol), 1)
    jj = jnp.zeros_like(col)
    for u in range(1, CMP_PER_SLC):
        jj = jj + jnp.where(col >= u * nbp, 1, 0)
    c_end = (col - jj * nbp) * SLC_BLOCK + jj * CMP_STRIDE + (CMP_LEN - 1)
    maskf = _tile_rows(jnp.where(c_end <= qpos, 1.0, 0.0), NSA_GROUP)
    keep = maskf > 0.5
    s = jnp.where(keep, _dot_nt(q, kc), NEG)
    m = jnp.max(s, axis=1, keepdims=True)
    p = jnp.where(keep, jnp.exp(s - m), 0.0)
    p = p / jnp.maximum(jnp.sum(p, axis=1, keepdims=True), 1e-30)
    o_ref[0, 0, 0] = _dot(p.astype(BF16), vc)
    pg = p[0:tq]
    for r in range(1, NSA_GROUP):
        pg = pg + p[r * tq:(r + 1) * tq]
    imp = pg[:, 0:nbp]
    for j in range(1, CMP_PER_SLC):
        imp = imp + pg[:, j * nbp:(j + 1) * nbp]
    blk = _iota((tq, nbp), 1)
    qp = q_lo + _iota((tq, nbp), 0)
    q_blk = qp >> SLC_SHIFT
    visible = blk * SLC_BLOCK <= qp
    forced = jnp.logical_or(blk == 0, jnp.logical_and(blk <= q_blk, blk > q_blk - N_LOCAL))
    score = jnp.where(visible, jnp.where(forced, FORCE, imp), NEG)
    blkf = blk.astype(F32)
    sel = jnp.zeros((tq, nbp), F32)
    for _ in range(n_sel):
        top = jnp.max(score, axis=1, keepdims=True)
        idx = jnp.min(jnp.where(score == top, blkf, 1e9), axis=1, keepdims=True)
        pick = blkf == idx
        sel = jnp.where(pick, 1.0, sel)
        score = jnp.where(pick, -jnp.inf, score)
    sel_ref[0, 0] = sel


def _nsa_cmp(q, kc, vc, *, tq, q_off, n_sel):
    b, g, nq, rows, _ = q.shape
    ncol = kc.shape[2]
    nbp = ncol // CMP_PER_SLC
    kern = functools.partial(_nsa_cmp_kernel, tq=tq, q_off=q_off, n_sel=n_sel)
    return pl.pallas_call(
        kern,
        grid=(b, g, nq),
        in_specs=[
            pl.BlockSpec((1, 1, 1, rows, HD), lambda b_, g_, i: (b_, g_, i, 0, 0)),
            pl.BlockSpec((1, 1, ncol, HD), lambda b_, g_, i: (b_, g_, 0, 0)),
            pl.BlockSpec((1, 1, ncol, HD), lambda b_, g_, i: (b_, g_, 0, 0)),
        ],
        out_specs=[
            pl.BlockSpec((1, 1, 1, rows, HD), lambda b_, g_, i: (b_, g_, i, 0, 0)),
            pl.BlockSpec((1, 1, tq, nbp), lambda b_, g_, i: (b_, g_, i, 0)),
        ],
        out_shape=[
            jax.ShapeDtypeStruct((b, g, nq, rows, HD), F32),
            jax.ShapeDtypeStruct((b, g, nq * tq, nbp), F32),
        ],
        compiler_params=_cparams(("parallel", "parallel", "arbitrary")),
    )(q, kc, vc)


def _dsa_kernel(qi_ref, wi_ref, ki_ref, qd_ref, kd_ref, vd_ref, o_ref, key_ref, *, tq, tk, q_off, n_top):
    i = pl.program_id(1)
    qi = qi_ref[0, 0]
    qd = qd_ref[0, 0]
    w = wi_ref[0] * (N_IDX ** -0.5)
    n_kt = ki_ref.shape[1] // tk
    q_lo = q_off + i * tq
    hi = jnp.minimum(lax.div(q_lo + tq - 1, tk) + 1, n_kt)
    qpos = q_lo + _iota((tq, tk), 0)

    def visible(j):
        return (j * tk + _iota((tq, tk), 1)) <= qpos

    def score_tile(j, _):
        k0 = pl.multiple_of(j * tk, tk)
        sc = jnp.maximum(_dot_nt(qi, ki_ref[0, pl.ds(k0, tk), :]), 0.0)
        tot = w[:, 0:1] * sc[0:tq]
        for h in range(1, N_IDX):
            tot = tot + w[:, h:h + 1] * sc[h * tq:(h + 1) * tq]
        tot = jnp.where(visible(j), tot, NEG)
        bits = pltpu.bitcast(tot, jnp.int32)
        key = jnp.where(bits < 0, bits ^ jnp.int32(0x7FFFFFFF), bits)
        key_ref[:, pl.ds(k0, tk)] = jnp.where(tot == 0.0, 0, key)
        return 0

    lax.fori_loop(0, hi, score_tile, 0)

    def count_ge(c):
        def body(j, acc):
            blk = key_ref[:, pl.ds(pl.multiple_of(j * tk, tk), tk)]
            hit = jnp.where(blk >= c, 1.0, 0.0)
            part = hit[:, 0:LANES]
            for u in range(1, tk // LANES):
                part = part + hit[:, u * LANES:(u + 1) * LANES]
            return acc + part
        acc = lax.fori_loop(0, hi, body, jnp.zeros((tq, LANES), F32))
        return jnp.sum(acc, axis=1, keepdims=True)

    kf = float(n_top)
    tau = jnp.where(count_ge(jnp.zeros((tq, 1), jnp.int32)) >= kf, 0, INT_MIN).astype(jnp.int32)

    def bit_body(t, tau):
        cand = tau + jnp.left_shift(jnp.int32(1), 30 - t)
        return jnp.where(count_ge(cand) >= kf, cand, tau)

    tau = lax.fori_loop(0, 31, bit_body, tau)
    need = kf - count_ge(tau + 1)
    before = jnp.where(_iota((tk, tk), 0) < _iota((tk, tk), 1), 1.0, 0.0).astype(BF16)

    def attend(j, carry):
        m, l, acc, n_eq = carry
        k0 = pl.multiple_of(j * tk, tk)
        key = key_ref[:, pl.ds(k0, tk)]
        eqf = jnp.where(key == tau, 1.0, 0.0)
        rank = n_eq + _dot(eqf.astype(BF16), before)
        kept = jnp.where(key > tau, 1.0, jnp.where(rank < need, eqf, 0.0))
        maskf = jnp.where(visible(j), kept, 0.0)
        s = _dot_nt(qd, kd_ref[0, pl.ds(k0, tk), :])
        m, l, acc = _softmax_step(s, _tile_rows(maskf, N_DSA), vd_ref[0, pl.ds(k0, tk), :], m, l, acc)
        return m, l, acc, n_eq + jnp.sum(eqf, axis=1, keepdims=True)

    rows = qd.shape[0]
    init = (jnp.full((rows, 1), NEG, F32), jnp.zeros((rows, 1), F32), jnp.zeros((rows, HD), F32),
            jnp.zeros((tq, 1), F32))
    m, l, acc, _ = lax.fori_loop(0, hi, attend, init)
    o_ref[0, 0] = acc / jnp.maximum(l, 1e-30)


def _dsa(qi, wi, ki, qd, kd, vd, *, tq, tk, q_off, n_top):
    b, nq, rows, _ = qi.shape
    lk = ki.shape[1]
    assert lk % tk == 0
    kern = functools.partial(_dsa_kernel, tq=tq, tk=tk, q_off=q_off, n_top=n_top)
    qspec = pl.BlockSpec((1, 1, rows, HD), lambda b_, i: (b_, i, 0, 0))
    kspec = pl.BlockSpec((1, lk, HD), lambda b_, i: (b_, 0, 0))
    return pl.pallas_call(
        kern,
        grid=(b, nq),
        in_specs=[qspec, pl.BlockSpec((1, tq, N_IDX), lambda b_, i: (b_, i, 0)), kspec, qspec, kspec, kspec],
        out_specs=pl.BlockSpec((1, 1, rows, HD), lambda b_, i: (b_, i, 0, 0)),
        out_shape=jax.ShapeDtypeStruct((b, nq, rows, HD), F32),
        scratch_shapes=[pltpu.VMEM((tq, lk), jnp.int32)],
        compiler_params=_cparams(("parallel", "arbitrary")),
    )(qi, wi, ki, qd, kd, vd)


def _gather_kernel(tbl_ref, pool_ref, new_ref, o_ref, *, n_pages):
    p = pl.program_id(1)

    @pl.when(p < n_pages)
    def _():
        o_ref[...] = pool_ref[...]

    @pl.when(p >= n_pages)
    def _():
        o_ref[...] = new_ref[...]


def _page_gather(pool, table, new):
    b, n_pages = table.shape
    width = pool.shape[2]
    n_new = new.shape[1] // PAGE
    kern = functools.partial(_gather_kernel, n_pages=n_pages)
    grid_spec = pltpu.PrefetchScalarGridSpec(
        num_scalar_prefetch=1,
        grid=(b, n_pages + n_new),
        in_specs=[
            pl.BlockSpec((1, PAGE, width), lambda b_, p, t: (t[b_, jnp.minimum(p, n_pages - 1)], 0, 0)),
            pl.BlockSpec((1, PAGE, width), lambda b_, p, t: (b_, jnp.maximum(p - n_pages, 0), 0)),
        ],
        out_specs=pl.BlockSpec((1, PAGE, width), lambda b_, p, t: (b_, p, 0)),
    )
    return pl.pallas_call(
        kern,
        grid_spec=grid_spec,
        out_shape=jax.ShapeDtypeStruct((b, (n_pages + n_new) * PAGE, width), pool.dtype),
        compiler_params=_cparams(("parallel", "arbitrary")),
    )(table, pool, new)


def _rope_tables(pos):
    half = HD // 2
    inv = ROPE_THETA ** (-jnp.arange(half, dtype=F32) / half)
    ang = pos.astype(F32)[:, None] * inv[None, :]
    cos, sin = jnp.cos(ang), jnp.sin(ang)
    cos128 = jnp.tile(jnp.concatenate([cos, cos], axis=1), (1, LANES // HD))
    sin128 = jnp.tile(jnp.concatenate([-sin, sin], axis=1), (1, LANES // HD))
    return cos128, sin128


def _tile_gain(g):
    return jnp.tile(g.reshape(1, HD), (1, LANES // HD))


def _heads(a, b, t, h, d, scale=None):
    a = a.reshape(b, t, h, d)
    if scale is not None:
        a = a * scale
    return a.transpose(0, 2, 1, 3).astype(BF16)


def _stack_q(a, tq):
    b, hk, r, t, d = a.shape
    return a.reshape(b, hk, r, t // tq, tq, d).transpose(0, 1, 3, 2, 4, 5).reshape(b, hk, t // tq, r * tq, d)


def _unstack_q(a, r, tq):
    b, hk, nq, _, d = a.shape
    a = a.reshape(b, hk, nq, r, tq, d).transpose(0, 2, 4, 1, 3, 5)
    return a.reshape(b * nq * tq, hk * r * d)


def _pad_rows(a, rows):
    return jnp.pad(a, ((0, 0), (0, rows - a.shape[1])) + ((0, 0),) * (a.ndim - 2))


def _even_mixer(x2, b, t, q_off, cs, past, prm, cfg):
    n = b * t
    program = [
        (0, SB_W, None, None, [(0, False)]),
        (3 * SB_W, DF_W, 0, None, [(SB_W, True)]),
        (SB_W, SB_W, None, None, [(SB_W + DF_W, False)]),
        (3 * SB_W + DF_W, DF_W, 1, None, [(2 * SB_W + DF_W, True)]),
        (2 * SB_W, SB_W, None, None, [(2 * (SB_W + DF_W), False)]),
        (3 * SB_W + 2 * DF_W, DF_W, None, None, [(3 * SB_W + 2 * DF_W, False)]),
    ]
    gains = jnp.concatenate([_tile_gain(prm["df_qk_gain"][0]), _tile_gain(prm["df_qk_gain"][1])], axis=0)
    proj = _mm([x2, cs[0], cs[1]], [prm["g0"], gains], prm["ev_w_in"], _lhs_norm, program, 3 * (SB_W + DF_W),
               tm=cfg["tm"], rope_idx=(1, 2), gains_idx=1)
    mw = SB_W + DF_W
    new_k = proj[:, mw:2 * mw].reshape(b, t, mw)
    new_v = proj[:, 2 * mw:3 * mw].reshape(b, t, mw)
    lam_init = 0.8 - 0.6 * math.exp(-0.3 * prm["layer"])
    lv, subln = prm["df_lambda"], prm["df_subln_gain"].reshape(1, 2 * HD)
    d = x2.shape[1]
    if past is None:
        proj3 = proj.reshape(b, t, 3 * mw)
        ns = SB_W // LANES
        o_sb = _sb(proj3, t=cfg["t_even"], q_slab=0, k_slab=2 * ns, v_slab=4 * ns, n_slabs=ns)
        o_df = _df(proj3, lv, subln, t=cfg["t_even"], q_slab=ns, k_slab=3 * ns, v_slab=5 * ns, n_slabs=ns,
                   lam_init=lam_init)
        return _mm([o_sb.reshape(n, SB_W), o_df.reshape(n, DF_W), x2], [], prm["ev_w_out"], _lhs_cat2,
                   [(0, d, None, None, [(0, False)])], d, tm=cfg["tm"], res_idx=2), new_k, new_v
    pool_k, pool_v, table = past
    qcat = proj[:, :mw].reshape(b, t, mw).transpose(0, 2, 1) * SCALE
    qt = (jnp.tile(qcat, (1, 1, LANES // t)) * _dec_even_mask(t)).astype(BF16)
    o = _dec_even(qt, pool_k, pool_v, table, _pad_rows(new_k, PAGE), _pad_rows(new_v, PAGE), lv, subln,
                  pp=cfg["pp"], ts=t, lam_init=lam_init)
    return _mm([o.reshape(n, mw), x2], [], prm["ev_w_out"], _lhs_plain,
               [(0, d, None, None, [(0, False)])], d, tm=cfg["tm"], res_idx=1), new_k, new_v


def _dec_even_mask(ts):
    f = np.arange(SB_W + DF_W)[:, None]
    c = np.arange(LANES)[None, :]
    half = LANES // 2
    sb = (f < SB_W) & (c < half) & (f // HD == c // ts)
    df = (f >= SB_W) & (c >= half) & ((f - SB_W) // HD == (c - half) // ts)
    return jnp.asarray((sb | df).astype(np.float32))


def _odd_layout():
    widths = (N_NSA * HD, 128, 128, 128, 128, 128, 128, N_NSA * 3, N_DSA * HD, HD, HD, N_IDX * HD, HD, N_IDX)
    offs = np.concatenate([[0], np.cumsum(widths)])
    (q_n, k_c, v_c, k_s, v_s, k_w, v_w, gate, q_d, k_d, v_d, q_i, k_i, w_i) = [
        (int(offs[j]), int(offs[j + 1])) for j in range(len(widths))]
    pieces = [q_n, (k_c[0], v_w[1]), q_d, q_i, k_d, k_i, v_d, w_i, ("pad", HD - N_IDX), gate,
              ("pad", LANES - N_NSA * 3)]
    program = [
        (0, 512, 0, None, [(0, False), (512, True)]),
        (512, 128, None, None, [(1024, False)]),
        (640, 128, None, None, [(1152, False)]),
        (768, 128, 1, None, [(1280, True)]),
        (896, 128, None, None, [(1408, False)]),
        (1024, 128, 2, None, [(1536, True)]),
        (1152, 128, None, None, [(1664, False)]),
        (1280, 512, 3, None, [(1792, True)]),
        (1792, 512, None, None, [(2304, True)]),
        (2304, 128, 4, None, [(2816, True)]),
        (2432, 128, None, None, [(2944, False)]),
        (2560, 128, None, "sigmoid", [(3072, False)]),
    ]
    return pieces, program, 3200


def _permute_cols(w, pieces):
    cols = []
    for p in pieces:
        if p[0] == "pad":
            cols.append(jnp.zeros((w.shape[0], p[1]), w.dtype))
        else:
            cols.append(w[:, p[0]:p[1]])
    return jnp.concatenate(cols, axis=1)


def _odd_mixer(x2, b, t, t_real, q_off, cs, past, prm, cfg):
    n = b * t
    pieces, program, out_cols = _odd_layout()
    ng, dg = prm["nsa_qk_gain"], prm["dsa_qk_gain"]
    gains = jnp.concatenate([
        _tile_gain(ng[0]), _tile_gain(ng[2]), _tile_gain(ng[3]), _tile_gain(dg[0]),
        jnp.concatenate([dg[1], dg[2]]).reshape(1, LANES)], axis=0)
    w_in = _permute_cols(prm["od_w_in"], pieces)
    proj = _mm([x2, cs[0], cs[1]], [prm["g0"], gains], w_in, _lhs_norm, program, out_cols,
               tm=cfg["tm"], rope_idx=(1, 2), gains_idx=1)
    g = N_NSA_KV
    new_nsa = proj[:, 1024:1536].reshape(b, t, 4 * g * HD)
    new_win = proj[:, 1536:1792].reshape(b, t, 2, g, HD)
    new_dsa = jnp.concatenate([proj[:, 2816:2880], proj[:, 2944:3008], proj[:, 2880:2944]], axis=1).reshape(b, t, 3 * HD)
    w_i = proj[:, 3008:3008 + N_IDX].reshape(b, t, N_IDX)
    gate = proj[:, 3072:3072 + N_NSA * 3].reshape(n, N_NSA, 3)
    if past is None:
        nsa_buf, dsa_buf = new_nsa, new_dsa
        win_buf = new_win
        win_off = 0
        new_state = new_win[:, -min(WINDOW, t):]
        l_real = t
    else:
        pool_nsa, pool_dsa, state, table = past
        nsa_buf = _page_gather(pool_nsa, table, _pad_rows(new_nsa, cfg["new_rows"]))
        dsa_buf = _page_gather(pool_dsa, table, _pad_rows(new_dsa, cfg["new_rows"]))
        wb = state.shape[1]
        win_buf = _pad_rows(jnp.concatenate([state, new_win], axis=1), wb + cfg["tk_win"])
        win_off = q_off - wb
        new_state = jnp.concatenate([state, new_win[:, :t_real]], axis=1)[:, -wb:]
        l_real = table.shape[1] * PAGE + t_real
    lk = nsa_buf.shape[1]
    nsa5 = nsa_buf.reshape(b, lk, 4, g, HD)
    tq = cfg["tq"]

    n_cmp = lk // CMP_STRIDE
    t_c = nsa5[:, :, 0:2].transpose(0, 2, 3, 1, 4).reshape(b, 2, g, n_cmp, CMP_STRIDE * HD)
    pe_flat = prm["cmp_pe"].reshape(2, 1, CMP_LEN * HD)
    cmp = _compress(t_c, prm["cmp_w1"], pe_flat, prm["cmp_w2"], ng[1].reshape(1, HD))
    nb = n_cmp // CMP_PER_SLC
    nbp = -(-nb // LANES) * LANES
    cmp = cmp.reshape(b, 2, g, nb, CMP_PER_SLC, HD).transpose(0, 1, 2, 4, 3, 5)
    cmp = jnp.pad(cmp, ((0, 0),) * 4 + ((0, nbp - nb), (0, 0))).reshape(b, 2, g, CMP_PER_SLC * nbp, HD).astype(BF16)

    def group_q(cols):
        a = _heads(cols, b, t, N_NSA, HD, SCALE).reshape(b, g, NSA_GROUP, t, HD)
        return _stack_q(a, tq)

    q_n = group_q(proj[:, 0:512])
    q_r = group_q(proj[:, 512:1024])
    n_blk = -(-l_real // SLC_BLOCK)
    o_c, sel = _nsa_cmp(q_n, cmp[:, 0], cmp[:, 1], tq=tq, q_off=q_off, n_sel=min(N_SLC, n_blk))

    def kv_heads(a):
        return a.transpose(0, 2, 1, 3).astype(BF16)

    o_s = _flash(q_r, kv_heads(nsa5[:, :, 2]), kv_heads(nsa5[:, :, 3]), reps=NSA_GROUP, tq=tq, tk=cfg["tk"],
                 q_off=q_off, k_off=0, mode="causal", bm=sel)
    o_w = _flash(q_r, kv_heads(win_buf[:, :, 0]), kv_heads(win_buf[:, :, 1]), reps=NSA_GROUP, tq=tq,
                 tk=cfg["tk_win"], q_off=q_off, k_off=win_off, mode="window")

    dsa4 = dsa_buf.reshape(b, lk, 3, HD).astype(BF16)
    q_d = _stack_q(_heads(proj[:, 1792:2304], b, t, N_DSA, HD, SCALE)[:, None], tq)[:, 0]
    q_i = _stack_q(_heads(proj[:, 2304:2816], b, t, N_IDX, HD, IDX_SCALE)[:, None], tq)[:, 0]
    o_d = _dsa(q_i, w_i, dsa4[:, :, 2], q_d, dsa4[:, :, 0], dsa4[:, :, 1], tq=tq, tk=cfg["tk"], q_off=q_off,
               n_top=min(DSA_TOPK_MAX, l_real // 4))

    o_c, o_s, o_w = (_unstack_q(o, NSA_GROUP, tq) for o in (o_c, o_s, o_w))
    o_d = _unstack_q(o_d[:, None], N_DSA, tq)
    gfull = [jnp.repeat(gate[:, :, j], HD, axis=1) for j in range(3)]
    d = x2.shape[1]
    out = _mm([o_c, o_s, o_w, o_d] + gfull + [x2], [], prm["od_w_out"], _lhs_odd,
              [(0, d, None, None, [(0, False)])], d, tm=cfg["tm"], res_idx=7)
    return out, new_nsa, new_dsa, new_state


def _cross(x2, b, t, mem_k, mem_v, prm, cfg):
    d = x2.shape[1]
    xw = N_XH * HD
    q = _mm([x2], [prm["g1"], _tile_gain(prm["x_gq"])], prm["x_wq"], _lhs_norm,
            [(0, xw, 0, None, [(0, False)])], xw, tm=cfg["tm"], gains_idx=1)
    tq = cfg["tq"]
    qh = _stack_q(_heads(q, b, t, N_XH, HD, SCALE)[:, :, None], tq)
    o = _flash(qh, mem_k, mem_v, reps=1, tq=tq, tk=mem_k.shape[2], q_off=0, k_off=0, mode="full")
    o = _unstack_q(o, 1, tq)
    return _mm([o, x2], [], prm["x_wo"], _lhs_plain, [(0, d, None, None, [(0, False)])], d, tm=cfg["tm"], res_idx=1)


def _memory_kv(mem2, prm):
    xw = N_XH * HD
    w = jnp.concatenate([prm["x_wk"], prm["x_wv"]], axis=1)
    return _mm([mem2], [_tile_gain(prm["x_gk"])], w, _lhs_plain,
               [(0, xw, 0, None, [(0, False)]), (xw, xw, None, None, [(xw, False)])], 2 * xw,
               tm=min(256, mem2.shape[0]), gains_idx=0)


def _run_group(x, q_off, t_real, mem_kvs, pasts, layers, cfg):
    b, t, d = x.shape
    x2 = x.reshape(b * t, d)
    pos = q_off + jnp.arange(t, dtype=jnp.int32)
    cos, sin = _rope_tables(pos)
    cs = (jnp.tile(cos, (b, 1)), jnp.tile(sin, (b, 1)))
    outs = {}
    for li, prm in enumerate(layers):
        if li % 2 == 0:
            x2, nk, nv = _even_mixer(x2, b, t, q_off, cs, pasts[li], prm, cfg)
            outs["ek"], outs["ev"] = nk, nv
        else:
            x2, nn, nd, nw = _odd_mixer(x2, b, t, t_real, q_off, cs, pasts[li], prm, cfg)
            outs["on"], outs["od"], outs["ow"] = nn, nd, nw
        x2 = _cross(x2, b, t, mem_kvs[li][0], mem_kvs[li][1], prm, cfg)
        if li % 2 == 0:
            x2 = _ffn(x2, prm["g2"], prm["router"], prm["w1"], prm["w3"], prm["w2"], tm=cfg["tm_ffn"],
                      tf=cfg["tf"], routed=False)
        else:
            x2 = _ffn(x2, prm["g2"], prm["router"], prm["w1"], prm["w3"], prm["w2"], tm=cfg["tm_ffn"],
                      tf=cfg["tf"], routed=True)
    return x2.reshape(b, t, d), outs


def kernel(x_prompt, x_sample, mem_prompt, cache_even_k, cache_even_v, cache_odd_nsa, cache_odd_dsa, state_odd_win, cache_mem, page_table, norm_gain, ev_w_in, ev_w_out, df_qk_gain, df_lambda, df_subln_gain, ffn_w1, ffn_w3, ffn_w2, od_w_in, od_w_out, nsa_qk_gain, cmp_pe, cmp_w1, cmp_w2, dsa_qk_gain, moe_router, moe_w1, moe_w3, moe_w2, x_wq, x_wk, x_wv, x_wo, x_qk_gain):
    depth = norm_gain.shape[0]
    bp, tp, d = x_prompt.shape
    bs, ts, _ = x_sample.shape
    n_mem = mem_prompt.shape[1]
    xw = N_XH * HD

    layers = []
    for l in range(depth):
        i = l // 2
        prm = {
            "layer": l,
            "g0": norm_gain[l, 0].reshape(1, d), "g1": norm_gain[l, 1].reshape(1, d), "g2": norm_gain[l, 2].reshape(1, d),
            "x_wq": x_wq[l].astype(BF16), "x_wk": x_wk[l].astype(BF16), "x_wv": x_wv[l].astype(BF16),
            "x_wo": x_wo[l].astype(BF16), "x_gq": x_qk_gain[l, 0], "x_gk": x_qk_gain[l, 1],
        }
        if l % 2 == 0:
            prm.update({
                "ev_w_in": ev_w_in[i].astype(BF16), "ev_w_out": ev_w_out[i].astype(BF16),
                "df_qk_gain": df_qk_gain[i], "df_lambda": df_lambda[i], "df_subln_gain": df_subln_gain[i],
                "router": jnp.zeros((SUBLANES, LANES), F32),
                "w1": ffn_w1[i][None].astype(BF16), "w3": ffn_w3[i][None].astype(BF16), "w2": ffn_w2[i][None].astype(BF16),
            })
        else:
            prm.update({
                "od_w_in": od_w_in[i].astype(BF16), "od_w_out": od_w_out[i].astype(BF16),
                "nsa_qk_gain": nsa_qk_gain[i], "dsa_qk_gain": dsa_qk_gain[i],
                "cmp_pe": cmp_pe[i], "cmp_w1": cmp_w1[i].astype(BF16), "cmp_w2": cmp_w2[i].astype(BF16),
                "router": jnp.pad(moe_router[i], ((0, 0), (0, LANES - N_EXPERTS))),
                "w1": moe_w1[i].astype(BF16), "w3": moe_w3[i].astype(BF16), "w2": moe_w2[i].astype(BF16),
            })
        layers.append(prm)

    def mem_heads(kv, b):
        k = kv[:, :, 0].transpose(0, 2, 1, 3).astype(BF16)
        v = kv[:, :, 1].transpose(0, 2, 1, 3).astype(BF16)
        return k, v

    mem2 = mem_prompt.reshape(bp * n_mem, d)
    mem_p = [_memory_kv(mem2, layers[l]).reshape(bp, n_mem, 2, N_XH, HD) for l in range(depth)]
    ff = ffn_w1.shape[2]
    tf = ff // 2 if (ff // 2) % LANES == 0 else ff
    cfg_p = {"tm": 256, "tq": 128, "tk": 256, "t_even": 256, "tk_win": 128, "tm_ffn": 512, "tf": tf}
    cfg_p["tq"] = min(cfg_p["tq"], tp)
    y_prompt, op = _run_group(x_prompt, 0, tp, [mem_heads(m, bp) for m in mem_p], [None] * depth, layers, cfg_p)
    p_mem = jnp.stack(mem_p)

    n_past = page_table.shape[1] * cache_even_k.shape[2]
    ts_pad = -(-ts // SUBLANES) * SUBLANES
    xs = _pad_rows(x_sample, ts_pad)
    pasts = []
    for l in range(depth):
        i = l // 2
        if l % 2 == 0:
            pasts.append((cache_even_k[i], cache_even_v[i], page_table))
        else:
            pn = cache_odd_nsa[i]
            pd = cache_odd_dsa[i]
            pasts.append((pn.reshape(pn.shape[0], pn.shape[1], -1), pd.reshape(pd.shape[0], pd.shape[1], -1),
                          state_odd_win[i], page_table))
    cfg_s = {"tm": bs * ts_pad, "tq": ts_pad, "tk": 512, "pp": 4, "tk_win": 128, "tm_ffn": bs * ts_pad,
             "tf": tf, "new_rows": 512}
    y_s, os_ = _run_group(xs, n_past, ts, [mem_heads(cache_mem[l], bs) for l in range(depth)], pasts, layers, cfg_s)
    y_sample = y_s[:, :ts]

    g = N_NSA_KV
    return (
        y_prompt, y_sample,
        op["ek"][None], op["ev"][None],
        op["on"].reshape(1, bp, tp, 4, g, HD), op["od"].reshape(1, bp, tp, 3, HD),
        op["ow"][None], p_mem,
        os_["ek"][:, :ts][None], os_["ev"][:, :ts][None],
        os_["on"][:, :ts].reshape(1, bs, ts, 4, g, HD), os_["od"][:, :ts].reshape(1, bs, ts, 3, HD),
        os_["ow"][None],
    )
```

```python
import functools
import math

import jax
import jax.numpy as jnp
import numpy as np
from jax import lax
from jax.experimental import pallas as pl
from jax.experimental.pallas import tpu as pltpu

F32 = jnp.float32
BF16 = jnp.bfloat16

HD = 64
N_SB = 8
N_DF = 4
N_NSA = 8
N_NSA_KV = 2
NSA_GROUP = N_NSA // N_NSA_KV
N_DSA = 8
N_IDX = 8
N_XH = 4
N_EXPERTS = 8
ROPE_THETA = 10000.0
CMP_LEN = 32
CMP_STRIDE = 16
SLC_BLOCK = 64
CMP_PER_SLC = SLC_BLOCK // CMP_STRIDE
N_SLC = 16
N_LOCAL = 2
WINDOW = 512
DSA_TOPK_MAX = 256
EPS = 1e-6
NEG = -1e30
FORCE = 1e9
SCALE = HD ** -0.5
IDX_SCALE = HD ** -0.5
SB_W = N_SB * HD
HD_SHIFT = 6
SLC_SHIFT = 6
DF_W = N_DF * 2 * HD

LANES = 128
SUBLANES = 8
PAGE = 128
VMEM_LIMIT = 52 * 1024 * 1024
INT_MIN = -2 ** 31
ROW_SPLIT = 2
SB_CUT = 120.0
KEY_UNROLL = 4

_NT = (((1,), (1,)), ((), ()))


def _cparams(sem):
    return pltpu.CompilerParams(dimension_semantics=sem, vmem_limit_bytes=VMEM_LIMIT)


def _dot(a, b):
    return jnp.dot(a, b, preferred_element_type=F32)


def _dot_nt(a, b):
    return lax.dot_general(a, b, _NT, preferred_element_type=F32)


def _split_dot(x, m_bf16):
    hi = x.astype(BF16)
    lo = (x - hi.astype(F32)).astype(BF16)
    return _dot(hi, m_bf16) + _dot(lo, m_bf16)


def _iota(shape, dim):
    return lax.broadcasted_iota(jnp.int32, shape, dim)


def _rms_rows(x, g):
    return x * lax.rsqrt(jnp.mean(x * x, axis=-1, keepdims=True) + EPS) * g


def _group_mean_matrix():
    r = _iota((LANES, LANES), 0) >> HD_SHIFT
    c = _iota((LANES, LANES), 1) >> HD_SHIFT
    return jnp.where(r == c, 1.0 / HD, 0.0).astype(BF16)


def _head_norm(y, g, gm):
    ms = _split_dot(y * y, gm)
    return y * lax.rsqrt(ms + EPS) * g


def _rope_slab(y, cos, sin):
    lane = _iota(y.shape, 1)
    first = (lane & (HD - 1)) < (HD // 2)
    swapped = jnp.where(first, pltpu.roll(y, LANES - HD // 2, 1), pltpu.roll(y, HD // 2, 1))
    return y * cos + swapped * sin


def _mm_kernel(*refs, n_rows, n_consts, lhs_fn, program, rope_idx, gains_idx, res_idx):
    rows = refs[:n_rows]
    consts = refs[n_rows:n_rows + n_consts]
    w_ref = refs[n_rows + n_consts]
    o_ref = refs[-1]
    lhs = lhs_fn(rows, consts).astype(BF16)
    gm = _group_mean_matrix() if gains_idx is not None else None
    for (src, width, gain_row, act, outs) in program:
        y_full = _dot(lhs, w_ref[:, src:src + width])
        for s in range(width // LANES):
            y = y_full[:, s * LANES:(s + 1) * LANES]
            if gain_row is not None:
                y = _head_norm(y, consts[gains_idx][gain_row:gain_row + 1, :], gm)
            if act == "sigmoid":
                y = 1.0 / (1.0 + jnp.exp(-y))
            for (dst, rope) in outs:
                z = y
                if rope:
                    z = _rope_slab(y, rows[rope_idx[0]][...], rows[rope_idx[1]][...])
                d0 = dst + s * LANES
                if res_idx is not None:
                    z = z + rows[res_idx][:, d0:d0 + LANES]
                o_ref[:, d0:d0 + LANES] = z


def _mm(rows, consts, w, lhs_fn, program, out_cols, *, tm, rope_idx=None, gains_idx=None, res_idx=None, name="mm"):
    n = rows[0].shape[0]
    assert n % tm == 0
    in_specs = [pl.BlockSpec((tm, r.shape[1]), lambda i: (i, 0)) for r in rows]
    in_specs += [pl.BlockSpec(c.shape, lambda i: (0, 0)) for c in consts]
    in_specs += [pl.BlockSpec(w.shape, lambda i: (0, 0))]
    kern = functools.partial(_mm_kernel, n_rows=len(rows), n_consts=len(consts), lhs_fn=lhs_fn,
                             program=program, rope_idx=rope_idx, gains_idx=gains_idx, res_idx=res_idx)
    return pl.pallas_call(
        kern,
        grid=(n // tm,),
        in_specs=in_specs,
        out_specs=pl.BlockSpec((tm, out_cols), lambda i: (i, 0)),
        out_shape=jax.ShapeDtypeStruct((n, out_cols), F32),
        compiler_params=_cparams(("parallel",)),
        name=name,
    )(*rows, *consts, w)


def _lhs_norm(rows, consts):
    return _rms_rows(rows[0][...], consts[0][...])


def _lhs_plain(rows, consts):
    return rows[0][...]


def _lhs_cat2(rows, consts):
    return jnp.concatenate([rows[0][...], rows[1][...]], axis=1)


def _lhs_odd(rows, consts):
    oc, os_, ow, od = rows[0][...], rows[1][...], rows[2][...], rows[3][...]
    g0, g1, g2 = rows[4][...], rows[5][...], rows[6][...]
    return jnp.concatenate([g0 * oc + g1 * os_ + g2 * ow, od], axis=1)


def _ffn_kernel(x_ref, g_ref, r_ref, w1_ref, w3_ref, w2_ref, o_ref, h_ref, acc_ref, gate_ref, *, routed):
    e = pl.program_id(1)
    f = pl.program_id(2)
    first = jnp.logical_and(e == 0, f == 0)
    last = jnp.logical_and(e == pl.num_programs(1) - 1, f == pl.num_programs(2) - 1)

    @pl.when(first)
    def _():
        x = x_ref[...]
        h = _rms_rows(x, g_ref[...])
        h_ref[...] = h.astype(BF16)
        acc_ref[...] = x
        if routed:
            logits = jnp.dot(h, r_ref[...], preferred_element_type=F32, precision=lax.Precision.HIGHEST)
            col = _iota(logits.shape, 1).astype(F32)
            logits = jnp.where(col < N_EXPERTS, logits, -jnp.inf)
            m1 = jnp.max(logits, axis=1, keepdims=True)
            i1 = jnp.min(jnp.where(logits == m1, col, 1e9), axis=1, keepdims=True)
            rest = jnp.where(col == i1, -jnp.inf, logits)
            m2 = jnp.max(rest, axis=1, keepdims=True)
            i2 = jnp.min(jnp.where(rest == m2, col, 1e9), axis=1, keepdims=True)
            e2 = jnp.exp(m2 - m1)
            g1 = 1.0 / (1.0 + e2)
            g2 = e2 / (1.0 + e2)
            gate_ref[...] = jnp.where(col == i1, g1, 0.0) + jnp.where(col == i2, g2, 0.0)

    def compute(gcol):
        h = h_ref[...]
        u = _dot(h, w1_ref[0])
        v = _dot(h, w3_ref[0])
        a = (u * (1.0 / (1.0 + jnp.exp(-u)))) * v
        y = _dot(a.astype(BF16), w2_ref[0])
        if gcol is not None:
            y = gcol * y
        acc_ref[...] += y

    if routed:
        col = _iota(gate_ref.shape, 1)
        gcol = jnp.sum(jnp.where(col == e, gate_ref[...], 0.0), axis=1, keepdims=True)
        active = jnp.max(gcol) > 0.0

        @pl.when(active)
        def _():
            compute(gcol)
    else:
        compute(None)

    @pl.when(last)
    def _():
        o_ref[...] = acc_ref[...]


def _ffn(x, g, router, w1, w3, w2, *, tm, tf, routed):
    n, d = x.shape
    ne, _, ff = w1.shape
    assert n % tm == 0 and ff % tf == 0
    kern = functools.partial(_ffn_kernel, routed=routed)
    return pl.pallas_call(
        kern,
        grid=(n // tm, ne, ff // tf),
        in_specs=[
            pl.BlockSpec((tm, d), lambda i, e, f: (i, 0)),
            pl.BlockSpec((1, d), lambda i, e, f: (0, 0)),
            pl.BlockSpec(router.shape, lambda i, e, f: (0, 0)),
            pl.BlockSpec((1, d, tf), lambda i, e, f: (e, 0, f)),
            pl.BlockSpec((1, d, tf), lambda i, e, f: (e, 0, f)),
            pl.BlockSpec((1, tf, d), lambda i, e, f: (e, f, 0)),
        ],
        out_specs=pl.BlockSpec((tm, d), lambda i, e, f: (i, 0)),
        out_shape=jax.ShapeDtypeStruct((n, d), F32),
        scratch_shapes=[pltpu.VMEM((tm, d), BF16), pltpu.VMEM((tm, d), F32), pltpu.VMEM((tm, LANES), F32)],
        compiler_params=_cparams(("parallel", "arbitrary", "arbitrary")),
        name="moe" if routed else "ffn",
    )(x, g, router, w1, w3, w2)


def _tile_rows(m, reps):
    return m if reps == 1 else jnp.concatenate([m] * reps, axis=0)


def _online_step(s, v, m, l, acc):
    m_new = jnp.maximum(m, jnp.max(s, axis=1, keepdims=True))
    p = jnp.exp(s - m_new)
    alpha = jnp.exp(m - m_new)
    return m_new, alpha * l + jnp.sum(p, axis=1, keepdims=True), alpha * acc + _dot(p.astype(BF16), v)


def _online_init(rows, dv):
    return (jnp.full((rows, 1), NEG, F32), jnp.zeros((rows, 1), F32), jnp.zeros((rows, dv), F32))


def _unrolled_loop(lo, hi, body, init):
    shift = KEY_UNROLL.bit_length() - 1
    n_group = (hi - lo) >> shift

    def group(p, st):
        for u in range(KEY_UNROLL):
            st = body(lo + KEY_UNROLL * p + u, st)
        return st

    st = lax.fori_loop(0, n_group, group, init)
    return lax.fori_loop(lo + n_group * KEY_UNROLL, hi, body, st)


def _flash_kernel(*refs, reps, tq, tk, q_off, k_off, mode, has_bm):
    if has_bm:
        q_ref, k_ref, v_ref, bm_ref, o_ref = refs
    else:
        q_ref, k_ref, v_ref, o_ref = refs
    i = pl.program_id(2)
    q = q_ref[0, 0, 0]
    dv = v_ref.shape[-1]
    n_kt = k_ref.shape[2] // tk
    q_lo = q_off + i * tq
    if mode == "full":
        lo, hi = 0, n_kt
    else:
        hi = jnp.minimum(lax.div(q_lo + tq - 1 - k_off, tk) + 1, n_kt)
        lo = 0
        if mode == "window":
            lo = lax.div(jnp.maximum(q_lo - (WINDOW - 1) - k_off, 0), tk)
    qpos = q_lo + _iota((tq, tk), 0)
    if has_bm:
        bm = bm_ref[0, 0].astype(BF16)
        nbp = bm.shape[1]

    def body(j, state):
        k0 = pl.multiple_of(j * tk, tk)
        k = k_ref[0, 0, pl.ds(k0, tk), :]
        v = v_ref[0, 0, pl.ds(k0, tk), :]
        bias = None
        if mode != "full":
            kpos = k_off + j * tk + _iota((tq, tk), 1)
            ok = kpos <= qpos
            if mode == "window":
                ok = jnp.logical_and(ok, qpos - kpos < WINDOW)
                ok = jnp.logical_and(ok, kpos >= 0)
            bias = jnp.where(ok, 0.0, NEG)
        if has_bm:
            blk = _iota((nbp, tk), 0)
            tok = (j * tk + _iota((nbp, tk), 1)) >> SLC_SHIFT
            expand = jnp.where(blk == tok, 1.0, 0.0).astype(BF16)
            bias = jnp.where(_dot(bm, expand) > 0.5, bias, NEG)
        new = []
        for r in range(reps):
            s = _dot_nt(q[r * tq:(r + 1) * tq].astype(BF16), k)
            if bias is not None:
                s = s + bias
            new.append(_online_step(s, v, *state[r]))
        return tuple(new)

    state = _unrolled_loop(lo, hi, body, tuple(_online_init(tq, dv) for _ in range(reps)))
    for r in range(reps):
        o_ref[0, 0, 0, r * tq:(r + 1) * tq, :] = state[r][2] / state[r][1]


def _flash(q, k, v, *, reps, tq, tk, q_off, k_off, mode, bm=None, v_map=None):
    b, hk, nq, rows, _ = q.shape
    lk = k.shape[2]
    dv = v.shape[3]
    assert rows == reps * tq and lk % tk == 0
    if v_map is None:
        v_map = lambda h: h
    in_specs = [
        pl.BlockSpec((1, 1, 1, rows, HD), lambda b_, h, i: (b_, h, i, 0, 0)),
        pl.BlockSpec((1, 1, lk, HD), lambda b_, h, i: (b_, h, 0, 0)),
        pl.BlockSpec((1, 1, lk, dv), lambda b_, h, i: (b_, v_map(h), 0, 0)),
    ]
    args = [q, k, v]
    if bm is not None:
        in_specs.append(pl.BlockSpec((1, 1, tq, bm.shape[3]), lambda b_, h, i: (b_, h, i, 0)))
        args.append(bm)
    kern = functools.partial(_flash_kernel, reps=reps, tq=tq, tk=tk, q_off=q_off, k_off=k_off, mode=mode,
                             has_bm=bm is not None)
    return pl.pallas_call(
        kern,
        grid=(b, hk, nq),
        in_specs=in_specs,
        out_specs=pl.BlockSpec((1, 1, 1, rows, dv), lambda b_, h, i: (b_, h, i, 0, 0)),
        out_shape=jax.ShapeDtypeStruct((b, hk, nq, rows, dv), F32),
        compiler_params=_cparams(("parallel", "parallel", "arbitrary")),
        name="flash_" + mode + ("_blockmask" if bm is not None else ""),
    )(*args)


def _softplus(z):
    return jnp.maximum(z, 0.0) + jnp.log(1.0 + jnp.exp(-jnp.abs(z)))


def _later_matrix(n):
    return jnp.where(_iota((n, n), 0) > _iota((n, n), 1), 1.0, 0.0).astype(BF16)


def _sb_kernel(q_ref, k_ref, v_ref, o_ref, *, t):
    i = pl.program_id(2)
    th = t // ROW_SPLIT
    lane = _iota((th, LANES), 1)
    later = _later_matrix(t)
    qs = []
    for part in range(ROW_SPLIT):
        qf = q_ref[0, part * th:(part + 1) * th, :] * SCALE
        qs.append((jnp.where(lane < HD, qf, 0.0).astype(BF16), jnp.where(lane >= HD, qf, 0.0).astype(BF16)))

    def tile(j, carries, outs, masked):
        k0 = pl.multiple_of(j * t, t)
        k = k_ref[0, pl.ds(k0, t), :].astype(BF16)
        v = v_ref[0, pl.ds(k0, t), :].astype(BF16)
        new_carries, new_outs = [], []
        for part in range(ROW_SPLIT):
            heads = []
            for hd in range(2):
                carry = carries[2 * part + hd]
                z = _dot_nt(qs[part][hd], k)
                sp = _softplus(z)
                log_sig = z - sp
                if masked:
                    vis = _iota((th, t), 1) < part * th + _iota((th, t), 0)
                    sp = jnp.where(vis, sp, 0.0)
                a = jnp.exp(log_sig - _dot(sp.astype(BF16), later) - carry)
                if masked:
                    a = jnp.where(vis, a, 0.0)
                heads.append(_dot(a.astype(BF16), v))
                new_carries.append(carry + jnp.sum(sp, axis=1, keepdims=True))
            new_outs.append(outs[part] + jnp.where(lane < HD, heads[0], heads[1]))
        return tuple(new_carries), tuple(new_outs)

    def min_carry(carries):
        m = jnp.min(carries[0])
        for c in carries[1:]:
            m = jnp.minimum(m, jnp.min(c))
        return m

    zero = jnp.zeros((th, 1), F32)
    carries, outs = tile(i, (zero,) * (2 * ROW_SPLIT), (jnp.zeros((th, LANES), F32),) * ROW_SPLIT, True)

    def cond(st):
        return jnp.logical_and(st[0] < i, st[1] < SB_CUT)

    def body(st):
        carries, outs = tile(i - 1 - st[0], st[2], st[3], False)
        return st[0] + 1, min_carry(carries), carries, outs

    _, _, carries, outs = lax.while_loop(cond, body, (jnp.int32(0), min_carry(carries), carries, outs))
    for part in range(ROW_SPLIT):
        o_ref[0, part * th:(part + 1) * th, :] = outs[part]


def _sb(proj3, *, t, q_slab, k_slab, v_slab, n_slabs):
    b, tt, _ = proj3.shape
    assert tt % t == 0
    kern = functools.partial(_sb_kernel, t=t)
    return pl.pallas_call(
        kern,
        grid=(b, n_slabs, tt // t),
        in_specs=[
            pl.BlockSpec((1, t, LANES), lambda b_, p, i: (b_, i, q_slab + p)),
            pl.BlockSpec((1, tt, LANES), lambda b_, p, i: (b_, 0, k_slab + p)),
            pl.BlockSpec((1, tt, LANES), lambda b_, p, i: (b_, 0, v_slab + p)),
        ],
        out_specs=pl.BlockSpec((1, t, LANES), lambda b_, p, i: (b_, i, p)),
        out_shape=jax.ShapeDtypeStruct((b, tt, n_slabs * LANES), F32),
        compiler_params=_cparams(("parallel", "parallel", "arbitrary")),
        name="sb_prompt",
    )(proj3, proj3, proj3)


def _lam(lv, lam_init):
    a = jnp.sum(jnp.sum(lv[0:1] * lv[1:2], axis=1, keepdims=True), axis=0, keepdims=True)
    b = jnp.sum(jnp.sum(lv[2:3] * lv[3:4], axis=1, keepdims=True), axis=0, keepdims=True)
    return jnp.exp(a) - jnp.exp(b) + lam_init


def _df_kernel(lv_ref, g_ref, q_ref, k_ref, v_ref, o_ref, *, t, lam_init):
    i = pl.program_id(2)
    th = t // ROW_SPLIT
    lane = _iota((th, LANES), 1)
    qs = []
    for part in range(ROW_SPLIT):
        qf = q_ref[0, part * th:(part + 1) * th, :] * SCALE
        qs.append((jnp.where(lane < HD, qf, 0.0).astype(BF16), jnp.where(lane >= HD, qf, 0.0).astype(BF16)))

    def tile(j, state, masked):
        k0 = pl.multiple_of(j * t, t)
        k = k_ref[0, pl.ds(k0, t), :].astype(BF16)
        v = v_ref[0, pl.ds(k0, t), :].astype(BF16)
        new = []
        for part in range(ROW_SPLIT):
            for mp in range(2):
                s = _dot_nt(qs[part][mp], k)
                if masked:
                    s = jnp.where(_iota((th, t), 1) <= part * th + _iota((th, t), 0), s, NEG)
                new.append(_online_step(s, v, *state[2 * part + mp]))
        return tuple(new)

    init = tuple(_online_init(th, LANES) for _ in range(2 * ROW_SPLIT))
    state = _unrolled_loop(0, i, lambda j, st: tile(j, st, False), init)
    state = tile(i, state, True)
    lam = _lam(lv_ref[...], lam_init)
    for part in range(ROW_SPLIT):
        (_, l0, a0), (_, l1, a1) = state[2 * part], state[2 * part + 1]
        d = a0 / l0 - lam * (a1 / l1)
        o_ref[0, part * th:(part + 1) * th, :] = _rms_rows(d, g_ref[...]) * (1.0 - lam_init)


def _df(proj3, lv, subln, *, t, q_slab, k_slab, v_slab, n_slabs, lam_init):
    b, tt, _ = proj3.shape
    assert tt % t == 0
    kern = functools.partial(_df_kernel, t=t, lam_init=lam_init)
    return pl.pallas_call(
        kern,
        grid=(b, n_slabs, tt // t),
        in_specs=[
            pl.BlockSpec(lv.shape, lambda b_, p, i: (0, 0)),
            pl.BlockSpec(subln.shape, lambda b_, p, i: (0, 0)),
            pl.BlockSpec((1, t, LANES), lambda b_, p, i: (b_, i, q_slab + p)),
            pl.BlockSpec((1, tt, LANES), lambda b_, p, i: (b_, 0, k_slab + p)),
            pl.BlockSpec((1, tt, LANES), lambda b_, p, i: (b_, 0, v_slab + p)),
        ],
        out_specs=pl.BlockSpec((1, t, LANES), lambda b_, p, i: (b_, i, p)),
        out_shape=jax.ShapeDtypeStruct((b, tt, n_slabs * LANES), F32),
        compiler_params=_cparams(("parallel", "parallel", "arbitrary")),
        name="df_prompt",
    )(lv, subln, proj3, proj3, proj3)


def _dec_even_kernel(*refs, pp, ts, lam_init):
    tbl_ref, qt_ref = refs[0], refs[1]
    k_refs = refs[2:2 + pp]
    v_refs = refs[2 + pp:2 + 2 * pp]
    kn_ref, vn_ref, lv_ref, g_ref, o_ref, later_ref, st_ref, asb_ref, adf_ref = refs[2 + 2 * pp:]
    del tbl_ref
    s_id = pl.program_id(1)
    nk = pp * PAGE
    half = LANES // 2
    qt = qt_ref[0]

    def col_of(row):
        return jnp.transpose(jnp.broadcast_to(row, (SUBLANES, LANES)))[half:, 0:1]

    def tile(kt, vt, later, vis_sb, vis_df):
        carry, m, l = st_ref[0:1, :], st_ref[1:2, :], st_ref[2:3, :]
        zt = _dot(kt, qt)
        lane = _iota(zt.shape, 1)
        sp = _softplus(zt)
        log_sig = zt - sp
        s = zt
        if vis_sb is not None:
            sp = jnp.where(vis_sb, sp, 0.0)
            s = jnp.where(vis_df, s, NEG)
        a = jnp.exp(log_sig - _dot(later, sp.astype(BF16)) - carry)
        if vis_sb is not None:
            a = jnp.where(vis_sb, a, 0.0)
        m_new = jnp.maximum(m, jnp.max(s, axis=0, keepdims=True))
        p = jnp.exp(s - m_new)
        alpha = jnp.exp(m - m_new)
        st_ref[0:1, :] = carry + jnp.sum(sp, axis=0, keepdims=True)
        st_ref[1:2, :] = m_new
        st_ref[2:3, :] = alpha * l + jnp.sum(p, axis=0, keepdims=True)
        w = jnp.transpose(jnp.where(lane < half, a, p)).astype(BF16)
        asb_ref[...] += _dot(w[:half], vt[:, :SB_W])
        adf_ref[...] = col_of(alpha) * adf_ref[...] + _dot(w[half:], vt[:, SB_W:])

    @pl.when(s_id == 0)
    def _():
        later_ref[...] = jnp.where(_iota((nk, nk), 1) > _iota((nk, nk), 0), 1.0, 0.0).astype(BF16)
        st_ref[...] = jnp.where(_iota(st_ref.shape, 0) == 1, NEG, 0.0)
        asb_ref[...] = jnp.zeros_like(asb_ref)
        adf_ref[...] = jnp.zeros_like(adf_ref)
        key = _iota((PAGE, LANES), 0)
        tok = _iota((PAGE, LANES), 1) & (ts - 1)
        tile(kn_ref[0].astype(BF16), vn_ref[0].astype(BF16), later_ref[0:PAGE, 0:PAGE], key < tok, key <= tok)

    kt = jnp.concatenate([r[0].astype(BF16) for r in k_refs], axis=0)
    vt = jnp.concatenate([r[0].astype(BF16) for r in v_refs], axis=0)
    tile(kt, vt, later_ref[...], None, None)

    @pl.when(s_id == pl.num_programs(1) - 1)
    def _():
        row = _iota((half, SB_W), 0)
        lane = _iota((half, SB_W), 1)

        def fold(x):
            out = x[0:ts]
            for u in range(1, half // ts):
                out = out + x[u * ts:(u + 1) * ts]
            return out

        o_sb = fold(jnp.where((row >> 3) == (lane >> HD_SHIFT), asb_ref[...], 0.0))
        pn = adf_ref[...] / col_of(st_ref[2:3, :])
        same_head = (row >> 4) == (lane >> 7)
        o0 = fold(jnp.where(jnp.logical_and(same_head, ((row >> 3) & 1) == 0), pn, 0.0))
        o1 = fold(jnp.where(jnp.logical_and(same_head, ((row >> 3) & 1) == 1), pn, 0.0))
        d = o0 - _lam(lv_ref[...], lam_init) * o1
        parts = [o_sb]
        for h in range(N_DF):
            parts.append(_rms_rows(d[:, h * LANES:(h + 1) * LANES], g_ref[...]) * (1.0 - lam_init))
        o_ref[0] = jnp.concatenate(parts, axis=1)


def _dec_even(qt, pool_k, pool_v, table, k_new, v_new, lv, subln, *, pp, ts, lam_init):
    b, n_pages = table.shape
    width = pool_k.shape[2]
    assert n_pages % pp == 0 and ts == SUBLANES and k_new.shape[1] == PAGE
    n_steps = n_pages // pp

    def page_map(u):
        return lambda b_, s, t: (t[b_, n_pages - (s + 1) * pp + u], 0, 0)

    page_specs = [pl.BlockSpec((1, PAGE, width), page_map(u)) for u in range(pp)]
    new_spec = pl.BlockSpec((1, PAGE, width), lambda b_, s, t: (b_, 0, 0))
    grid_spec = pltpu.PrefetchScalarGridSpec(
        num_scalar_prefetch=1,
        grid=(b, n_steps),
        in_specs=[pl.BlockSpec((1,) + qt.shape[1:], lambda b_, s, t: (b_, 0, 0))] + page_specs + page_specs
        + [new_spec, new_spec, pl.BlockSpec(lv.shape, lambda b_, s, t: (0, 0)),
           pl.BlockSpec(subln.shape, lambda b_, s, t: (0, 0))],
        out_specs=pl.BlockSpec((1, ts, width), lambda b_, s, t: (b_, 0, 0)),
        scratch_shapes=[
            pltpu.VMEM((pp * PAGE, pp * PAGE), BF16),
            pltpu.VMEM((SUBLANES, LANES), F32),
            pltpu.VMEM((LANES // 2, SB_W), F32),
            pltpu.VMEM((LANES // 2, DF_W), F32),
        ],
    )
    kern = functools.partial(_dec_even_kernel, pp=pp, ts=ts, lam_init=lam_init)
    return pl.pallas_call(
        kern,
        grid_spec=grid_spec,
        out_shape=jax.ShapeDtypeStruct((b, ts, width), F32),
        compiler_params=_cparams(("parallel", "arbitrary")),
        name="dec_even",
    )(table, qt, *([pool_k] * pp), *([pool_v] * pp), k_new, v_new, lv, subln)


def _compress_kernel(t_ref, w1_ref, pe_ref, w2_ref, g_ref, o_ref):
    kind = pl.program_id(1)
    half = CMP_STRIDE * HD
    c = t_ref[0, 0, 0].astype(BF16)
    n = c.shape[0]
    a1 = _dot(c, w1_ref[0, :half, :])
    a2 = _dot(c, w1_ref[0, half:, :])
    a2 = jnp.where(_iota(a2.shape, 0) < n - 1, pltpu.roll(a2, n - 1, 0), 0.0)
    pe = _dot(jnp.broadcast_to(pe_ref[0], (SUBLANES, CMP_LEN * HD)).astype(BF16), w1_ref[0])[0:1]
    hid = a1 + a2 + pe
    hid = hid * (1.0 / (1.0 + jnp.exp(-hid)))
    out = _dot(hid.astype(BF16), w2_ref[0])
    normed = _rms_rows(out, g_ref[...])
    o_ref[0, 0, 0] = jnp.where(kind == 0, normed, out)


def _compress(t, w1, pe_flat, w2, gain):
    b, _, g, n, width = t.shape
    return pl.pallas_call(
        _compress_kernel,
        grid=(b, 2, g),
        in_specs=[
            pl.BlockSpec((1, 1, 1, n, width), lambda b_, k, g_: (b_, k, g_, 0, 0)),
            pl.BlockSpec((1,) + w1.shape[1:], lambda b_, k, g_: (k, 0, 0)),
            pl.BlockSpec((1,) + pe_flat.shape[1:], lambda b_, k, g_: (k, 0, 0)),
            pl.BlockSpec((1,) + w2.shape[1:], lambda b_, k, g_: (k, 0, 0)),
            pl.BlockSpec(gain.shape, lambda b_, k, g_: (0, 0)),
        ],
        out_specs=pl.BlockSpec((1, 1, 1, n, HD), lambda b_, k, g_: (b_, k, g_, 0, 0)),
        out_shape=jax.ShapeDtypeStruct((b, 2, g, n, HD), F32),
        compiler_params=_cparams(("parallel", "arbitrary", "arbitrary")),
        name="compress",
    )(t, w1, pe_flat, w2, gain)


def _nsa_cmp_kernel(q_ref, kc_ref, vc_ref, o_ref, sel_ref, *, tq, q_off, n_sel):
    i = pl.program_id(2)
    q = q_ref[0, 0, 0].astype(BF16)
    kc = kc_ref[0, 0]
    vc = vc_ref[0, 0]
    ncol = kc.shape[0]
    nbp = ncol // CMP_PER_SLC
    q_lo = q_off + i * tq
    qpos = q_lo + _iota((tq, ncol), 0)
    col = _iota((tq, ncol), 1)
    jj = jnp.zeros_like(col)
    for u in range(1, CMP_PER_SLC):
        jj = jj + jnp.where(col >= u * nbp, 1, 0)
    c_end = (col - jj * nbp) * SLC_BLOCK + jj * CMP_STRIDE + (CMP_LEN - 1)
    maskf = _tile_rows(jnp.where(c_end <= qpos, 1.0, 0.0), NSA_GROUP)
    keep = maskf > 0.5
    s = jnp.where(keep, _dot_nt(q, kc), NEG)
    m = jnp.max(s, axis=1, keepdims=True)
    p = jnp.where(keep, jnp.exp(s - m), 0.0)
    p = p / jnp.maximum(jnp.sum(p, axis=1, keepdims=True), 1e-30)
    o_ref[0, 0, 0] = _dot(p.astype(BF16), vc)
    pg = p[0:tq]
    for r in range(1, NSA_GROUP):
        pg = pg + p[r * tq:(r + 1) * tq]
    imp = pg[:, 0:nbp]
    for j in range(1, CMP_PER_SLC):
        imp = imp + pg[:, j * nbp:(j + 1) * nbp]
    blk = _iota((tq, nbp), 1)
    qp = q_lo + _iota((tq, nbp), 0)
    q_blk = qp >> SLC_SHIFT
    visible = blk * SLC_BLOCK <= qp
    forced = jnp.logical_or(blk == 0, jnp.logical_and(blk <= q_blk, blk > q_blk - N_LOCAL))
    score = jnp.where(visible, jnp.where(forced, FORCE, imp), NEG)
    blkf = blk.astype(F32)
    sel = jnp.zeros((tq, nbp), F32)
    for _ in range(n_sel):
        top = jnp.max(score, axis=1, keepdims=True)
        idx = jnp.min(jnp.where(score == top, blkf, 1e9), axis=1, keepdims=True)
        pick = blkf == idx
        sel = jnp.where(pick, 1.0, sel)
        score = jnp.where(pick, -jnp.inf, score)
    sel_ref[0, 0] = sel


def _nsa_cmp(q, kc, vc, *, tq, q_off, n_sel):
    b, g, nq, rows, _ = q.shape
    ncol = kc.shape[2]
    nbp = ncol // CMP_PER_SLC
    kern = functools.partial(_nsa_cmp_kernel, tq=tq, q_off=q_off, n_sel=n_sel)
    return pl.pallas_call(
        kern,
        grid=(b, g, nq),
        in_specs=[
            pl.BlockSpec((1, 1, 1, rows, HD), lambda b_, g_, i: (b_, g_, i, 0, 0)),
            pl.BlockSpec((1, 1, ncol, HD), lambda b_, g_, i: (b_, g_, 0, 0)),
            pl.BlockSpec((1, 1, ncol, HD), lambda b_, g_, i: (b_, g_, 0, 0)),
        ],
        out_specs=[
            pl.BlockSpec((1, 1, 1, rows, HD), lambda b_, g_, i: (b_, g_, i, 0, 0)),
            pl.BlockSpec((1, 1, tq, nbp), lambda b_, g_, i: (b_, g_, i, 0)),
        ],
        out_shape=[
            jax.ShapeDtypeStruct((b, g, nq, rows, HD), F32),
            jax.ShapeDtypeStruct((b, g, nq * tq, nbp), F32),
        ],
        compiler_params=_cparams(("parallel", "parallel", "arbitrary")),
        name="nsa_cmp",
    )(q, kc, vc)


def _dsa_kernel(qi_ref, wi_ref, ki_ref, qd_ref, kd_ref, vd_ref, o_ref, key_ref, *, tq, tk, q_off, n_top):
    i = pl.program_id(1)
    qi = qi_ref[0, 0]
    qd = qd_ref[0, 0]
    w = wi_ref[0] * (N_IDX ** -0.5)
    n_kt = ki_ref.shape[1] // tk
    q_lo = q_off + i * tq
    hi = jnp.minimum(lax.div(q_lo + tq - 1, tk) + 1, n_kt)
    qpos = q_lo + _iota((tq, tk), 0)

    def visible(j):
        return (j * tk + _iota((tq, tk), 1)) <= qpos

    def score_tile(j, _):
        k0 = pl.multiple_of(j * tk, tk)
        k = ki_ref[0, pl.ds(k0, tk), :]
        tot = None
        for h in range(N_IDX):
            sc = w[:, h:h + 1] * jnp.maximum(_dot_nt(qi[h * tq:(h + 1) * tq].astype(BF16), k), 0.0)
            tot = sc if tot is None else tot + sc
        tot = jnp.where(visible(j), tot, NEG)
        bits = pltpu.bitcast(tot, jnp.int32)
        key = jnp.where(bits < 0, bits ^ jnp.int32(0x7FFFFFFF), bits)
        key_ref[:, pl.ds(k0, tk)] = jnp.where(tot == 0.0, 0, key)
        return 0

    _unrolled_loop(0, hi, score_tile, 0)

    def count_ge(c):
        def body(j, acc):
            blk = key_ref[:, pl.ds(pl.multiple_of(j * tk, tk), tk)]
            hit = jnp.where(blk >= c, 1.0, 0.0)
            part = hit[:, 0:LANES]
            for u in range(1, tk // LANES):
                part = part + hit[:, u * LANES:(u + 1) * LANES]
            return acc + part
        acc = _unrolled_loop(0, hi, body, jnp.zeros((tq, LANES), F32))
        return jnp.sum(acc, axis=1, keepdims=True)

    kf = float(n_top)
    tau = jnp.where(count_ge(jnp.zeros((tq, 1), jnp.int32)) >= kf, 0, INT_MIN).astype(jnp.int32)

    def bit_body(t, tau):
        cand = tau + jnp.left_shift(jnp.int32(1), 30 - t)
        return jnp.where(count_ge(cand) >= kf, cand, tau)

    tau = lax.fori_loop(0, 31, bit_body, tau)
    need = kf - count_ge(tau + 1)
    before = jnp.where(_iota((tk, tk), 0) < _iota((tk, tk), 1), 1.0, 0.0).astype(BF16)

    def attend(j, carry):
        state, n_eq = carry
        k0 = pl.multiple_of(j * tk, tk)
        key = key_ref[:, pl.ds(k0, tk)]
        eqf = jnp.where(key == tau, 1.0, 0.0)
        rank = n_eq + _dot(eqf.astype(BF16), before)
        kept = jnp.logical_or(key > tau, jnp.logical_and(key == tau, rank < need))
        bias = jnp.where(jnp.logical_and(kept, visible(j)), 0.0, NEG)
        k = kd_ref[0, pl.ds(k0, tk), :]
        v = vd_ref[0, pl.ds(k0, tk), :]
        new = []
        for h in range(N_DSA):
            s = _dot_nt(qd[h * tq:(h + 1) * tq].astype(BF16), k) + bias
            new.append(_online_step(s, v, *state[h]))
        return tuple(new), n_eq + jnp.sum(eqf, axis=1, keepdims=True)

    init = (tuple(_online_init(tq, HD) for _ in range(N_DSA)), jnp.zeros((tq, 1), F32))
    state, _ = _unrolled_loop(0, hi, attend, init)
    for h in range(N_DSA):
        o_ref[0, 0, h * tq:(h + 1) * tq, :] = state[h][2] / state[h][1]


def _dsa(qi, wi, ki, qd, kd, vd, *, tq, tk, q_off, n_top):
    b, nq, rows, _ = qi.shape
    lk = ki.shape[1]
    assert lk % tk == 0
    kern = functools.partial(_dsa_kernel, tq=tq, tk=tk, q_off=q_off, n_top=n_top)
    qspec = pl.BlockSpec((1, 1, rows, HD), lambda b_, i: (b_, i, 0, 0))
    kspec = pl.BlockSpec((1, lk, HD), lambda b_, i: (b_, 0, 0))
    return pl.pallas_call(
        kern,
        grid=(b, nq),
        in_specs=[qspec, pl.BlockSpec((1, tq, N_IDX), lambda b_, i: (b_, i, 0)), kspec, qspec, kspec, kspec],
        out_specs=pl.BlockSpec((1, 1, rows, HD), lambda b_, i: (b_, i, 0, 0)),
        out_shape=jax.ShapeDtypeStruct((b, nq, rows, HD), F32),
        scratch_shapes=[pltpu.VMEM((tq, lk), jnp.int32)],
        compiler_params=_cparams(("parallel", "arbitrary")),
        name="dsa",
    )(qi, wi, ki, qd, kd, vd)


def _gather_kernel(*refs, pp, n_steps):
    pool_refs = refs[1:1 + pp]
    new_ref, o_ref = refs[1 + pp], refs[2 + pp]
    s = pl.program_id(1)

    @pl.when(s < n_steps)
    def _():
        for u in range(pp):
            o_ref[0, u * PAGE:(u + 1) * PAGE, :] = pool_refs[u][0]

    @pl.when(s >= n_steps)
    def _():
        o_ref[...] = new_ref[...]


def _page_gather(pool, table, new, *, pp):
    b, n_pages = table.shape
    width = pool.shape[2]
    assert n_pages % pp == 0 and new.shape[1] == pp * PAGE
    n_steps = n_pages // pp

    def page_map(u):
        return lambda b_, s, t: (t[b_, jnp.minimum(s, n_steps - 1) * pp + u], 0, 0)

    kern = functools.partial(_gather_kernel, pp=pp, n_steps=n_steps)
    grid_spec = pltpu.PrefetchScalarGridSpec(
        num_scalar_prefetch=1,
        grid=(b, n_steps + 1),
        in_specs=[pl.BlockSpec((1, PAGE, width), page_map(u)) for u in range(pp)]
        + [pl.BlockSpec((1, pp * PAGE, width), lambda b_, s, t: (b_, 0, 0))],
        out_specs=pl.BlockSpec((1, pp * PAGE, width), lambda b_, s, t: (b_, s, 0)),
    )
    return pl.pallas_call(
        kern,
        grid_spec=grid_spec,
        out_shape=jax.ShapeDtypeStruct((b, (n_pages + pp) * PAGE, width), pool.dtype),
        compiler_params=_cparams(("parallel", "arbitrary")),
        name="page_gather",
    )(table, *([pool] * pp), new)


def _rope_tables(pos):
    half = HD // 2
    inv = ROPE_THETA ** (-jnp.arange(half, dtype=F32) / half)
    ang = pos.astype(F32)[:, None] * inv[None, :]
    cos, sin = jnp.cos(ang), jnp.sin(ang)
    cos128 = jnp.tile(jnp.concatenate([cos, cos], axis=1), (1, LANES // HD))
    sin128 = jnp.tile(jnp.concatenate([-sin, sin], axis=1), (1, LANES // HD))
    return cos128, sin128


def _tile_gain(g):
    return jnp.tile(g.reshape(1, HD), (1, LANES // HD))


def _heads(a, b, t, h, d, scale=None):
    a = a.reshape(b, t, h, d)
    if scale is not None:
        a = a * scale
    return a.transpose(0, 2, 1, 3)


def _stack_q(a, tq):
    b, hk, r, t, d = a.shape
    return a.reshape(b, hk, r, t // tq, tq, d).transpose(0, 1, 3, 2, 4, 5).reshape(b, hk, t // tq, r * tq, d)


def _unstack_q(a, r, tq):
    b, hk, nq, _, d = a.shape
    a = a.reshape(b, hk, nq, r, tq, d).transpose(0, 2, 4, 1, 3, 5)
    return a.reshape(b * nq * tq, hk * r * d)


def _pad_rows(a, rows):
    return jnp.pad(a, ((0, 0), (0, rows - a.shape[1])) + ((0, 0),) * (a.ndim - 2))


def _even_mixer(x2, b, t, q_off, cs, past, prm, cfg):
    n = b * t
    program = [
        (0, SB_W, None, None, [(0, False)]),
        (3 * SB_W, DF_W, 0, None, [(SB_W, True)]),
        (SB_W, SB_W, None, None, [(SB_W + DF_W, False)]),
        (3 * SB_W + DF_W, DF_W, 1, None, [(2 * SB_W + DF_W, True)]),
        (2 * SB_W, SB_W, None, None, [(2 * (SB_W + DF_W), False)]),
        (3 * SB_W + 2 * DF_W, DF_W, None, None, [(3 * SB_W + 2 * DF_W, False)]),
    ]
    gains = jnp.concatenate([_tile_gain(prm["df_qk_gain"][0]), _tile_gain(prm["df_qk_gain"][1])], axis=0)
    proj = _mm([x2, cs[0], cs[1]], [prm["g0"], gains], prm["ev_w_in"], _lhs_norm, program, 3 * (SB_W + DF_W),
               tm=cfg["tm"], rope_idx=(1, 2), gains_idx=1)
    mw = SB_W + DF_W
    new_k = proj[:, mw:2 * mw].reshape(b, t, mw)
    new_v = proj[:, 2 * mw:3 * mw].reshape(b, t, mw)
    lam_init = 0.8 - 0.6 * math.exp(-0.3 * prm["layer"])
    lv, subln = prm["df_lambda"], prm["df_subln_gain"].reshape(1, 2 * HD)
    d = x2.shape[1]
    if past is None:
        proj3 = proj.reshape(b, t, 3 * mw)
        ns = SB_W // LANES
        o_sb = _sb(proj3, t=cfg["t_even"], q_slab=0, k_slab=2 * ns, v_slab=4 * ns, n_slabs=ns)
        o_df = _df(proj3, lv, subln, t=cfg["t_even"], q_slab=ns, k_slab=3 * ns, v_slab=5 * ns, n_slabs=ns,
                   lam_init=lam_init)
        return _mm([o_sb.reshape(n, SB_W), o_df.reshape(n, DF_W), x2], [], prm["ev_w_out"], _lhs_cat2,
                   [(0, d, None, None, [(0, False)])], d, tm=cfg["tm"], res_idx=2), new_k, new_v
    pool_k, pool_v, table = past
    qcat = proj[:, :mw].reshape(b, t, mw).transpose(0, 2, 1) * SCALE
    qt = (jnp.tile(qcat, (1, 1, LANES // t)) * _dec_even_mask(t)).astype(BF16)
    o = _dec_even(qt, pool_k, pool_v, table, _pad_rows(new_k, PAGE), _pad_rows(new_v, PAGE), lv, subln,
                  pp=cfg["pp"], ts=t, lam_init=lam_init)
    return _mm([o.reshape(n, mw), x2], [], prm["ev_w_out"], _lhs_plain,
               [(0, d, None, None, [(0, False)])], d, tm=cfg["tm"], res_idx=1), new_k, new_v


def _dec_even_mask(ts):
    f = np.arange(SB_W + DF_W)[:, None]
    c = np.arange(LANES)[None, :]
    half = LANES // 2
    sb = (f < SB_W) & (c < half) & (f // HD == c // ts)
    df = (f >= SB_W) & (c >= half) & ((f - SB_W) // HD == (c - half) // ts)
    return jnp.asarray((sb | df).astype(np.float32))


def _odd_layout():
    widths = (N_NSA * HD, 128, 128, 128, 128, 128, 128, N_NSA * 3, N_DSA * HD, HD, HD, N_IDX * HD, HD, N_IDX)
    offs = np.concatenate([[0], np.cumsum(widths)])
    (q_n, k_c, v_c, k_s, v_s, k_w, v_w, gate, q_d, k_d, v_d, q_i, k_i, w_i) = [
        (int(offs[j]), int(offs[j + 1])) for j in range(len(widths))]
    pieces = [q_n, (k_c[0], v_w[1]), q_d, q_i, k_d, k_i, v_d, w_i, ("pad", HD - N_IDX), gate,
              ("pad", LANES - N_NSA * 3)]
    program = [
        (0, 512, 0, None, [(0, False), (512, True)]),
        (512, 128, None, None, [(1024, False)]),
        (640, 128, None, None, [(1152, False)]),
        (768, 128, 1, None, [(1280, True)]),
        (896, 128, None, None, [(1408, False)]),
        (1024, 128, 2, None, [(1536, True)]),
        (1152, 128, None, None, [(1664, False)]),
        (1280, 512, 3, None, [(1792, True)]),
        (1792, 512, None, None, [(2304, True)]),
        (2304, 128, 4, None, [(2816, True)]),
        (2432, 128, None, None, [(2944, False)]),
        (2560, 128, None, "sigmoid", [(3072, False)]),
    ]
    return pieces, program, 3200


def _permute_cols(w, pieces):
    cols = []
    for p in pieces:
        if p[0] == "pad":
            cols.append(jnp.zeros((w.shape[0], p[1]), w.dtype))
        else:
            cols.append(w[:, p[0]:p[1]])
    return jnp.concatenate(cols, axis=1)


def _odd_mixer(x2, b, t, t_real, q_off, cs, past, prm, cfg):
    n = b * t
    pieces, program, out_cols = _odd_layout()
    ng, dg = prm["nsa_qk_gain"], prm["dsa_qk_gain"]
    gains = jnp.concatenate([
        _tile_gain(ng[0]), _tile_gain(ng[2]), _tile_gain(ng[3]), _tile_gain(dg[0]),
        jnp.concatenate([dg[1], dg[2]]).reshape(1, LANES)], axis=0)
    w_in = _permute_cols(prm["od_w_in"], pieces)
    proj = _mm([x2, cs[0], cs[1]], [prm["g0"], gains], w_in, _lhs_norm, program, out_cols,
               tm=cfg["tm"], rope_idx=(1, 2), gains_idx=1)
    g = N_NSA_KV
    new_nsa = proj[:, 1024:1536].reshape(b, t, 4 * g * HD)
    new_win = proj[:, 1536:1792].reshape(b, t, 2, g, HD)
    new_dsa = jnp.concatenate([proj[:, 2816:2880], proj[:, 2944:3008], proj[:, 2880:2944]], axis=1).reshape(b, t, 3 * HD)
    w_i = proj[:, 3008:3008 + N_IDX].reshape(b, t, N_IDX)
    gate = proj[:, 3072:3072 + N_NSA * 3].reshape(n, N_NSA, 3)
    if past is None:
        nsa_buf, dsa_buf = new_nsa, new_dsa
        win_buf = new_win
        win_off = 0
        new_state = new_win[:, -min(WINDOW, t):]
        l_real = t
    else:
        pool_nsa, pool_dsa, state, table = past
        nsa_buf = _page_gather(pool_nsa, table, _pad_rows(new_nsa, cfg["pp"] * PAGE), pp=cfg["pp"])
        dsa_buf = _page_gather(pool_dsa, table, _pad_rows(new_dsa, cfg["pp"] * PAGE), pp=cfg["pp"])
        wb = state.shape[1]
        win_buf = _pad_rows(jnp.concatenate([state, new_win], axis=1), wb + cfg["tk_win"])
        win_off = q_off - wb
        new_state = jnp.concatenate([state, new_win[:, :t_real]], axis=1)[:, -wb:]
        l_real = table.shape[1] * PAGE + t_real
    lk = nsa_buf.shape[1]
    nsa5 = nsa_buf.reshape(b, lk, 4, g, HD)
    tq = cfg["tq"]

    n_cmp = lk // CMP_STRIDE
    t_c = nsa5[:, :, 0:2].transpose(0, 2, 3, 1, 4).reshape(b, 2, g, n_cmp, CMP_STRIDE * HD)
    pe_flat = prm["cmp_pe"].reshape(2, 1, CMP_LEN * HD)
    cmp = _compress(t_c, prm["cmp_w1"], pe_flat, prm["cmp_w2"], ng[1].reshape(1, HD))
    nb = n_cmp // CMP_PER_SLC
    nbp = -(-nb // LANES) * LANES
    cmp = cmp.reshape(b, 2, g, nb, CMP_PER_SLC, HD).transpose(0, 1, 2, 4, 3, 5)
    cmp = jnp.pad(cmp, ((0, 0),) * 4 + ((0, nbp - nb), (0, 0))).reshape(b, 2, g, CMP_PER_SLC * nbp, HD).astype(BF16)

    def group_q(cols):
        a = _heads(cols, b, t, N_NSA, HD, SCALE).reshape(b, g, NSA_GROUP, t, HD)
        return _stack_q(a, tq)

    q_n = group_q(proj[:, 0:512])
    q_r = group_q(proj[:, 512:1024])
    n_blk = -(-l_real // SLC_BLOCK)
    o_c, sel = _nsa_cmp(q_n, cmp[:, 0], cmp[:, 1], tq=tq, q_off=q_off, n_sel=min(N_SLC, n_blk))

    def kv_heads(a):
        return a.transpose(0, 2, 1, 3).astype(BF16)

    o_s = _flash(q_r, kv_heads(nsa5[:, :, 2]), kv_heads(nsa5[:, :, 3]), reps=NSA_GROUP, tq=tq, tk=cfg["tk"],
                 q_off=q_off, k_off=0, mode="causal", bm=sel)
    o_w = _flash(q_r, kv_heads(win_buf[:, :, 0]), kv_heads(win_buf[:, :, 1]), reps=NSA_GROUP, tq=tq,
                 tk=cfg["tk_win"], q_off=q_off, k_off=win_off, mode="window")

    dsa4 = dsa_buf.reshape(b, lk, 3, HD).astype(BF16)
    q_d = _stack_q(_heads(proj[:, 1792:2304], b, t, N_DSA, HD, SCALE)[:, None], tq)[:, 0]
    q_i = _stack_q(_heads(proj[:, 2304:2816], b, t, N_IDX, HD, IDX_SCALE)[:, None], tq)[:, 0]
    o_d = _dsa(q_i, w_i, dsa4[:, :, 2], q_d, dsa4[:, :, 0], dsa4[:, :, 1], tq=tq, tk=cfg["tk"], q_off=q_off,
               n_top=min(DSA_TOPK_MAX, l_real // 4))

    o_c, o_s, o_w = (_unstack_q(o, NSA_GROUP, tq) for o in (o_c, o_s, o_w))
    o_d = _unstack_q(o_d[:, None], N_DSA, tq)
    gfull = [jnp.repeat(gate[:, :, j], HD, axis=1) for j in range(3)]
    d = x2.shape[1]
    out = _mm([o_c, o_s, o_w, o_d] + gfull + [x2], [], prm["od_w_out"], _lhs_odd,
              [(0, d, None, None, [(0, False)])], d, tm=cfg["tm"], res_idx=7)
    return out, new_nsa, new_dsa, new_state


def _cross(x2, b, t, mem_k, mem_v, prm, cfg):
    d = x2.shape[1]
    xw = N_XH * HD
    q = _mm([x2], [prm["g1"], _tile_gain(prm["x_gq"])], prm["x_wq"], _lhs_norm,
            [(0, xw, 0, None, [(0, False)])], xw, tm=cfg["tm"], gains_idx=1)
    tq = cfg["tq"]
    qh = _stack_q(_heads(q, b, t, N_XH, HD, SCALE)[:, :, None], tq)
    o = _flash(qh, mem_k, mem_v, reps=1, tq=tq, tk=mem_k.shape[2], q_off=0, k_off=0, mode="full")
    o = _unstack_q(o, 1, tq)
    return _mm([o, x2], [], prm["x_wo"], _lhs_plain, [(0, d, None, None, [(0, False)])], d, tm=cfg["tm"], res_idx=1)


def _memory_kv(mem2, prm):
    xw = N_XH * HD
    w = jnp.concatenate([prm["x_wk"], prm["x_wv"]], axis=1)
    return _mm([mem2], [_tile_gain(prm["x_gk"])], w, _lhs_plain,
               [(0, xw, 0, None, [(0, False)]), (xw, xw, None, None, [(xw, False)])], 2 * xw,
               tm=min(256, mem2.shape[0]), gains_idx=0)


def _run_group(x, q_off, t_real, mem_kvs, pasts, layers, cfg):
    b, t, d = x.shape
    x2 = x.reshape(b * t, d)
    pos = q_off + jnp.arange(t, dtype=jnp.int32)
    cos, sin = _rope_tables(pos)
    cs = (jnp.tile(cos, (b, 1)), jnp.tile(sin, (b, 1)))
    outs = {}
    for li, prm in enumerate(layers):
        if li % 2 == 0:
            x2, nk, nv = _even_mixer(x2, b, t, q_off, cs, pasts[li], prm, cfg)
            outs["ek"], outs["ev"] = nk, nv
        else:
            x2, nn, nd, nw = _odd_mixer(x2, b, t, t_real, q_off, cs, pasts[li], prm, cfg)
            outs["on"], outs["od"], outs["ow"] = nn, nd, nw
        x2 = _cross(x2, b, t, mem_kvs[li][0], mem_kvs[li][1], prm, cfg)
        if li % 2 == 0:
            x2 = _ffn(x2, prm["g2"], prm["router"], prm["w1"], prm["w3"], prm["w2"], tm=cfg["tm_ffn"],
                      tf=cfg["tf"], routed=False)
        else:
            x2 = _ffn(x2, prm["g2"], prm["router"], prm["w1"], prm["w3"], prm["w2"], tm=cfg["tm_ffn"],
                      tf=cfg["tf"], routed=True)
    return x2.reshape(b, t, d), outs


def kernel(x_prompt, x_sample, mem_prompt, cache_even_k, cache_even_v, cache_odd_nsa, cache_odd_dsa, state_odd_win, cache_mem, page_table, norm_gain, ev_w_in, ev_w_out, df_qk_gain, df_lambda, df_subln_gain, ffn_w1, ffn_w3, ffn_w2, od_w_in, od_w_out, nsa_qk_gain, cmp_pe, cmp_w1, cmp_w2, dsa_qk_gain, moe_router, moe_w1, moe_w3, moe_w2, x_wq, x_wk, x_wv, x_wo, x_qk_gain):
    depth = norm_gain.shape[0]
    bp, tp, d = x_prompt.shape
    bs, ts, _ = x_sample.shape
    n_mem = mem_prompt.shape[1]
    xw = N_XH * HD

    layers = []
    for l in range(depth):
        i = l // 2
        prm = {
            "layer": l,
            "g0": norm_gain[l, 0].reshape(1, d), "g1": norm_gain[l, 1].reshape(1, d), "g2": norm_gain[l, 2].reshape(1, d),
            "x_wq": x_wq[l].astype(BF16), "x_wk": x_wk[l].astype(BF16), "x_wv": x_wv[l].astype(BF16),
            "x_wo": x_wo[l].astype(BF16), "x_gq": x_qk_gain[l, 0], "x_gk": x_qk_gain[l, 1],
        }
        if l % 2 == 0:
            prm.update({
                "ev_w_in": ev_w_in[i].astype(BF16), "ev_w_out": ev_w_out[i].astype(BF16),
                "df_qk_gain": df_qk_gain[i], "df_lambda": df_lambda[i], "df_subln_gain": df_subln_gain[i],
                "router": jnp.zeros((SUBLANES, LANES), F32),
                "w1": ffn_w1[i][None].astype(BF16), "w3": ffn_w3[i][None].astype(BF16), "w2": ffn_w2[i][None].astype(BF16),
            })
        else:
            prm.update({
                "od_w_in": od_w_in[i].astype(BF16), "od_w_out": od_w_out[i].astype(BF16),
                "nsa_qk_gain": nsa_qk_gain[i], "dsa_qk_gain": dsa_qk_gain[i],
                "cmp_pe": cmp_pe[i], "cmp_w1": cmp_w1[i].astype(BF16), "cmp_w2": cmp_w2[i].astype(BF16),
                "router": jnp.pad(moe_router[i], ((0, 0), (0, LANES - N_EXPERTS))),
                "w1": moe_w1[i].astype(BF16), "w3": moe_w3[i].astype(BF16), "w2": moe_w2[i].astype(BF16),
            })
        layers.append(prm)

    def mem_heads(kv, b):
        k = kv[:, :, 0].transpose(0, 2, 1, 3).astype(BF16)
        v = kv[:, :, 1].transpose(0, 2, 1, 3).astype(BF16)
        return k, v

    mem2 = mem_prompt.reshape(bp * n_mem, d)
    mem_p = [_memory_kv(mem2, layers[l]).reshape(bp, n_mem, 2, N_XH, HD) for l in range(depth)]
    ff = ffn_w1.shape[2]
    tf = ff // 2 if (ff // 2) % LANES == 0 else ff
    cfg_p = {"tm": 256, "tq": 128, "tk": 256, "t_even": 256, "tk_win": 128, "tm_ffn": 512, "tf": tf}
    cfg_p["tq"] = min(cfg_p["tq"], tp)
    y_prompt, op = _run_group(x_prompt, 0, tp, [mem_heads(m, bp) for m in mem_p], [None] * depth, layers, cfg_p)
    p_mem = jnp.stack(mem_p)

    n_past = page_table.shape[1] * cache_even_k.shape[2]
    ts_pad = -(-ts // SUBLANES) * SUBLANES
    xs = _pad_rows(x_sample, ts_pad)
    pasts = []
    for l in range(depth):
        i = l // 2
        if l % 2 == 0:
            pasts.append((cache_even_k[i], cache_even_v[i], page_table))
        else:
            pn = cache_odd_nsa[i]
            pd = cache_odd_dsa[i]
            pasts.append((pn.reshape(pn.shape[0], pn.shape[1], -1), pd.reshape(pd.shape[0], pd.shape[1], -1),
                          state_odd_win[i], page_table))
    cfg_s = {"tm": bs * ts_pad, "tq": ts_pad, "tk": 512, "pp": 4, "tk_win": 128, "tm_ffn": bs * ts_pad,
             "tf": tf, "new_rows": 512}
    y_s, os_ = _run_group(xs, n_past, ts, [mem_heads(cache_mem[l], bs) for l in range(depth)], pasts, layers, cfg_s)
    y_sample = y_s[:, :ts]

    g = N_NSA_KV
    return (
        y_prompt, y_sample,
        op["ek"][None], op["ev"][None],
        op["on"].reshape(1, bp, tp, 4, g, HD), op["od"].reshape(1, bp, tp, 3, HD),
        op["ow"][None], p_mem,
        os_["ek"][:, :ts][None], os_["ev"][:, :ts][None],
        os_["on"][:, :ts].reshape(1, bs, ts, 4, g, HD), os_["od"][:, :ts].reshape(1, bs, ts, 3, HD),
        os_["ow"][None],
    )
```

```python
import functools
import math

import jax
import jax.numpy as jnp
import numpy as np
from jax import lax
from jax.experimental import pallas as pl
from jax.experimental.pallas import tpu as pltpu

F32 = jnp.float32
BF16 = jnp.bfloat16

HD = 64
N_SB = 8
N_DF = 4
N_NSA = 8
N_NSA_KV = 2
NSA_GROUP = N_NSA // N_NSA_KV
N_DSA = 8
N_IDX = 8
N_XH = 4
N_EXPERTS = 8
ROPE_THETA = 10000.0
CMP_LEN = 32
CMP_STRIDE = 16
SLC_BLOCK = 64
CMP_PER_SLC = SLC_BLOCK // CMP_STRIDE
N_SLC = 16
N_LOCAL = 2
WINDOW = 512
DSA_TOPK_MAX = 256
EPS = 1e-6
NEG = -1e30
FORCE = 1e9
SCALE = HD ** -0.5
IDX_SCALE = HD ** -0.5
SB_W = N_SB * HD
HD_SHIFT = 6
SLC_SHIFT = 6
DF_W = N_DF * 2 * HD

LANES = 128
SUBLANES = 8
PAGE = 128
VMEM_LIMIT = 52 * 1024 * 1024
INT_MIN = -2 ** 31
ROW_SPLIT = 2
SB_CUT = 120.0
SEL_CHUNK_ROWS = 16
MIN_CHAIN_ROWS = 64
KEY_UNROLL = 4

_NT = (((1,), (1,)), ((), ()))


def _cparams(sem):
    return pltpu.CompilerParams(dimension_semantics=sem, vmem_limit_bytes=VMEM_LIMIT)


def _dot(a, b):
    return jnp.dot(a, b, preferred_element_type=F32)


def _dot_nt(a, b):
    return lax.dot_general(a, b, _NT, preferred_element_type=F32)


def _split_dot(x, m_bf16):
    hi = x.astype(BF16)
    lo = (x - hi.astype(F32)).astype(BF16)
    return _dot(hi, m_bf16) + _dot(lo, m_bf16)


def _iota(shape, dim):
    return lax.broadcasted_iota(jnp.int32, shape, dim)


def _rms_rows(x, g):
    return x * lax.rsqrt(jnp.mean(x * x, axis=-1, keepdims=True) + EPS) * g


def _group_mean_matrix():
    r = _iota((LANES, LANES), 0) >> HD_SHIFT
    c = _iota((LANES, LANES), 1) >> HD_SHIFT
    return jnp.where(r == c, 1.0 / HD, 0.0).astype(BF16)


def _head_norm(y, g, gm):
    ms = _split_dot(y * y, gm)
    return y * lax.rsqrt(ms + EPS) * g


def _rope_slab(y, cos, sin):
    lane = _iota(y.shape, 1)
    first = (lane & (HD - 1)) < (HD // 2)
    swapped = jnp.where(first, pltpu.roll(y, LANES - HD // 2, 1), pltpu.roll(y, HD // 2, 1))
    return y * cos + swapped * sin


def _mm_kernel(*refs, n_rows, n_consts, lhs_fn, program, rope_idx, gains_idx, res_idx):
    rows = refs[:n_rows]
    consts = refs[n_rows:n_rows + n_consts]
    w_ref = refs[n_rows + n_consts]
    o_ref = refs[-1]
    lhs = lhs_fn(rows, consts).astype(BF16)
    gm = _group_mean_matrix() if gains_idx is not None else None
    for (src, width, gain_row, act, outs) in program:
        y_full = _dot(lhs, w_ref[:, src:src + width])
        for s in range(width // LANES):
            y = y_full[:, s * LANES:(s + 1) * LANES]
            if gain_row is not None:
                y = _head_norm(y, consts[gains_idx][gain_row:gain_row + 1, :], gm)
            if act == "sigmoid":
                y = 1.0 / (1.0 + jnp.exp(-y))
            for (dst, rope) in outs:
                z = y
                if rope:
                    z = _rope_slab(y, rows[rope_idx[0]][...], rows[rope_idx[1]][...])
                d0 = dst + s * LANES
                if res_idx is not None:
                    z = z + rows[res_idx][:, d0:d0 + LANES]
                o_ref[:, d0:d0 + LANES] = z


def _mm(rows, consts, w, lhs_fn, program, out_cols, *, tm, rope_idx=None, gains_idx=None, res_idx=None, name="mm"):
    n = rows[0].shape[0]
    assert n % tm == 0
    in_specs = [pl.BlockSpec((tm, r.shape[1]), lambda i: (i, 0)) for r in rows]
    in_specs += [pl.BlockSpec(c.shape, lambda i: (0, 0)) for c in consts]
    in_specs += [pl.BlockSpec(w.shape, lambda i: (0, 0))]
    kern = functools.partial(_mm_kernel, n_rows=len(rows), n_consts=len(consts), lhs_fn=lhs_fn,
                             program=program, rope_idx=rope_idx, gains_idx=gains_idx, res_idx=res_idx)
    return pl.pallas_call(
        kern,
        grid=(n // tm,),
        in_specs=in_specs,
        out_specs=pl.BlockSpec((tm, out_cols), lambda i: (i, 0)),
        out_shape=jax.ShapeDtypeStruct((n, out_cols), F32),
        compiler_params=_cparams(("parallel",)),
        name=name,
    )(*rows, *consts, w)


def _lhs_norm(rows, consts):
    return _rms_rows(rows[0][...], consts[0][...])


def _lhs_plain(rows, consts):
    return rows[0][...]


def _lhs_cat2(rows, consts):
    return jnp.concatenate([rows[0][...], rows[1][...]], axis=1)


def _lhs_odd(rows, consts):
    oc, os_, ow, od = rows[0][...], rows[1][...], rows[2][...], rows[3][...]
    g0, g1, g2 = rows[4][...], rows[5][...], rows[6][...]
    return jnp.concatenate([g0 * oc + g1 * os_ + g2 * ow, od], axis=1)


def _ffn_kernel(x_ref, g_ref, r_ref, w1_ref, w3_ref, w2_ref, o_ref, h_ref, acc_ref, gate_ref, *, routed):
    e = pl.program_id(1)
    f = pl.program_id(2)
    first = jnp.logical_and(e == 0, f == 0)
    last = jnp.logical_and(e == pl.num_programs(1) - 1, f == pl.num_programs(2) - 1)

    @pl.when(first)
    def _():
        x = x_ref[...]
        h = _rms_rows(x, g_ref[...])
        h_ref[...] = h.astype(BF16)
        acc_ref[...] = x
        if routed:
            logits = jnp.dot(h, r_ref[...], preferred_element_type=F32, precision=lax.Precision.HIGHEST)
            col = _iota(logits.shape, 1).astype(F32)
            logits = jnp.where(col < N_EXPERTS, logits, -jnp.inf)
            m1 = jnp.max(logits, axis=1, keepdims=True)
            i1 = jnp.min(jnp.where(logits == m1, col, 1e9), axis=1, keepdims=True)
            rest = jnp.where(col == i1, -jnp.inf, logits)
            m2 = jnp.max(rest, axis=1, keepdims=True)
            i2 = jnp.min(jnp.where(rest == m2, col, 1e9), axis=1, keepdims=True)
            e2 = jnp.exp(m2 - m1)
            g1 = 1.0 / (1.0 + e2)
            g2 = e2 / (1.0 + e2)
            gate_ref[...] = jnp.where(col == i1, g1, 0.0) + jnp.where(col == i2, g2, 0.0)

    def compute(gcol):
        h = h_ref[...]
        u = _dot(h, w1_ref[0])
        v = _dot(h, w3_ref[0])
        a = (u * (1.0 / (1.0 + jnp.exp(-u)))) * v
        y = _dot(a.astype(BF16), w2_ref[0])
        if gcol is not None:
            y = gcol * y
        acc_ref[...] += y

    if routed:
        col = _iota(gate_ref.shape, 1)
        gcol = jnp.sum(jnp.where(col == e, gate_ref[...], 0.0), axis=1, keepdims=True)
        active = jnp.max(gcol) > 0.0

        @pl.when(active)
        def _():
            compute(gcol)
    else:
        compute(None)

    @pl.when(last)
    def _():
        o_ref[...] = acc_ref[...]


def _ffn(x, g, router, w1, w3, w2, *, tm, tf, routed):
    n, d = x.shape
    ne, _, ff = w1.shape
    assert n % tm == 0 and ff % tf == 0
    kern = functools.partial(_ffn_kernel, routed=routed)
    return pl.pallas_call(
        kern,
        grid=(n // tm, ne, ff // tf),
        in_specs=[
            pl.BlockSpec((tm, d), lambda i, e, f: (i, 0)),
            pl.BlockSpec((1, d), lambda i, e, f: (0, 0)),
            pl.BlockSpec(router.shape, lambda i, e, f: (0, 0)),
            pl.BlockSpec((1, d, tf), lambda i, e, f: (e, 0, f)),
            pl.BlockSpec((1, d, tf), lambda i, e, f: (e, 0, f)),
            pl.BlockSpec((1, tf, d), lambda i, e, f: (e, f, 0)),
        ],
        out_specs=pl.BlockSpec((tm, d), lambda i, e, f: (i, 0)),
        out_shape=jax.ShapeDtypeStruct((n, d), F32),
        scratch_shapes=[pltpu.VMEM((tm, d), BF16), pltpu.VMEM((tm, d), F32), pltpu.VMEM((tm, LANES), F32)],
        compiler_params=_cparams(("parallel", "arbitrary", "arbitrary")),
        name="moe" if routed else "ffn",
    )(x, g, router, w1, w3, w2)


def _tile_rows(m, reps):
    return m if reps == 1 else jnp.concatenate([m] * reps, axis=0)


def _online_step(s, v, m, l, acc):
    m_new = jnp.maximum(m, jnp.max(s, axis=1, keepdims=True))
    p = jnp.exp(s - m_new)
    alpha = jnp.exp(m - m_new)
    return m_new, alpha * l + jnp.sum(p, axis=1, keepdims=True), alpha * acc + _dot(p.astype(BF16), v)


def _online_init(rows, dv):
    return (jnp.full((rows, 1), NEG, F32), jnp.zeros((rows, 1), F32), jnp.zeros((rows, dv), F32))


def _unrolled_loop(lo, hi, body, init):
    shift = KEY_UNROLL.bit_length() - 1
    n_group = (hi - lo) >> shift

    def group(p, st):
        for u in range(KEY_UNROLL):
            st = body(lo + KEY_UNROLL * p + u, st)
        return st

    st = lax.fori_loop(0, n_group, group, init)
    return lax.fori_loop(lo + n_group * KEY_UNROLL, hi, body, st)


def _flash_kernel(*refs, reps, tq, tk, q_off, k_off, mode, has_bm):
    if has_bm:
        q_ref, k_ref, v_ref, bm_ref, o_ref = refs
    else:
        q_ref, k_ref, v_ref, o_ref = refs
    i = pl.program_id(2)
    q = q_ref[0, 0, 0]
    dv = v_ref.shape[-1]
    n_kt = k_ref.shape[2] // tk
    q_lo = q_off + i * tq
    if mode == "full":
        lo, hi = 0, n_kt
    else:
        hi = jnp.minimum(lax.div(q_lo + tq - 1 - k_off, tk) + 1, n_kt)
        lo = 0
        if mode == "window":
            lo = lax.div(jnp.maximum(q_lo - (WINDOW - 1) - k_off, 0), tk)
    qpos = q_lo + _iota((tq, tk), 0)
    if has_bm:
        bm = bm_ref[0, 0].astype(BF16)
        nbp = bm.shape[1]

    def body(j, state):
        k0 = pl.multiple_of(j * tk, tk)
        k = k_ref[0, 0, pl.ds(k0, tk), :]
        v = v_ref[0, 0, pl.ds(k0, tk), :]
        bias = None
        if mode != "full":
            kpos = k_off + j * tk + _iota((tq, tk), 1)
            ok = kpos <= qpos
            if mode == "window":
                ok = jnp.logical_and(ok, qpos - kpos < WINDOW)
                ok = jnp.logical_and(ok, kpos >= 0)
            bias = jnp.where(ok, 0.0, NEG)
        if has_bm:
            blk = _iota((nbp, tk), 0)
            tok = (j * tk + _iota((nbp, tk), 1)) >> SLC_SHIFT
            expand = jnp.where(blk == tok, 1.0, 0.0).astype(BF16)
            bias = jnp.where(_dot(bm, expand) > 0.5, bias, NEG)
        if bias is not None:
            bias = _tile_rows(bias, cr // tq)
        new = []
        for c in range(n_chain):
            s = _dot_nt(q[c * cr:(c + 1) * cr].astype(BF16), k)
            if bias is not None:
                s = s + bias
            new.append(_online_step(s, v, *state[c]))
        return tuple(new)

    n_chain = reps if tq >= MIN_CHAIN_ROWS else 1
    cr = reps * tq // n_chain
    state = _unrolled_loop(lo, hi, body, tuple(_online_init(cr, dv) for _ in range(n_chain)))
    for c in range(n_chain):
        o_ref[0, 0, 0, c * cr:(c + 1) * cr, :] = state[c][2] / state[c][1]


def _flash(q, k, v, *, reps, tq, tk, q_off, k_off, mode, bm=None, v_map=None):
    b, hk, nq, rows, _ = q.shape
    lk = k.shape[2]
    dv = v.shape[3]
    assert rows == reps * tq and lk % tk == 0
    if v_map is None:
        v_map = lambda h: h
    in_specs = [
        pl.BlockSpec((1, 1, 1, rows, HD), lambda b_, h, i: (b_, h, i, 0, 0)),
        pl.BlockSpec((1, 1, lk, HD), lambda b_, h, i: (b_, h, 0, 0)),
        pl.BlockSpec((1, 1, lk, dv), lambda b_, h, i: (b_, v_map(h), 0, 0)),
    ]
    args = [q, k, v]
    if bm is not None:
        in_specs.append(pl.BlockSpec((1, 1, tq, bm.shape[3]), lambda b_, h, i: (b_, h, i, 0)))
        args.append(bm)
    kern = functools.partial(_flash_kernel, reps=reps, tq=tq, tk=tk, q_off=q_off, k_off=k_off, mode=mode,
                             has_bm=bm is not None)
    return pl.pallas_call(
        kern,
        grid=(b, hk, nq),
        in_specs=in_specs,
        out_specs=pl.BlockSpec((1, 1, 1, rows, dv), lambda b_, h, i: (b_, h, i, 0, 0)),
        out_shape=jax.ShapeDtypeStruct((b, hk, nq, rows, dv), F32),
        compiler_params=_cparams(("parallel", "parallel", "arbitrary")),
        name="flash_" + mode + ("_blockmask" if bm is not None else ""),
    )(*args)


def _softplus(z):
    return jnp.maximum(z, 0.0) + jnp.log(1.0 + jnp.exp(-jnp.abs(z)))


def _later_matrix(n):
    return jnp.where(_iota((n, n), 0) > _iota((n, n), 1), 1.0, 0.0).astype(BF16)


def _sb_kernel(q_ref, k_ref, v_ref, o_ref, *, t):
    i = pl.program_id(2)
    th = t // ROW_SPLIT
    lane = _iota((th, LANES), 1)
    later = _later_matrix(t)
    qs = []
    for part in range(ROW_SPLIT):
        qf = q_ref[0, part * th:(part + 1) * th, :] * SCALE
        qs.append((jnp.where(lane < HD, qf, 0.0).astype(BF16), jnp.where(lane >= HD, qf, 0.0).astype(BF16)))

    def tile(j, carries, outs, masked):
        k0 = pl.multiple_of(j * t, t)
        k = k_ref[0, pl.ds(k0, t), :].astype(BF16)
        v = v_ref[0, pl.ds(k0, t), :].astype(BF16)
        new_carries, new_outs = [], []
        for part in range(ROW_SPLIT):
            heads = []
            for hd in range(2):
                carry = carries[2 * part + hd]
                z = _dot_nt(qs[part][hd], k)
                sp = _softplus(z)
                log_sig = z - sp
                if masked:
                    vis = _iota((th, t), 1) < part * th + _iota((th, t), 0)
                    sp = jnp.where(vis, sp, 0.0)
                a = jnp.exp(log_sig - _dot(sp.astype(BF16), later) - carry)
                if masked:
                    a = jnp.where(vis, a, 0.0)
                heads.append(_dot(a.astype(BF16), v))
                new_carries.append(carry + jnp.sum(sp, axis=1, keepdims=True))
            new_outs.append(outs[part] + jnp.where(lane < HD, heads[0], heads[1]))
        return tuple(new_carries), tuple(new_outs)

    def min_carry(carries):
        m = jnp.min(carries[0])
        for c in carries[1:]:
            m = jnp.minimum(m, jnp.min(c))
        return m

    zero = jnp.zeros((th, 1), F32)
    carries, outs = tile(i, (zero,) * (2 * ROW_SPLIT), (jnp.zeros((th, LANES), F32),) * ROW_SPLIT, True)

    def cond(st):
        return jnp.logical_and(st[0] < i, st[1] < SB_CUT)

    def body(st):
        carries, outs = tile(i - 1 - st[0], st[2], st[3], False)
        return st[0] + 1, min_carry(carries), carries, outs

    _, _, carries, outs = lax.while_loop(cond, body, (jnp.int32(0), min_carry(carries), carries, outs))
    for part in range(ROW_SPLIT):
        o_ref[0, part * th:(part + 1) * th, :] = outs[part]


def _sb(proj3, *, t, q_slab, k_slab, v_slab, n_slabs):
    b, tt, _ = proj3.shape
    assert tt % t == 0
    kern = functools.partial(_sb_kernel, t=t)
    return pl.pallas_call(
        kern,
        grid=(b, n_slabs, tt // t),
        in_specs=[
            pl.BlockSpec((1, t, LANES), lambda b_, p, i: (b_, i, q_slab + p)),
            pl.BlockSpec((1, tt, LANES), lambda b_, p, i: (b_, 0, k_slab + p)),
            pl.BlockSpec((1, tt, LANES), lambda b_, p, i: (b_, 0, v_slab + p)),
        ],
        out_specs=pl.BlockSpec((1, t, LANES), lambda b_, p, i: (b_, i, p)),
        out_shape=jax.ShapeDtypeStruct((b, tt, n_slabs * LANES), F32),
        compiler_params=_cparams(("parallel", "parallel", "arbitrary")),
        name="sb_prompt",
    )(proj3, proj3, proj3)


def _lam(lv, lam_init):
    a = jnp.sum(jnp.sum(lv[0:1] * lv[1:2], axis=1, keepdims=True), axis=0, keepdims=True)
    b = jnp.sum(jnp.sum(lv[2:3] * lv[3:4], axis=1, keepdims=True), axis=0, keepdims=True)
    return jnp.exp(a) - jnp.exp(b) + lam_init


def _df_kernel(lv_ref, g_ref, q_ref, k_ref, v_ref, o_ref, *, t, lam_init):
    i = pl.program_id(2)
    th = t // ROW_SPLIT
    lane = _iota((th, LANES), 1)
    qs = []
    for part in range(ROW_SPLIT):
        qf = q_ref[0, part * th:(part + 1) * th, :] * SCALE
        qs.append((jnp.where(lane < HD, qf, 0.0).astype(BF16), jnp.where(lane >= HD, qf, 0.0).astype(BF16)))

    def tile(j, state, masked):
        k0 = pl.multiple_of(j * t, t)
        k = k_ref[0, pl.ds(k0, t), :].astype(BF16)
        v = v_ref[0, pl.ds(k0, t), :].astype(BF16)
        new = []
        for part in range(ROW_SPLIT):
            for mp in range(2):
                s = _dot_nt(qs[part][mp], k)
                if masked:
                    s = jnp.where(_iota((th, t), 1) <= part * th + _iota((th, t), 0), s, NEG)
                new.append(_online_step(s, v, *state[2 * part + mp]))
        return tuple(new)

    init = tuple(_online_init(th, LANES) for _ in range(2 * ROW_SPLIT))
    state = _unrolled_loop(0, i, lambda j, st: tile(j, st, False), init)
    state = tile(i, state, True)
    lam = _lam(lv_ref[...], lam_init)
    for part in range(ROW_SPLIT):
        (_, l0, a0), (_, l1, a1) = state[2 * part], state[2 * part + 1]
        d = a0 / l0 - lam * (a1 / l1)
        o_ref[0, part * th:(part + 1) * th, :] = _rms_rows(d, g_ref[...]) * (1.0 - lam_init)


def _df(proj3, lv, subln, *, t, q_slab, k_slab, v_slab, n_slabs, lam_init):
    b, tt, _ = proj3.shape
    assert tt % t == 0
    kern = functools.partial(_df_kernel, t=t, lam_init=lam_init)
    return pl.pallas_call(
        kern,
        grid=(b, n_slabs, tt // t),
        in_specs=[
            pl.BlockSpec(lv.shape, lambda b_, p, i: (0, 0)),
            pl.BlockSpec(subln.shape, lambda b_, p, i: (0, 0)),
            pl.BlockSpec((1, t, LANES), lambda b_, p, i: (b_, i, q_slab + p)),
            pl.BlockSpec((1, tt, LANES), lambda b_, p, i: (b_, 0, k_slab + p)),
            pl.BlockSpec((1, tt, LANES), lambda b_, p, i: (b_, 0, v_slab + p)),
        ],
        out_specs=pl.BlockSpec((1, t, LANES), lambda b_, p, i: (b_, i, p)),
        out_shape=jax.ShapeDtypeStruct((b, tt, n_slabs * LANES), F32),
        compiler_params=_cparams(("parallel", "parallel", "arbitrary")),
        name="df_prompt",
    )(lv, subln, proj3, proj3, proj3)


def _dec_even_kernel(*refs, pp, ts, lam_init):
    tbl_ref, qt_ref = refs[0], refs[1]
    k_refs = refs[2:2 + pp]
    v_refs = refs[2 + pp:2 + 2 * pp]
    kn_ref, vn_ref, lv_ref, g_ref, o_ref, later_ref, st_ref, asb_ref, adf_ref = refs[2 + 2 * pp:]
    del tbl_ref
    s_id = pl.program_id(1)
    nk = pp * PAGE
    half = LANES // 2
    qt = qt_ref[0]

    def col_of(row):
        return jnp.transpose(jnp.broadcast_to(row, (SUBLANES, LANES)))[half:, 0:1]

    def tile(kt, vt, later, vis_sb, vis_df):
        carry, m, l = st_ref[0:1, :], st_ref[1:2, :], st_ref[2:3, :]
        zt = _dot(kt, qt)
        lane = _iota(zt.shape, 1)
        sp = _softplus(zt)
        log_sig = zt - sp
        s = zt
        if vis_sb is not None:
            sp = jnp.where(vis_sb, sp, 0.0)
            s = jnp.where(vis_df, s, NEG)
        a = jnp.exp(log_sig - _dot(later, sp.astype(BF16)) - carry)
        if vis_sb is not None:
            a = jnp.where(vis_sb, a, 0.0)
        m_new = jnp.maximum(m, jnp.max(s, axis=0, keepdims=True))
        p = jnp.exp(s - m_new)
        alpha = jnp.exp(m - m_new)
        st_ref[0:1, :] = carry + jnp.sum(sp, axis=0, keepdims=True)
        st_ref[1:2, :] = m_new
        st_ref[2:3, :] = alpha * l + jnp.sum(p, axis=0, keepdims=True)
        w = jnp.transpose(jnp.where(lane < half, a, p)).astype(BF16)
        asb_ref[...] += _dot(w[:half], vt[:, :SB_W])
        adf_ref[...] = col_of(alpha) * adf_ref[...] + _dot(w[half:], vt[:, SB_W:])

    @pl.when(s_id == 0)
    def _():
        later_ref[...] = jnp.where(_iota((nk, nk), 1) > _iota((nk, nk), 0), 1.0, 0.0).astype(BF16)
        st_ref[...] = jnp.where(_iota(st_ref.shape, 0) == 1, NEG, 0.0)
        asb_ref[...] = jnp.zeros_like(asb_ref)
        adf_ref[...] = jnp.zeros_like(adf_ref)
        key = _iota((PAGE, LANES), 0)
        tok = _iota((PAGE, LANES), 1) & (ts - 1)
        tile(kn_ref[0].astype(BF16), vn_ref[0].astype(BF16), later_ref[0:PAGE, 0:PAGE], key < tok, key <= tok)

    kt = jnp.concatenate([r[0].astype(BF16) for r in k_refs], axis=0)
    vt = jnp.concatenate([r[0].astype(BF16) for r in v_refs], axis=0)
    tile(kt, vt, later_ref[...], None, None)

    @pl.when(s_id == pl.num_programs(1) - 1)
    def _():
        row = _iota((half, SB_W), 0)
        lane = _iota((half, SB_W), 1)

        def fold(x):
            out = x[0:ts]
            for u in range(1, half // ts):
                out = out + x[u * ts:(u + 1) * ts]
            return out

        o_sb = fold(jnp.where((row >> 3) == (lane >> HD_SHIFT), asb_ref[...], 0.0))
        pn = adf_ref[...] / col_of(st_ref[2:3, :])
        same_head = (row >> 4) == (lane >> 7)
        o0 = fold(jnp.where(jnp.logical_and(same_head, ((row >> 3) & 1) == 0), pn, 0.0))
        o1 = fold(jnp.where(jnp.logical_and(same_head, ((row >> 3) & 1) == 1), pn, 0.0))
        d = o0 - _lam(lv_ref[...], lam_init) * o1
        parts = [o_sb]
        for h in range(N_DF):
            parts.append(_rms_rows(d[:, h * LANES:(h + 1) * LANES], g_ref[...]) * (1.0 - lam_init))
        o_ref[0] = jnp.concatenate(parts, axis=1)


def _dec_even(qt, pool_k, pool_v, table, k_new, v_new, lv, subln, *, pp, ts, lam_init):
    b, n_pages = table.shape
    width = pool_k.shape[2]
    assert n_pages % pp == 0 and ts == SUBLANES and k_new.shape[1] == PAGE
    n_steps = n_pages // pp

    def page_map(u):
        return lambda b_, s, t: (t[b_, n_pages - (s + 1) * pp + u], 0, 0)

    page_specs = [pl.BlockSpec((1, PAGE, width), page_map(u)) for u in range(pp)]
    new_spec = pl.BlockSpec((1, PAGE, width), lambda b_, s, t: (b_, 0, 0))
    grid_spec = pltpu.PrefetchScalarGridSpec(
        num_scalar_prefetch=1,
        grid=(b, n_steps),
        in_specs=[pl.BlockSpec((1,) + qt.shape[1:], lambda b_, s, t: (b_, 0, 0))] + page_specs + page_specs
        + [new_spec, new_spec, pl.BlockSpec(lv.shape, lambda b_, s, t: (0, 0)),
           pl.BlockSpec(subln.shape, lambda b_, s, t: (0, 0))],
        out_specs=pl.BlockSpec((1, ts, width), lambda b_, s, t: (b_, 0, 0)),
        scratch_shapes=[
            pltpu.VMEM((pp * PAGE, pp * PAGE), BF16),
            pltpu.VMEM((SUBLANES, LANES), F32),
            pltpu.VMEM((LANES // 2, SB_W), F32),
            pltpu.VMEM((LANES // 2, DF_W), F32),
        ],
    )
    kern = functools.partial(_dec_even_kernel, pp=pp, ts=ts, lam_init=lam_init)
    return pl.pallas_call(
        kern,
        grid_spec=grid_spec,
        out_shape=jax.ShapeDtypeStruct((b, ts, width), F32),
        compiler_params=_cparams(("parallel", "arbitrary")),
        name="dec_even",
    )(table, qt, *([pool_k] * pp), *([pool_v] * pp), k_new, v_new, lv, subln)


def _compress_kernel(t_ref, w1_ref, pe_ref, w2_ref, g_ref, o_ref):
    kind = pl.program_id(1)
    half = CMP_STRIDE * HD
    c = t_ref[0, 0, 0].astype(BF16)
    n = c.shape[0]
    a1 = _dot(c, w1_ref[0, :half, :])
    a2 = _dot(c, w1_ref[0, half:, :])
    a2 = jnp.where(_iota(a2.shape, 0) < n - 1, pltpu.roll(a2, n - 1, 0), 0.0)
    pe = _dot(jnp.broadcast_to(pe_ref[0], (SUBLANES, CMP_LEN * HD)).astype(BF16), w1_ref[0])[0:1]
    hid = a1 + a2 + pe
    hid = hid * (1.0 / (1.0 + jnp.exp(-hid)))
    out = _dot(hid.astype(BF16), w2_ref[0])
    normed = _rms_rows(out, g_ref[...])
    o_ref[0, 0, 0] = jnp.where(kind == 0, normed, out)


def _compress(t, w1, pe_flat, w2, gain):
    b, _, g, n, width = t.shape
    return pl.pallas_call(
        _compress_kernel,
        grid=(b, 2, g),
        in_specs=[
            pl.BlockSpec((1, 1, 1, n, width), lambda b_, k, g_: (b_, k, g_, 0, 0)),
            pl.BlockSpec((1,) + w1.shape[1:], lambda b_, k, g_: (k, 0, 0)),
            pl.BlockSpec((1,) + pe_flat.shape[1:], lambda b_, k, g_: (k, 0, 0)),
            pl.BlockSpec((1,) + w2.shape[1:], lambda b_, k, g_: (k, 0, 0)),
            pl.BlockSpec(gain.shape, lambda b_, k, g_: (0, 0)),
        ],
        out_specs=pl.BlockSpec((1, 1, 1, n, HD), lambda b_, k, g_: (b_, k, g_, 0, 0)),
        out_shape=jax.ShapeDtypeStruct((b, 2, g, n, HD), F32),
        compiler_params=_cparams(("parallel", "arbitrary", "arbitrary")),
        name="compress",
    )(t, w1, pe_flat, w2, gain)


def _nsa_cmp_kernel(q_ref, kc_ref, vc_ref, o_ref, sel_ref, *, tq, q_off, n_sel):
    i = pl.program_id(2)
    q = q_ref[0, 0, 0].astype(BF16)
    kc = kc_ref[0, 0]
    vc = vc_ref[0, 0]
    ncol = kc.shape[0]
    nbp = ncol // CMP_PER_SLC
    q_lo = q_off + i * tq
    qpos = q_lo + _iota((tq, ncol), 0)
    col = _iota((tq, ncol), 1)
    jj = jnp.zeros_like(col)
    for u in range(1, CMP_PER_SLC):
        jj = jj + jnp.where(col >= u * nbp, 1, 0)
    c_end = (col - jj * nbp) * SLC_BLOCK + jj * CMP_STRIDE + (CMP_LEN - 1)
    maskf = _tile_rows(jnp.where(c_end <= qpos, 1.0, 0.0), NSA_GROUP)
    keep = maskf > 0.5
    s = jnp.where(keep, _dot_nt(q, kc), NEG)
    m = jnp.max(s, axis=1, keepdims=True)
    p = jnp.where(keep, jnp.exp(s - m), 0.0)
    p = p / jnp.maximum(jnp.sum(p, axis=1, keepdims=True), 1e-30)
    o_ref[0, 0, 0] = _dot(p.astype(BF16), vc)
    pg = p[0:tq]
    for r in range(1, NSA_GROUP):
        pg = pg + p[r * tq:(r + 1) * tq]
    imp = pg[:, 0:nbp]
    for j in range(1, CMP_PER_SLC):
        imp = imp + pg[:, j * nbp:(j + 1) * nbp]
    blk = _iota((tq, nbp), 1)
    qp = q_lo + _iota((tq, nbp), 0)
    q_blk = qp >> SLC_SHIFT
    visible = blk * SLC_BLOCK <= qp
    forced = jnp.logical_or(blk == 0, jnp.logical_and(blk <= q_blk, blk > q_blk - N_LOCAL))
    score = jnp.where(visible, jnp.where(forced, FORCE, imp), NEG)
    rc = min(tq, SEL_CHUNK_ROWS)
    blkf = _iota((rc, nbp), 1).astype(F32)
    scores = [score[c * rc:(c + 1) * rc] for c in range(tq // rc)]
    sels = [jnp.zeros((rc, nbp), F32) for _ in scores]
    for _ in range(n_sel):
        for c in range(len(scores)):
            top = jnp.max(scores[c], axis=1, keepdims=True)
            idx = jnp.min(jnp.where(scores[c] == top, blkf, 1e9), axis=1, keepdims=True)
            pick = blkf == idx
            sels[c] = jnp.where(pick, 1.0, sels[c])
            scores[c] = jnp.where(pick, -jnp.inf, scores[c])
    for c in range(len(scores)):
        sel_ref[0, 0, c * rc:(c + 1) * rc, :] = sels[c]


def _nsa_cmp(q, kc, vc, *, tq, q_off, n_sel):
    b, g, nq, rows, _ = q.shape
    ncol = kc.shape[2]
    nbp = ncol // CMP_PER_SLC
    kern = functools.partial(_nsa_cmp_kernel, tq=tq, q_off=q_off, n_sel=n_sel)
    return pl.pallas_call(
        kern,
        grid=(b, g, nq),
        in_specs=[
            pl.BlockSpec((1, 1, 1, rows, HD), lambda b_, g_, i: (b_, g_, i, 0, 0)),
            pl.BlockSpec((1, 1, ncol, HD), lambda b_, g_, i: (b_, g_, 0, 0)),
            pl.BlockSpec((1, 1, ncol, HD), lambda b_, g_, i: (b_, g_, 0, 0)),
        ],
        out_specs=[
            pl.BlockSpec((1, 1, 1, rows, HD), lambda b_, g_, i: (b_, g_, i, 0, 0)),
            pl.BlockSpec((1, 1, tq, nbp), lambda b_, g_, i: (b_, g_, i, 0)),
        ],
        out_shape=[
            jax.ShapeDtypeStruct((b, g, nq, rows, HD), F32),
            jax.ShapeDtypeStruct((b, g, nq * tq, nbp), F32),
        ],
        compiler_params=_cparams(("parallel", "parallel", "arbitrary")),
        name="nsa_cmp",
    )(q, kc, vc)


def _dsa_kernel(qi_ref, wi_ref, ki_ref, qd_ref, kd_ref, vd_ref, o_ref, key_ref, *, tq, tk, q_off, n_top):
    i = pl.program_id(1)
    qi = qi_ref[0, 0]
    qd = qd_ref[0, 0]
    w = wi_ref[0] * (N_IDX ** -0.5)
    n_kt = ki_ref.shape[1] // tk
    q_lo = q_off + i * tq
    hi = jnp.minimum(lax.div(q_lo + tq - 1, tk) + 1, n_kt)
    qpos = q_lo + _iota((tq, tk), 0)
    n_chain = N_DSA if tq >= MIN_CHAIN_ROWS else 1
    cr = N_DSA * tq // n_chain

    def visible(j):
        return (j * tk + _iota((tq, tk), 1)) <= qpos

    def score_tile(j, _):
        k0 = pl.multiple_of(j * tk, tk)
        k = ki_ref[0, pl.ds(k0, tk), :]
        tot = None
        for c in range(n_chain):
            sc = jnp.maximum(_dot_nt(qi[c * cr:(c + 1) * cr].astype(BF16), k), 0.0)
            for u in range(cr // tq):
                h = c * (cr // tq) + u
                term = w[:, h:h + 1] * sc[u * tq:(u + 1) * tq]
                tot = term if tot is None else tot + term
        tot = jnp.where(visible(j), tot, NEG)
        bits = pltpu.bitcast(tot, jnp.int32)
        key = jnp.where(bits < 0, bits ^ jnp.int32(0x7FFFFFFF), bits)
        key_ref[:, pl.ds(k0, tk)] = jnp.where(tot == 0.0, 0, key)
        return 0

    _unrolled_loop(0, hi, score_tile, 0)

    def count_ge(c):
        def body(j, acc):
            blk = key_ref[:, pl.ds(pl.multiple_of(j * tk, tk), tk)]
            hit = jnp.where(blk >= c, 1.0, 0.0)
            part = hit[:, 0:LANES]
            for u in range(1, tk // LANES):
                part = part + hit[:, u * LANES:(u + 1) * LANES]
            return acc + part
        acc = _unrolled_loop(0, hi, body, jnp.zeros((tq, LANES), F32))
        return jnp.sum(acc, axis=1, keepdims=True)

    kf = float(n_top)
    tau = jnp.where(count_ge(jnp.zeros((tq, 1), jnp.int32)) >= kf, 0, INT_MIN).astype(jnp.int32)

    def bit_body(t, tau):
        cand = tau + jnp.left_shift(jnp.int32(1), 30 - t)
        return jnp.where(count_ge(cand) >= kf, cand, tau)

    tau = lax.fori_loop(0, 31, bit_body, tau)
    need = kf - count_ge(tau + 1)
    before = jnp.where(_iota((tk, tk), 0) < _iota((tk, tk), 1), 1.0, 0.0).astype(BF16)

    def attend(j, carry):
        state, n_eq = carry
        k0 = pl.multiple_of(j * tk, tk)
        key = key_ref[:, pl.ds(k0, tk)]
        eqf = jnp.where(key == tau, 1.0, 0.0)
        rank = n_eq + _dot(eqf.astype(BF16), before)
        kept = jnp.logical_or(key > tau, jnp.logical_and(key == tau, rank < need))
        bias = jnp.where(jnp.logical_and(kept, visible(j)), 0.0, NEG)
        k = kd_ref[0, pl.ds(k0, tk), :]
        v = vd_ref[0, pl.ds(k0, tk), :]
        bias = _tile_rows(bias, cr // tq)
        new = []
        for c in range(n_chain):
            s = _dot_nt(qd[c * cr:(c + 1) * cr].astype(BF16), k) + bias
            new.append(_online_step(s, v, *state[c]))
        return tuple(new), n_eq + jnp.sum(eqf, axis=1, keepdims=True)

    init = (tuple(_online_init(cr, HD) for _ in range(n_chain)), jnp.zeros((tq, 1), F32))
    state, _ = _unrolled_loop(0, hi, attend, init)
    for c in range(n_chain):
        o_ref[0, 0, c * cr:(c + 1) * cr, :] = state[c][2] / state[c][1]


def _dsa(qi, wi, ki, qd, kd, vd, *, tq, tk, q_off, n_top):
    b, nq, rows, _ = qi.shape
    lk = ki.shape[1]
    assert lk % tk == 0
    kern = functools.partial(_dsa_kernel, tq=tq, tk=tk, q_off=q_off, n_top=n_top)
    qspec = pl.BlockSpec((1, 1, rows, HD), lambda b_, i: (b_, i, 0, 0))
    kspec = pl.BlockSpec((1, lk, HD), lambda b_, i: (b_, 0, 0))
    return pl.pallas_call(
        kern,
        grid=(b, nq),
        in_specs=[qspec, pl.BlockSpec((1, tq, N_IDX), lambda b_, i: (b_, i, 0)), kspec, qspec, kspec, kspec],
        out_specs=pl.BlockSpec((1, 1, rows, HD), lambda b_, i: (b_, i, 0, 0)),
        out_shape=jax.ShapeDtypeStruct((b, nq, rows, HD), F32),
        scratch_shapes=[pltpu.VMEM((tq, lk), jnp.int32)],
        compiler_params=_cparams(("parallel", "arbitrary")),
        name="dsa",
    )(qi, wi, ki, qd, kd, vd)


def _gather_kernel(*refs, pp, n_steps):
    pool_refs = refs[1:1 + pp]
    new_ref, o_ref = refs[1 + pp], refs[2 + pp]
    s = pl.program_id(1)

    @pl.when(s < n_steps)
    def _():
        for u in range(pp):
            o_ref[0, u * PAGE:(u + 1) * PAGE, :] = pool_refs[u][0]

    @pl.when(s >= n_steps)
    def _():
        o_ref[...] = new_ref[...]


def _page_gather(pool, table, new, *, pp):
    b, n_pages = table.shape
    width = pool.shape[2]
    assert n_pages % pp == 0 and new.shape[1] == pp * PAGE
    n_steps = n_pages // pp

    def page_map(u):
        return lambda b_, s, t: (t[b_, jnp.minimum(s, n_steps - 1) * pp + u], 0, 0)

    kern = functools.partial(_gather_kernel, pp=pp, n_steps=n_steps)
    grid_spec = pltpu.PrefetchScalarGridSpec(
        num_scalar_prefetch=1,
        grid=(b, n_steps + 1),
        in_specs=[pl.BlockSpec((1, PAGE, width), page_map(u)) for u in range(pp)]
        + [pl.BlockSpec((1, pp * PAGE, width), lambda b_, s, t: (b_, 0, 0))],
        out_specs=pl.BlockSpec((1, pp * PAGE, width), lambda b_, s, t: (b_, s, 0)),
    )
    return pl.pallas_call(
        kern,
        grid_spec=grid_spec,
        out_shape=jax.ShapeDtypeStruct((b, (n_pages + pp) * PAGE, width), pool.dtype),
        compiler_params=_cparams(("parallel", "arbitrary")),
        name="page_gather",
    )(table, *([pool] * pp), new)


def _rope_tables(pos):
    half = HD // 2
    inv = ROPE_THETA ** (-jnp.arange(half, dtype=F32) / half)
    ang = pos.astype(F32)[:, None] * inv[None, :]
    cos, sin = jnp.cos(ang), jnp.sin(ang)
    cos128 = jnp.tile(jnp.concatenate([cos, cos], axis=1), (1, LANES // HD))
    sin128 = jnp.tile(jnp.concatenate([-sin, sin], axis=1), (1, LANES // HD))
    return cos128, sin128


def _tile_gain(g):
    return jnp.tile(g.reshape(1, HD), (1, LANES // HD))


def _heads(a, b, t, h, d, scale=None):
    a = a.reshape(b, t, h, d)
    if scale is not None:
        a = a * scale
    return a.transpose(0, 2, 1, 3)


def _stack_q(a, tq):
    b, hk, r, t, d = a.shape
    return a.reshape(b, hk, r, t // tq, tq, d).transpose(0, 1, 3, 2, 4, 5).reshape(b, hk, t // tq, r * tq, d)


def _unstack_q(a, r, tq):
    b, hk, nq, _, d = a.shape
    a = a.reshape(b, hk, nq, r, tq, d).transpose(0, 2, 4, 1, 3, 5)
    return a.reshape(b * nq * tq, hk * r * d)


def _pad_rows(a, rows):
    return jnp.pad(a, ((0, 0), (0, rows - a.shape[1])) + ((0, 0),) * (a.ndim - 2))


def _even_mixer(x2, b, t, q_off, cs, past, prm, cfg):
    n = b * t
    program = [
        (0, SB_W, None, None, [(0, False)]),
        (3 * SB_W, DF_W, 0, None, [(SB_W, True)]),
        (SB_W, SB_W, None, None, [(SB_W + DF_W, False)]),
        (3 * SB_W + DF_W, DF_W, 1, None, [(2 * SB_W + DF_W, True)]),
        (2 * SB_W, SB_W, None, None, [(2 * (SB_W + DF_W), False)]),
        (3 * SB_W + 2 * DF_W, DF_W, None, None, [(3 * SB_W + 2 * DF_W, False)]),
    ]
    gains = jnp.concatenate([_tile_gain(prm["df_qk_gain"][0]), _tile_gain(prm["df_qk_gain"][1])], axis=0)
    proj = _mm([x2, cs[0], cs[1]], [prm["g0"], gains], prm["ev_w_in"], _lhs_norm, program, 3 * (SB_W + DF_W),
               tm=cfg["tm"], rope_idx=(1, 2), gains_idx=1)
    mw = SB_W + DF_W
    new_k = proj[:, mw:2 * mw].reshape(b, t, mw)
    new_v = proj[:, 2 * mw:3 * mw].reshape(b, t, mw)
    lam_init = 0.8 - 0.6 * math.exp(-0.3 * prm["layer"])
    lv, subln = prm["df_lambda"], prm["df_subln_gain"].reshape(1, 2 * HD)
    d = x2.shape[1]
    if past is None:
        proj3 = proj.reshape(b, t, 3 * mw)
        ns = SB_W // LANES
        o_sb = _sb(proj3, t=cfg["t_even"], q_slab=0, k_slab=2 * ns, v_slab=4 * ns, n_slabs=ns)
        o_df = _df(proj3, lv, subln, t=cfg["t_even"], q_slab=ns, k_slab=3 * ns, v_slab=5 * ns, n_slabs=ns,
                   lam_init=lam_init)
        return _mm([o_sb.reshape(n, SB_W), o_df.reshape(n, DF_W), x2], [], prm["ev_w_out"], _lhs_cat2,
                   [(0, d, None, None, [(0, False)])], d, tm=cfg["tm"], res_idx=2), new_k, new_v
    pool_k, pool_v, table = past
    qcat = proj[:, :mw].reshape(b, t, mw).transpose(0, 2, 1) * SCALE
    qt = (jnp.tile(qcat, (1, 1, LANES // t)) * _dec_even_mask(t)).astype(BF16)
    o = _dec_even(qt, pool_k, pool_v, table, _pad_rows(new_k, PAGE), _pad_rows(new_v, PAGE), lv, subln,
                  pp=cfg["pp"], ts=t, lam_init=lam_init)
    return _mm([o.reshape(n, mw), x2], [], prm["ev_w_out"], _lhs_plain,
               [(0, d, None, None, [(0, False)])], d, tm=cfg["tm"], res_idx=1), new_k, new_v


def _dec_even_mask(ts):
    f = np.arange(SB_W + DF_W)[:, None]
    c = np.arange(LANES)[None, :]
    half = LANES // 2
    sb = (f < SB_W) & (c < half) & (f // HD == c // ts)
    df = (f >= SB_W) & (c >= half) & ((f - SB_W) // HD == (c - half) // ts)
    return jnp.asarray((sb | df).astype(np.float32))


def _odd_layout():
    widths = (N_NSA * HD, 128, 128, 128, 128, 128, 128, N_NSA * 3, N_DSA * HD, HD, HD, N_IDX * HD, HD, N_IDX)
    offs = np.concatenate([[0], np.cumsum(widths)])
    (q_n, k_c, v_c, k_s, v_s, k_w, v_w, gate, q_d, k_d, v_d, q_i, k_i, w_i) = [
        (int(offs[j]), int(offs[j + 1])) for j in range(len(widths))]
    pieces = [q_n, (k_c[0], v_w[1]), q_d, q_i, k_d, k_i, v_d, w_i, ("pad", HD - N_IDX), gate,
              ("pad", LANES - N_NSA * 3)]
    program = [
        (0, 512, 0, None, [(0, False), (512, True)]),
        (512, 128, None, None, [(1024, False)]),
        (640, 128, None, None, [(1152, False)]),
        (768, 128, 1, None, [(1280, True)]),
        (896, 128, None, None, [(1408, False)]),
        (1024, 128, 2, None, [(1536, True)]),
        (1152, 128, None, None, [(1664, False)]),
        (1280, 512, 3, None, [(1792, True)]),
        (1792, 512, None, None, [(2304, True)]),
        (2304, 128, 4, None, [(2816, True)]),
        (2432, 128, None, None, [(2944, False)]),
        (2560, 128, None, "sigmoid", [(3072, False)]),
    ]
    return pieces, program, 3200


def _permute_cols(w, pieces):
    cols = []
    for p in pieces:
        if p[0] == "pad":
            cols.append(jnp.zeros((w.shape[0], p[1]), w.dtype))
        else:
            cols.append(w[:, p[0]:p[1]])
    return jnp.concatenate(cols, axis=1)


def _odd_mixer(x2, b, t, t_real, q_off, cs, past, prm, cfg):
    n = b * t
    pieces, program, out_cols = _odd_layout()
    ng, dg = prm["nsa_qk_gain"], prm["dsa_qk_gain"]
    gains = jnp.concatenate([
        _tile_gain(ng[0]), _tile_gain(ng[2]), _tile_gain(ng[3]), _tile_gain(dg[0]),
        jnp.concatenate([dg[1], dg[2]]).reshape(1, LANES)], axis=0)
    w_in = _permute_cols(prm["od_w_in"], pieces)
    proj = _mm([x2, cs[0], cs[1]], [prm["g0"], gains], w_in, _lhs_norm, program, out_cols,
               tm=cfg["tm"], rope_idx=(1, 2), gains_idx=1)
    g = N_NSA_KV
    new_nsa = proj[:, 1024:1536].reshape(b, t, 4 * g * HD)
    new_win = proj[:, 1536:1792].reshape(b, t, 2, g, HD)
    new_dsa = jnp.concatenate([proj[:, 2816:2880], proj[:, 2944:3008], proj[:, 2880:2944]], axis=1).reshape(b, t, 3 * HD)
    w_i = proj[:, 3008:3008 + N_IDX].reshape(b, t, N_IDX)
    gate = proj[:, 3072:3072 + N_NSA * 3].reshape(n, N_NSA, 3)
    if past is None:
        nsa_buf, dsa_buf = new_nsa, new_dsa
        win_buf = new_win
        win_off = 0
        new_state = new_win[:, -min(WINDOW, t):]
        l_real = t
    else:
        pool_nsa, pool_dsa, state, table = past
        nsa_buf = _page_gather(pool_nsa, table, _pad_rows(new_nsa, cfg["pp"] * PAGE), pp=cfg["pp"])
        dsa_buf = _page_gather(pool_dsa, table, _pad_rows(new_dsa, cfg["pp"] * PAGE), pp=cfg["pp"])
        wb = state.shape[1]
        win_buf = _pad_rows(jnp.concatenate([state, new_win], axis=1), wb + cfg["tk_win"])
        win_off = q_off - wb
        new_state = jnp.concatenate([state, new_win[:, :t_real]], axis=1)[:, -wb:]
        l_real = table.shape[1] * PAGE + t_real
    lk = nsa_buf.shape[1]
    nsa5 = nsa_buf.reshape(b, lk, 4, g, HD)
    tq = cfg["tq"]

    n_cmp = lk // CMP_STRIDE
    t_c = nsa5[:, :, 0:2].transpose(0, 2, 3, 1, 4).reshape(b, 2, g, n_cmp, CMP_STRIDE * HD)
    pe_flat = prm["cmp_pe"].reshape(2, 1, CMP_LEN * HD)
    cmp = _compress(t_c, prm["cmp_w1"], pe_flat, prm["cmp_w2"], ng[1].reshape(1, HD))
    nb = n_cmp // CMP_PER_SLC
    nbp = -(-nb // LANES) * LANES
    cmp = cmp.reshape(b, 2, g, nb, CMP_PER_SLC, HD).transpose(0, 1, 2, 4, 3, 5)
    cmp = jnp.pad(cmp, ((0, 0),) * 4 + ((0, nbp - nb), (0, 0))).reshape(b, 2, g, CMP_PER_SLC * nbp, HD).astype(BF16)

    def group_q(cols):
        a = _heads(cols, b, t, N_NSA, HD, SCALE).reshape(b, g, NSA_GROUP, t, HD)
        return _stack_q(a, tq)

    q_n = group_q(proj[:, 0:512])
    q_r = group_q(proj[:, 512:1024])
    n_blk = -(-l_real // SLC_BLOCK)
    o_c, sel = _nsa_cmp(q_n, cmp[:, 0], cmp[:, 1], tq=tq, q_off=q_off, n_sel=min(N_SLC, n_blk))

    def kv_heads(a):
        return a.transpose(0, 2, 1, 3).astype(BF16)

    o_s = _flash(q_r, kv_heads(nsa5[:, :, 2]), kv_heads(nsa5[:, :, 3]), reps=NSA_GROUP, tq=tq, tk=cfg["tk"],
                 q_off=q_off, k_off=0, mode="causal", bm=sel)
    o_w = _flash(q_r, kv_heads(win_buf[:, :, 0]), kv_heads(win_buf[:, :, 1]), reps=NSA_GROUP, tq=tq,
                 tk=cfg["tk_win"], q_off=q_off, k_off=win_off, mode="window")

    dsa4 = dsa_buf.reshape(b, lk, 3, HD).astype(BF16)
    q_d = _stack_q(_heads(proj[:, 1792:2304], b, t, N_DSA, HD, SCALE)[:, None], tq)[:, 0]
    q_i = _stack_q(_heads(proj[:, 2304:2816], b, t, N_IDX, HD, IDX_SCALE)[:, None], tq)[:, 0]
    o_d = _dsa(q_i, w_i, dsa4[:, :, 2], q_d, dsa4[:, :, 0], dsa4[:, :, 1], tq=tq, tk=cfg["tk"], q_off=q_off,
               n_top=min(DSA_TOPK_MAX, l_real // 4))

    o_c, o_s, o_w = (_unstack_q(o, NSA_GROUP, tq) for o in (o_c, o_s, o_w))
    o_d = _unstack_q(o_d[:, None], N_DSA, tq)
    gfull = [jnp.repeat(gate[:, :, j], HD, axis=1) for j in range(3)]
    d = x2.shape[1]
    out = _mm([o_c, o_s, o_w, o_d] + gfull + [x2], [], prm["od_w_out"], _lhs_odd,
              [(0, d, None, None, [(0, False)])], d, tm=cfg["tm"], res_idx=7)
    return out, new_nsa, new_dsa, new_state


def _cross(x2, b, t, mem_k, mem_v, prm, cfg):
    d = x2.shape[1]
    xw = N_XH * HD
    q = _mm([x2], [prm["g1"], _tile_gain(prm["x_gq"])], prm["x_wq"], _lhs_norm,
            [(0, xw, 0, None, [(0, False)])], xw, tm=cfg["tm"], gains_idx=1)
    tq = cfg["tq"]
    qh = _stack_q(_heads(q, b, t, N_XH, HD, SCALE)[:, :, None], tq)
    o = _flash(qh, mem_k, mem_v, reps=1, tq=tq, tk=mem_k.shape[2], q_off=0, k_off=0, mode="full")
    o = _unstack_q(o, 1, tq)
    return _mm([o, x2], [], prm["x_wo"], _lhs_plain, [(0, d, None, None, [(0, False)])], d, tm=cfg["tm"], res_idx=1)


def _memory_kv(mem2, prm):
    xw = N_XH * HD
    w = jnp.concatenate([prm["x_wk"], prm["x_wv"]], axis=1)
    return _mm([mem2], [_tile_gain(prm["x_gk"])], w, _lhs_plain,
               [(0, xw, 0, None, [(0, False)]), (xw, xw, None, None, [(xw, False)])], 2 * xw,
               tm=min(256, mem2.shape[0]), gains_idx=0)


def _run_group(x, q_off, t_real, mem_kvs, pasts, layers, cfg):
    b, t, d = x.shape
    x2 = x.reshape(b * t, d)
    pos = q_off + jnp.arange(t, dtype=jnp.int32)
    cos, sin = _rope_tables(pos)
    cs = (jnp.tile(cos, (b, 1)), jnp.tile(sin, (b, 1)))
    outs = {}
    for li, prm in enumerate(layers):
        if li % 2 == 0:
            x2, nk, nv = _even_mixer(x2, b, t, q_off, cs, pasts[li], prm, cfg)
            outs["ek"], outs["ev"] = nk, nv
        else:
            x2, nn, nd, nw = _odd_mixer(x2, b, t, t_real, q_off, cs, pasts[li], prm, cfg)
            outs["on"], outs["od"], outs["ow"] = nn, nd, nw
        x2 = _cross(x2, b, t, mem_kvs[li][0], mem_kvs[li][1], prm, cfg)
        if li % 2 == 0:
            x2 = _ffn(x2, prm["g2"], prm["router"], prm["w1"], prm["w3"], prm["w2"], tm=cfg["tm_ffn"],
                      tf=cfg["tf"], routed=False)
        else:
            x2 = _ffn(x2, prm["g2"], prm["router"], prm["w1"], prm["w3"], prm["w2"], tm=cfg["tm_ffn"],
                      tf=cfg["tf"], routed=True)
    return x2.reshape(b, t, d), outs


def kernel(x_prompt, x_sample, mem_prompt, cache_even_k, cache_even_v, cache_odd_nsa, cache_odd_dsa, state_odd_win, cache_mem, page_table, norm_gain, ev_w_in, ev_w_out, df_qk_gain, df_lambda, df_subln_gain, ffn_w1, ffn_w3, ffn_w2, od_w_in, od_w_out, nsa_qk_gain, cmp_pe, cmp_w1, cmp_w2, dsa_qk_gain, moe_router, moe_w1, moe_w3, moe_w2, x_wq, x_wk, x_wv, x_wo, x_qk_gain):
    depth = norm_gain.shape[0]
    bp, tp, d = x_prompt.shape
    bs, ts, _ = x_sample.shape
    n_mem = mem_prompt.shape[1]
    xw = N_XH * HD

    layers = []
    for l in range(depth):
        i = l // 2
        prm = {
            "layer": l,
            "g0": norm_gain[l, 0].reshape(1, d), "g1": norm_gain[l, 1].reshape(1, d), "g2": norm_gain[l, 2].reshape(1, d),
            "x_wq": x_wq[l].astype(BF16), "x_wk": x_wk[l].astype(BF16), "x_wv": x_wv[l].astype(BF16),
            "x_wo": x_wo[l].astype(BF16), "x_gq": x_qk_gain[l, 0], "x_gk": x_qk_gain[l, 1],
        }
        if l % 2 == 0:
            prm.update({
                "ev_w_in": ev_w_in[i].astype(BF16), "ev_w_out": ev_w_out[i].astype(BF16),
                "df_qk_gain": df_qk_gain[i], "df_lambda": df_lambda[i], "df_subln_gain": df_subln_gain[i],
                "router": jnp.zeros((SUBLANES, LANES), F32),
                "w1": ffn_w1[i][None].astype(BF16), "w3": ffn_w3[i][None].astype(BF16), "w2": ffn_w2[i][None].astype(BF16),
            })
        else:
            prm.update({
                "od_w_in": od_w_in[i].astype(BF16), "od_w_out": od_w_out[i].astype(BF16),
                "nsa_qk_gain": nsa_qk_gain[i], "dsa_qk_gain": dsa_qk_gain[i],
                "cmp_pe": cmp_pe[i], "cmp_w1": cmp_w1[i].astype(BF16), "cmp_w2": cmp_w2[i].astype(BF16),
                "router": jnp.pad(moe_router[i], ((0, 0), (0, LANES - N_EXPERTS))),
                "w1": moe_w1[i].astype(BF16), "w3": moe_w3[i].astype(BF16), "w2": moe_w2[i].astype(BF16),
            })
        layers.append(prm)

    def mem_heads(kv, b):
        k = kv[:, :, 0].transpose(0, 2, 1, 3).astype(BF16)
        v = kv[:, :, 1].transpose(0, 2, 1, 3).astype(BF16)
        return k, v

    mem2 = mem_prompt.reshape(bp * n_mem, d)
    mem_p = [_memory_kv(mem2, layers[l]).reshape(bp, n_mem, 2, N_XH, HD) for l in range(depth)]
    ff = ffn_w1.shape[2]
    tf = ff // 2 if (ff // 2) % LANES == 0 else ff
    cfg_p = {"tm": 256, "tq": 128, "tk": 256, "t_even": 256, "tk_win": 256, "tm_ffn": 512, "tf": tf}
    cfg_p["tq"] = min(cfg_p["tq"], tp)
    y_prompt, op = _run_group(x_prompt, 0, tp, [mem_heads(m, bp) for m in mem_p], [None] * depth, layers, cfg_p)
    p_mem = jnp.stack(mem_p)

    n_past = page_table.shape[1] * cache_even_k.shape[2]
    ts_pad = -(-ts // SUBLANES) * SUBLANES
    xs = _pad_rows(x_sample, ts_pad)
    pasts = []
    for l in range(depth):
        i = l // 2
        if l % 2 == 0:
            pasts.append((cache_even_k[i], cache_even_v[i], page_table))
        else:
            pn = cache_odd_nsa[i]
            pd = cache_odd_dsa[i]
            pasts.append((pn.reshape(pn.shape[0], pn.shape[1], -1), pd.reshape(pd.shape[0], pd.shape[1], -1),
                          state_odd_win[i], page_table))
    cfg_s = {"tm": bs * ts_pad, "tq": ts_pad, "tk": 512, "pp": 4, "tk_win": 128, "tm_ffn": bs * ts_pad,
             "tf": tf, "new_rows": 512}
    y_s, os_ = _run_group(xs, n_past, ts, [mem_heads(cache_mem[l], bs) for l in range(depth)], pasts, layers, cfg_s)
    y_sample = y_s[:, :ts]

    g = N_NSA_KV
    return (
        y_prompt, y_sample,
        op["ek"][None], op["ev"][None],
        op["on"].reshape(1, bp, tp, 4, g, HD), op["od"].reshape(1, bp, tp, 3, HD),
        op["ow"][None], p_mem,
        os_["ek"][:, :ts][None], os_["ev"][:, :ts][None],
        os_["on"][:, :ts].reshape(1, bs, ts, 4, g, HD), os_["od"][:, :ts].reshape(1, bs, ts, 3, HD),
        os_["ow"][None],
    )
```

```python
import functools
import math

import jax
import jax.numpy as jnp
import numpy as np
from jax import lax
from jax.experimental import pallas as pl
from jax.experimental.pallas import tpu as pltpu

F32 = jnp.float32
BF16 = jnp.bfloat16

HD = 64
N_SB = 8
N_DF = 4
N_NSA = 8
N_NSA_KV = 2
NSA_GROUP = N_NSA // N_NSA_KV
N_DSA = 8
N_IDX = 8
N_XH = 4
N_EXPERTS = 8
ROPE_THETA = 10000.0
CMP_LEN = 32
CMP_STRIDE = 16
SLC_BLOCK = 64
CMP_PER_SLC = SLC_BLOCK // CMP_STRIDE
N_SLC = 16
N_LOCAL = 2
WINDOW = 512
DSA_TOPK_MAX = 256
EPS = 1e-6
NEG = -1e30
FORCE = 1e9
SCALE = HD ** -0.5
IDX_SCALE = HD ** -0.5
SB_W = N_SB * HD
HD_SHIFT = 6
SLC_SHIFT = 6
DF_W = N_DF * 2 * HD

LANES = 128
SUBLANES = 8
PAGE = 128
VMEM_LIMIT = 52 * 1024 * 1024
INT_MIN = -2 ** 31
ROW_SPLIT = 2
SB_CUT = 120.0
SEL_CHUNK_ROWS = 16
MIN_CHAIN_ROWS = 64
KEY_UNROLL = 4

_NT = (((1,), (1,)), ((), ()))


def _cparams(sem):
    return pltpu.CompilerParams(dimension_semantics=sem, vmem_limit_bytes=VMEM_LIMIT)


def _dot(a, b):
    return jnp.dot(a, b, preferred_element_type=F32)


def _dot_nt(a, b):
    return lax.dot_general(a, b, _NT, preferred_element_type=F32)


def _split_dot(x, m_bf16):
    hi = x.astype(BF16)
    lo = (x - hi.astype(F32)).astype(BF16)
    return _dot(hi, m_bf16) + _dot(lo, m_bf16)


def _iota(shape, dim):
    return lax.broadcasted_iota(jnp.int32, shape, dim)


def _rms_rows(x, g):
    return x * lax.rsqrt(jnp.mean(x * x, axis=-1, keepdims=True) + EPS) * g


def _group_mean_matrix():
    r = _iota((LANES, LANES), 0) >> HD_SHIFT
    c = _iota((LANES, LANES), 1) >> HD_SHIFT
    return jnp.where(r == c, 1.0 / HD, 0.0).astype(BF16)


def _head_norm(y, g, gm):
    ms = _split_dot(y * y, gm)
    return y * lax.rsqrt(ms + EPS) * g


def _rope_slab(y, cos, sin):
    lane = _iota(y.shape, 1)
    first = (lane & (HD - 1)) < (HD // 2)
    swapped = jnp.where(first, pltpu.roll(y, LANES - HD // 2, 1), pltpu.roll(y, HD // 2, 1))
    return y * cos + swapped * sin


def _mm_kernel(*refs, n_rows, n_consts, lhs_fn, program, rope_idx, gains_idx, res_idx):
    rows = refs[:n_rows]
    consts = refs[n_rows:n_rows + n_consts]
    w_ref = refs[n_rows + n_consts]
    o_ref = refs[-1]
    lhs = lhs_fn(rows, consts).astype(BF16)
    gm = _group_mean_matrix() if gains_idx is not None else None
    for (src, width, gain_row, act, outs) in program:
        y_full = _dot(lhs, w_ref[:, src:src + width])
        for s in range(width // LANES):
            y = y_full[:, s * LANES:(s + 1) * LANES]
            if gain_row is not None:
                y = _head_norm(y, consts[gains_idx][gain_row:gain_row + 1, :], gm)
            if act == "sigmoid":
                y = 1.0 / (1.0 + jnp.exp(-y))
            for (dst, rope) in outs:
                z = y
                if rope:
                    z = _rope_slab(y, rows[rope_idx[0]][...], rows[rope_idx[1]][...])
                d0 = dst + s * LANES
                if res_idx is not None:
                    z = z + rows[res_idx][:, d0:d0 + LANES]
                o_ref[:, d0:d0 + LANES] = z


def _mm(rows, consts, w, lhs_fn, program, out_cols, *, tm, rope_idx=None, gains_idx=None, res_idx=None, name="mm"):
    n = rows[0].shape[0]
    assert n % tm == 0
    in_specs = [pl.BlockSpec((tm, r.shape[1]), lambda i: (i, 0)) for r in rows]
    in_specs += [pl.BlockSpec(c.shape, lambda i: (0, 0)) for c in consts]
    in_specs += [pl.BlockSpec(w.shape, lambda i: (0, 0))]
    kern = functools.partial(_mm_kernel, n_rows=len(rows), n_consts=len(consts), lhs_fn=lhs_fn,
                             program=program, rope_idx=rope_idx, gains_idx=gains_idx, res_idx=res_idx)
    return pl.pallas_call(
        kern,
        grid=(n // tm,),
        in_specs=in_specs,
        out_specs=pl.BlockSpec((tm, out_cols), lambda i: (i, 0)),
        out_shape=jax.ShapeDtypeStruct((n, out_cols), F32),
        compiler_params=_cparams(("parallel",)),
        name=name,
    )(*rows, *consts, w)


def _lhs_norm(rows, consts):
    return _rms_rows(rows[0][...], consts[0][...])


def _lhs_plain(rows, consts):
    return rows[0][...]


def _lhs_cat2(rows, consts):
    return jnp.concatenate([rows[0][...], rows[1][...]], axis=1)


def _lhs_odd(rows, consts):
    oc, os_, ow, od = rows[0][...], rows[1][...], rows[2][...], rows[3][...]
    g0, g1, g2 = rows[4][...], rows[5][...], rows[6][...]
    return jnp.concatenate([g0 * oc + g1 * os_ + g2 * ow, od], axis=1)


def _ffn_kernel(x_ref, g_ref, r_ref, w1_ref, w3_ref, w2_ref, o_ref, h_ref, acc_ref, gate_ref, *, routed):
    e = pl.program_id(1)
    f = pl.program_id(2)
    first = jnp.logical_and(e == 0, f == 0)
    last = jnp.logical_and(e == pl.num_programs(1) - 1, f == pl.num_programs(2) - 1)

    @pl.when(first)
    def _():
        x = x_ref[...]
        h = _rms_rows(x, g_ref[...])
        h_ref[...] = h.astype(BF16)
        acc_ref[...] = x
        if routed:
            logits = jnp.dot(h, r_ref[...], preferred_element_type=F32, precision=lax.Precision.HIGHEST)
            col = _iota(logits.shape, 1).astype(F32)
            logits = jnp.where(col < N_EXPERTS, logits, -jnp.inf)
            m1 = jnp.max(logits, axis=1, keepdims=True)
            i1 = jnp.min(jnp.where(logits == m1, col, 1e9), axis=1, keepdims=True)
            rest = jnp.where(col == i1, -jnp.inf, logits)
            m2 = jnp.max(rest, axis=1, keepdims=True)
            i2 = jnp.min(jnp.where(rest == m2, col, 1e9), axis=1, keepdims=True)
            e2 = jnp.exp(m2 - m1)
            g1 = 1.0 / (1.0 + e2)
            g2 = e2 / (1.0 + e2)
            gate_ref[...] = jnp.where(col == i1, g1, 0.0) + jnp.where(col == i2, g2, 0.0)

    def compute(gcol):
        h = h_ref[...]
        u = _dot(h, w1_ref[0])
        v = _dot(h, w3_ref[0])
        a = (u * (1.0 / (1.0 + jnp.exp(-u)))) * v
        y = _dot(a.astype(BF16), w2_ref[0])
        if gcol is not None:
            y = gcol * y
        acc_ref[...] += y

    if routed:
        col = _iota(gate_ref.shape, 1)
        gcol = jnp.sum(jnp.where(col == e, gate_ref[...], 0.0), axis=1, keepdims=True)
        active = jnp.max(gcol) > 0.0

        @pl.when(active)
        def _():
            compute(gcol)
    else:
        compute(None)

    @pl.when(last)
    def _():
        o_ref[...] = acc_ref[...]


def _ffn(x, g, router, w1, w3, w2, *, tm, tf, routed):
    n, d = x.shape
    ne, _, ff = w1.shape
    assert n % tm == 0 and ff % tf == 0
    kern = functools.partial(_ffn_kernel, routed=routed)
    return pl.pallas_call(
        kern,
        grid=(n // tm, ne, ff // tf),
        in_specs=[
            pl.BlockSpec((tm, d), lambda i, e, f: (i, 0)),
            pl.BlockSpec((1, d), lambda i, e, f: (0, 0)),
            pl.BlockSpec(router.shape, lambda i, e, f: (0, 0)),
            pl.BlockSpec((1, d, tf), lambda i, e, f: (e, 0, f)),
            pl.BlockSpec((1, d, tf), lambda i, e, f: (e, 0, f)),
            pl.BlockSpec((1, tf, d), lambda i, e, f: (e, f, 0)),
        ],
        out_specs=pl.BlockSpec((tm, d), lambda i, e, f: (i, 0)),
        out_shape=jax.ShapeDtypeStruct((n, d), F32),
        scratch_shapes=[pltpu.VMEM((tm, d), BF16), pltpu.VMEM((tm, d), F32), pltpu.VMEM((tm, LANES), F32)],
        compiler_params=_cparams(("parallel", "arbitrary", "arbitrary")),
        name="moe" if routed else "ffn",
    )(x, g, router, w1, w3, w2)


def _tile_rows(m, reps):
    return m if reps == 1 else jnp.concatenate([m] * reps, axis=0)


def _online_step(s, v, m, l, acc):
    m_new = jnp.maximum(m, jnp.max(s, axis=1, keepdims=True))
    p = jnp.exp(s - m_new)
    alpha = jnp.exp(m - m_new)
    return m_new, alpha * l + jnp.sum(p, axis=1, keepdims=True), alpha * acc + _dot(p.astype(BF16), v)


def _online_init(rows, dv):
    return (jnp.full((rows, 1), NEG, F32), jnp.zeros((rows, 1), F32), jnp.zeros((rows, dv), F32))


def _unrolled_loop(lo, hi, body, init):
    shift = KEY_UNROLL.bit_length() - 1
    n_group = (hi - lo) >> shift

    def group(p, st):
        for u in range(KEY_UNROLL):
            st = body(lo + KEY_UNROLL * p + u, st)
        return st

    st = lax.fori_loop(0, n_group, group, init)
    return lax.fori_loop(lo + n_group * KEY_UNROLL, hi, body, st)


def _flash_kernel(*refs, reps, tq, tk, q_off, k_off, mode, has_bm, slab):
    if has_bm:
        q_ref, k_ref, v_ref, bm_ref, o_ref = refs
    else:
        q_ref, k_ref, v_ref, o_ref = refs
    i = pl.program_id(2)
    q = q_ref[0, 0, 0]
    dv = HD if slab else v_ref.shape[-1]
    n_kt = k_ref.shape[-2] // tk
    q_lo = q_off + i * tq

    def kv_tile(ref, k0):
        if not slab:
            return ref[0, 0, pl.ds(k0, tk), :]
        t = ref[0, pl.ds(k0, tk), :]
        return jnp.where(pl.program_id(1) == 0, t[:, :HD], t[:, HD:]).astype(BF16)

    if mode == "full":
        lo, hi = 0, n_kt
    else:
        hi = jnp.minimum(lax.div(q_lo + tq - 1 - k_off, tk) + 1, n_kt)
        lo = 0
        if mode == "window":
            lo = lax.div(jnp.maximum(q_lo - (WINDOW - 1) - k_off, 0), tk)
    qpos = q_lo + _iota((tq, tk), 0)
    if has_bm:
        bm = bm_ref[0, 0].astype(BF16)
        nbp = bm.shape[1]

    def body(j, state):
        k0 = pl.multiple_of(j * tk, tk)
        k = kv_tile(k_ref, k0)
        v = kv_tile(v_ref, k0)
        bias = None
        if mode != "full":
            kpos = k_off + j * tk + _iota((tq, tk), 1)
            ok = kpos <= qpos
            if mode == "window":
                ok = jnp.logical_and(ok, qpos - kpos < WINDOW)
                ok = jnp.logical_and(ok, kpos >= 0)
            bias = jnp.where(ok, 0.0, NEG)
        if has_bm:
            blk = _iota((nbp, tk), 0)
            tok = (j * tk + _iota((nbp, tk), 1)) >> SLC_SHIFT
            expand = jnp.where(blk == tok, 1.0, 0.0).astype(BF16)
            bias = jnp.where(_dot(bm, expand) > 0.5, bias, NEG)
        if bias is not None:
            bias = _tile_rows(bias, cr // tq)
        new = []
        for c in range(n_chain):
            s = _dot_nt(q[c * cr:(c + 1) * cr].astype(BF16), k)
            if bias is not None:
                s = s + bias
            new.append(_online_step(s, v, *state[c]))
        return tuple(new)

    n_chain = reps if tq >= MIN_CHAIN_ROWS else 1
    cr = reps * tq // n_chain
    state = _unrolled_loop(lo, hi, body, tuple(_online_init(cr, dv) for _ in range(n_chain)))
    for c in range(n_chain):
        o_ref[0, 0, 0, c * cr:(c + 1) * cr, :] = state[c][2] / state[c][1]


def _flash(q, k, v, *, reps, tq, tk, q_off, k_off, mode, bm=None, slabs=None):
    b, hk, nq, rows, _ = q.shape
    lk = k.shape[-2]
    assert rows == reps * tq and lk % tk == 0
    if slabs is None:
        dv = v.shape[3]
        kv_specs = [pl.BlockSpec((1, 1, lk, HD), lambda b_, h, i: (b_, h, 0, 0)),
                    pl.BlockSpec((1, 1, lk, dv), lambda b_, h, i: (b_, h, 0, 0))]
    else:
        assert hk == 2
        dv = HD
        kv_specs = [pl.BlockSpec((1, lk, LANES), lambda b_, h, i: (b_, 0, slabs[0])),
                    pl.BlockSpec((1, lk, LANES), lambda b_, h, i: (b_, 0, slabs[1]))]
    in_specs = [pl.BlockSpec((1, 1, 1, rows, HD), lambda b_, h, i: (b_, h, i, 0, 0))] + kv_specs
    args = [q, k, v]
    if bm is not None:
        in_specs.append(pl.BlockSpec((1, 1, tq, bm.shape[3]), lambda b_, h, i: (b_, h, i, 0)))
        args.append(bm)
    kern = functools.partial(_flash_kernel, reps=reps, tq=tq, tk=tk, q_off=q_off, k_off=k_off, mode=mode,
                             has_bm=bm is not None, slab=slabs is not None)
    return pl.pallas_call(
        kern,
        grid=(b, hk, nq),
        in_specs=in_specs,
        out_specs=pl.BlockSpec((1, 1, 1, rows, dv), lambda b_, h, i: (b_, h, i, 0, 0)),
        out_shape=jax.ShapeDtypeStruct((b, hk, nq, rows, dv), F32),
        compiler_params=_cparams(("parallel", "parallel", "arbitrary")),
        name="flash_" + mode + ("_blockmask" if bm is not None else ""),
    )(*args)


def _softplus(z):
    return jnp.maximum(z, 0.0) + jnp.log(1.0 + jnp.exp(-jnp.abs(z)))


def _later_matrix(n):
    return jnp.where(_iota((n, n), 0) > _iota((n, n), 1), 1.0, 0.0).astype(BF16)


def _sb_kernel(q_ref, k_ref, v_ref, o_ref, *, t):
    i = pl.program_id(2)
    th = t // ROW_SPLIT
    lane = _iota((th, LANES), 1)
    later = _later_matrix(t)
    qs = []
    for part in range(ROW_SPLIT):
        qf = q_ref[0, part * th:(part + 1) * th, :] * SCALE
        qs.append((jnp.where(lane < HD, qf, 0.0).astype(BF16), jnp.where(lane >= HD, qf, 0.0).astype(BF16)))

    def tile(j, carries, outs, masked):
        k0 = pl.multiple_of(j * t, t)
        k = k_ref[0, pl.ds(k0, t), :].astype(BF16)
        v = v_ref[0, pl.ds(k0, t), :].astype(BF16)
        new_carries, new_outs = [], []
        for part in range(ROW_SPLIT):
            heads = []
            for hd in range(2):
                carry = carries[2 * part + hd]
                z = _dot_nt(qs[part][hd], k)
                sp = _softplus(z)
                log_sig = z - sp
                if masked:
                    vis = _iota((th, t), 1) < part * th + _iota((th, t), 0)
                    sp = jnp.where(vis, sp, 0.0)
                a = jnp.exp(log_sig - _dot(sp.astype(BF16), later) - carry)
                if masked:
                    a = jnp.where(vis, a, 0.0)
                heads.append(_dot(a.astype(BF16), v))
                new_carries.append(carry + jnp.sum(sp, axis=1, keepdims=True))
            new_outs.append(outs[part] + jnp.where(lane < HD, heads[0], heads[1]))
        return tuple(new_carries), tuple(new_outs)

    def min_carry(carries):
        m = jnp.min(carries[0])
        for c in carries[1:]:
            m = jnp.minimum(m, jnp.min(c))
        return m

    zero = jnp.zeros((th, 1), F32)
    carries, outs = tile(i, (zero,) * (2 * ROW_SPLIT), (jnp.zeros((th, LANES), F32),) * ROW_SPLIT, True)

    def cond(st):
        return jnp.logical_and(st[0] < i, st[1] < SB_CUT)

    def body(st):
        carries, outs = tile(i - 1 - st[0], st[2], st[3], False)
        return st[0] + 1, min_carry(carries), carries, outs

    _, _, carries, outs = lax.while_loop(cond, body, (jnp.int32(0), min_carry(carries), carries, outs))
    for part in range(ROW_SPLIT):
        o_ref[0, part * th:(part + 1) * th, :] = outs[part]


def _sb(proj3, *, t, q_slab, k_slab, v_slab, n_slabs):
    b, tt, _ = proj3.shape
    assert tt % t == 0
    kern = functools.partial(_sb_kernel, t=t)
    return pl.pallas_call(
        kern,
        grid=(b, n_slabs, tt // t),
        in_specs=[
            pl.BlockSpec((1, t, LANES), lambda b_, p, i: (b_, i, q_slab + p)),
            pl.BlockSpec((1, tt, LANES), lambda b_, p, i: (b_, 0, k_slab + p)),
            pl.BlockSpec((1, tt, LANES), lambda b_, p, i: (b_, 0, v_slab + p)),
        ],
        out_specs=pl.BlockSpec((1, t, LANES), lambda b_, p, i: (b_, i, p)),
        out_shape=jax.ShapeDtypeStruct((b, tt, n_slabs * LANES), F32),
        compiler_params=_cparams(("parallel", "parallel", "arbitrary")),
        name="sb_prompt",
    )(proj3, proj3, proj3)


def _lam(lv, lam_init):
    a = jnp.sum(jnp.sum(lv[0:1] * lv[1:2], axis=1, keepdims=True), axis=0, keepdims=True)
    b = jnp.sum(jnp.sum(lv[2:3] * lv[3:4], axis=1, keepdims=True), axis=0, keepdims=True)
    return jnp.exp(a) - jnp.exp(b) + lam_init


def _df_kernel(lv_ref, g_ref, q_ref, k_ref, v_ref, o_ref, *, t, lam_init):
    i = pl.program_id(2)
    th = t // ROW_SPLIT
    lane = _iota((th, LANES), 1)
    qs = []
    for part in range(ROW_SPLIT):
        qf = q_ref[0, part * th:(part + 1) * th, :] * SCALE
        qs.append((jnp.where(lane < HD, qf, 0.0).astype(BF16), jnp.where(lane >= HD, qf, 0.0).astype(BF16)))

    def tile(j, state, masked):
        k0 = pl.multiple_of(j * t, t)
        k = k_ref[0, pl.ds(k0, t), :].astype(BF16)
        v = v_ref[0, pl.ds(k0, t), :].astype(BF16)
        new = []
        for part in range(ROW_SPLIT):
            for mp in range(2):
                s = _dot_nt(qs[part][mp], k)
                if masked:
                    s = jnp.where(_iota((th, t), 1) <= part * th + _iota((th, t), 0), s, NEG)
                new.append(_online_step(s, v, *state[2 * part + mp]))
        return tuple(new)

    init = tuple(_online_init(th, LANES) for _ in range(2 * ROW_SPLIT))
    state = _unrolled_loop(0, i, lambda j, st: tile(j, st, False), init)
    state = tile(i, state, True)
    lam = _lam(lv_ref[...], lam_init)
    for part in range(ROW_SPLIT):
        (_, l0, a0), (_, l1, a1) = state[2 * part], state[2 * part + 1]
        d = a0 / l0 - lam * (a1 / l1)
        o_ref[0, part * th:(part + 1) * th, :] = _rms_rows(d, g_ref[...]) * (1.0 - lam_init)


def _df(proj3, lv, subln, *, t, q_slab, k_slab, v_slab, n_slabs, lam_init):
    b, tt, _ = proj3.shape
    assert tt % t == 0
    kern = functools.partial(_df_kernel, t=t, lam_init=lam_init)
    return pl.pallas_call(
        kern,
        grid=(b, n_slabs, tt // t),
        in_specs=[
            pl.BlockSpec(lv.shape, lambda b_, p, i: (0, 0)),
            pl.BlockSpec(subln.shape, lambda b_, p, i: (0, 0)),
            pl.BlockSpec((1, t, LANES), lambda b_, p, i: (b_, i, q_slab + p)),
            pl.BlockSpec((1, tt, LANES), lambda b_, p, i: (b_, 0, k_slab + p)),
            pl.BlockSpec((1, tt, LANES), lambda b_, p, i: (b_, 0, v_slab + p)),
        ],
        out_specs=pl.BlockSpec((1, t, LANES), lambda b_, p, i: (b_, i, p)),
        out_shape=jax.ShapeDtypeStruct((b, tt, n_slabs * LANES), F32),
        compiler_params=_cparams(("parallel", "parallel", "arbitrary")),
        name="df_prompt",
    )(lv, subln, proj3, proj3, proj3)


def _dec_even_kernel(*refs, pp, ts, lam_init):
    tbl_ref, qt_ref = refs[0], refs[1]
    k_refs = refs[2:2 + pp]
    v_refs = refs[2 + pp:2 + 2 * pp]
    kn_ref, vn_ref, lv_ref, g_ref, o_ref, later_ref, st_ref, asb_ref, adf_ref = refs[2 + 2 * pp:]
    del tbl_ref
    s_id = pl.program_id(1)
    nk = pp * PAGE
    half = LANES // 2
    qt = qt_ref[0]

    def col_of(row):
        return jnp.transpose(jnp.broadcast_to(row, (SUBLANES, LANES)))[half:, 0:1]

    def tile(kt, vt, later, vis_sb, vis_df):
        carry, m, l = st_ref[0:1, :], st_ref[1:2, :], st_ref[2:3, :]
        zt = _dot(kt, qt)
        lane = _iota(zt.shape, 1)
        sp = _softplus(zt)
        log_sig = zt - sp
        s = zt
        if vis_sb is not None:
            sp = jnp.where(vis_sb, sp, 0.0)
            s = jnp.where(vis_df, s, NEG)
        a = jnp.exp(log_sig - _dot(later, sp.astype(BF16)) - carry)
        if vis_sb is not None:
            a = jnp.where(vis_sb, a, 0.0)
        m_new = jnp.maximum(m, jnp.max(s, axis=0, keepdims=True))
        p = jnp.exp(s - m_new)
        alpha = jnp.exp(m - m_new)
        st_ref[0:1, :] = carry + jnp.sum(sp, axis=0, keepdims=True)
        st_ref[1:2, :] = m_new
        st_ref[2:3, :] = alpha * l + jnp.sum(p, axis=0, keepdims=True)
        w = jnp.transpose(jnp.where(lane < half, a, p)).astype(BF16)
        asb_ref[...] += _dot(w[:half], vt[:, :SB_W])
        adf_ref[...] = col_of(alpha) * adf_ref[...] + _dot(w[half:], vt[:, SB_W:])

    @pl.when(s_id == 0)
    def _():
        later_ref[...] = jnp.where(_iota((nk, nk), 1) > _iota((nk, nk), 0), 1.0, 0.0).astype(BF16)
        st_ref[...] = jnp.where(_iota(st_ref.shape, 0) == 1, NEG, 0.0)
        asb_ref[...] = jnp.zeros_like(asb_ref)
        adf_ref[...] = jnp.zeros_like(adf_ref)
        key = _iota((PAGE, LANES), 0)
        tok = _iota((PAGE, LANES), 1) & (ts - 1)
        tile(kn_ref[0].astype(BF16), vn_ref[0].astype(BF16), later_ref[0:PAGE, 0:PAGE], key < tok, key <= tok)

    kt = jnp.concatenate([r[0].astype(BF16) for r in k_refs], axis=0)
    vt = jnp.concatenate([r[0].astype(BF16) for r in v_refs], axis=0)
    tile(kt, vt, later_ref[...], None, None)

    @pl.when(s_id == pl.num_programs(1) - 1)
    def _():
        row = _iota((half, SB_W), 0)
        lane = _iota((half, SB_W), 1)

        def fold(x):
            out = x[0:ts]
            for u in range(1, half // ts):
                out = out + x[u * ts:(u + 1) * ts]
            return out

        o_sb = fold(jnp.where((row >> 3) == (lane >> HD_SHIFT), asb_ref[...], 0.0))
        pn = adf_ref[...] / col_of(st_ref[2:3, :])
        same_head = (row >> 4) == (lane >> 7)
        o0 = fold(jnp.where(jnp.logical_and(same_head, ((row >> 3) & 1) == 0), pn, 0.0))
        o1 = fold(jnp.where(jnp.logical_and(same_head, ((row >> 3) & 1) == 1), pn, 0.0))
        d = o0 - _lam(lv_ref[...], lam_init) * o1
        parts = [o_sb]
        for h in range(N_DF):
            parts.append(_rms_rows(d[:, h * LANES:(h + 1) * LANES], g_ref[...]) * (1.0 - lam_init))
        o_ref[0] = jnp.concatenate(parts, axis=1)


def _dec_even(qt, pool_k, pool_v, table, k_new, v_new, lv, subln, *, pp, ts, lam_init):
    b, n_pages = table.shape
    width = pool_k.shape[2]
    assert n_pages % pp == 0 and ts == SUBLANES and k_new.shape[1] == PAGE
    n_steps = n_pages // pp

    def page_map(u):
        return lambda b_, s, t: (t[b_, n_pages - (s + 1) * pp + u], 0, 0)

    page_specs = [pl.BlockSpec((1, PAGE, width), page_map(u)) for u in range(pp)]
    new_spec = pl.BlockSpec((1, PAGE, width), lambda b_, s, t: (b_, 0, 0))
    grid_spec = pltpu.PrefetchScalarGridSpec(
        num_scalar_prefetch=1,
        grid=(b, n_steps),
        in_specs=[pl.BlockSpec((1,) + qt.shape[1:], lambda b_, s, t: (b_, 0, 0))] + page_specs + page_specs
        + [new_spec, new_spec, pl.BlockSpec(lv.shape, lambda b_, s, t: (0, 0)),
           pl.BlockSpec(subln.shape, lambda b_, s, t: (0, 0))],
        out_specs=pl.BlockSpec((1, ts, width), lambda b_, s, t: (b_, 0, 0)),
        scratch_shapes=[
            pltpu.VMEM((pp * PAGE, pp * PAGE), BF16),
            pltpu.VMEM((SUBLANES, LANES), F32),
            pltpu.VMEM((LANES // 2, SB_W), F32),
            pltpu.VMEM((LANES // 2, DF_W), F32),
        ],
    )
    kern = functools.partial(_dec_even_kernel, pp=pp, ts=ts, lam_init=lam_init)
    return pl.pallas_call(
        kern,
        grid_spec=grid_spec,
        out_shape=jax.ShapeDtypeStruct((b, ts, width), F32),
        compiler_params=_cparams(("parallel", "arbitrary")),
        name="dec_even",
    )(table, qt, *([pool_k] * pp), *([pool_v] * pp), k_new, v_new, lv, subln)


def _compress_kernel(t_ref, w1_ref, w1g_ref, pe_ref, w2_ref, g_ref, o_ref):
    kind = pl.program_id(1)
    n = t_ref.shape[1] // CMP_STRIDE
    a1 = [None] * N_NSA_KV
    a2 = [None] * N_NSA_KV
    for p in range(CMP_STRIDE):
        rows = t_ref[0, pl.ds(p, n, stride=CMP_STRIDE), :].astype(BF16)
        for g in range(N_NSA_KV):
            d1 = _dot(rows, w1g_ref[0, g, p])
            d2 = _dot(rows, w1g_ref[0, g, CMP_STRIDE + p])
            a1[g] = d1 if a1[g] is None else a1[g] + d1
            a2[g] = d2 if a2[g] is None else a2[g] + d2
    pe = _dot(jnp.broadcast_to(pe_ref[0], (SUBLANES, CMP_LEN * HD)).astype(BF16), w1_ref[0])[0:1]
    for g in range(N_NSA_KV):
        nxt = jnp.where(_iota(a2[g].shape, 0) < n - 1, pltpu.roll(a2[g], n - 1, 0), 0.0)
        hid = a1[g] + nxt + pe
        hid = hid * (1.0 / (1.0 + jnp.exp(-hid)))
        out = _dot(hid.astype(BF16), w2_ref[0])
        normed = _rms_rows(out, g_ref[...])
        o_ref[0, 0, g] = jnp.where(kind == 0, normed, out)


def _compress(t, slab0, w1, pe_flat, w2, gain):
    b, lk, _ = t.shape
    g = N_NSA_KV
    n = lk // CMP_STRIDE
    pieces = w1.reshape(2, CMP_LEN, HD, w1.shape[2])
    w1g = jnp.stack([jnp.pad(pieces, ((0, 0), (0, 0), (HD * gi, HD * (g - 1 - gi)), (0, 0))) for gi in range(g)],
                    axis=1)
    return pl.pallas_call(
        _compress_kernel,
        grid=(b, 2),
        in_specs=[
            pl.BlockSpec((1, lk, LANES), lambda b_, k: (b_, 0, slab0 + k)),
            pl.BlockSpec((1,) + w1.shape[1:], lambda b_, k: (k, 0, 0)),
            pl.BlockSpec((1,) + w1g.shape[1:], lambda b_, k: (k, 0, 0, 0, 0)),
            pl.BlockSpec((1,) + pe_flat.shape[1:], lambda b_, k: (k, 0, 0)),
            pl.BlockSpec((1,) + w2.shape[1:], lambda b_, k: (k, 0, 0)),
            pl.BlockSpec(gain.shape, lambda b_, k: (0, 0)),
        ],
        out_specs=pl.BlockSpec((1, 1, g, n, HD), lambda b_, k: (b_, k, 0, 0, 0)),
        out_shape=jax.ShapeDtypeStruct((b, 2, g, n, HD), F32),
        compiler_params=_cparams(("parallel", "arbitrary")),
        name="compress",
    )(t, w1, w1g, pe_flat, w2, gain)


def _nsa_cmp_kernel(q_ref, kc_ref, vc_ref, o_ref, sel_ref, *, tq, q_off, n_sel):
    i = pl.program_id(2)
    q = q_ref[0, 0, 0].astype(BF16)
    kc = kc_ref[0, 0]
    vc = vc_ref[0, 0]
    ncol = kc.shape[0]
    nbp = ncol // CMP_PER_SLC
    q_lo = q_off + i * tq
    qpos = q_lo + _iota((tq, ncol), 0)
    col = _iota((tq, ncol), 1)
    jj = jnp.zeros_like(col)
    for u in range(1, CMP_PER_SLC):
        jj = jj + jnp.where(col >= u * nbp, 1, 0)
    c_end = (col - jj * nbp) * SLC_BLOCK + jj * CMP_STRIDE + (CMP_LEN - 1)
    maskf = _tile_rows(jnp.where(c_end <= qpos, 1.0, 0.0), NSA_GROUP)
    keep = maskf > 0.5
    s = jnp.where(keep, _dot_nt(q, kc), NEG)
    m = jnp.max(s, axis=1, keepdims=True)
    p = jnp.where(keep, jnp.exp(s - m), 0.0)
    p = p / jnp.maximum(jnp.sum(p, axis=1, keepdims=True), 1e-30)
    o_ref[0, 0, 0] = _dot(p.astype(BF16), vc)
    pg = p[0:tq]
    for r in range(1, NSA_GROUP):
        pg = pg + p[r * tq:(r + 1) * tq]
    imp = pg[:, 0:nbp]
    for j in range(1, CMP_PER_SLC):
        imp = imp + pg[:, j * nbp:(j + 1) * nbp]
    blk = _iota((tq, nbp), 1)
    qp = q_lo + _iota((tq, nbp), 0)
    q_blk = qp >> SLC_SHIFT
    visible = blk * SLC_BLOCK <= qp
    forced = jnp.logical_or(blk == 0, jnp.logical_and(blk <= q_blk, blk > q_blk - N_LOCAL))
    score = jnp.where(visible, jnp.where(forced, FORCE, imp), NEG)
    rc = min(tq, SEL_CHUNK_ROWS)
    blkf = _iota((rc, nbp), 1).astype(F32)
    scores = [score[c * rc:(c + 1) * rc] for c in range(tq // rc)]
    sels = [jnp.zeros((rc, nbp), F32) for _ in scores]
    for _ in range(n_sel):
        for c in range(len(scores)):
            top = jnp.max(scores[c], axis=1, keepdims=True)
            idx = jnp.min(jnp.where(scores[c] == top, blkf, 1e9), axis=1, keepdims=True)
            pick = blkf == idx
            sels[c] = jnp.where(pick, 1.0, sels[c])
            scores[c] = jnp.where(pick, -jnp.inf, scores[c])
    for c in range(len(scores)):
        sel_ref[0, 0, c * rc:(c + 1) * rc, :] = sels[c]


def _nsa_cmp(q, kc, vc, *, tq, q_off, n_sel):
    b, g, nq, rows, _ = q.shape
    ncol = kc.shape[2]
    nbp = ncol // CMP_PER_SLC
    kern = functools.partial(_nsa_cmp_kernel, tq=tq, q_off=q_off, n_sel=n_sel)
    return pl.pallas_call(
        kern,
        grid=(b, g, nq),
        in_specs=[
            pl.BlockSpec((1, 1, 1, rows, HD), lambda b_, g_, i: (b_, g_, i, 0, 0)),
            pl.BlockSpec((1, 1, ncol, HD), lambda b_, g_, i: (b_, g_, 0, 0)),
            pl.BlockSpec((1, 1, ncol, HD), lambda b_, g_, i: (b_, g_, 0, 0)),
        ],
        out_specs=[
            pl.BlockSpec((1, 1, 1, rows, HD), lambda b_, g_, i: (b_, g_, i, 0, 0)),
            pl.BlockSpec((1, 1, tq, nbp), lambda b_, g_, i: (b_, g_, i, 0)),
        ],
        out_shape=[
            jax.ShapeDtypeStruct((b, g, nq, rows, HD), F32),
            jax.ShapeDtypeStruct((b, g, nq * tq, nbp), F32),
        ],
        compiler_params=_cparams(("parallel", "parallel", "arbitrary")),
        name="nsa_cmp",
    )(q, kc, vc)


def _dsa_kernel(qi_ref, wi_ref, qd_ref, kk_ref, vv_ref, o_ref, key_ref, *, tq, tk, q_off, n_top, offs):
    i = pl.program_id(1)
    qi = qi_ref[0, 0]
    qd = qd_ref[0, 0]
    w = wi_ref[0] * (N_IDX ** -0.5)
    n_kt = kk_ref.shape[1] // tk
    kd_off, ki_off, vd_off = offs

    def cols(ref, k0, off):
        return ref[0, pl.ds(k0, tk), :][:, off:off + HD].astype(BF16)

    q_lo = q_off + i * tq
    hi = jnp.minimum(lax.div(q_lo + tq - 1, tk) + 1, n_kt)
    qpos = q_lo + _iota((tq, tk), 0)
    n_chain = N_DSA if tq >= MIN_CHAIN_ROWS else 1
    cr = N_DSA * tq // n_chain

    def visible(j):
        return (j * tk + _iota((tq, tk), 1)) <= qpos

    def score_tile(j, _):
        k0 = pl.multiple_of(j * tk, tk)
        k = cols(kk_ref, k0, ki_off)
        tot = None
        for c in range(n_chain):
            sc = jnp.maximum(_dot_nt(qi[c * cr:(c + 1) * cr].astype(BF16), k), 0.0)
            for u in range(cr // tq):
                h = c * (cr // tq) + u
                term = w[:, h:h + 1] * sc[u * tq:(u + 1) * tq]
                tot = term if tot is None else tot + term
        tot = jnp.where(visible(j), tot, NEG)
        bits = pltpu.bitcast(tot, jnp.int32)
        key = jnp.where(bits < 0, bits ^ jnp.int32(0x7FFFFFFF), bits)
        key_ref[:, pl.ds(k0, tk)] = jnp.where(tot == 0.0, 0, key)
        return 0

    _unrolled_loop(0, hi, score_tile, 0)

    def count_ge(c):
        def body(j, acc):
            blk = key_ref[:, pl.ds(pl.multiple_of(j * tk, tk), tk)]
            hit = jnp.where(blk >= c, 1.0, 0.0)
            part = hit[:, 0:LANES]
            for u in range(1, tk // LANES):
                part = part + hit[:, u * LANES:(u + 1) * LANES]
            return acc + part
        acc = _unrolled_loop(0, hi, body, jnp.zeros((tq, LANES), F32))
        return jnp.sum(acc, axis=1, keepdims=True)

    kf = float(n_top)
    tau = jnp.where(count_ge(jnp.zeros((tq, 1), jnp.int32)) >= kf, 0, INT_MIN).astype(jnp.int32)

    def bit_body(t, tau):
        cand = tau + jnp.left_shift(jnp.int32(1), 30 - t)
        return jnp.where(count_ge(cand) >= kf, cand, tau)

    tau = lax.fori_loop(0, 31, bit_body, tau)
    need = kf - count_ge(tau + 1)
    before = jnp.where(_iota((tk, tk), 0) < _iota((tk, tk), 1), 1.0, 0.0).astype(BF16)

    def attend(j, carry):
        state, n_eq = carry
        k0 = pl.multiple_of(j * tk, tk)
        key = key_ref[:, pl.ds(k0, tk)]
        eqf = jnp.where(key == tau, 1.0, 0.0)
        rank = n_eq + _dot(eqf.astype(BF16), before)
        kept = jnp.logical_or(key > tau, jnp.logical_and(key == tau, rank < need))
        bias = jnp.where(jnp.logical_and(kept, visible(j)), 0.0, NEG)
        k = cols(kk_ref, k0, kd_off)
        v = cols(vv_ref, k0, vd_off)
        bias = _tile_rows(bias, cr // tq)
        new = []
        for c in range(n_chain):
            s = _dot_nt(qd[c * cr:(c + 1) * cr].astype(BF16), k) + bias
            new.append(_online_step(s, v, *state[c]))
        return tuple(new), n_eq + jnp.sum(eqf, axis=1, keepdims=True)

    init = (tuple(_online_init(cr, HD) for _ in range(n_chain)), jnp.zeros((tq, 1), F32))
    state, _ = _unrolled_loop(0, hi, attend, init)
    for c in range(n_chain):
        o_ref[0, 0, c * cr:(c + 1) * cr, :] = state[c][2] / state[c][1]


def _dsa(qi, wi, qd, kk, vv, *, width, k_blk, v_blk, offs, tq, tk, q_off, n_top):
    b, nq, rows, _ = qi.shape
    lk = kk.shape[1]
    assert lk % tk == 0
    kern = functools.partial(_dsa_kernel, tq=tq, tk=tk, q_off=q_off, n_top=n_top, offs=offs)
    qspec = pl.BlockSpec((1, 1, rows, HD), lambda b_, i: (b_, i, 0, 0))
    return pl.pallas_call(
        kern,
        grid=(b, nq),
        in_specs=[qspec, pl.BlockSpec((1, tq, N_IDX), lambda b_, i: (b_, i, 0)), qspec,
                  pl.BlockSpec((1, lk, width), lambda b_, i: (b_, 0, k_blk)),
                  pl.BlockSpec((1, lk, width), lambda b_, i: (b_, 0, v_blk))],
        out_specs=pl.BlockSpec((1, 1, rows, HD), lambda b_, i: (b_, i, 0, 0)),
        out_shape=jax.ShapeDtypeStruct((b, nq, rows, HD), F32),
        scratch_shapes=[pltpu.VMEM((tq, lk), jnp.int32)],
        compiler_params=_cparams(("parallel", "arbitrary")),
        name="dsa",
    )(qi, wi, qd, kk, vv)


def _gather_kernel(*refs, pp, n_steps):
    pool_refs = refs[1:1 + pp]
    new_ref, o_ref = refs[1 + pp], refs[2 + pp]
    s = pl.program_id(1)

    @pl.when(s < n_steps)
    def _():
        for u in range(pp):
            o_ref[0, u * PAGE:(u + 1) * PAGE, :] = pool_refs[u][0]

    @pl.when(s >= n_steps)
    def _():
        o_ref[...] = new_ref[...]


def _page_gather(pool, table, new, *, pp):
    b, n_pages = table.shape
    width = pool.shape[2]
    assert n_pages % pp == 0 and new.shape[1] == pp * PAGE
    n_steps = n_pages // pp

    def page_map(u):
        return lambda b_, s, t: (t[b_, jnp.minimum(s, n_steps - 1) * pp + u], 0, 0)

    kern = functools.partial(_gather_kernel, pp=pp, n_steps=n_steps)
    grid_spec = pltpu.PrefetchScalarGridSpec(
        num_scalar_prefetch=1,
        grid=(b, n_steps + 1),
        in_specs=[pl.BlockSpec((1, PAGE, width), page_map(u)) for u in range(pp)]
        + [pl.BlockSpec((1, pp * PAGE, width), lambda b_, s, t: (b_, 0, 0))],
        out_specs=pl.BlockSpec((1, pp * PAGE, width), lambda b_, s, t: (b_, s, 0)),
    )
    return pl.pallas_call(
        kern,
        grid_spec=grid_spec,
        out_shape=jax.ShapeDtypeStruct((b, (n_pages + pp) * PAGE, width), pool.dtype),
        compiler_params=_cparams(("parallel", "arbitrary")),
        name="page_gather",
    )(table, *([pool] * pp), new)


def _rope_tables(pos):
    half = HD // 2
    inv = ROPE_THETA ** (-jnp.arange(half, dtype=F32) / half)
    ang = pos.astype(F32)[:, None] * inv[None, :]
    cos, sin = jnp.cos(ang), jnp.sin(ang)
    cos128 = jnp.tile(jnp.concatenate([cos, cos], axis=1), (1, LANES // HD))
    sin128 = jnp.tile(jnp.concatenate([-sin, sin], axis=1), (1, LANES // HD))
    return cos128, sin128


def _tile_gain(g):
    return jnp.tile(g.reshape(1, HD), (1, LANES // HD))


def _heads(a, b, t, h, d, scale=None):
    a = a.reshape(b, t, h, d)
    if scale is not None:
        a = a * scale
    return a.transpose(0, 2, 1, 3)


def _stack_q(a, tq):
    b, hk, r, t, d = a.shape
    return a.reshape(b, hk, r, t // tq, tq, d).transpose(0, 1, 3, 2, 4, 5).reshape(b, hk, t // tq, r * tq, d)


def _unstack_q(a, r, tq):
    b, hk, nq, _, d = a.shape
    a = a.reshape(b, hk, nq, r, tq, d).transpose(0, 2, 4, 1, 3, 5)
    return a.reshape(b * nq * tq, hk * r * d)


def _pad_rows(a, rows):
    return jnp.pad(a, ((0, 0), (0, rows - a.shape[1])) + ((0, 0),) * (a.ndim - 2))


def _even_mixer(x2, b, t, q_off, cs, past, prm, cfg):
    n = b * t
    program = [
        (0, SB_W, None, None, [(0, False)]),
        (3 * SB_W, DF_W, 0, None, [(SB_W, True)]),
        (SB_W, SB_W, None, None, [(SB_W + DF_W, False)]),
        (3 * SB_W + DF_W, DF_W, 1, None, [(2 * SB_W + DF_W, True)]),
        (2 * SB_W, SB_W, None, None, [(2 * (SB_W + DF_W), False)]),
        (3 * SB_W + 2 * DF_W, DF_W, None, None, [(3 * SB_W + 2 * DF_W, False)]),
    ]
    gains = jnp.concatenate([_tile_gain(prm["df_qk_gain"][0]), _tile_gain(prm["df_qk_gain"][1])], axis=0)
    proj = _mm([x2, cs[0], cs[1]], [prm["g0"], gains], prm["ev_w_in"], _lhs_norm, program, 3 * (SB_W + DF_W),
               tm=cfg["tm"], rope_idx=(1, 2), gains_idx=1)
    mw = SB_W + DF_W
    new_k = proj[:, mw:2 * mw].reshape(b, t, mw)
    new_v = proj[:, 2 * mw:3 * mw].reshape(b, t, mw)
    lam_init = 0.8 - 0.6 * math.exp(-0.3 * prm["layer"])
    lv, subln = prm["df_lambda"], prm["df_subln_gain"].reshape(1, 2 * HD)
    d = x2.shape[1]
    if past is None:
        proj3 = proj.reshape(b, t, 3 * mw)
        ns = SB_W // LANES
        o_sb = _sb(proj3, t=cfg["t_even"], q_slab=0, k_slab=2 * ns, v_slab=4 * ns, n_slabs=ns)
        o_df = _df(proj3, lv, subln, t=cfg["t_even"], q_slab=ns, k_slab=3 * ns, v_slab=5 * ns, n_slabs=ns,
                   lam_init=lam_init)
        return _mm([o_sb.reshape(n, SB_W), o_df.reshape(n, DF_W), x2], [], prm["ev_w_out"], _lhs_cat2,
                   [(0, d, None, None, [(0, False)])], d, tm=cfg["tm"], res_idx=2), new_k, new_v
    pool_k, pool_v, table = past
    qcat = proj[:, :mw].reshape(b, t, mw).transpose(0, 2, 1) * SCALE
    qt = (jnp.tile(qcat, (1, 1, LANES // t)) * _dec_even_mask(t)).astype(BF16)
    o = _dec_even(qt, pool_k, pool_v, table, _pad_rows(new_k, PAGE), _pad_rows(new_v, PAGE), lv, subln,
                  pp=cfg["pp"], ts=t, lam_init=lam_init)
    return _mm([o.reshape(n, mw), x2], [], prm["ev_w_out"], _lhs_plain,
               [(0, d, None, None, [(0, False)])], d, tm=cfg["tm"], res_idx=1), new_k, new_v


def _dec_even_mask(ts):
    f = np.arange(SB_W + DF_W)[:, None]
    c = np.arange(LANES)[None, :]
    half = LANES // 2
    sb = (f < SB_W) & (c < half) & (f // HD == c // ts)
    df = (f >= SB_W) & (c >= half) & ((f - SB_W) // HD == (c - half) // ts)
    return jnp.asarray((sb | df).astype(np.float32))


def _odd_layout():
    widths = (N_NSA * HD, 128, 128, 128, 128, 128, 128, N_NSA * 3, N_DSA * HD, HD, HD, N_IDX * HD, HD, N_IDX)
    offs = np.concatenate([[0], np.cumsum(widths)])
    (q_n, k_c, v_c, k_s, v_s, k_w, v_w, gate, q_d, k_d, v_d, q_i, k_i, w_i) = [
        (int(offs[j]), int(offs[j + 1])) for j in range(len(widths))]
    pieces = [q_n, (k_c[0], v_w[1]), q_d, q_i, k_d, k_i, v_d, w_i, ("pad", HD - N_IDX), gate,
              ("pad", LANES - N_NSA * 3)]
    program = [
        (0, 512, 0, None, [(0, False), (512, True)]),
        (512, 128, None, None, [(1024, False)]),
        (640, 128, None, None, [(1152, False)]),
        (768, 128, 1, None, [(1280, True)]),
        (896, 128, None, None, [(1408, False)]),
        (1024, 128, 2, None, [(1536, True)]),
        (1152, 128, None, None, [(1664, False)]),
        (1280, 512, 3, None, [(1792, True)]),
        (1792, 512, None, None, [(2304, True)]),
        (2304, 128, 4, None, [(2816, True)]),
        (2432, 128, None, None, [(2944, False)]),
        (2560, 128, None, "sigmoid", [(3072, False)]),
    ]
    return pieces, program, 3200


def _permute_cols(w, pieces):
    cols = []
    for p in pieces:
        if p[0] == "pad":
            cols.append(jnp.zeros((w.shape[0], p[1]), w.dtype))
        else:
            cols.append(w[:, p[0]:p[1]])
    return jnp.concatenate(cols, axis=1)


def _odd_mixer(x2, b, t, t_real, q_off, cs, past, prm, cfg):
    n = b * t
    pieces, program, out_cols = _odd_layout()
    ng, dg = prm["nsa_qk_gain"], prm["dsa_qk_gain"]
    gains = jnp.concatenate([
        _tile_gain(ng[0]), _tile_gain(ng[2]), _tile_gain(ng[3]), _tile_gain(dg[0]),
        jnp.concatenate([dg[1], dg[2]]).reshape(1, LANES)], axis=0)
    w_in = _permute_cols(prm["od_w_in"], pieces)
    proj = _mm([x2, cs[0], cs[1]], [prm["g0"], gains], w_in, _lhs_norm, program, out_cols,
               tm=cfg["tm"], rope_idx=(1, 2), gains_idx=1)
    g = N_NSA_KV
    new_nsa = proj[:, 1024:1536].reshape(b, t, 4 * g * HD)
    new_win = proj[:, 1536:1792].reshape(b, t, 2, g, HD)
    new_dsa = jnp.concatenate([proj[:, 2816:2880], proj[:, 2944:3008], proj[:, 2880:2944]], axis=1).reshape(b, t, 3 * HD)
    w_i = proj[:, 3008:3008 + N_IDX].reshape(b, t, N_IDX)
    gate = proj[:, 3072:3072 + N_NSA * 3].reshape(n, N_NSA, 3)
    if past is None:
        proj3 = proj.reshape(b, t, out_cols)
        nsa_src, win_src = (proj3, 8), (proj3, 12)
        dsa_src = dict(kk=proj3, vv=proj3, width=LANES, k_blk=22, v_blk=23, offs=(0, HD, 0))
        win_off = 0
        new_state = new_win[:, -min(WINDOW, t):]
        l_real = t
    else:
        pool_nsa, pool_dsa, state, table = past
        nsa_buf = _page_gather(pool_nsa, table, _pad_rows(new_nsa, cfg["pp"] * PAGE), pp=cfg["pp"])
        dsa_buf = _page_gather(pool_dsa, table, _pad_rows(new_dsa, cfg["pp"] * PAGE), pp=cfg["pp"])
        wb = state.shape[1]
        win_buf = _pad_rows(jnp.concatenate([state, new_win], axis=1), wb + cfg["tk_win"])
        nsa_src, win_src = (nsa_buf, 0), (win_buf.reshape(b, wb + cfg["tk_win"], 2 * g * HD), 0)
        dsa_src = dict(kk=dsa_buf, vv=dsa_buf, width=3 * HD, k_blk=0, v_blk=0, offs=(0, 2 * HD, HD))
        win_off = q_off - wb
        new_state = jnp.concatenate([state, new_win[:, :t_real]], axis=1)[:, -wb:]
        l_real = table.shape[1] * PAGE + t_real
    lk = nsa_src[0].shape[1]
    tq = cfg["tq"]

    n_cmp = lk // CMP_STRIDE
    pe_flat = prm["cmp_pe"].reshape(2, 1, CMP_LEN * HD)
    cmp = _compress(nsa_src[0], nsa_src[1], prm["cmp_w1"], pe_flat, prm["cmp_w2"], ng[1].reshape(1, HD))
    nb = n_cmp // CMP_PER_SLC
    nbp = -(-nb // LANES) * LANES
    cmp = cmp.reshape(b, 2, g, nb, CMP_PER_SLC, HD).transpose(0, 1, 2, 4, 3, 5)
    cmp = jnp.pad(cmp, ((0, 0),) * 4 + ((0, nbp - nb), (0, 0))).reshape(b, 2, g, CMP_PER_SLC * nbp, HD).astype(BF16)

    def group_q(cols):
        a = _heads(cols, b, t, N_NSA, HD, SCALE).reshape(b, g, NSA_GROUP, t, HD)
        return _stack_q(a, tq)

    q_n = group_q(proj[:, 0:512])
    q_r = group_q(proj[:, 512:1024])
    n_blk = -(-l_real // SLC_BLOCK)
    o_c, sel = _nsa_cmp(q_n, cmp[:, 0], cmp[:, 1], tq=tq, q_off=q_off, n_sel=min(N_SLC, n_blk))

    o_s = _flash(q_r, nsa_src[0], nsa_src[0], slabs=(nsa_src[1] + 2, nsa_src[1] + 3), reps=NSA_GROUP, tq=tq,
                 tk=cfg["tk"], q_off=q_off, k_off=0, mode="causal", bm=sel)
    o_w = _flash(q_r, win_src[0], win_src[0], slabs=(win_src[1], win_src[1] + 1), reps=NSA_GROUP, tq=tq,
                 tk=cfg["tk_win"], q_off=q_off, k_off=win_off, mode="window")

    q_d = _stack_q(_heads(proj[:, 1792:2304], b, t, N_DSA, HD, SCALE)[:, None], tq)[:, 0]
    q_i = _stack_q(_heads(proj[:, 2304:2816], b, t, N_IDX, HD, IDX_SCALE)[:, None], tq)[:, 0]
    o_d = _dsa(q_i, w_i, q_d, tq=tq, tk=cfg["tk"], q_off=q_off, n_top=min(DSA_TOPK_MAX, l_real // 4), **dsa_src)

    o_c, o_s, o_w = (_unstack_q(o, NSA_GROUP, tq) for o in (o_c, o_s, o_w))
    o_d = _unstack_q(o_d[:, None], N_DSA, tq)
    gfull = [jnp.repeat(gate[:, :, j], HD, axis=1) for j in range(3)]
    d = x2.shape[1]
    out = _mm([o_c, o_s, o_w, o_d] + gfull + [x2], [], prm["od_w_out"], _lhs_odd,
              [(0, d, None, None, [(0, False)])], d, tm=cfg["tm"], res_idx=7)
    return out, new_nsa, new_dsa, new_state


def _cross(x2, b, t, mem_k, mem_v, prm, cfg):
    d = x2.shape[1]
    xw = N_XH * HD
    q = _mm([x2], [prm["g1"], _tile_gain(prm["x_gq"])], prm["x_wq"], _lhs_norm,
            [(0, xw, 0, None, [(0, False)])], xw, tm=cfg["tm"], gains_idx=1)
    tq = cfg["tq"]
    qh = _stack_q(_heads(q, b, t, N_XH, HD, SCALE)[:, :, None], tq)
    o = _flash(qh, mem_k, mem_v, reps=1, tq=tq, tk=mem_k.shape[2], q_off=0, k_off=0, mode="full")
    o = _unstack_q(o, 1, tq)
    return _mm([o, x2], [], prm["x_wo"], _lhs_plain, [(0, d, None, None, [(0, False)])], d, tm=cfg["tm"], res_idx=1)


def _memory_kv(mem2, prm):
    xw = N_XH * HD
    w = jnp.concatenate([prm["x_wk"], prm["x_wv"]], axis=1)
    return _mm([mem2], [_tile_gain(prm["x_gk"])], w, _lhs_plain,
               [(0, xw, 0, None, [(0, False)]), (xw, xw, None, None, [(xw, False)])], 2 * xw,
               tm=min(256, mem2.shape[0]), gains_idx=0)


def _run_group(x, q_off, t_real, mem_kvs, pasts, layers, cfg):
    b, t, d = x.shape
    x2 = x.reshape(b * t, d)
    pos = q_off + jnp.arange(t, dtype=jnp.int32)
    cos, sin = _rope_tables(pos)
    cs = (jnp.tile(cos, (b, 1)), jnp.tile(sin, (b, 1)))
    outs = {}
    for li, prm in enumerate(layers):
        if li % 2 == 0:
            x2, nk, nv = _even_mixer(x2, b, t, q_off, cs, pasts[li], prm, cfg)
            outs["ek"], outs["ev"] = nk, nv
        else:
            x2, nn, nd, nw = _odd_mixer(x2, b, t, t_real, q_off, cs, pasts[li], prm, cfg)
            outs["on"], outs["od"], outs["ow"] = nn, nd, nw
        x2 = _cross(x2, b, t, mem_kvs[li][0], mem_kvs[li][1], prm, cfg)
        if li % 2 == 0:
            x2 = _ffn(x2, prm["g2"], prm["router"], prm["w1"], prm["w3"], prm["w2"], tm=cfg["tm_ffn"],
                      tf=cfg["tf"], routed=False)
        else:
            x2 = _ffn(x2, prm["g2"], prm["router"], prm["w1"], prm["w3"], prm["w2"], tm=cfg["tm_ffn"],
                      tf=cfg["tf"], routed=True)
    return x2.reshape(b, t, d), outs


def kernel(x_prompt, x_sample, mem_prompt, cache_even_k, cache_even_v, cache_odd_nsa, cache_odd_dsa, state_odd_win, cache_mem, page_table, norm_gain, ev_w_in, ev_w_out, df_qk_gain, df_lambda, df_subln_gain, ffn_w1, ffn_w3, ffn_w2, od_w_in, od_w_out, nsa_qk_gain, cmp_pe, cmp_w1, cmp_w2, dsa_qk_gain, moe_router, moe_w1, moe_w3, moe_w2, x_wq, x_wk, x_wv, x_wo, x_qk_gain):
    depth = norm_gain.shape[0]
    bp, tp, d = x_prompt.shape
    bs, ts, _ = x_sample.shape
    n_mem = mem_prompt.shape[1]
    xw = N_XH * HD

    layers = []
    for l in range(depth):
        i = l // 2
        prm = {
            "layer": l,
            "g0": norm_gain[l, 0].reshape(1, d), "g1": norm_gain[l, 1].reshape(1, d), "g2": norm_gain[l, 2].reshape(1, d),
            "x_wq": x_wq[l].astype(BF16), "x_wk": x_wk[l].astype(BF16), "x_wv": x_wv[l].astype(BF16),
            "x_wo": x_wo[l].astype(BF16), "x_gq": x_qk_gain[l, 0], "x_gk": x_qk_gain[l, 1],
        }
        if l % 2 == 0:
            prm.update({
                "ev_w_in": ev_w_in[i].astype(BF16), "ev_w_out": ev_w_out[i].astype(BF16),
                "df_qk_gain": df_qk_gain[i], "df_lambda": df_lambda[i], "df_subln_gain": df_subln_gain[i],
                "router": jnp.zeros((SUBLANES, LANES), F32),
                "w1": ffn_w1[i][None].astype(BF16), "w3": ffn_w3[i][None].astype(BF16), "w2": ffn_w2[i][None].astype(BF16),
            })
        else:
            prm.update({
                "od_w_in": od_w_in[i].astype(BF16), "od_w_out": od_w_out[i].astype(BF16),
                "nsa_qk_gain": nsa_qk_gain[i], "dsa_qk_gain": dsa_qk_gain[i],
                "cmp_pe": cmp_pe[i], "cmp_w1": cmp_w1[i].astype(BF16), "cmp_w2": cmp_w2[i].astype(BF16),
                "router": jnp.pad(moe_router[i], ((0, 0), (0, LANES - N_EXPERTS))),
                "w1": moe_w1[i].astype(BF16), "w3": moe_w3[i].astype(BF16), "w2": moe_w2[i].astype(BF16),
            })
        layers.append(prm)

    def mem_heads(kv, b):
        k = kv[:, :, 0].transpose(0, 2, 1, 3).astype(BF16)
        v = kv[:, :, 1].transpose(0, 2, 1, 3).astype(BF16)
        return k, v

    mem2 = mem_prompt.reshape(bp * n_mem, d)
    mem_p = [_memory_kv(mem2, layers[l]).reshape(bp, n_mem, 2, N_XH, HD) for l in range(depth)]
    ff = ffn_w1.shape[2]
    tf = ff // 2 if (ff // 2) % LANES == 0 else ff
    cfg_p = {"tm": 256, "tq": 128, "tk": 256, "t_even": 256, "tk_win": 256, "tm_ffn": 512, "tf": tf}
    cfg_p["tq"] = min(cfg_p["tq"], tp)
    y_prompt, op = _run_group(x_prompt, 0, tp, [mem_heads(m, bp) for m in mem_p], [None] * depth, layers, cfg_p)
    p_mem = jnp.stack(mem_p)

    n_past = page_table.shape[1] * cache_even_k.shape[2]
    ts_pad = -(-ts // SUBLANES) * SUBLANES
    xs = _pad_rows(x_sample, ts_pad)
    pasts = []
    for l in range(depth):
        i = l // 2
        if l % 2 == 0:
            pasts.append((cache_even_k[i], cache_even_v[i], page_table))
        else:
            pn = cache_odd_nsa[i]
            pd = cache_odd_dsa[i]
            pasts.append((pn.reshape(pn.shape[0], pn.shape[1], -1), pd.reshape(pd.shape[0], pd.shape[1], -1),
                          state_odd_win[i], page_table))
    cfg_s = {"tm": bs * ts_pad, "tq": ts_pad, "tk": 512, "pp": 4, "tk_win": 128, "tm_ffn": bs * ts_pad,
             "tf": tf, "new_rows": 512}
    y_s, os_ = _run_group(xs, n_past, ts, [mem_heads(cache_mem[l], bs) for l in range(depth)], pasts, layers, cfg_s)
    y_sample = y_s[:, :ts]

    g = N_NSA_KV
    return (
        y_prompt, y_sample,
        op["ek"][None], op["ev"][None],
        op["on"].reshape(1, bp, tp, 4, g, HD), op["od"].reshape(1, bp, tp, 3, HD),
        op["ow"][None], p_mem,
        os_["ek"][:, :ts][None], os_["ev"][:, :ts][None],
        os_["on"][:, :ts].reshape(1, bs, ts, 4, g, HD), os_["od"][:, :ts].reshape(1, bs, ts, 3, HD),
        os_["ow"][None],
    )
```

```python
import functools
import math

import jax
import jax.numpy as jnp
import numpy as np
from jax import lax
from jax.experimental import pallas as pl
from jax.experimental.pallas import tpu as pltpu

F32 = jnp.float32
BF16 = jnp.bfloat16

HD = 64
N_SB = 8
N_DF = 4
N_NSA = 8
N_NSA_KV = 2
NSA_GROUP = N_NSA // N_NSA_KV
N_DSA = 8
N_IDX = 8
N_XH = 4
N_EXPERTS = 8
ROPE_THETA = 10000.0
CMP_LEN = 32
CMP_STRIDE = 16
SLC_BLOCK = 64
CMP_PER_SLC = SLC_BLOCK // CMP_STRIDE
N_SLC = 16
N_LOCAL = 2
WINDOW = 512
DSA_TOPK_MAX = 256
EPS = 1e-6
NEG = -1e30
FORCE = 1e9
SCALE = HD ** -0.5
IDX_SCALE = HD ** -0.5
SB_W = N_SB * HD
HD_SHIFT = 6
SLC_SHIFT = 6
DF_W = N_DF * 2 * HD

LANES = 128
SUBLANES = 8
PAGE = 128
VMEM_LIMIT = 52 * 1024 * 1024
INT_MIN = -2 ** 31
ROW_SPLIT = 2
SB_CUT = 120.0
SEL_CHUNK_ROWS = 16
MIN_CHAIN_ROWS = 64
KEY_UNROLL = 4

_NT = (((1,), (1,)), ((), ()))


def _cparams(sem):
    return pltpu.CompilerParams(dimension_semantics=sem, vmem_limit_bytes=VMEM_LIMIT)


def _dot(a, b):
    return jnp.dot(a, b, preferred_element_type=F32)


def _dot_nt(a, b):
    return lax.dot_general(a, b, _NT, preferred_element_type=F32)


def _split_dot(x, m_bf16):
    hi = x.astype(BF16)
    lo = (x - hi.astype(F32)).astype(BF16)
    return _dot(hi, m_bf16) + _dot(lo, m_bf16)


def _split_dot_rhs(m_bf16, x):
    hi = x.astype(BF16)
    lo = (x - hi.astype(F32)).astype(BF16)
    return _dot(m_bf16, hi) + _dot(m_bf16, lo)


def _iota(shape, dim):
    return lax.broadcasted_iota(jnp.int32, shape, dim)


def _rms_rows(x, g):
    return x * lax.rsqrt(jnp.mean(x * x, axis=-1, keepdims=True) + EPS) * g


def _group_mean_matrix():
    r = _iota((LANES, LANES), 0) >> HD_SHIFT
    c = _iota((LANES, LANES), 1) >> HD_SHIFT
    return jnp.where(r == c, 1.0 / HD, 0.0).astype(BF16)


def _head_norm(y, g, gm):
    ms = _split_dot(y * y, gm)
    return y * lax.rsqrt(ms + EPS) * g


def _rope_slab(y, cos, sin):
    lane = _iota(y.shape, 1)
    first = (lane & (HD - 1)) < (HD // 2)
    swapped = jnp.where(first, pltpu.roll(y, LANES - HD // 2, 1), pltpu.roll(y, HD // 2, 1))
    return y * cos + swapped * sin


def _mm_kernel(*refs, n_rows, n_consts, lhs_fn, program, rope_idx, gains_idx, res_idx):
    rows = refs[:n_rows]
    consts = refs[n_rows:n_rows + n_consts]
    w_ref = refs[n_rows + n_consts]
    o_ref = refs[-1]
    lhs = lhs_fn(rows, consts).astype(BF16)
    gm = _group_mean_matrix() if gains_idx is not None else None
    for (src, width, gain_row, act, outs) in program:
        y_full = _dot(lhs, w_ref[:, src:src + width])
        for s in range(width // LANES):
            y = y_full[:, s * LANES:(s + 1) * LANES]
            if gain_row is not None:
                y = _head_norm(y, consts[gains_idx][gain_row:gain_row + 1, :], gm)
            if act == "sigmoid":
                y = 1.0 / (1.0 + jnp.exp(-y))
            for (dst, rope) in outs:
                z = y
                if rope:
                    z = _rope_slab(y, rows[rope_idx[0]][...], rows[rope_idx[1]][...])
                d0 = dst + s * LANES
                if res_idx is not None:
                    z = z + rows[res_idx][:, d0:d0 + LANES]
                o_ref[:, d0:d0 + LANES] = z


def _mm(rows, consts, w, lhs_fn, program, out_cols, *, tm, rope_idx=None, gains_idx=None, res_idx=None, name="mm"):
    n = rows[0].shape[0]
    assert n % tm == 0
    in_specs = [pl.BlockSpec((tm, r.shape[1]), lambda i: (i, 0)) for r in rows]
    in_specs += [pl.BlockSpec(c.shape, lambda i: (0, 0)) for c in consts]
    in_specs += [pl.BlockSpec(w.shape, lambda i: (0, 0))]
    kern = functools.partial(_mm_kernel, n_rows=len(rows), n_consts=len(consts), lhs_fn=lhs_fn,
                             program=program, rope_idx=rope_idx, gains_idx=gains_idx, res_idx=res_idx)
    return pl.pallas_call(
        kern,
        grid=(n // tm,),
        in_specs=in_specs,
        out_specs=pl.BlockSpec((tm, out_cols), lambda i: (i, 0)),
        out_shape=jax.ShapeDtypeStruct((n, out_cols), F32),
        compiler_params=_cparams(("parallel",)),
        name=name,
    )(*rows, *consts, w)


def _lhs_norm(rows, consts):
    return _rms_rows(rows[0][...], consts[0][...])


def _lhs_plain(rows, consts):
    return rows[0][...]


def _lhs_cat2(rows, consts):
    return jnp.concatenate([rows[0][...], rows[1][...]], axis=1)


def _lhs_odd(rows, consts):
    oc, os_, ow, od = rows[0][...], rows[1][...], rows[2][...], rows[3][...]
    g0, g1, g2 = rows[4][...], rows[5][...], rows[6][...]
    return jnp.concatenate([g0 * oc + g1 * os_ + g2 * ow, od], axis=1)


def _ffn_kernel(x_ref, g_ref, r_ref, w1_ref, w3_ref, w2_ref, o_ref, h_ref, acc_ref, gate_ref, *, routed):
    e = pl.program_id(1)
    f = pl.program_id(2)
    first = jnp.logical_and(e == 0, f == 0)
    last = jnp.logical_and(e == pl.num_programs(1) - 1, f == pl.num_programs(2) - 1)

    @pl.when(first)
    def _():
        x = x_ref[...]
        h = _rms_rows(x, g_ref[...])
        h_ref[...] = h.astype(BF16)
        acc_ref[...] = x
        if routed:
            logits = jnp.dot(h, r_ref[...], preferred_element_type=F32, precision=lax.Precision.HIGHEST)
            col = _iota(logits.shape, 1).astype(F32)
            logits = jnp.where(col < N_EXPERTS, logits, -jnp.inf)
            m1 = jnp.max(logits, axis=1, keepdims=True)
            i1 = jnp.min(jnp.where(logits == m1, col, 1e9), axis=1, keepdims=True)
            rest = jnp.where(col == i1, -jnp.inf, logits)
            m2 = jnp.max(rest, axis=1, keepdims=True)
            i2 = jnp.min(jnp.where(rest == m2, col, 1e9), axis=1, keepdims=True)
            e2 = jnp.exp(m2 - m1)
            g1 = 1.0 / (1.0 + e2)
            g2 = e2 / (1.0 + e2)
            gate_ref[...] = jnp.where(col == i1, g1, 0.0) + jnp.where(col == i2, g2, 0.0)

    def compute(gcol):
        h = h_ref[...]
        u = _dot(h, w1_ref[0])
        v = _dot(h, w3_ref[0])
        a = (u * (1.0 / (1.0 + jnp.exp(-u)))) * v
        y = _dot(a.astype(BF16), w2_ref[0])
        if gcol is not None:
            y = gcol * y
        acc_ref[...] += y

    if routed:
        col = _iota(gate_ref.shape, 1)
        gcol = jnp.sum(jnp.where(col == e, gate_ref[...], 0.0), axis=1, keepdims=True)
        active = jnp.max(gcol) > 0.0

        @pl.when(active)
        def _():
            compute(gcol)
    else:
        compute(None)

    @pl.when(last)
    def _():
        o_ref[...] = acc_ref[...]


def _moe_kernel(x_ref, g_ref, r_ref, w1_ref, w3_ref, w2_ref, o_ref, h_ref, acc_ref, gate_ref, slot_ref,
                slot_t_ref, he_ref, ye_ref, *, cap):
    e = pl.program_id(1)
    f = pl.program_id(2)
    nf = pl.num_programs(2)
    tm = x_ref.shape[0]
    first = jnp.logical_and(e == 0, f == 0)
    last = jnp.logical_and(e == pl.num_programs(1) - 1, f == nf - 1)

    @pl.when(first)
    def _():
        x = x_ref[...]
        h = _rms_rows(x, g_ref[...])
        h_ref[...] = h.astype(BF16)
        acc_ref[...] = x
        logits = jnp.dot(h, r_ref[...], preferred_element_type=F32, precision=lax.Precision.HIGHEST)
        col = _iota(logits.shape, 1).astype(F32)
        logits = jnp.where(col < N_EXPERTS, logits, -jnp.inf)
        m1 = jnp.max(logits, axis=1, keepdims=True)
        i1 = jnp.min(jnp.where(logits == m1, col, 1e9), axis=1, keepdims=True)
        rest = jnp.where(col == i1, -jnp.inf, logits)
        m2 = jnp.max(rest, axis=1, keepdims=True)
        i2 = jnp.min(jnp.where(rest == m2, col, 1e9), axis=1, keepdims=True)
        e2 = jnp.exp(m2 - m1)
        gate_ref[...] = jnp.where(col == i1, 1.0 / (1.0 + e2), 0.0) + jnp.where(col == i2, e2 / (1.0 + e2), 0.0)
        chosen = jnp.where(jnp.logical_or(col == i1, col == i2), 1.0, 0.0)
        earlier = jnp.where(_iota((tm, tm), 1) < _iota((tm, tm), 0), 1.0, 0.0).astype(BF16)
        slot = jnp.where(chosen > 0.5, _dot(earlier, chosen.astype(BF16)), -1.0)
        slot_ref[...] = slot
        slot_t_ref[...] = jnp.transpose(slot)

    lane = _iota((tm, LANES), 1)
    mine = lane == e
    gcol = jnp.sum(jnp.where(mine, gate_ref[...], 0.0), axis=1, keepdims=True)
    slot_col = jnp.sum(jnp.where(mine, slot_ref[...], 0.0), axis=1, keepdims=True)
    slot_row = slot_t_ref[pl.ds(e, 1), :]
    count = jnp.sum(jnp.where(slot_row >= 0.0, 1.0, 0.0)).astype(jnp.int32)
    n_chunk = lax.div(count + (cap - 1), cap)

    def pack(c):
        want = (c * cap + _iota((cap, tm), 0)).astype(F32)
        return jnp.where(slot_row == want, 1.0, 0.0).astype(BF16)

    def unpack(c):
        want = (c * cap + _iota((tm, cap), 1)).astype(F32)
        return jnp.where(slot_col == want, 1.0, 0.0).astype(BF16)

    def expert(rows):
        u = _dot(rows, w1_ref[0])
        v = _dot(rows, w3_ref[0])
        a = (u * (1.0 / (1.0 + jnp.exp(-u)))) * v
        return _dot(a.astype(BF16), w2_ref[0])

    @pl.when(n_chunk > 0)
    def _():
        @pl.when(f == 0)
        def _():
            he_ref[...] = _dot(pack(0), h_ref[...]).astype(BF16)
            ye_ref[...] = expert(he_ref[...])

        @pl.when(f > 0)
        def _():
            ye_ref[...] += expert(he_ref[...])

        @pl.when(f == nf - 1)
        def _():
            acc_ref[...] += gcol * _split_dot_rhs(unpack(0), ye_ref[...])

    def extra(c, _):
        part = expert(_dot(pack(c), h_ref[...]).astype(BF16))
        acc_ref[...] += gcol * _split_dot_rhs(unpack(c), part)
        return 0

    lax.fori_loop(1, n_chunk, extra, 0)

    @pl.when(last)
    def _():
        o_ref[...] = acc_ref[...]


def _moe(x, g, router, w1, w3, w2, *, tm, tf, cap):
    n, d = x.shape
    ne, _, ff = w1.shape
    assert n % tm == 0 and ff % tf == 0
    kern = functools.partial(_moe_kernel, cap=cap)
    return pl.pallas_call(
        kern,
        grid=(n // tm, ne, ff // tf),
        in_specs=[
            pl.BlockSpec((tm, d), lambda i, e, f: (i, 0), pipeline_mode=pl.Buffered(1)),
            pl.BlockSpec((1, d), lambda i, e, f: (0, 0)),
            pl.BlockSpec(router.shape, lambda i, e, f: (0, 0)),
            pl.BlockSpec((1, d, tf), lambda i, e, f: (e, 0, f)),
            pl.BlockSpec((1, d, tf), lambda i, e, f: (e, 0, f)),
            pl.BlockSpec((1, tf, d), lambda i, e, f: (e, f, 0)),
        ],
        out_specs=pl.BlockSpec((tm, d), lambda i, e, f: (i, 0), pipeline_mode=pl.Buffered(1)),
        out_shape=jax.ShapeDtypeStruct((n, d), F32),
        scratch_shapes=[pltpu.VMEM((tm, d), BF16), pltpu.VMEM((tm, d), F32), pltpu.VMEM((tm, LANES), F32),
                        pltpu.VMEM((tm, LANES), F32), pltpu.VMEM((LANES, tm), F32),
                        pltpu.VMEM((cap, d), BF16), pltpu.VMEM((cap, d), F32)],
        compiler_params=_cparams(("parallel", "arbitrary", "arbitrary")),
        name="moe_routed",
    )(x, g, router, w1, w3, w2)


def _ffn(x, g, router, w1, w3, w2, *, tm, tf, routed):
    n, d = x.shape
    ne, _, ff = w1.shape
    assert n % tm == 0 and ff % tf == 0
    kern = functools.partial(_ffn_kernel, routed=routed)
    return pl.pallas_call(
        kern,
        grid=(n // tm, ne, ff // tf),
        in_specs=[
            pl.BlockSpec((tm, d), lambda i, e, f: (i, 0)),
            pl.BlockSpec((1, d), lambda i, e, f: (0, 0)),
            pl.BlockSpec(router.shape, lambda i, e, f: (0, 0)),
            pl.BlockSpec((1, d, tf), lambda i, e, f: (e, 0, f)),
            pl.BlockSpec((1, d, tf), lambda i, e, f: (e, 0, f)),
            pl.BlockSpec((1, tf, d), lambda i, e, f: (e, f, 0)),
        ],
        out_specs=pl.BlockSpec((tm, d), lambda i, e, f: (i, 0)),
        out_shape=jax.ShapeDtypeStruct((n, d), F32),
        scratch_shapes=[pltpu.VMEM((tm, d), BF16), pltpu.VMEM((tm, d), F32), pltpu.VMEM((tm, LANES), F32)],
        compiler_params=_cparams(("parallel", "arbitrary", "arbitrary")),
        name="moe" if routed else "ffn",
    )(x, g, router, w1, w3, w2)


def _tile_rows(m, reps):
    return m if reps == 1 else jnp.concatenate([m] * reps, axis=0)


def _online_step(s, v, m, l, acc):
    m_new = jnp.maximum(m, jnp.max(s, axis=1, keepdims=True))
    p = jnp.exp(s - m_new)
    alpha = jnp.exp(m - m_new)
    return m_new, alpha * l + jnp.sum(p, axis=1, keepdims=True), alpha * acc + _dot(p.astype(BF16), v)


def _online_init(rows, dv):
    return (jnp.full((rows, 1), NEG, F32), jnp.zeros((rows, 1), F32), jnp.zeros((rows, dv), F32))


def _unrolled_loop(lo, hi, body, init):
    shift = KEY_UNROLL.bit_length() - 1
    n_group = (hi - lo) >> shift

    def group(p, st):
        for u in range(KEY_UNROLL):
            st = body(lo + KEY_UNROLL * p + u, st)
        return st

    st = lax.fori_loop(0, n_group, group, init)
    return lax.fori_loop(lo + n_group * KEY_UNROLL, hi, body, st)


def _flash_kernel(*refs, reps, tq, tk, q_off, k_off, mode, has_bm, slab):
    refs = list(refs)
    q_ref = refs.pop(0)
    k_ref = refs.pop(0)
    v_ref = None if slab else refs.pop(0)
    bm_ref = refs.pop(0) if has_bm else None
    o_ref = refs.pop(0)
    i = pl.program_id(2)
    q = q_ref[0, 0, 0]
    dv = HD if slab else v_ref.shape[-1]
    n_kt = k_ref.shape[-2] // tk
    q_lo = q_off + i * tq

    def kv_tiles(k0):
        if not slab:
            return k_ref[0, 0, pl.ds(k0, tk), :], v_ref[0, 0, pl.ds(k0, tk), :]
        t = k_ref[0, pl.ds(k0, tk), :]
        first = pl.program_id(1) == 0
        return (jnp.where(first, t[:, :HD], t[:, HD:2 * HD]).astype(BF16),
                jnp.where(first, t[:, 2 * HD:3 * HD], t[:, 3 * HD:]).astype(BF16))

    if mode == "full":
        lo, hi = 0, n_kt
    else:
        hi = jnp.minimum(lax.div(q_lo + tq - 1 - k_off, tk) + 1, n_kt)
        lo = 0
        if mode == "window":
            lo = lax.div(jnp.maximum(q_lo - (WINDOW - 1) - k_off, 0), tk)
    qpos = q_lo + _iota((tq, tk), 0)
    if has_bm:
        bm = bm_ref[0, 0].astype(BF16)
        nbp = bm.shape[1]

    def body(j, state):
        k0 = pl.multiple_of(j * tk, tk)
        k, v = kv_tiles(k0)
        bias = None
        if mode != "full":
            kpos = k_off + j * tk + _iota((tq, tk), 1)
            ok = kpos <= qpos
            if mode == "window":
                ok = jnp.logical_and(ok, qpos - kpos < WINDOW)
                ok = jnp.logical_and(ok, kpos >= 0)
            bias = jnp.where(ok, 0.0, NEG)
        if has_bm:
            blk = _iota((nbp, tk), 0)
            tok = (j * tk + _iota((nbp, tk), 1)) >> SLC_SHIFT
            expand = jnp.where(blk == tok, 1.0, 0.0).astype(BF16)
            bias = jnp.where(_dot(bm, expand) > 0.5, bias, NEG)
        if bias is not None:
            bias = _tile_rows(bias, cr // tq)
        new = []
        for c in range(n_chain):
            s = _dot_nt(q[c * cr:(c + 1) * cr].astype(BF16), k)
            if bias is not None:
                s = s + bias
            new.append(_online_step(s, v, *state[c]))
        return tuple(new)

    n_chain = reps if tq >= MIN_CHAIN_ROWS else 1
    cr = reps * tq // n_chain
    state = _unrolled_loop(lo, hi, body, tuple(_online_init(cr, dv) for _ in range(n_chain)))
    for c in range(n_chain):
        o_ref[0, 0, 0, c * cr:(c + 1) * cr, :] = state[c][2] / state[c][1]


def _flash(q, k, v=None, *, reps, tq, tk, q_off, k_off, mode, bm=None, kv_blk=None):
    b, hk, nq, rows, _ = q.shape
    lk = k.shape[-2]
    assert rows == reps * tq and lk % tk == 0
    if kv_blk is None:
        dv = v.shape[3]
        kv_specs = [pl.BlockSpec((1, 1, lk, HD), lambda b_, h, i: (b_, h, 0, 0)),
                    pl.BlockSpec((1, 1, lk, dv), lambda b_, h, i: (b_, h, 0, 0))]
        args = [q, k, v]
    else:
        assert hk == 2 and v is None
        dv = HD
        kv_specs = [pl.BlockSpec((1, lk, 2 * LANES), lambda b_, h, i: (b_, 0, kv_blk))]
        args = [q, k]
    in_specs = [pl.BlockSpec((1, 1, 1, rows, HD), lambda b_, h, i: (b_, h, i, 0, 0))] + kv_specs
    if bm is not None:
        in_specs.append(pl.BlockSpec((1, 1, tq, bm.shape[3]), lambda b_, h, i: (b_, h, i, 0)))
        args.append(bm)
    kern = functools.partial(_flash_kernel, reps=reps, tq=tq, tk=tk, q_off=q_off, k_off=k_off, mode=mode,
                             has_bm=bm is not None, slab=kv_blk is not None)
    return pl.pallas_call(
        kern,
        grid=(b, hk, nq),
        in_specs=in_specs,
        out_specs=pl.BlockSpec((1, 1, 1, rows, dv), lambda b_, h, i: (b_, h, i, 0, 0)),
        out_shape=jax.ShapeDtypeStruct((b, hk, nq, rows, dv), F32),
        compiler_params=_cparams(("parallel", "parallel", "arbitrary")),
        name="flash_" + mode + ("_blockmask" if bm is not None else ""),
    )(*args)


def _softplus(z):
    return jnp.maximum(z, 0.0) + jnp.log(1.0 + jnp.exp(-jnp.abs(z)))


def _later_matrix(n):
    return jnp.where(_iota((n, n), 0) > _iota((n, n), 1), 1.0, 0.0).astype(BF16)


def _sb_kernel(q_ref, k_ref, v_ref, o_ref, *, t):
    i = pl.program_id(2)
    th = t // ROW_SPLIT
    lane = _iota((th, LANES), 1)
    later = _later_matrix(t)
    qs = []
    for part in range(ROW_SPLIT):
        qf = q_ref[0, part * th:(part + 1) * th, :] * SCALE
        qs.append((jnp.where(lane < HD, qf, 0.0).astype(BF16), jnp.where(lane >= HD, qf, 0.0).astype(BF16)))

    def tile(j, carries, outs, masked):
        k0 = pl.multiple_of(j * t, t)
        k = k_ref[0, pl.ds(k0, t), :].astype(BF16)
        v = v_ref[0, pl.ds(k0, t), :].astype(BF16)
        new_carries, new_outs = [], []
        for part in range(ROW_SPLIT):
            heads = []
            for hd in range(2):
                carry = carries[2 * part + hd]
                z = _dot_nt(qs[part][hd], k)
                sp = _softplus(z)
                log_sig = z - sp
                if masked:
                    vis = _iota((th, t), 1) < part * th + _iota((th, t), 0)
                    sp = jnp.where(vis, sp, 0.0)
                a = jnp.exp(log_sig - _dot(sp.astype(BF16), later) - carry)
                if masked:
                    a = jnp.where(vis, a, 0.0)
                heads.append(_dot(a.astype(BF16), v))
                new_carries.append(carry + jnp.sum(sp, axis=1, keepdims=True))
            new_outs.append(outs[part] + jnp.where(lane < HD, heads[0], heads[1]))
        return tuple(new_carries), tuple(new_outs)

    def min_carry(carries):
        m = jnp.min(carries[0])
        for c in carries[1:]:
            m = jnp.minimum(m, jnp.min(c))
        return m

    zero = jnp.zeros((th, 1), F32)
    carries, outs = tile(i, (zero,) * (2 * ROW_SPLIT), (jnp.zeros((th, LANES), F32),) * ROW_SPLIT, True)

    def cond(st):
        return jnp.logical_and(st[0] < i, st[1] < SB_CUT)

    def body(st):
        carries, outs = tile(i - 1 - st[0], st[2], st[3], False)
        return st[0] + 1, min_carry(carries), carries, outs

    _, _, carries, outs = lax.while_loop(cond, body, (jnp.int32(0), min_carry(carries), carries, outs))
    for part in range(ROW_SPLIT):
        o_ref[0, part * th:(part + 1) * th, :] = outs[part]


def _sb(proj3, *, t, q_slab, k_slab, v_slab, n_slabs):
    b, tt, _ = proj3.shape
    assert tt % t == 0
    kern = functools.partial(_sb_kernel, t=t)
    return pl.pallas_call(
        kern,
        grid=(b, n_slabs, tt // t),
        in_specs=[
            pl.BlockSpec((1, t, LANES), lambda b_, p, i: (b_, i, q_slab + p)),
            pl.BlockSpec((1, tt, LANES), lambda b_, p, i: (b_, 0, k_slab + p)),
            pl.BlockSpec((1, tt, LANES), lambda b_, p, i: (b_, 0, v_slab + p)),
        ],
        out_specs=pl.BlockSpec((1, t, LANES), lambda b_, p, i: (b_, i, p)),
        out_shape=jax.ShapeDtypeStruct((b, tt, n_slabs * LANES), F32),
        compiler_params=_cparams(("parallel", "parallel", "arbitrary")),
        name="sb_prompt",
    )(proj3, proj3, proj3)


def _lam(lv, lam_init):
    a = jnp.sum(jnp.sum(lv[0:1] * lv[1:2], axis=1, keepdims=True), axis=0, keepdims=True)
    b = jnp.sum(jnp.sum(lv[2:3] * lv[3:4], axis=1, keepdims=True), axis=0, keepdims=True)
    return jnp.exp(a) - jnp.exp(b) + lam_init


def _df_kernel(lv_ref, g_ref, q_ref, k_ref, v_ref, o_ref, *, t, lam_init):
    i = pl.program_id(2)
    th = t // ROW_SPLIT
    lane = _iota((th, LANES), 1)
    qs = []
    for part in range(ROW_SPLIT):
        qf = q_ref[0, part * th:(part + 1) * th, :] * SCALE
        qs.append((jnp.where(lane < HD, qf, 0.0).astype(BF16), jnp.where(lane >= HD, qf, 0.0).astype(BF16)))

    def tile(j, state, masked):
        k0 = pl.multiple_of(j * t, t)
        k = k_ref[0, pl.ds(k0, t), :].astype(BF16)
        v = v_ref[0, pl.ds(k0, t), :].astype(BF16)
        new = []
        for part in range(ROW_SPLIT):
            for mp in range(2):
                s = _dot_nt(qs[part][mp], k)
                if masked:
                    s = jnp.where(_iota((th, t), 1) <= part * th + _iota((th, t), 0), s, NEG)
                new.append(_online_step(s, v, *state[2 * part + mp]))
        return tuple(new)

    init = tuple(_online_init(th, LANES) for _ in range(2 * ROW_SPLIT))
    state = _unrolled_loop(0, i, lambda j, st: tile(j, st, False), init)
    state = tile(i, state, True)
    lam = _lam(lv_ref[...], lam_init)
    for part in range(ROW_SPLIT):
        (_, l0, a0), (_, l1, a1) = state[2 * part], state[2 * part + 1]
        d = a0 / l0 - lam * (a1 / l1)
        o_ref[0, part * th:(part + 1) * th, :] = _rms_rows(d, g_ref[...]) * (1.0 - lam_init)


def _df(proj3, lv, subln, *, t, q_slab, k_slab, v_slab, n_slabs, lam_init):
    b, tt, _ = proj3.shape
    assert tt % t == 0
    kern = functools.partial(_df_kernel, t=t, lam_init=lam_init)
    return pl.pallas_call(
        kern,
        grid=(b, n_slabs, tt // t),
        in_specs=[
            pl.BlockSpec(lv.shape, lambda b_, p, i: (0, 0)),
            pl.BlockSpec(subln.shape, lambda b_, p, i: (0, 0)),
            pl.BlockSpec((1, t, LANES), lambda b_, p, i: (b_, i, q_slab + p)),
            pl.BlockSpec((1, tt, LANES), lambda b_, p, i: (b_, 0, k_slab + p)),
            pl.BlockSpec((1, tt, LANES), lambda b_, p, i: (b_, 0, v_slab + p)),
        ],
        out_specs=pl.BlockSpec((1, t, LANES), lambda b_, p, i: (b_, i, p)),
        out_shape=jax.ShapeDtypeStruct((b, tt, n_slabs * LANES), F32),
        compiler_params=_cparams(("parallel", "parallel", "arbitrary")),
        name="df_prompt",
    )(lv, subln, proj3, proj3, proj3)


def _dec_even_kernel(*refs, pp, ts, lam_init):
    tbl_ref, qt_ref = refs[0], refs[1]
    k_refs = refs[2:2 + pp]
    v_refs = refs[2 + pp:2 + 2 * pp]
    kn_ref, vn_ref, lv_ref, g_ref, o_ref, later_ref, st_ref, asb_ref, adf_ref = refs[2 + 2 * pp:]
    del tbl_ref
    s_id = pl.program_id(1)
    nk = pp * PAGE
    half = LANES // 2
    qt = qt_ref[0]

    def col_of(row):
        return jnp.transpose(jnp.broadcast_to(row, (SUBLANES, LANES)))[half:, 0:1]

    def tile(kt, vt, later, vis_sb, vis_df):
        carry, m, l = st_ref[0:1, :], st_ref[1:2, :], st_ref[2:3, :]
        zt = _dot(kt, qt)
        lane = _iota(zt.shape, 1)
        sp = _softplus(zt)
        log_sig = zt - sp
        s = zt
        if vis_sb is not None:
            sp = jnp.where(vis_sb, sp, 0.0)
            s = jnp.where(vis_df, s, NEG)
        a = jnp.exp(log_sig - _dot(later, sp.astype(BF16)) - carry)
        if vis_sb is not None:
            a = jnp.where(vis_sb, a, 0.0)
        m_new = jnp.maximum(m, jnp.max(s, axis=0, keepdims=True))
        p = jnp.exp(s - m_new)
        alpha = jnp.exp(m - m_new)
        st_ref[0:1, :] = carry + jnp.sum(sp, axis=0, keepdims=True)
        st_ref[1:2, :] = m_new
        st_ref[2:3, :] = alpha * l + jnp.sum(p, axis=0, keepdims=True)
        w = jnp.transpose(jnp.where(lane < half, a, p)).astype(BF16)
        asb_ref[...] += _dot(w[:half], vt[:, :SB_W])
        adf_ref[...] = col_of(alpha) * adf_ref[...] + _dot(w[half:], vt[:, SB_W:])

    @pl.when(s_id == 0)
    def _():
        later_ref[...] = jnp.where(_iota((nk, nk), 1) > _iota((nk, nk), 0), 1.0, 0.0).astype(BF16)
        st_ref[...] = jnp.where(_iota(st_ref.shape, 0) == 1, NEG, 0.0)
        asb_ref[...] = jnp.zeros_like(asb_ref)
        adf_ref[...] = jnp.zeros_like(adf_ref)
        key = _iota((PAGE, LANES), 0)
        tok = _iota((PAGE, LANES), 1) & (ts - 1)
        tile(kn_ref[0].astype(BF16), vn_ref[0].astype(BF16), later_ref[0:PAGE, 0:PAGE], key < tok, key <= tok)

    kt = jnp.concatenate([r[0].astype(BF16) for r in k_refs], axis=0)
    vt = jnp.concatenate([r[0].astype(BF16) for r in v_refs], axis=0)
    tile(kt, vt, later_ref[...], None, None)

    @pl.when(s_id == pl.num_programs(1) - 1)
    def _():
        row = _iota((half, SB_W), 0)
        lane = _iota((half, SB_W), 1)

        def fold(x):
            out = x[0:ts]
            for u in range(1, half // ts):
                out = out + x[u * ts:(u + 1) * ts]
            return out

        o_sb = fold(jnp.where((row >> 3) == (lane >> HD_SHIFT), asb_ref[...], 0.0))
        pn = adf_ref[...] / col_of(st_ref[2:3, :])
        same_head = (row >> 4) == (lane >> 7)
        o0 = fold(jnp.where(jnp.logical_and(same_head, ((row >> 3) & 1) == 0), pn, 0.0))
        o1 = fold(jnp.where(jnp.logical_and(same_head, ((row >> 3) & 1) == 1), pn, 0.0))
        d = o0 - _lam(lv_ref[...], lam_init) * o1
        parts = [o_sb]
        for h in range(N_DF):
            parts.append(_rms_rows(d[:, h * LANES:(h + 1) * LANES], g_ref[...]) * (1.0 - lam_init))
        o_ref[0] = jnp.concatenate(parts, axis=1)


def _dec_even(qt, pool_k, pool_v, table, k_new, v_new, lv, subln, *, pp, ts, lam_init):
    b, n_pages = table.shape
    width = pool_k.shape[2]
    assert n_pages % pp == 0 and ts == SUBLANES and k_new.shape[1] == PAGE
    n_steps = n_pages // pp

    def page_map(u):
        return lambda b_, s, t: (t[b_, n_pages - (s + 1) * pp + u], 0, 0)

    page_specs = [pl.BlockSpec((1, PAGE, width), page_map(u)) for u in range(pp)]
    new_spec = pl.BlockSpec((1, PAGE, width), lambda b_, s, t: (b_, 0, 0))
    grid_spec = pltpu.PrefetchScalarGridSpec(
        num_scalar_prefetch=1,
        grid=(b, n_steps),
        in_specs=[pl.BlockSpec((1,) + qt.shape[1:], lambda b_, s, t: (b_, 0, 0))] + page_specs + page_specs
        + [new_spec, new_spec, pl.BlockSpec(lv.shape, lambda b_, s, t: (0, 0)),
           pl.BlockSpec(subln.shape, lambda b_, s, t: (0, 0))],
        out_specs=pl.BlockSpec((1, ts, width), lambda b_, s, t: (b_, 0, 0)),
        scratch_shapes=[
            pltpu.VMEM((pp * PAGE, pp * PAGE), BF16),
            pltpu.VMEM((SUBLANES, LANES), F32),
            pltpu.VMEM((LANES // 2, SB_W), F32),
            pltpu.VMEM((LANES // 2, DF_W), F32),
        ],
    )
    kern = functools.partial(_dec_even_kernel, pp=pp, ts=ts, lam_init=lam_init)
    return pl.pallas_call(
        kern,
        grid_spec=grid_spec,
        out_shape=jax.ShapeDtypeStruct((b, ts, width), F32),
        compiler_params=_cparams(("parallel", "arbitrary")),
        name="dec_even",
    )(table, qt, *([pool_k] * pp), *([pool_v] * pp), k_new, v_new, lv, subln)


def _compress_kernel(t_ref, w1_ref, w1g_ref, pe_ref, w2_ref, g_ref, o_ref):
    kind = pl.program_id(1)
    n = t_ref.shape[1] // CMP_STRIDE
    a1 = [None] * N_NSA_KV
    a2 = [None] * N_NSA_KV
    for p in range(CMP_STRIDE):
        rows = t_ref[0, pl.ds(p, n, stride=CMP_STRIDE), :].astype(BF16)
        for g in range(N_NSA_KV):
            d1 = _dot(rows, w1g_ref[0, g, p])
            d2 = _dot(rows, w1g_ref[0, g, CMP_STRIDE + p])
            a1[g] = d1 if a1[g] is None else a1[g] + d1
            a2[g] = d2 if a2[g] is None else a2[g] + d2
    pe = _dot(jnp.broadcast_to(pe_ref[0], (SUBLANES, CMP_LEN * HD)).astype(BF16), w1_ref[0])[0:1]
    for g in range(N_NSA_KV):
        nxt = jnp.where(_iota(a2[g].shape, 0) < n - 1, pltpu.roll(a2[g], n - 1, 0), 0.0)
        hid = a1[g] + nxt + pe
        hid = hid * (1.0 / (1.0 + jnp.exp(-hid)))
        out = _dot(hid.astype(BF16), w2_ref[0])
        normed = _rms_rows(out, g_ref[...])
        o_ref[0, 0, g] = jnp.where(kind == 0, normed, out)


def _compress(t, slab0, w1, pe_flat, w2, gain):
    b, lk, _ = t.shape
    g = N_NSA_KV
    n = lk // CMP_STRIDE
    pieces = w1.reshape(2, CMP_LEN, HD, w1.shape[2])
    w1g = jnp.stack([jnp.pad(pieces, ((0, 0), (0, 0), (HD * gi, HD * (g - 1 - gi)), (0, 0))) for gi in range(g)],
                    axis=1)
    return pl.pallas_call(
        _compress_kernel,
        grid=(b, 2),
        in_specs=[
            pl.BlockSpec((1, lk, LANES), lambda b_, k: (b_, 0, slab0 + k)),
            pl.BlockSpec((1,) + w1.shape[1:], lambda b_, k: (k, 0, 0)),
            pl.BlockSpec((1,) + w1g.shape[1:], lambda b_, k: (k, 0, 0, 0, 0)),
            pl.BlockSpec((1,) + pe_flat.shape[1:], lambda b_, k: (k, 0, 0)),
            pl.BlockSpec((1,) + w2.shape[1:], lambda b_, k: (k, 0, 0)),
            pl.BlockSpec(gain.shape, lambda b_, k: (0, 0)),
        ],
        out_specs=pl.BlockSpec((1, 1, g, n, HD), lambda b_, k: (b_, k, 0, 0, 0)),
        out_shape=jax.ShapeDtypeStruct((b, 2, g, n, HD), F32),
        compiler_params=_cparams(("parallel", "arbitrary")),
        name="compress",
    )(t, w1, w1g, pe_flat, w2, gain)


def _nsa_cmp_kernel(q_ref, kc_ref, vc_ref, o_ref, sel_ref, *, tq, q_off, n_sel):
    i = pl.program_id(2)
    q = q_ref[0, 0, 0].astype(BF16)
    kc = kc_ref[0, 0]
    vc = vc_ref[0, 0]
    ncol = kc.shape[0]
    nbp = ncol // CMP_PER_SLC
    q_lo = q_off + i * tq
    qpos = q_lo + _iota((tq, ncol), 0)
    col = _iota((tq, ncol), 1)
    jj = jnp.zeros_like(col)
    for u in range(1, CMP_PER_SLC):
        jj = jj + jnp.where(col >= u * nbp, 1, 0)
    c_end = (col - jj * nbp) * SLC_BLOCK + jj * CMP_STRIDE + (CMP_LEN - 1)
    maskf = _tile_rows(jnp.where(c_end <= qpos, 1.0, 0.0), NSA_GROUP)
    keep = maskf > 0.5
    s = jnp.where(keep, _dot_nt(q, kc), NEG)
    m = jnp.max(s, axis=1, keepdims=True)
    p = jnp.where(keep, jnp.exp(s - m), 0.0)
    p = p / jnp.maximum(jnp.sum(p, axis=1, keepdims=True), 1e-30)
    o_ref[0, 0, 0] = _dot(p.astype(BF16), vc)
    pg = p[0:tq]
    for r in range(1, NSA_GROUP):
        pg = pg + p[r * tq:(r + 1) * tq]
    imp = pg[:, 0:nbp]
    for j in range(1, CMP_PER_SLC):
        imp = imp + pg[:, j * nbp:(j + 1) * nbp]
    blk = _iota((tq, nbp), 1)
    qp = q_lo + _iota((tq, nbp), 0)
    q_blk = qp >> SLC_SHIFT
    visible = blk * SLC_BLOCK <= qp
    forced = jnp.logical_or(blk == 0, jnp.logical_and(blk <= q_blk, blk > q_blk - N_LOCAL))
    score = jnp.where(visible, jnp.where(forced, FORCE, imp), NEG)
    rc = min(tq, SEL_CHUNK_ROWS)
    blkf = _iota((rc, nbp), 1).astype(F32)
    scores = [score[c * rc:(c + 1) * rc] for c in range(tq // rc)]
    sels = [jnp.zeros((rc, nbp), F32) for _ in scores]
    for _ in range(n_sel):
        for c in range(len(scores)):
            top = jnp.max(scores[c], axis=1, keepdims=True)
            idx = jnp.min(jnp.where(scores[c] == top, blkf, 1e9), axis=1, keepdims=True)
            pick = blkf == idx
            sels[c] = jnp.where(pick, 1.0, sels[c])
            scores[c] = jnp.where(pick, -jnp.inf, scores[c])
    for c in range(len(scores)):
        sel_ref[0, 0, c * rc:(c + 1) * rc, :] = sels[c]


def _nsa_cmp(q, kc, vc, *, tq, q_off, n_sel):
    b, g, nq, rows, _ = q.shape
    ncol = kc.shape[2]
    nbp = ncol // CMP_PER_SLC
    kern = functools.partial(_nsa_cmp_kernel, tq=tq, q_off=q_off, n_sel=n_sel)
    return pl.pallas_call(
        kern,
        grid=(b, g, nq),
        in_specs=[
            pl.BlockSpec((1, 1, 1, rows, HD), lambda b_, g_, i: (b_, g_, i, 0, 0)),
            pl.BlockSpec((1, 1, ncol, HD), lambda b_, g_, i: (b_, g_, 0, 0)),
            pl.BlockSpec((1, 1, ncol, HD), lambda b_, g_, i: (b_, g_, 0, 0)),
        ],
        out_specs=[
            pl.BlockSpec((1, 1, 1, rows, HD), lambda b_, g_, i: (b_, g_, i, 0, 0)),
            pl.BlockSpec((1, 1, tq, nbp), lambda b_, g_, i: (b_, g_, i, 0)),
        ],
        out_shape=[
            jax.ShapeDtypeStruct((b, g, nq, rows, HD), F32),
            jax.ShapeDtypeStruct((b, g, nq * tq, nbp), F32),
        ],
        compiler_params=_cparams(("parallel", "parallel", "arbitrary")),
        name="nsa_cmp",
    )(q, kc, vc)


def _dsa_kernel(qi_ref, wi_ref, qd_ref, kk_ref, o_ref, key_ref, *, tq, tk, q_off, n_top, offs):
    i = pl.program_id(1)
    qi = qi_ref[0, 0]
    qd = qd_ref[0, 0]
    w = wi_ref[0] * (N_IDX ** -0.5)
    n_kt = kk_ref.shape[1] // tk
    kd_off, ki_off, vd_off = offs

    def cols(ref, k0, off):
        return ref[0, pl.ds(k0, tk), :][:, off:off + HD].astype(BF16)

    q_lo = q_off + i * tq
    hi = jnp.minimum(lax.div(q_lo + tq - 1, tk) + 1, n_kt)
    qpos = q_lo + _iota((tq, tk), 0)
    n_chain = N_DSA if tq >= MIN_CHAIN_ROWS else 1
    cr = N_DSA * tq // n_chain

    def visible(j):
        return (j * tk + _iota((tq, tk), 1)) <= qpos

    def score_tile(j, _):
        k0 = pl.multiple_of(j * tk, tk)
        k = cols(kk_ref, k0, ki_off)
        tot = None
        for c in range(n_chain):
            sc = jnp.maximum(_dot_nt(qi[c * cr:(c + 1) * cr].astype(BF16), k), 0.0)
            for u in range(cr // tq):
                h = c * (cr // tq) + u
                term = w[:, h:h + 1] * sc[u * tq:(u + 1) * tq]
                tot = term if tot is None else tot + term
        tot = jnp.where(visible(j), tot, NEG)
        bits = pltpu.bitcast(tot, jnp.int32)
        key = jnp.where(bits < 0, bits ^ jnp.int32(0x7FFFFFFF), bits)
        key_ref[:, pl.ds(k0, tk)] = jnp.where(tot == 0.0, 0, key)
        return 0

    _unrolled_loop(0, hi, score_tile, 0)

    def count_ge(c):
        def body(j, acc):
            blk = key_ref[:, pl.ds(pl.multiple_of(j * tk, tk), tk)]
            hit = jnp.where(blk >= c, 1.0, 0.0)
            part = hit[:, 0:LANES]
            for u in range(1, tk // LANES):
                part = part + hit[:, u * LANES:(u + 1) * LANES]
            return acc + part
        acc = _unrolled_loop(0, hi, body, jnp.zeros((tq, LANES), F32))
        return jnp.sum(acc, axis=1, keepdims=True)

    kf = float(n_top)
    tau = jnp.where(count_ge(jnp.zeros((tq, 1), jnp.int32)) >= kf, 0, INT_MIN).astype(jnp.int32)

    def bit_body(t, tau):
        cand = tau + jnp.left_shift(jnp.int32(1), 30 - t)
        return jnp.where(count_ge(cand) >= kf, cand, tau)

    tau = lax.fori_loop(0, 31, bit_body, tau)
    need = kf - count_ge(tau + 1)
    before = jnp.where(_iota((tk, tk), 0) < _iota((tk, tk), 1), 1.0, 0.0).astype(BF16)

    def attend(j, carry):
        state, n_eq = carry
        k0 = pl.multiple_of(j * tk, tk)
        key = key_ref[:, pl.ds(k0, tk)]
        eqf = jnp.where(key == tau, 1.0, 0.0)
        rank = n_eq + _dot(eqf.astype(BF16), before)
        kept = jnp.logical_or(key > tau, jnp.logical_and(key == tau, rank < need))
        bias = jnp.where(jnp.logical_and(kept, visible(j)), 0.0, NEG)
        k = cols(kk_ref, k0, kd_off)
        v = cols(kk_ref, k0, vd_off)
        bias = _tile_rows(bias, cr // tq)
        new = []
        for c in range(n_chain):
            s = _dot_nt(qd[c * cr:(c + 1) * cr].astype(BF16), k) + bias
            new.append(_online_step(s, v, *state[c]))
        return tuple(new), n_eq + jnp.sum(eqf, axis=1, keepdims=True)

    init = (tuple(_online_init(cr, HD) for _ in range(n_chain)), jnp.zeros((tq, 1), F32))
    state, _ = _unrolled_loop(0, hi, attend, init)
    for c in range(n_chain):
        o_ref[0, 0, c * cr:(c + 1) * cr, :] = state[c][2] / state[c][1]


def _dsa(qi, wi, qd, kk, *, width, blk, offs, tq, tk, q_off, n_top):
    b, nq, rows, _ = qi.shape
    lk = kk.shape[1]
    assert lk % tk == 0
    kern = functools.partial(_dsa_kernel, tq=tq, tk=tk, q_off=q_off, n_top=n_top, offs=offs)
    qspec = pl.BlockSpec((1, 1, rows, HD), lambda b_, i: (b_, i, 0, 0))
    return pl.pallas_call(
        kern,
        grid=(b, nq),
        in_specs=[qspec, pl.BlockSpec((1, tq, N_IDX), lambda b_, i: (b_, i, 0)), qspec,
                  pl.BlockSpec((1, lk, width), lambda b_, i: (b_, 0, blk))],
        out_specs=pl.BlockSpec((1, 1, rows, HD), lambda b_, i: (b_, i, 0, 0)),
        out_shape=jax.ShapeDtypeStruct((b, nq, rows, HD), F32),
        scratch_shapes=[pltpu.VMEM((tq, lk), jnp.int32)],
        compiler_params=_cparams(("parallel", "arbitrary")),
        name="dsa",
    )(qi, wi, qd, kk)


def _gather_kernel(*refs, pp, n_steps):
    pool_refs = refs[1:1 + pp]
    new_ref, o_ref = refs[1 + pp], refs[2 + pp]
    s = pl.program_id(1)

    @pl.when(s < n_steps)
    def _():
        for u in range(pp):
            o_ref[0, u * PAGE:(u + 1) * PAGE, :] = pool_refs[u][0]

    @pl.when(s >= n_steps)
    def _():
        o_ref[...] = new_ref[...]


def _page_gather(pool, table, new, *, pp):
    b, n_pages = table.shape
    width = pool.shape[2]
    assert n_pages % pp == 0 and new.shape[1] == pp * PAGE
    n_steps = n_pages // pp

    def page_map(u):
        return lambda b_, s, t: (t[b_, jnp.minimum(s, n_steps - 1) * pp + u], 0, 0)

    kern = functools.partial(_gather_kernel, pp=pp, n_steps=n_steps)
    grid_spec = pltpu.PrefetchScalarGridSpec(
        num_scalar_prefetch=1,
        grid=(b, n_steps + 1),
        in_specs=[pl.BlockSpec((1, PAGE, width), page_map(u)) for u in range(pp)]
        + [pl.BlockSpec((1, pp * PAGE, width), lambda b_, s, t: (b_, 0, 0))],
        out_specs=pl.BlockSpec((1, pp * PAGE, width), lambda b_, s, t: (b_, s, 0)),
    )
    return pl.pallas_call(
        kern,
        grid_spec=grid_spec,
        out_shape=jax.ShapeDtypeStruct((b, (n_pages + pp) * PAGE, width), pool.dtype),
        compiler_params=_cparams(("parallel", "arbitrary")),
        name="page_gather",
    )(table, *([pool] * pp), new)


def _rope_tables(pos):
    half = HD // 2
    inv = ROPE_THETA ** (-jnp.arange(half, dtype=F32) / half)
    ang = pos.astype(F32)[:, None] * inv[None, :]
    cos, sin = jnp.cos(ang), jnp.sin(ang)
    cos128 = jnp.tile(jnp.concatenate([cos, cos], axis=1), (1, LANES // HD))
    sin128 = jnp.tile(jnp.concatenate([-sin, sin], axis=1), (1, LANES // HD))
    return cos128, sin128


def _tile_gain(g):
    return jnp.tile(g.reshape(1, HD), (1, LANES // HD))


def _heads(a, b, t, h, d, scale=None):
    a = a.reshape(b, t, h, d)
    if scale is not None:
        a = a * scale
    return a.transpose(0, 2, 1, 3)


def _stack_q(a, tq):
    b, hk, r, t, d = a.shape
    return a.reshape(b, hk, r, t // tq, tq, d).transpose(0, 1, 3, 2, 4, 5).reshape(b, hk, t // tq, r * tq, d)


def _unstack_q(a, r, tq):
    b, hk, nq, _, d = a.shape
    a = a.reshape(b, hk, nq, r, tq, d).transpose(0, 2, 4, 1, 3, 5)
    return a.reshape(b * nq * tq, hk * r * d)


def _pad_rows(a, rows):
    return jnp.pad(a, ((0, 0), (0, rows - a.shape[1])) + ((0, 0),) * (a.ndim - 2))


def _even_mixer(x2, b, t, q_off, cs, past, prm, cfg):
    n = b * t
    program = [
        (0, SB_W, None, None, [(0, False)]),
        (3 * SB_W, DF_W, 0, None, [(SB_W, True)]),
        (SB_W, SB_W, None, None, [(SB_W + DF_W, False)]),
        (3 * SB_W + DF_W, DF_W, 1, None, [(2 * SB_W + DF_W, True)]),
        (2 * SB_W, SB_W, None, None, [(2 * (SB_W + DF_W), False)]),
        (3 * SB_W + 2 * DF_W, DF_W, None, None, [(3 * SB_W + 2 * DF_W, False)]),
    ]
    gains = jnp.concatenate([_tile_gain(prm["df_qk_gain"][0]), _tile_gain(prm["df_qk_gain"][1])], axis=0)
    proj = _mm([x2, cs[0], cs[1]], [prm["g0"], gains], prm["ev_w_in"], _lhs_norm, program, 3 * (SB_W + DF_W),
               tm=cfg["tm"], rope_idx=(1, 2), gains_idx=1)
    mw = SB_W + DF_W
    new_k = proj[:, mw:2 * mw].reshape(b, t, mw)
    new_v = proj[:, 2 * mw:3 * mw].reshape(b, t, mw)
    lam_init = 0.8 - 0.6 * math.exp(-0.3 * prm["layer"])
    lv, subln = prm["df_lambda"], prm["df_subln_gain"].reshape(1, 2 * HD)
    d = x2.shape[1]
    if past is None:
        proj3 = proj.reshape(b, t, 3 * mw)
        ns = SB_W // LANES
        o_sb = _sb(proj3, t=cfg["t_even"], q_slab=0, k_slab=2 * ns, v_slab=4 * ns, n_slabs=ns)
        o_df = _df(proj3, lv, subln, t=cfg["t_even"], q_slab=ns, k_slab=3 * ns, v_slab=5 * ns, n_slabs=ns,
                   lam_init=lam_init)
        return _mm([o_sb.reshape(n, SB_W), o_df.reshape(n, DF_W), x2], [], prm["ev_w_out"], _lhs_cat2,
                   [(0, d, None, None, [(0, False)])], d, tm=cfg["tm"], res_idx=2), new_k, new_v
    pool_k, pool_v, table = past
    qcat = proj[:, :mw].reshape(b, t, mw).transpose(0, 2, 1) * SCALE
    qt = (jnp.tile(qcat, (1, 1, LANES // t)) * _dec_even_mask(t)).astype(BF16)
    o = _dec_even(qt, pool_k, pool_v, table, _pad_rows(new_k, PAGE), _pad_rows(new_v, PAGE), lv, subln,
                  pp=cfg["pp"], ts=t, lam_init=lam_init)
    return _mm([o.reshape(n, mw), x2], [], prm["ev_w_out"], _lhs_plain,
               [(0, d, None, None, [(0, False)])], d, tm=cfg["tm"], res_idx=1), new_k, new_v


def _dec_even_mask(ts):
    f = np.arange(SB_W + DF_W)[:, None]
    c = np.arange(LANES)[None, :]
    half = LANES // 2
    sb = (f < SB_W) & (c < half) & (f // HD == c // ts)
    df = (f >= SB_W) & (c >= half) & ((f - SB_W) // HD == (c - half) // ts)
    return jnp.asarray((sb | df).astype(np.float32))


def _odd_layout():
    widths = (N_NSA * HD, 128, 128, 128, 128, 128, 128, N_NSA * 3, N_DSA * HD, HD, HD, N_IDX * HD, HD, N_IDX)
    offs = np.concatenate([[0], np.cumsum(widths)])
    (q_n, k_c, v_c, k_s, v_s, k_w, v_w, gate, q_d, k_d, v_d, q_i, k_i, w_i) = [
        (int(offs[j]), int(offs[j + 1])) for j in range(len(widths))]
    pieces = [q_n, (k_c[0], v_w[1]), q_d, q_i, k_d, k_i, v_d, w_i, ("pad", HD - N_IDX), gate,
              ("pad", LANES - N_NSA * 3)]
    program = [
        (0, 512, 0, None, [(0, False), (512, True)]),
        (512, 128, None, None, [(1024, False)]),
        (640, 128, None, None, [(1152, False)]),
        (768, 128, 1, None, [(1280, True)]),
        (896, 128, None, None, [(1408, False)]),
        (1024, 128, 2, None, [(1536, True)]),
        (1152, 128, None, None, [(1664, False)]),
        (1280, 512, 3, None, [(1792, True)]),
        (1792, 512, None, None, [(2304, True)]),
        (2304, 128, 4, None, [(2816, True)]),
        (2432, 128, None, None, [(2944, False)]),
        (2560, 128, None, "sigmoid", [(3072, False)]),
    ]
    return pieces, program, 3200


def _permute_cols(w, pieces):
    cols = []
    for p in pieces:
        if p[0] == "pad":
            cols.append(jnp.zeros((w.shape[0], p[1]), w.dtype))
        else:
            cols.append(w[:, p[0]:p[1]])
    return jnp.concatenate(cols, axis=1)


def _odd_mixer(x2, b, t, t_real, q_off, cs, past, prm, cfg):
    n = b * t
    pieces, program, out_cols = _odd_layout()
    ng, dg = prm["nsa_qk_gain"], prm["dsa_qk_gain"]
    gains = jnp.concatenate([
        _tile_gain(ng[0]), _tile_gain(ng[2]), _tile_gain(ng[3]), _tile_gain(dg[0]),
        jnp.concatenate([dg[1], dg[2]]).reshape(1, LANES)], axis=0)
    w_in = _permute_cols(prm["od_w_in"], pieces)
    proj = _mm([x2, cs[0], cs[1]], [prm["g0"], gains], w_in, _lhs_norm, program, out_cols,
               tm=cfg["tm"], rope_idx=(1, 2), gains_idx=1)
    g = N_NSA_KV
    new_nsa = proj[:, 1024:1536].reshape(b, t, 4 * g * HD)
    new_win = proj[:, 1536:1792].reshape(b, t, 2, g, HD)
    new_dsa = jnp.concatenate([proj[:, 2816:2880], proj[:, 2944:3008], proj[:, 2880:2944]], axis=1).reshape(b, t, 3 * HD)
    w_i = proj[:, 3008:3008 + N_IDX].reshape(b, t, N_IDX)
    gate = proj[:, 3072:3072 + N_NSA * 3].reshape(n, N_NSA, 3)
    if past is None:
        proj3 = proj.reshape(b, t, out_cols)
        nsa_src, win_src = (proj3, 8), (proj3, 12)
        dsa_src = dict(kk=proj3, width=2 * LANES, blk=11, offs=(0, HD, 2 * HD))
        win_off = 0
        new_state = new_win[:, -min(WINDOW, t):]
        l_real = t
    else:
        pool_nsa, pool_dsa, state, table = past
        nsa_buf = _page_gather(pool_nsa, table, _pad_rows(new_nsa, cfg["pp"] * PAGE), pp=cfg["pp"])
        dsa_buf = _page_gather(pool_dsa, table, _pad_rows(new_dsa, cfg["pp"] * PAGE), pp=cfg["pp"])
        wb = state.shape[1]
        win_buf = _pad_rows(jnp.concatenate([state, new_win], axis=1), wb + cfg["tk_win"])
        nsa_src, win_src = (nsa_buf, 0), (win_buf.reshape(b, wb + cfg["tk_win"], 2 * g * HD), 0)
        dsa_src = dict(kk=dsa_buf, width=3 * HD, blk=0, offs=(0, 2 * HD, HD))
        win_off = q_off - wb
        new_state = jnp.concatenate([state, new_win[:, :t_real]], axis=1)[:, -wb:]
        l_real = table.shape[1] * PAGE + t_real
    lk = nsa_src[0].shape[1]
    tq = cfg["tq"]

    n_cmp = lk // CMP_STRIDE
    pe_flat = prm["cmp_pe"].reshape(2, 1, CMP_LEN * HD)
    cmp = _compress(nsa_src[0], nsa_src[1], prm["cmp_w1"], pe_flat, prm["cmp_w2"], ng[1].reshape(1, HD))
    nb = n_cmp // CMP_PER_SLC
    nbp = -(-nb // LANES) * LANES
    cmp = cmp.reshape(b, 2, g, nb, CMP_PER_SLC, HD).transpose(0, 1, 2, 4, 3, 5)
    cmp = jnp.pad(cmp, ((0, 0),) * 4 + ((0, nbp - nb), (0, 0))).reshape(b, 2, g, CMP_PER_SLC * nbp, HD).astype(BF16)

    def group_q(cols):
        a = _heads(cols, b, t, N_NSA, HD, SCALE).reshape(b, g, NSA_GROUP, t, HD)
        return _stack_q(a, tq)

    q_n = group_q(proj[:, 0:512])
    q_r = group_q(proj[:, 512:1024])
    n_blk = -(-l_real // SLC_BLOCK)
    o_c, sel = _nsa_cmp(q_n, cmp[:, 0], cmp[:, 1], tq=tq, q_off=q_off, n_sel=min(N_SLC, n_blk))

    o_s = _flash(q_r, nsa_src[0], kv_blk=(nsa_src[1] + 2) // 2, reps=NSA_GROUP, tq=tq,
                 tk=cfg["tk"], q_off=q_off, k_off=0, mode="causal", bm=sel)
    o_w = _flash(q_r, win_src[0], kv_blk=win_src[1] // 2, reps=NSA_GROUP, tq=tq,
                 tk=cfg["tk_win"], q_off=q_off, k_off=win_off, mode="window")

    q_d = _stack_q(_heads(proj[:, 1792:2304], b, t, N_DSA, HD, SCALE)[:, None], tq)[:, 0]
    q_i = _stack_q(_heads(proj[:, 2304:2816], b, t, N_IDX, HD, IDX_SCALE)[:, None], tq)[:, 0]
    o_d = _dsa(q_i, w_i, q_d, tq=tq, tk=cfg["tk"], q_off=q_off, n_top=min(DSA_TOPK_MAX, l_real // 4), **dsa_src)

    o_c, o_s, o_w = (_unstack_q(o, NSA_GROUP, tq) for o in (o_c, o_s, o_w))
    o_d = _unstack_q(o_d[:, None], N_DSA, tq)
    gfull = [jnp.repeat(gate[:, :, j], HD, axis=1) for j in range(3)]
    d = x2.shape[1]
    out = _mm([o_c, o_s, o_w, o_d] + gfull + [x2], [], prm["od_w_out"], _lhs_odd,
              [(0, d, None, None, [(0, False)])], d, tm=cfg["tm"], res_idx=7)
    return out, new_nsa, new_dsa, new_state


def _cross(x2, b, t, mem_k, mem_v, prm, cfg):
    d = x2.shape[1]
    xw = N_XH * HD
    q = _mm([x2], [prm["g1"], _tile_gain(prm["x_gq"])], prm["x_wq"], _lhs_norm,
            [(0, xw, 0, None, [(0, False)])], xw, tm=cfg["tm"], gains_idx=1)
    tq = cfg["tq"]
    qh = _stack_q(_heads(q, b, t, N_XH, HD, SCALE)[:, :, None], tq)
    o = _flash(qh, mem_k, mem_v, reps=1, tq=tq, tk=mem_k.shape[2], q_off=0, k_off=0, mode="full")
    o = _unstack_q(o, 1, tq)
    return _mm([o, x2], [], prm["x_wo"], _lhs_plain, [(0, d, None, None, [(0, False)])], d, tm=cfg["tm"], res_idx=1)


def _memory_kv(mem2, prm):
    xw = N_XH * HD
    w = jnp.concatenate([prm["x_wk"], prm["x_wv"]], axis=1)
    return _mm([mem2], [_tile_gain(prm["x_gk"])], w, _lhs_plain,
               [(0, xw, 0, None, [(0, False)]), (xw, xw, None, None, [(xw, False)])], 2 * xw,
               tm=min(256, mem2.shape[0]), gains_idx=0)


def _run_group(x, q_off, t_real, mem_kvs, pasts, layers, cfg):
    b, t, d = x.shape
    x2 = x.reshape(b * t, d)
    pos = q_off + jnp.arange(t, dtype=jnp.int32)
    cos, sin = _rope_tables(pos)
    cs = (jnp.tile(cos, (b, 1)), jnp.tile(sin, (b, 1)))
    outs = {}
    for li, prm in enumerate(layers):
        if li % 2 == 0:
            x2, nk, nv = _even_mixer(x2, b, t, q_off, cs, pasts[li], prm, cfg)
            outs["ek"], outs["ev"] = nk, nv
        else:
            x2, nn, nd, nw = _odd_mixer(x2, b, t, t_real, q_off, cs, pasts[li], prm, cfg)
            outs["on"], outs["od"], outs["ow"] = nn, nd, nw
        x2 = _cross(x2, b, t, mem_kvs[li][0], mem_kvs[li][1], prm, cfg)
        if li % 2 == 0:
            x2 = _ffn(x2, prm["g2"], prm["router"], prm["w1"], prm["w3"], prm["w2"], tm=cfg["tm_ffn"],
                      tf=cfg["tf"], routed=False)
        elif "moe_cap" in cfg and x2.shape[0] % cfg["tm_moe"] == 0:
            x2 = _moe(x2, prm["g2"], prm["router"], prm["w1"], prm["w3"], prm["w2"], tm=cfg["tm_moe"],
                      tf=cfg["tf"], cap=cfg["moe_cap"])
        else:
            x2 = _ffn(x2, prm["g2"], prm["router"], prm["w1"], prm["w3"], prm["w2"], tm=cfg["tm_ffn"],
                      tf=cfg["tf"], routed=True)
    return x2.reshape(b, t, d), outs


def kernel(x_prompt, x_sample, mem_prompt, cache_even_k, cache_even_v, cache_odd_nsa, cache_odd_dsa, state_odd_win, cache_mem, page_table, norm_gain, ev_w_in, ev_w_out, df_qk_gain, df_lambda, df_subln_gain, ffn_w1, ffn_w3, ffn_w2, od_w_in, od_w_out, nsa_qk_gain, cmp_pe, cmp_w1, cmp_w2, dsa_qk_gain, moe_router, moe_w1, moe_w3, moe_w2, x_wq, x_wk, x_wv, x_wo, x_qk_gain):
    depth = norm_gain.shape[0]
    bp, tp, d = x_prompt.shape
    bs, ts, _ = x_sample.shape
    n_mem = mem_prompt.shape[1]
    xw = N_XH * HD

    layers = []
    for l in range(depth):
        i = l // 2
        prm = {
            "layer": l,
            "g0": norm_gain[l, 0].reshape(1, d), "g1": norm_gain[l, 1].reshape(1, d), "g2": norm_gain[l, 2].reshape(1, d),
            "x_wq": x_wq[l].astype(BF16), "x_wk": x_wk[l].astype(BF16), "x_wv": x_wv[l].astype(BF16),
            "x_wo": x_wo[l].astype(BF16), "x_gq": x_qk_gain[l, 0], "x_gk": x_qk_gain[l, 1],
        }
        if l % 2 == 0:
            prm.update({
                "ev_w_in": ev_w_in[i].astype(BF16), "ev_w_out": ev_w_out[i].astype(BF16),
                "df_qk_gain": df_qk_gain[i], "df_lambda": df_lambda[i], "df_subln_gain": df_subln_gain[i],
                "router": jnp.zeros((SUBLANES, LANES), F32),
                "w1": ffn_w1[i][None].astype(BF16), "w3": ffn_w3[i][None].astype(BF16), "w2": ffn_w2[i][None].astype(BF16),
            })
        else:
            prm.update({
                "od_w_in": od_w_in[i].astype(BF16), "od_w_out": od_w_out[i].astype(BF16),
                "nsa_qk_gain": nsa_qk_gain[i], "dsa_qk_gain": dsa_qk_gain[i],
                "cmp_pe": cmp_pe[i], "cmp_w1": cmp_w1[i].astype(BF16), "cmp_w2": cmp_w2[i].astype(BF16),
                "router": jnp.pad(moe_router[i], ((0, 0), (0, LANES - N_EXPERTS))),
                "w1": moe_w1[i].astype(BF16), "w3": moe_w3[i].astype(BF16), "w2": moe_w2[i].astype(BF16),
            })
        layers.append(prm)

    def mem_heads(kv, b):
        k = kv[:, :, 0].transpose(0, 2, 1, 3).astype(BF16)
        v = kv[:, :, 1].transpose(0, 2, 1, 3).astype(BF16)
        return k, v

    mem2 = mem_prompt.reshape(bp * n_mem, d)
    mem_p = [_memory_kv(mem2, layers[l]).reshape(bp, n_mem, 2, N_XH, HD) for l in range(depth)]
    ff = ffn_w1.shape[2]
    tf = ff // 2 if (ff // 2) % LANES == 0 else ff
    cfg_p = {"tm": 256, "tq": 128, "tk": 256, "t_even": 256, "tk_win": 256, "tm_ffn": 512, "tf": tf,
             "tm_moe": 1024, "moe_cap": 320}
    cfg_p["tq"] = min(cfg_p["tq"], tp)
    y_prompt, op = _run_group(x_prompt, 0, tp, [mem_heads(m, bp) for m in mem_p], [None] * depth, layers, cfg_p)
    p_mem = jnp.stack(mem_p)

    n_past = page_table.shape[1] * cache_even_k.shape[2]
    ts_pad = -(-ts // SUBLANES) * SUBLANES
    xs = _pad_rows(x_sample, ts_pad)
    pasts = []
    for l in range(depth):
        i = l // 2
        if l % 2 == 0:
            pasts.append((cache_even_k[i], cache_even_v[i], page_table))
        else:
            pn = cache_odd_nsa[i]
            pd = cache_odd_dsa[i]
            pasts.append((pn.reshape(pn.shape[0], pn.shape[1], -1), pd.reshape(pd.shape[0], pd.shape[1], -1),
                          state_odd_win[i], page_table))
    cfg_s = {"tm": bs * ts_pad, "tq": ts_pad, "tk": 512, "pp": 4, "tk_win": 128, "tm_ffn": bs * ts_pad,
             "tf": tf, "new_rows": 512}
    y_s, os_ = _run_group(xs, n_past, ts, [mem_heads(cache_mem[l], bs) for l in range(depth)], pasts, layers, cfg_s)
    y_sample = y_s[:, :ts]

    g = N_NSA_KV
    return (
        y_prompt, y_sample,
        op["ek"][None], op["ev"][None],
        op["on"].reshape(1, bp, tp, 4, g, HD), op["od"].reshape(1, bp, tp, 3, HD),
        op["ow"][None], p_mem,
        os_["ek"][:, :ts][None], os_["ev"][:, :ts][None],
        os_["on"][:, :ts].reshape(1, bs, ts, 4, g, HD), os_["od"][:, :ts].reshape(1, bs, ts, 3, HD),
        os_["ow"][None],
    )
```

```python
import functools
import math

import jax
import jax.numpy as jnp
import numpy as np
from jax import lax
from jax.experimental import pallas as pl
from jax.experimental.pallas import tpu as pltpu

F32 = jnp.float32
BF16 = jnp.bfloat16

HD = 64
N_SB = 8
N_DF = 4
N_NSA = 8
N_NSA_KV = 2
NSA_GROUP = N_NSA // N_NSA_KV
N_DSA = 8
N_IDX = 8
N_XH = 4
N_EXPERTS = 8
ROPE_THETA = 10000.0
CMP_LEN = 32
CMP_STRIDE = 16
SLC_BLOCK = 64
CMP_PER_SLC = SLC_BLOCK // CMP_STRIDE
N_SLC = 16
N_LOCAL = 2
WINDOW = 512
DSA_TOPK_MAX = 256
EPS = 1e-6
NEG = -1e30
FORCE = 1e9
SCALE = HD ** -0.5
IDX_SCALE = HD ** -0.5
SB_W = N_SB * HD
HD_SHIFT = 6
SLC_SHIFT = 6
DF_W = N_DF * 2 * HD

LANES = 128
SUBLANES = 8
PAGE = 128
VMEM_LIMIT = 52 * 1024 * 1024
INT_MIN = -2 ** 31
ROW_SPLIT = 2
SB_CUT = 120.0
SEL_CHUNK_ROWS = 16
MIN_CHAIN_ROWS = 64
KEY_UNROLL = 4

_NT = (((1,), (1,)), ((), ()))


def _cparams(sem):
    return pltpu.CompilerParams(dimension_semantics=sem, vmem_limit_bytes=VMEM_LIMIT)


def _dot(a, b):
    return jnp.dot(a, b, preferred_element_type=F32)


def _dot_nt(a, b):
    return lax.dot_general(a, b, _NT, preferred_element_type=F32)


def _split_dot(x, m_bf16):
    hi = x.astype(BF16)
    lo = (x - hi.astype(F32)).astype(BF16)
    return _dot(hi, m_bf16) + _dot(lo, m_bf16)


def _split_dot_rhs(m_bf16, x):
    hi = x.astype(BF16)
    lo = (x - hi.astype(F32)).astype(BF16)
    return _dot(m_bf16, hi) + _dot(m_bf16, lo)


def _iota(shape, dim):
    return lax.broadcasted_iota(jnp.int32, shape, dim)


def _rms_rows(x, g):
    return x * lax.rsqrt(jnp.mean(x * x, axis=-1, keepdims=True) + EPS) * g


def _group_mean_matrix():
    r = _iota((LANES, LANES), 0) >> HD_SHIFT
    c = _iota((LANES, LANES), 1) >> HD_SHIFT
    return jnp.where(r == c, 1.0 / HD, 0.0).astype(BF16)


def _head_norm(y, g, gm):
    ms = _split_dot(y * y, gm)
    return y * lax.rsqrt(ms + EPS) * g


def _rope_slab(y, cos, sin):
    lane = _iota(y.shape, 1)
    first = (lane & (HD - 1)) < (HD // 2)
    swapped = jnp.where(first, pltpu.roll(y, LANES - HD // 2, 1), pltpu.roll(y, HD // 2, 1))
    return y * cos + swapped * sin


def _mm_kernel(*refs, n_rows, n_consts, lhs_fn, program, rope_idx, gains_idx, res_idx):
    rows = refs[:n_rows]
    consts = refs[n_rows:n_rows + n_consts]
    w_ref = refs[n_rows + n_consts]
    o_ref = refs[-1]
    lhs = lhs_fn(rows, consts).astype(BF16)
    gm = _group_mean_matrix() if gains_idx is not None else None
    for (src, width, gain_row, act, outs) in program:
        y_full = _dot(lhs, w_ref[:, src:src + width])
        for s in range(width // LANES):
            y = y_full[:, s * LANES:(s + 1) * LANES]
            if gain_row is not None:
                y = _head_norm(y, consts[gains_idx][gain_row:gain_row + 1, :], gm)
            if act == "sigmoid":
                y = 1.0 / (1.0 + jnp.exp(-y))
            for (dst, rope) in outs:
                z = y
                if rope:
                    z = _rope_slab(y, rows[rope_idx[0]][...], rows[rope_idx[1]][...])
                d0 = dst + s * LANES
                if res_idx is not None:
                    z = z + rows[res_idx][:, d0:d0 + LANES]
                o_ref[:, d0:d0 + LANES] = z


def _mm(rows, consts, w, lhs_fn, program, out_cols, *, tm, rope_idx=None, gains_idx=None, res_idx=None, name="mm"):
    n = rows[0].shape[0]
    assert n % tm == 0
    in_specs = [pl.BlockSpec((tm, r.shape[1]), lambda i: (i, 0)) for r in rows]
    in_specs += [pl.BlockSpec(c.shape, lambda i: (0, 0)) for c in consts]
    in_specs += [pl.BlockSpec(w.shape, lambda i: (0, 0))]
    kern = functools.partial(_mm_kernel, n_rows=len(rows), n_consts=len(consts), lhs_fn=lhs_fn,
                             program=program, rope_idx=rope_idx, gains_idx=gains_idx, res_idx=res_idx)
    return pl.pallas_call(
        kern,
        grid=(n // tm,),
        in_specs=in_specs,
        out_specs=pl.BlockSpec((tm, out_cols), lambda i: (i, 0)),
        out_shape=jax.ShapeDtypeStruct((n, out_cols), F32),
        compiler_params=_cparams(("parallel",)),
        name=name,
    )(*rows, *consts, w)


def _lhs_norm(rows, consts):
    return _rms_rows(rows[0][...], consts[0][...])


def _lhs_plain(rows, consts):
    return rows[0][...]


def _lhs_cat2(rows, consts):
    return jnp.concatenate([rows[0][...], rows[1][...]], axis=1)


def _lhs_odd(rows, consts):
    oc, os_, ow, od = rows[0][...], rows[1][...], rows[2][...], rows[3][...]
    g0, g1, g2 = rows[4][...], rows[5][...], rows[6][...]
    return jnp.concatenate([g0 * oc + g1 * os_ + g2 * ow, od], axis=1)


def _ffn_kernel(x_ref, g_ref, r_ref, w1_ref, w3_ref, w2_ref, o_ref, h_ref, acc_ref, gate_ref, *, routed):
    e = pl.program_id(1)
    f = pl.program_id(2)
    first = jnp.logical_and(e == 0, f == 0)
    last = jnp.logical_and(e == pl.num_programs(1) - 1, f == pl.num_programs(2) - 1)

    @pl.when(first)
    def _():
        x = x_ref[...]
        h = _rms_rows(x, g_ref[...])
        h_ref[...] = h.astype(BF16)
        acc_ref[...] = x
        if routed:
            logits = jnp.dot(h, r_ref[...], preferred_element_type=F32, precision=lax.Precision.HIGHEST)
            col = _iota(logits.shape, 1).astype(F32)
            logits = jnp.where(col < N_EXPERTS, logits, -jnp.inf)
            m1 = jnp.max(logits, axis=1, keepdims=True)
            i1 = jnp.min(jnp.where(logits == m1, col, 1e9), axis=1, keepdims=True)
            rest = jnp.where(col == i1, -jnp.inf, logits)
            m2 = jnp.max(rest, axis=1, keepdims=True)
            i2 = jnp.min(jnp.where(rest == m2, col, 1e9), axis=1, keepdims=True)
            e2 = jnp.exp(m2 - m1)
            g1 = 1.0 / (1.0 + e2)
            g2 = e2 / (1.0 + e2)
            gate_ref[...] = jnp.where(col == i1, g1, 0.0) + jnp.where(col == i2, g2, 0.0)

    def compute(gcol):
        h = h_ref[...]
        u = _dot(h, w1_ref[0])
        v = _dot(h, w3_ref[0])
        a = (u * (1.0 / (1.0 + jnp.exp(-u)))) * v
        y = _dot(a.astype(BF16), w2_ref[0])
        if gcol is not None:
            y = gcol * y
        acc_ref[...] += y

    if routed:
        col = _iota(gate_ref.shape, 1)
        gcol = jnp.sum(jnp.where(col == e, gate_ref[...], 0.0), axis=1, keepdims=True)
        active = jnp.max(gcol) > 0.0

        @pl.when(active)
        def _():
            compute(gcol)
    else:
        compute(None)

    @pl.when(last)
    def _():
        o_ref[...] = acc_ref[...]


def _moe_kernel(x_ref, g_ref, r_ref, w1_ref, w3_ref, w2_ref, o_ref, h_ref, acc_ref, gate_ref, slot_ref,
                slot_t_ref, he_ref, ye_ref, *, cap):
    e = pl.program_id(1)
    f = pl.program_id(2)
    nf = pl.num_programs(2)
    tm = x_ref.shape[0]
    first = jnp.logical_and(e == 0, f == 0)
    last = jnp.logical_and(e == pl.num_programs(1) - 1, f == nf - 1)

    @pl.when(first)
    def _():
        x = x_ref[...]
        h = _rms_rows(x, g_ref[...])
        h_ref[...] = h.astype(BF16)
        acc_ref[...] = x
        logits = jnp.dot(h, r_ref[...], preferred_element_type=F32, precision=lax.Precision.HIGHEST)
        col = _iota(logits.shape, 1).astype(F32)
        logits = jnp.where(col < N_EXPERTS, logits, -jnp.inf)
        m1 = jnp.max(logits, axis=1, keepdims=True)
        i1 = jnp.min(jnp.where(logits == m1, col, 1e9), axis=1, keepdims=True)
        rest = jnp.where(col == i1, -jnp.inf, logits)
        m2 = jnp.max(rest, axis=1, keepdims=True)
        i2 = jnp.min(jnp.where(rest == m2, col, 1e9), axis=1, keepdims=True)
        e2 = jnp.exp(m2 - m1)
        gate_ref[...] = jnp.where(col == i1, 1.0 / (1.0 + e2), 0.0) + jnp.where(col == i2, e2 / (1.0 + e2), 0.0)
        chosen = jnp.where(jnp.logical_or(col == i1, col == i2), 1.0, 0.0)
        earlier = jnp.where(_iota((tm, tm), 1) < _iota((tm, tm), 0), 1.0, 0.0).astype(BF16)
        slot = jnp.where(chosen > 0.5, _dot(earlier, chosen.astype(BF16)), -1.0)
        slot_ref[...] = slot
        slot_t_ref[...] = jnp.transpose(slot)

    lane = _iota((tm, LANES), 1)
    mine = lane == e
    gcol = jnp.sum(jnp.where(mine, gate_ref[...], 0.0), axis=1, keepdims=True)
    slot_col = jnp.sum(jnp.where(mine, slot_ref[...], 0.0), axis=1, keepdims=True)
    slot_row = slot_t_ref[pl.ds(e, 1), :]
    count = jnp.sum(jnp.where(slot_row >= 0.0, 1.0, 0.0)).astype(jnp.int32)
    n_chunk = lax.div(count + (cap - 1), cap)

    def pack(c):
        want = (c * cap + _iota((cap, tm), 0)).astype(F32)
        return jnp.where(slot_row == want, 1.0, 0.0).astype(BF16)

    def unpack(c):
        want = (c * cap + _iota((tm, cap), 1)).astype(F32)
        return jnp.where(slot_col == want, 1.0, 0.0).astype(BF16)

    def expert(rows):
        u = _dot(rows, w1_ref[0])
        v = _dot(rows, w3_ref[0])
        a = (u * (1.0 / (1.0 + jnp.exp(-u)))) * v
        return _dot(a.astype(BF16), w2_ref[0])

    @pl.when(n_chunk > 0)
    def _():
        @pl.when(f == 0)
        def _():
            he_ref[...] = _dot(pack(0), h_ref[...]).astype(BF16)
            ye_ref[...] = expert(he_ref[...])

        @pl.when(f > 0)
        def _():
            ye_ref[...] += expert(he_ref[...])

        @pl.when(f == nf - 1)
        def _():
            acc_ref[...] += gcol * _split_dot_rhs(unpack(0), ye_ref[...])

    def extra(c, _):
        part = expert(_dot(pack(c), h_ref[...]).astype(BF16))
        acc_ref[...] += gcol * _split_dot_rhs(unpack(c), part)
        return 0

    lax.fori_loop(1, n_chunk, extra, 0)

    @pl.when(last)
    def _():
        o_ref[...] = acc_ref[...]


def _moe(x, g, router, w1, w3, w2, *, tm, tf, cap):
    n, d = x.shape
    ne, _, ff = w1.shape
    assert n % tm == 0 and ff % tf == 0
    kern = functools.partial(_moe_kernel, cap=cap)
    return pl.pallas_call(
        kern,
        grid=(n // tm, ne, ff // tf),
        in_specs=[
            pl.BlockSpec((tm, d), lambda i, e, f: (i, 0), pipeline_mode=pl.Buffered(1)),
            pl.BlockSpec((1, d), lambda i, e, f: (0, 0)),
            pl.BlockSpec(router.shape, lambda i, e, f: (0, 0)),
            pl.BlockSpec((1, d, tf), lambda i, e, f: (e, 0, f)),
            pl.BlockSpec((1, d, tf), lambda i, e, f: (e, 0, f)),
            pl.BlockSpec((1, tf, d), lambda i, e, f: (e, f, 0)),
        ],
        out_specs=pl.BlockSpec((tm, d), lambda i, e, f: (i, 0), pipeline_mode=pl.Buffered(1)),
        out_shape=jax.ShapeDtypeStruct((n, d), F32),
        scratch_shapes=[pltpu.VMEM((tm, d), BF16), pltpu.VMEM((tm, d), F32), pltpu.VMEM((tm, LANES), F32),
                        pltpu.VMEM((tm, LANES), F32), pltpu.VMEM((LANES, tm), F32),
                        pltpu.VMEM((cap, d), BF16), pltpu.VMEM((cap, d), F32)],
        compiler_params=_cparams(("parallel", "arbitrary", "arbitrary")),
        name="moe_routed",
    )(x, g, router, w1, w3, w2)


def _ffn(x, g, router, w1, w3, w2, *, tm, tf, routed):
    n, d = x.shape
    ne, _, ff = w1.shape
    assert n % tm == 0 and ff % tf == 0
    kern = functools.partial(_ffn_kernel, routed=routed)
    return pl.pallas_call(
        kern,
        grid=(n // tm, ne, ff // tf),
        in_specs=[
            pl.BlockSpec((tm, d), lambda i, e, f: (i, 0)),
            pl.BlockSpec((1, d), lambda i, e, f: (0, 0)),
            pl.BlockSpec(router.shape, lambda i, e, f: (0, 0)),
            pl.BlockSpec((1, d, tf), lambda i, e, f: (e, 0, f)),
            pl.BlockSpec((1, d, tf), lambda i, e, f: (e, 0, f)),
            pl.BlockSpec((1, tf, d), lambda i, e, f: (e, f, 0)),
        ],
        out_specs=pl.BlockSpec((tm, d), lambda i, e, f: (i, 0)),
        out_shape=jax.ShapeDtypeStruct((n, d), F32),
        scratch_shapes=[pltpu.VMEM((tm, d), BF16), pltpu.VMEM((tm, d), F32), pltpu.VMEM((tm, LANES), F32)],
        compiler_params=_cparams(("parallel", "arbitrary", "arbitrary")),
        name="moe" if routed else "ffn",
    )(x, g, router, w1, w3, w2)


def _tile_rows(m, reps):
    return m if reps == 1 else jnp.concatenate([m] * reps, axis=0)


def _online_step(s, v, m, l, acc):
    m_new = jnp.maximum(m, jnp.max(s, axis=1, keepdims=True))
    p = jnp.exp(s - m_new)
    alpha = jnp.exp(m - m_new)
    return m_new, alpha * l + jnp.sum(p, axis=1, keepdims=True), alpha * acc + _dot(p.astype(BF16), v)


def _with_ones(v):
    return jnp.concatenate([v, jnp.ones_like(v)], axis=1)


def _online_step_ones(s, v1, m, acc):
    m_new = jnp.maximum(m, jnp.max(s, axis=1, keepdims=True))
    p = jnp.exp(s - m_new)
    return m_new, jnp.exp(m - m_new) * acc + _dot(p.astype(BF16), v1)


def _online_init_ones(rows):
    return (jnp.full((rows, 1), NEG, F32), jnp.zeros((rows, 2 * HD), F32))


def _online_init(rows, dv):
    return (jnp.full((rows, 1), NEG, F32), jnp.zeros((rows, 1), F32), jnp.zeros((rows, dv), F32))


def _unrolled_loop(lo, hi, body, init, unroll=KEY_UNROLL):
    shift = unroll.bit_length() - 1
    n_group = (hi - lo) >> shift

    def group(p, st):
        for u in range(unroll):
            st = body(lo + unroll * p + u, st)
        return st

    st = lax.fori_loop(0, n_group, group, init)
    return lax.fori_loop(lo + n_group * unroll, hi, body, st)


def _flash_kernel(*refs, reps, tq, tk, q_off, k_off, mode, has_bm, slab):
    refs = list(refs)
    q_ref = refs.pop(0)
    k_ref = refs.pop(0)
    v_ref = None if slab else refs.pop(0)
    bm_ref = refs.pop(0) if has_bm else None
    o_ref = refs.pop(0)
    i = pl.program_id(2)
    q = q_ref[0, 0, 0]
    dv = HD if slab else v_ref.shape[-1]
    n_kt = k_ref.shape[-2] // tk
    q_lo = q_off + i * tq

    def kv_tiles(k0):
        if not slab:
            return k_ref[0, 0, pl.ds(k0, tk), :], v_ref[0, 0, pl.ds(k0, tk), :]
        t = k_ref[0, pl.ds(k0, tk), :]
        first = pl.program_id(1) == 0
        return (jnp.where(first, t[:, :HD], t[:, HD:2 * HD]).astype(BF16),
                jnp.where(first, t[:, 2 * HD:3 * HD], t[:, 3 * HD:]).astype(BF16))

    if mode == "full":
        lo, hi = 0, n_kt
    else:
        hi = jnp.minimum(lax.div(q_lo + tq - 1 - k_off, tk) + 1, n_kt)
        lo = 0
        if mode == "window":
            lo = lax.div(jnp.maximum(q_lo - (WINDOW - 1) - k_off, 0), tk)
    qpos = q_lo + _iota((tq, tk), 0)
    if has_bm:
        bm = bm_ref[0, 0].astype(BF16)
        nbp = bm.shape[1]

    def body(j, state):
        k0 = pl.multiple_of(j * tk, tk)
        k, v = kv_tiles(k0)
        bias = None
        if mode != "full":
            kpos = k_off + j * tk + _iota((tq, tk), 1)
            ok = kpos <= qpos
            if mode == "window":
                ok = jnp.logical_and(ok, qpos - kpos < WINDOW)
                ok = jnp.logical_and(ok, kpos >= 0)
            bias = jnp.where(ok, 0.0, NEG)
        if has_bm:
            blk = _iota((nbp, tk), 0)
            tok = (j * tk + _iota((nbp, tk), 1)) >> SLC_SHIFT
            expand = jnp.where(blk == tok, 1.0, 0.0).astype(BF16)
            bias = jnp.where(_dot(bm, expand) > 0.5, bias, NEG)
        if bias is not None:
            bias = _tile_rows(bias, cr // tq)
        if slab:
            v = _with_ones(v)
        new = []
        for c in range(n_chain):
            s = _dot_nt(q[c * cr:(c + 1) * cr].astype(BF16), k)
            if bias is not None:
                s = s + bias
            new.append(_online_step_ones(s, v, *state[c]) if slab else _online_step(s, v, *state[c]))
        return tuple(new)

    n_chain = reps if tq >= MIN_CHAIN_ROWS else 1
    cr = reps * tq // n_chain
    init = tuple((_online_init_ones(cr) if slab else _online_init(cr, dv)) for _ in range(n_chain))
    state = _unrolled_loop(lo, hi, body, init)
    for c in range(n_chain):
        if slab:
            acc = state[c][1]
            o_ref[0, 0, 0, c * cr:(c + 1) * cr, :] = acc[:, :HD] / acc[:, HD:HD + 1]
        else:
            o_ref[0, 0, 0, c * cr:(c + 1) * cr, :] = state[c][2] / state[c][1]


def _flash(q, k, v=None, *, reps, tq, tk, q_off, k_off, mode, bm=None, kv_blk=None):
    b, hk, nq, rows, _ = q.shape
    lk = k.shape[-2]
    assert rows == reps * tq and lk % tk == 0
    if kv_blk is None:
        dv = v.shape[3]
        kv_specs = [pl.BlockSpec((1, 1, lk, HD), lambda b_, h, i: (b_, h, 0, 0)),
                    pl.BlockSpec((1, 1, lk, dv), lambda b_, h, i: (b_, h, 0, 0))]
        args = [q, k, v]
    else:
        assert hk == 2 and v is None
        dv = HD
        kv_specs = [pl.BlockSpec((1, lk, 2 * LANES), lambda b_, h, i: (b_, 0, kv_blk))]
        args = [q, k]
    in_specs = [pl.BlockSpec((1, 1, 1, rows, HD), lambda b_, h, i: (b_, h, i, 0, 0))] + kv_specs
    if bm is not None:
        in_specs.append(pl.BlockSpec((1, 1, tq, bm.shape[3]), lambda b_, h, i: (b_, h, i, 0)))
        args.append(bm)
    kern = functools.partial(_flash_kernel, reps=reps, tq=tq, tk=tk, q_off=q_off, k_off=k_off, mode=mode,
                             has_bm=bm is not None, slab=kv_blk is not None)
    return pl.pallas_call(
        kern,
        grid=(b, hk, nq),
        in_specs=in_specs,
        out_specs=pl.BlockSpec((1, 1, 1, rows, dv), lambda b_, h, i: (b_, h, i, 0, 0)),
        out_shape=jax.ShapeDtypeStruct((b, hk, nq, rows, dv), F32),
        compiler_params=_cparams(("parallel", "parallel", "arbitrary")),
        name="flash_" + mode + ("_blockmask" if bm is not None else ""),
    )(*args)


def _softplus(z):
    return jnp.maximum(z, 0.0) + jnp.log(1.0 + jnp.exp(-jnp.abs(z)))


def _later_matrix(n):
    return jnp.where(_iota((n, n), 0) > _iota((n, n), 1), 1.0, 0.0).astype(BF16)


def _sb_kernel(q_ref, k_ref, v_ref, o_ref, *, t):
    i = pl.program_id(2)
    th = t // ROW_SPLIT
    lane = _iota((th, LANES), 1)
    later = _later_matrix(t)
    qs = []
    for part in range(ROW_SPLIT):
        qf = q_ref[0, part * th:(part + 1) * th, :] * SCALE
        qs.append((jnp.where(lane < HD, qf, 0.0).astype(BF16), jnp.where(lane >= HD, qf, 0.0).astype(BF16)))

    def tile(j, carries, outs, masked):
        k0 = pl.multiple_of(j * t, t)
        k = k_ref[0, pl.ds(k0, t), :].astype(BF16)
        v = v_ref[0, pl.ds(k0, t), :].astype(BF16)
        new_carries, new_outs = [], []
        for part in range(ROW_SPLIT):
            heads = []
            for hd in range(2):
                carry = carries[2 * part + hd]
                z = _dot_nt(qs[part][hd], k)
                sp = _softplus(z)
                log_sig = z - sp
                if masked:
                    vis = _iota((th, t), 1) < part * th + _iota((th, t), 0)
                    sp = jnp.where(vis, sp, 0.0)
                a = jnp.exp(log_sig - _dot(sp.astype(BF16), later) - carry)
                if masked:
                    a = jnp.where(vis, a, 0.0)
                heads.append(_dot(a.astype(BF16), v))
                new_carries.append(carry + jnp.sum(sp, axis=1, keepdims=True))
            new_outs.append(outs[part] + jnp.where(lane < HD, heads[0], heads[1]))
        return tuple(new_carries), tuple(new_outs)

    def min_carry(carries):
        m = jnp.min(carries[0])
        for c in carries[1:]:
            m = jnp.minimum(m, jnp.min(c))
        return m

    zero = jnp.zeros((th, 1), F32)
    carries, outs = tile(i, (zero,) * (2 * ROW_SPLIT), (jnp.zeros((th, LANES), F32),) * ROW_SPLIT, True)

    def cond(st):
        return jnp.logical_and(st[0] < i, st[1] < SB_CUT)

    def body(st):
        carries, outs = tile(i - 1 - st[0], st[2], st[3], False)
        return st[0] + 1, min_carry(carries), carries, outs

    _, _, carries, outs = lax.while_loop(cond, body, (jnp.int32(0), min_carry(carries), carries, outs))
    for part in range(ROW_SPLIT):
        o_ref[0, part * th:(part + 1) * th, :] = outs[part]


def _sb(proj3, *, t, q_slab, k_slab, v_slab, n_slabs):
    b, tt, _ = proj3.shape
    assert tt % t == 0
    kern = functools.partial(_sb_kernel, t=t)
    return pl.pallas_call(
        kern,
        grid=(b, n_slabs, tt // t),
        in_specs=[
            pl.BlockSpec((1, t, LANES), lambda b_, p, i: (b_, i, q_slab + p)),
            pl.BlockSpec((1, tt, LANES), lambda b_, p, i: (b_, 0, k_slab + p)),
            pl.BlockSpec((1, tt, LANES), lambda b_, p, i: (b_, 0, v_slab + p)),
        ],
        out_specs=pl.BlockSpec((1, t, LANES), lambda b_, p, i: (b_, i, p)),
        out_shape=jax.ShapeDtypeStruct((b, tt, n_slabs * LANES), F32),
        compiler_params=_cparams(("parallel", "parallel", "arbitrary")),
        name="sb_prompt",
    )(proj3, proj3, proj3)


def _lam(lv, lam_init):
    a = jnp.sum(jnp.sum(lv[0:1] * lv[1:2], axis=1, keepdims=True), axis=0, keepdims=True)
    b = jnp.sum(jnp.sum(lv[2:3] * lv[3:4], axis=1, keepdims=True), axis=0, keepdims=True)
    return jnp.exp(a) - jnp.exp(b) + lam_init


def _df_kernel(lv_ref, g_ref, q_ref, k_ref, v_ref, o_ref, *, t, lam_init):
    i = pl.program_id(2)
    th = t // ROW_SPLIT
    lane = _iota((th, LANES), 1)
    qs = []
    for part in range(ROW_SPLIT):
        qf = q_ref[0, part * th:(part + 1) * th, :] * SCALE
        qs.append((jnp.where(lane < HD, qf, 0.0).astype(BF16), jnp.where(lane >= HD, qf, 0.0).astype(BF16)))

    def tile(j, state, masked):
        k0 = pl.multiple_of(j * t, t)
        k = k_ref[0, pl.ds(k0, t), :].astype(BF16)
        v = v_ref[0, pl.ds(k0, t), :].astype(BF16)
        new = []
        for part in range(ROW_SPLIT):
            for mp in range(2):
                s = _dot_nt(qs[part][mp], k)
                if masked:
                    s = jnp.where(_iota((th, t), 1) <= part * th + _iota((th, t), 0), s, NEG)
                new.append(_online_step(s, v, *state[2 * part + mp]))
        return tuple(new)

    init = tuple(_online_init(th, LANES) for _ in range(2 * ROW_SPLIT))
    state = _unrolled_loop(0, i, lambda j, st: tile(j, st, False), init)
    state = tile(i, state, True)
    lam = _lam(lv_ref[...], lam_init)
    for part in range(ROW_SPLIT):
        (_, l0, a0), (_, l1, a1) = state[2 * part], state[2 * part + 1]
        d = a0 / l0 - lam * (a1 / l1)
        o_ref[0, part * th:(part + 1) * th, :] = _rms_rows(d, g_ref[...]) * (1.0 - lam_init)


def _df(proj3, lv, subln, *, t, q_slab, k_slab, v_slab, n_slabs, lam_init):
    b, tt, _ = proj3.shape
    assert tt % t == 0
    kern = functools.partial(_df_kernel, t=t, lam_init=lam_init)
    return pl.pallas_call(
        kern,
        grid=(b, n_slabs, tt // t),
        in_specs=[
            pl.BlockSpec(lv.shape, lambda b_, p, i: (0, 0)),
            pl.BlockSpec(subln.shape, lambda b_, p, i: (0, 0)),
            pl.BlockSpec((1, t, LANES), lambda b_, p, i: (b_, i, q_slab + p)),
            pl.BlockSpec((1, tt, LANES), lambda b_, p, i: (b_, 0, k_slab + p)),
            pl.BlockSpec((1, tt, LANES), lambda b_, p, i: (b_, 0, v_slab + p)),
        ],
        out_specs=pl.BlockSpec((1, t, LANES), lambda b_, p, i: (b_, i, p)),
        out_shape=jax.ShapeDtypeStruct((b, tt, n_slabs * LANES), F32),
        compiler_params=_cparams(("parallel", "parallel", "arbitrary")),
        name="df_prompt",
    )(lv, subln, proj3, proj3, proj3)


def _dec_even_kernel(*refs, pp, ts, lam_init):
    tbl_ref, qt_ref = refs[0], refs[1]
    k_refs = refs[2:2 + pp]
    v_refs = refs[2 + pp:2 + 2 * pp]
    kn_ref, vn_ref, lv_ref, g_ref, o_ref, later_ref, st_ref, asb_ref, adf_ref = refs[2 + 2 * pp:]
    del tbl_ref
    s_id = pl.program_id(1)
    nk = pp * PAGE
    half = LANES // 2
    qt = qt_ref[0]

    def col_of(row):
        return jnp.transpose(jnp.broadcast_to(row, (SUBLANES, LANES)))[half:, 0:1]

    def tile(kt, vt, later, vis_sb, vis_df):
        carry, m, l = st_ref[0:1, :], st_ref[1:2, :], st_ref[2:3, :]
        zt = _dot(kt, qt)
        lane = _iota(zt.shape, 1)
        sp = _softplus(zt)
        log_sig = zt - sp
        s = zt
        if vis_sb is not None:
            sp = jnp.where(vis_sb, sp, 0.0)
            s = jnp.where(vis_df, s, NEG)
        a = jnp.exp(log_sig - _dot(later, sp.astype(BF16)) - carry)
        if vis_sb is not None:
            a = jnp.where(vis_sb, a, 0.0)
        m_new = jnp.maximum(m, jnp.max(s, axis=0, keepdims=True))
        p = jnp.exp(s - m_new)
        alpha = jnp.exp(m - m_new)
        st_ref[0:1, :] = carry + jnp.sum(sp, axis=0, keepdims=True)
        st_ref[1:2, :] = m_new
        st_ref[2:3, :] = alpha * l + jnp.sum(p, axis=0, keepdims=True)
        w = jnp.transpose(jnp.where(lane < half, a, p)).astype(BF16)
        asb_ref[...] += _dot(w[:half], vt[:, :SB_W])
        adf_ref[...] = col_of(alpha) * adf_ref[...] + _dot(w[half:], vt[:, SB_W:])

    @pl.when(s_id == 0)
    def _():
        later_ref[...] = jnp.where(_iota((nk, nk), 1) > _iota((nk, nk), 0), 1.0, 0.0).astype(BF16)
        st_ref[...] = jnp.where(_iota(st_ref.shape, 0) == 1, NEG, 0.0)
        asb_ref[...] = jnp.zeros_like(asb_ref)
        adf_ref[...] = jnp.zeros_like(adf_ref)
        key = _iota((PAGE, LANES), 0)
        tok = _iota((PAGE, LANES), 1) & (ts - 1)
        tile(kn_ref[0].astype(BF16), vn_ref[0].astype(BF16), later_ref[0:PAGE, 0:PAGE], key < tok, key <= tok)

    kt = jnp.concatenate([r[0].astype(BF16) for r in k_refs], axis=0)
    vt = jnp.concatenate([r[0].astype(BF16) for r in v_refs], axis=0)
    tile(kt, vt, later_ref[...], None, None)

    @pl.when(s_id == pl.num_programs(1) - 1)
    def _():
        row = _iota((half, SB_W), 0)
        lane = _iota((half, SB_W), 1)

        def fold(x):
            out = x[0:ts]
            for u in range(1, half // ts):
                out = out + x[u * ts:(u + 1) * ts]
            return out

        o_sb = fold(jnp.where((row >> 3) == (lane >> HD_SHIFT), asb_ref[...], 0.0))
        pn = adf_ref[...] / col_of(st_ref[2:3, :])
        same_head = (row >> 4) == (lane >> 7)
        o0 = fold(jnp.where(jnp.logical_and(same_head, ((row >> 3) & 1) == 0), pn, 0.0))
        o1 = fold(jnp.where(jnp.logical_and(same_head, ((row >> 3) & 1) == 1), pn, 0.0))
        d = o0 - _lam(lv_ref[...], lam_init) * o1
        parts = [o_sb]
        for h in range(N_DF):
            parts.append(_rms_rows(d[:, h * LANES:(h + 1) * LANES], g_ref[...]) * (1.0 - lam_init))
        o_ref[0] = jnp.concatenate(parts, axis=1)


def _dec_even(qt, pool_k, pool_v, table, k_new, v_new, lv, subln, *, pp, ts, lam_init):
    b, n_pages = table.shape
    width = pool_k.shape[2]
    assert n_pages % pp == 0 and ts == SUBLANES and k_new.shape[1] == PAGE
    n_steps = n_pages // pp

    def page_map(u):
        return lambda b_, s, t: (t[b_, n_pages - (s + 1) * pp + u], 0, 0)

    page_specs = [pl.BlockSpec((1, PAGE, width), page_map(u)) for u in range(pp)]
    new_spec = pl.BlockSpec((1, PAGE, width), lambda b_, s, t: (b_, 0, 0))
    grid_spec = pltpu.PrefetchScalarGridSpec(
        num_scalar_prefetch=1,
        grid=(b, n_steps),
        in_specs=[pl.BlockSpec((1,) + qt.shape[1:], lambda b_, s, t: (b_, 0, 0))] + page_specs + page_specs
        + [new_spec, new_spec, pl.BlockSpec(lv.shape, lambda b_, s, t: (0, 0)),
           pl.BlockSpec(subln.shape, lambda b_, s, t: (0, 0))],
        out_specs=pl.BlockSpec((1, ts, width), lambda b_, s, t: (b_, 0, 0)),
        scratch_shapes=[
            pltpu.VMEM((pp * PAGE, pp * PAGE), BF16),
            pltpu.VMEM((SUBLANES, LANES), F32),
            pltpu.VMEM((LANES // 2, SB_W), F32),
            pltpu.VMEM((LANES // 2, DF_W), F32),
        ],
    )
    kern = functools.partial(_dec_even_kernel, pp=pp, ts=ts, lam_init=lam_init)
    return pl.pallas_call(
        kern,
        grid_spec=grid_spec,
        out_shape=jax.ShapeDtypeStruct((b, ts, width), F32),
        compiler_params=_cparams(("parallel", "arbitrary")),
        name="dec_even",
    )(table, qt, *([pool_k] * pp), *([pool_v] * pp), k_new, v_new, lv, subln)


def _compress_kernel(t_ref, w1_ref, w1g_ref, pe_ref, w2_ref, g_ref, o_ref):
    kind = pl.program_id(1)
    n = t_ref.shape[1] // CMP_STRIDE
    a1 = [None] * N_NSA_KV
    a2 = [None] * N_NSA_KV
    for p in range(CMP_STRIDE):
        rows = t_ref[0, pl.ds(p, n, stride=CMP_STRIDE), :].astype(BF16)
        for g in range(N_NSA_KV):
            d1 = _dot(rows, w1g_ref[0, g, p])
            d2 = _dot(rows, w1g_ref[0, g, CMP_STRIDE + p])
            a1[g] = d1 if a1[g] is None else a1[g] + d1
            a2[g] = d2 if a2[g] is None else a2[g] + d2
    pe = _dot(jnp.broadcast_to(pe_ref[0], (SUBLANES, CMP_LEN * HD)).astype(BF16), w1_ref[0])[0:1]
    for g in range(N_NSA_KV):
        nxt = jnp.where(_iota(a2[g].shape, 0) < n - 1, pltpu.roll(a2[g], n - 1, 0), 0.0)
        hid = a1[g] + nxt + pe
        hid = hid * (1.0 / (1.0 + jnp.exp(-hid)))
        out = _dot(hid.astype(BF16), w2_ref[0])
        normed = _rms_rows(out, g_ref[...])
        o_ref[0, 0, g] = jnp.where(kind == 0, normed, out)


def _compress(t, slab0, w1, pe_flat, w2, gain):
    b, lk, _ = t.shape
    g = N_NSA_KV
    n = lk // CMP_STRIDE
    pieces = w1.reshape(2, CMP_LEN, HD, w1.shape[2])
    w1g = jnp.stack([jnp.pad(pieces, ((0, 0), (0, 0), (HD * gi, HD * (g - 1 - gi)), (0, 0))) for gi in range(g)],
                    axis=1)
    return pl.pallas_call(
        _compress_kernel,
        grid=(b, 2),
        in_specs=[
            pl.BlockSpec((1, lk, LANES), lambda b_, k: (b_, 0, slab0 + k)),
            pl.BlockSpec((1,) + w1.shape[1:], lambda b_, k: (k, 0, 0)),
            pl.BlockSpec((1,) + w1g.shape[1:], lambda b_, k: (k, 0, 0, 0, 0)),
            pl.BlockSpec((1,) + pe_flat.shape[1:], lambda b_, k: (k, 0, 0)),
            pl.BlockSpec((1,) + w2.shape[1:], lambda b_, k: (k, 0, 0)),
            pl.BlockSpec(gain.shape, lambda b_, k: (0, 0)),
        ],
        out_specs=pl.BlockSpec((1, 1, g, n, HD), lambda b_, k: (b_, k, 0, 0, 0)),
        out_shape=jax.ShapeDtypeStruct((b, 2, g, n, HD), F32),
        compiler_params=_cparams(("parallel", "arbitrary")),
        name="compress",
    )(t, w1, w1g, pe_flat, w2, gain)


def _nsa_cmp_kernel(q_ref, kc_ref, vc_ref, o_ref, sel_ref, *, tq, q_off, n_sel):
    i = pl.program_id(2)
    q = q_ref[0, 0, 0].astype(BF16)
    kc = kc_ref[0, 0]
    vc = vc_ref[0, 0]
    ncol = kc.shape[0]
    nbp = ncol // CMP_PER_SLC
    q_lo = q_off + i * tq
    qpos = q_lo + _iota((tq, ncol), 0)
    col = _iota((tq, ncol), 1)
    jj = jnp.zeros_like(col)
    for u in range(1, CMP_PER_SLC):
        jj = jj + jnp.where(col >= u * nbp, 1, 0)
    c_end = (col - jj * nbp) * SLC_BLOCK + jj * CMP_STRIDE + (CMP_LEN - 1)
    maskf = _tile_rows(jnp.where(c_end <= qpos, 1.0, 0.0), NSA_GROUP)
    keep = maskf > 0.5
    s = jnp.where(keep, _dot_nt(q, kc), NEG)
    m = jnp.max(s, axis=1, keepdims=True)
    p = jnp.where(keep, jnp.exp(s - m), 0.0)
    p = p / jnp.maximum(jnp.sum(p, axis=1, keepdims=True), 1e-30)
    o_ref[0, 0, 0] = _dot(p.astype(BF16), vc)
    pg = p[0:tq]
    for r in range(1, NSA_GROUP):
        pg = pg + p[r * tq:(r + 1) * tq]
    imp = pg[:, 0:nbp]
    for j in range(1, CMP_PER_SLC):
        imp = imp + pg[:, j * nbp:(j + 1) * nbp]
    blk = _iota((tq, nbp), 1)
    qp = q_lo + _iota((tq, nbp), 0)
    q_blk = qp >> SLC_SHIFT
    visible = blk * SLC_BLOCK <= qp
    forced = jnp.logical_or(blk == 0, jnp.logical_and(blk <= q_blk, blk > q_blk - N_LOCAL))
    score = jnp.where(visible, jnp.where(forced, FORCE, imp), NEG)
    sel0 = jnp.zeros((tq, nbp), F32)
    rounds = n_sel
    if n_sel > N_LOCAL + 1:
        pre = jnp.logical_and(forced, visible)
        sel0 = jnp.where(pre, 1.0, 0.0)
        score = jnp.where(pre, -jnp.inf, score)
        rounds = n_sel - (N_LOCAL + 1)
    rc = min(tq, SEL_CHUNK_ROWS)
    blkf = _iota((rc, nbp), 1).astype(F32)
    scores = [score[c * rc:(c + 1) * rc] for c in range(tq // rc)]
    sels = [sel0[c * rc:(c + 1) * rc] for c in range(tq // rc)]
    for _ in range(rounds):
        for c in range(len(scores)):
            top = jnp.max(scores[c], axis=1, keepdims=True)
            idx = jnp.min(jnp.where(scores[c] == top, blkf, 1e9), axis=1, keepdims=True)
            pick = blkf == idx
            sels[c] = jnp.where(pick, 1.0, sels[c])
            scores[c] = jnp.where(pick, -jnp.inf, scores[c])
    for c in range(len(scores)):
        sel_ref[0, 0, c * rc:(c + 1) * rc, :] = sels[c]


def _nsa_cmp(q, kc, vc, *, tq, q_off, n_sel):
    b, g, nq, rows, _ = q.shape
    ncol = kc.shape[2]
    nbp = ncol // CMP_PER_SLC
    kern = functools.partial(_nsa_cmp_kernel, tq=tq, q_off=q_off, n_sel=n_sel)
    return pl.pallas_call(
        kern,
        grid=(b, g, nq),
        in_specs=[
            pl.BlockSpec((1, 1, 1, rows, HD), lambda b_, g_, i: (b_, g_, i, 0, 0)),
            pl.BlockSpec((1, 1, ncol, HD), lambda b_, g_, i: (b_, g_, 0, 0)),
            pl.BlockSpec((1, 1, ncol, HD), lambda b_, g_, i: (b_, g_, 0, 0)),
        ],
        out_specs=[
            pl.BlockSpec((1, 1, 1, rows, HD), lambda b_, g_, i: (b_, g_, i, 0, 0)),
            pl.BlockSpec((1, 1, tq, nbp), lambda b_, g_, i: (b_, g_, i, 0)),
        ],
        out_shape=[
            jax.ShapeDtypeStruct((b, g, nq, rows, HD), F32),
            jax.ShapeDtypeStruct((b, g, nq * tq, nbp), F32),
        ],
        compiler_params=_cparams(("parallel", "parallel", "arbitrary")),
        name="nsa_cmp",
    )(q, kc, vc)


def _dsa_kernel(qi_ref, wi_ref, qd_ref, kk_ref, o_ref, key_ref, *, tq, tk, q_off, n_top, offs):
    i = pl.program_id(1)
    qi = qi_ref[0, 0]
    qd = qd_ref[0, 0]
    w = wi_ref[0] * (N_IDX ** -0.5)
    n_kt = kk_ref.shape[1] // tk
    kd_off, ki_off, vd_off = offs

    def cols(ref, k0, off):
        return ref[0, pl.ds(k0, tk), :][:, off:off + HD].astype(BF16)

    q_lo = q_off + i * tq
    hi = jnp.minimum(lax.div(q_lo + tq - 1, tk) + 1, n_kt)
    qpos = q_lo + _iota((tq, tk), 0)
    n_chain = N_DSA if tq >= MIN_CHAIN_ROWS else 1
    cr = N_DSA * tq // n_chain

    def visible(j):
        return (j * tk + _iota((tq, tk), 1)) <= qpos

    def score_tile(j, _):
        k0 = pl.multiple_of(j * tk, tk)
        k = cols(kk_ref, k0, ki_off)
        tot = None
        for c in range(n_chain):
            sc = jnp.maximum(_dot_nt(qi[c * cr:(c + 1) * cr].astype(BF16), k), 0.0)
            for u in range(cr // tq):
                h = c * (cr // tq) + u
                term = w[:, h:h + 1] * sc[u * tq:(u + 1) * tq]
                tot = term if tot is None else tot + term
        tot = jnp.where(visible(j), tot, NEG)
        bits = pltpu.bitcast(tot, jnp.int32)
        key = jnp.where(bits < 0, bits ^ jnp.int32(0x7FFFFFFF), bits)
        key_ref[:, pl.ds(k0, tk)] = jnp.where(tot == 0.0, 0, key)
        return 0

    _unrolled_loop(0, hi, score_tile, 0)

    def count_ge(c):
        def body(j, acc):
            blk = key_ref[:, pl.ds(pl.multiple_of(j * tk, tk), tk)]
            hit = jnp.where(blk >= c, 1.0, 0.0)
            part = hit[:, 0:LANES]
            for u in range(1, tk // LANES):
                part = part + hit[:, u * LANES:(u + 1) * LANES]
            return acc + part
        acc = _unrolled_loop(0, hi, body, jnp.zeros((tq, LANES), F32))
        return jnp.sum(acc, axis=1, keepdims=True)

    kf = float(n_top)
    tau = jnp.where(count_ge(jnp.zeros((tq, 1), jnp.int32)) >= kf, 0, INT_MIN).astype(jnp.int32)

    def bit_body(t, tau):
        cand = tau + jnp.left_shift(jnp.int32(1), 30 - t)
        return jnp.where(count_ge(cand) >= kf, cand, tau)

    tau = lax.fori_loop(0, 31, bit_body, tau)
    need = kf - count_ge(tau + 1)
    before = jnp.where(_iota((tk, tk), 0) < _iota((tk, tk), 1), 1.0, 0.0).astype(BF16)

    def attend(j, carry):
        state, n_eq = carry
        k0 = pl.multiple_of(j * tk, tk)
        key = key_ref[:, pl.ds(k0, tk)]
        eqf = jnp.where(key == tau, 1.0, 0.0)
        rank = n_eq + _dot(eqf.astype(BF16), before)
        kept = jnp.logical_or(key > tau, jnp.logical_and(key == tau, rank < need))
        bias = jnp.where(jnp.logical_and(kept, visible(j)), 0.0, NEG)
        k = cols(kk_ref, k0, kd_off)
        v1 = _with_ones(cols(kk_ref, k0, vd_off))
        bias = _tile_rows(bias, cr // tq)
        new = []
        for c in range(n_chain):
            s = _dot_nt(qd[c * cr:(c + 1) * cr].astype(BF16), k) + bias
            new.append(_online_step_ones(s, v1, *state[c]))
        return tuple(new), n_eq + jnp.sum(eqf, axis=1, keepdims=True)

    init = (tuple(_online_init_ones(cr) for _ in range(n_chain)), jnp.zeros((tq, 1), F32))
    state, _ = _unrolled_loop(0, hi, attend, init)
    for c in range(n_chain):
        acc = state[c][1]
        o_ref[0, 0, c * cr:(c + 1) * cr, :] = acc[:, :HD] / acc[:, HD:HD + 1]


def _dsa(qi, wi, qd, kk, *, width, blk, offs, tq, tk, q_off, n_top):
    b, nq, rows, _ = qi.shape
    lk = kk.shape[1]
    assert lk % tk == 0
    kern = functools.partial(_dsa_kernel, tq=tq, tk=tk, q_off=q_off, n_top=n_top, offs=offs)
    qspec = pl.BlockSpec((1, 1, rows, HD), lambda b_, i: (b_, i, 0, 0))
    return pl.pallas_call(
        kern,
        grid=(b, nq),
        in_specs=[qspec, pl.BlockSpec((1, tq, N_IDX), lambda b_, i: (b_, i, 0)), qspec,
                  pl.BlockSpec((1, lk, width), lambda b_, i: (b_, 0, blk))],
        out_specs=pl.BlockSpec((1, 1, rows, HD), lambda b_, i: (b_, i, 0, 0)),
        out_shape=jax.ShapeDtypeStruct((b, nq, rows, HD), F32),
        scratch_shapes=[pltpu.VMEM((tq, lk), jnp.int32)],
        compiler_params=_cparams(("parallel", "arbitrary")),
        name="dsa",
    )(qi, wi, qd, kk)


def _gather_kernel(*refs, pp, n_steps):
    pool_refs = refs[1:1 + pp]
    new_ref, o_ref = refs[1 + pp], refs[2 + pp]
    s = pl.program_id(1)

    def page(ref):
        mid = ref.shape[2:-1]
        if not mid:
            return ref[0]
        pieces = [ref[(0, slice(None)) + idx + (slice(None),)] for idx in np.ndindex(*mid)]
        return jnp.concatenate(pieces, axis=1)

    @pl.when(s < n_steps)
    def _():
        for u in range(pp):
            o_ref[0, u * PAGE:(u + 1) * PAGE, :] = page(pool_refs[u])

    @pl.when(s >= n_steps)
    def _():
        o_ref[...] = new_ref[...]


def _page_gather(pool, table, new, *, pp):
    b, n_pages = table.shape
    tail = pool.shape[2:]
    width = int(np.prod(tail))
    assert n_pages % pp == 0 and new.shape[1:] == (pp * PAGE, width)
    n_steps = n_pages // pp

    def page_map(u):
        return lambda b_, s, t: (t[b_, jnp.minimum(s, n_steps - 1) * pp + u], 0) + (0,) * len(tail)

    kern = functools.partial(_gather_kernel, pp=pp, n_steps=n_steps)
    grid_spec = pltpu.PrefetchScalarGridSpec(
        num_scalar_prefetch=1,
        grid=(b, n_steps + 1),
        in_specs=[pl.BlockSpec((1, PAGE) + tail, page_map(u)) for u in range(pp)]
        + [pl.BlockSpec((1, pp * PAGE, width), lambda b_, s, t: (b_, 0, 0))],
        out_specs=pl.BlockSpec((1, pp * PAGE, width), lambda b_, s, t: (b_, s, 0)),
    )
    return pl.pallas_call(
        kern,
        grid_spec=grid_spec,
        out_shape=jax.ShapeDtypeStruct((b, (n_pages + pp) * PAGE, width), pool.dtype),
        compiler_params=_cparams(("parallel", "arbitrary")),
        name="page_gather",
    )(table, *([pool] * pp), new)


def _rope_tables(pos):
    half = HD // 2
    inv = ROPE_THETA ** (-jnp.arange(half, dtype=F32) / half)
    ang = pos.astype(F32)[:, None] * inv[None, :]
    cos, sin = jnp.cos(ang), jnp.sin(ang)
    cos128 = jnp.tile(jnp.concatenate([cos, cos], axis=1), (1, LANES // HD))
    sin128 = jnp.tile(jnp.concatenate([-sin, sin], axis=1), (1, LANES // HD))
    return cos128, sin128


def _tile_gain(g):
    return jnp.tile(g.reshape(1, HD), (1, LANES // HD))


def _heads(a, b, t, h, d, scale=None):
    a = a.reshape(b, t, h, d)
    if scale is not None:
        a = a * scale
    return a.transpose(0, 2, 1, 3)


def _stack_q(a, tq):
    b, hk, r, t, d = a.shape
    return a.reshape(b, hk, r, t // tq, tq, d).transpose(0, 1, 3, 2, 4, 5).reshape(b, hk, t // tq, r * tq, d)


def _unstack_q(a, r, tq):
    b, hk, nq, _, d = a.shape
    a = a.reshape(b, hk, nq, r, tq, d).transpose(0, 2, 4, 1, 3, 5)
    return a.reshape(b * nq * tq, hk * r * d)


def _pad_rows(a, rows):
    return jnp.pad(a, ((0, 0), (0, rows - a.shape[1])) + ((0, 0),) * (a.ndim - 2))


def _even_mixer(x2, b, t, q_off, cs, past, prm, cfg):
    n = b * t
    program = [
        (0, SB_W, None, None, [(0, False)]),
        (3 * SB_W, DF_W, 0, None, [(SB_W, True)]),
        (SB_W, SB_W, None, None, [(SB_W + DF_W, False)]),
        (3 * SB_W + DF_W, DF_W, 1, None, [(2 * SB_W + DF_W, True)]),
        (2 * SB_W, SB_W, None, None, [(2 * (SB_W + DF_W), False)]),
        (3 * SB_W + 2 * DF_W, DF_W, None, None, [(3 * SB_W + 2 * DF_W, False)]),
    ]
    gains = jnp.concatenate([_tile_gain(prm["df_qk_gain"][0]), _tile_gain(prm["df_qk_gain"][1])], axis=0)
    proj = _mm([x2, cs[0], cs[1]], [prm["g0"], gains], prm["ev_w_in"], _lhs_norm, program, 3 * (SB_W + DF_W),
               tm=cfg["tm"], rope_idx=(1, 2), gains_idx=1)
    mw = SB_W + DF_W
    new_k = proj[:, mw:2 * mw].reshape(b, t, mw)
    new_v = proj[:, 2 * mw:3 * mw].reshape(b, t, mw)
    lam_init = 0.8 - 0.6 * math.exp(-0.3 * prm["layer"])
    lv, subln = prm["df_lambda"], prm["df_subln_gain"].reshape(1, 2 * HD)
    d = x2.shape[1]
    if past is None:
        proj3 = proj.reshape(b, t, 3 * mw)
        ns = SB_W // LANES
        o_sb = _sb(proj3, t=cfg["t_even"], q_slab=0, k_slab=2 * ns, v_slab=4 * ns, n_slabs=ns)
        o_df = _df(proj3, lv, subln, t=cfg["t_even"], q_slab=ns, k_slab=3 * ns, v_slab=5 * ns, n_slabs=ns,
                   lam_init=lam_init)
        return _mm([o_sb.reshape(n, SB_W), o_df.reshape(n, DF_W), x2], [], prm["ev_w_out"], _lhs_cat2,
                   [(0, d, None, None, [(0, False)])], d, tm=cfg["tm"], res_idx=2), new_k, new_v
    pool_k, pool_v, table = past
    qcat = proj[:, :mw].reshape(b, t, mw).transpose(0, 2, 1) * SCALE
    qt = (jnp.tile(qcat, (1, 1, LANES // t)) * _dec_even_mask(t)).astype(BF16)
    o = _dec_even(qt, pool_k, pool_v, table, _pad_rows(new_k, PAGE), _pad_rows(new_v, PAGE), lv, subln,
                  pp=cfg["pp"], ts=t, lam_init=lam_init)
    return _mm([o.reshape(n, mw), x2], [], prm["ev_w_out"], _lhs_plain,
               [(0, d, None, None, [(0, False)])], d, tm=cfg["tm"], res_idx=1), new_k, new_v


def _dec_even_mask(ts):
    f = np.arange(SB_W + DF_W)[:, None]
    c = np.arange(LANES)[None, :]
    half = LANES // 2
    sb = (f < SB_W) & (c < half) & (f // HD == c // ts)
    df = (f >= SB_W) & (c >= half) & ((f - SB_W) // HD == (c - half) // ts)
    return jnp.asarray((sb | df).astype(np.float32))


def _odd_layout():
    widths = (N_NSA * HD, 128, 128, 128, 128, 128, 128, N_NSA * 3, N_DSA * HD, HD, HD, N_IDX * HD, HD, N_IDX)
    offs = np.concatenate([[0], np.cumsum(widths)])
    (q_n, k_c, v_c, k_s, v_s, k_w, v_w, gate, q_d, k_d, v_d, q_i, k_i, w_i) = [
        (int(offs[j]), int(offs[j + 1])) for j in range(len(widths))]
    pieces = [q_n, (k_c[0], v_w[1]), q_d, q_i, k_d, k_i, v_d, w_i, ("pad", HD - N_IDX), gate,
              ("pad", LANES - N_NSA * 3)]
    program = [
        (0, 512, 0, None, [(0, False), (512, True)]),
        (512, 128, None, None, [(1024, False)]),
        (640, 128, None, None, [(1152, False)]),
        (768, 128, 1, None, [(1280, True)]),
        (896, 128, None, None, [(1408, False)]),
        (1024, 128, 2, None, [(1536, True)]),
        (1152, 128, None, None, [(1664, False)]),
        (1280, 512, 3, None, [(1792, True)]),
        (1792, 512, None, None, [(2304, True)]),
        (2304, 128, 4, None, [(2816, True)]),
        (2432, 128, None, None, [(2944, False)]),
        (2560, 128, None, "sigmoid", [(3072, False)]),
    ]
    return pieces, program, 3200


def _permute_cols(w, pieces):
    cols = []
    for p in pieces:
        if p[0] == "pad":
            cols.append(jnp.zeros((w.shape[0], p[1]), w.dtype))
        else:
            cols.append(w[:, p[0]:p[1]])
    return jnp.concatenate(cols, axis=1)


def _odd_mixer(x2, b, t, t_real, q_off, cs, past, prm, cfg):
    n = b * t
    pieces, program, out_cols = _odd_layout()
    ng, dg = prm["nsa_qk_gain"], prm["dsa_qk_gain"]
    gains = jnp.concatenate([
        _tile_gain(ng[0]), _tile_gain(ng[2]), _tile_gain(ng[3]), _tile_gain(dg[0]),
        jnp.concatenate([dg[1], dg[2]]).reshape(1, LANES)], axis=0)
    w_in = _permute_cols(prm["od_w_in"], pieces)
    proj = _mm([x2, cs[0], cs[1]], [prm["g0"], gains], w_in, _lhs_norm, program, out_cols,
               tm=cfg["tm"], rope_idx=(1, 2), gains_idx=1)
    g = N_NSA_KV
    new_nsa = proj[:, 1024:1536].reshape(b, t, 4 * g * HD)
    new_win = proj[:, 1536:1792].reshape(b, t, 2, g, HD)
    new_dsa = jnp.concatenate([proj[:, 2816:2880], proj[:, 2944:3008], proj[:, 2880:2944]], axis=1).reshape(b, t, 3 * HD)
    w_i = proj[:, 3008:3008 + N_IDX].reshape(b, t, N_IDX)
    gate = proj[:, 3072:3072 + N_NSA * 3].reshape(n, N_NSA, 3)
    if past is None:
        proj3 = proj.reshape(b, t, out_cols)
        nsa_src, win_src = (proj3, 8), (proj3, 12)
        dsa_src = dict(kk=proj3, width=2 * LANES, blk=11, offs=(0, HD, 2 * HD))
        win_off = 0
        new_state = new_win[:, -min(WINDOW, t):]
        l_real = t
    else:
        pool_nsa, pool_dsa, state, table = past
        nsa_buf = _page_gather(pool_nsa, table, _pad_rows(new_nsa, cfg["pp"] * PAGE), pp=cfg["pp"])
        dsa_buf = _page_gather(pool_dsa, table, _pad_rows(new_dsa, cfg["pp"] * PAGE), pp=cfg["pp"])
        wb = state.shape[1]
        win_buf = _pad_rows(jnp.concatenate([state, new_win], axis=1), wb + cfg["tk_win"])
        nsa_src, win_src = (nsa_buf, 0), (win_buf.reshape(b, wb + cfg["tk_win"], 2 * g * HD), 0)
        dsa_src = dict(kk=dsa_buf, width=3 * HD, blk=0, offs=(0, 2 * HD, HD))
        win_off = q_off - wb
        new_state = jnp.concatenate([state, new_win[:, :t_real]], axis=1)[:, -wb:]
        l_real = table.shape[1] * PAGE + t_real
    lk = nsa_src[0].shape[1]
    tq = cfg["tq"]

    n_cmp = lk // CMP_STRIDE
    pe_flat = prm["cmp_pe"].reshape(2, 1, CMP_LEN * HD)
    cmp = _compress(nsa_src[0], nsa_src[1], prm["cmp_w1"], pe_flat, prm["cmp_w2"], ng[1].reshape(1, HD))
    nb = n_cmp // CMP_PER_SLC
    nbp = -(-nb // LANES) * LANES
    cmp = cmp.reshape(b, 2, g, nb, CMP_PER_SLC, HD).transpose(0, 1, 2, 4, 3, 5)
    cmp = jnp.pad(cmp, ((0, 0),) * 4 + ((0, nbp - nb), (0, 0))).reshape(b, 2, g, CMP_PER_SLC * nbp, HD).astype(BF16)

    def group_q(cols):
        a = _heads(cols, b, t, N_NSA, HD, SCALE).reshape(b, g, NSA_GROUP, t, HD)
        return _stack_q(a, tq)

    q_n = group_q(proj[:, 0:512])
    q_r = group_q(proj[:, 512:1024])
    n_blk = -(-l_real // SLC_BLOCK)
    o_c, sel = _nsa_cmp(q_n, cmp[:, 0], cmp[:, 1], tq=tq, q_off=q_off, n_sel=min(N_SLC, n_blk))

    o_s = _flash(q_r, nsa_src[0], kv_blk=(nsa_src[1] + 2) // 2, reps=NSA_GROUP, tq=tq,
                 tk=cfg["tk"], q_off=q_off, k_off=0, mode="causal", bm=sel)
    o_w = _flash(q_r, win_src[0], kv_blk=win_src[1] // 2, reps=NSA_GROUP, tq=tq,
                 tk=cfg["tk_win"], q_off=q_off, k_off=win_off, mode="window")

    q_d = _stack_q(_heads(proj[:, 1792:2304], b, t, N_DSA, HD, SCALE)[:, None], tq)[:, 0]
    q_i = _stack_q(_heads(proj[:, 2304:2816], b, t, N_IDX, HD, IDX_SCALE)[:, None], tq)[:, 0]
    o_d = _dsa(q_i, w_i, q_d, tq=tq, tk=cfg["tk"], q_off=q_off, n_top=min(DSA_TOPK_MAX, l_real // 4), **dsa_src)

    o_c, o_s, o_w = (_unstack_q(o, NSA_GROUP, tq) for o in (o_c, o_s, o_w))
    o_d = _unstack_q(o_d[:, None], N_DSA, tq)
    gfull = [jnp.repeat(gate[:, :, j], HD, axis=1) for j in range(3)]
    d = x2.shape[1]
    out = _mm([o_c, o_s, o_w, o_d] + gfull + [x2], [], prm["od_w_out"], _lhs_odd,
              [(0, d, None, None, [(0, False)])], d, tm=cfg["tm"], res_idx=7)
    return out, new_nsa, new_dsa, new_state


def _cross(x2, b, t, mem_k, mem_v, prm, cfg):
    d = x2.shape[1]
    xw = N_XH * HD
    q = _mm([x2], [prm["g1"], _tile_gain(prm["x_gq"])], prm["x_wq"], _lhs_norm,
            [(0, xw, 0, None, [(0, False)])], xw, tm=cfg["tm"], gains_idx=1)
    tq = cfg["tq"]
    qh = _stack_q(_heads(q, b, t, N_XH, HD, SCALE)[:, :, None], tq)
    o = _flash(qh, mem_k, mem_v, reps=1, tq=tq, tk=mem_k.shape[2], q_off=0, k_off=0, mode="full")
    o = _unstack_q(o, 1, tq)
    return _mm([o, x2], [], prm["x_wo"], _lhs_plain, [(0, d, None, None, [(0, False)])], d, tm=cfg["tm"], res_idx=1)


def _memory_kv(mem2, prm):
    xw = N_XH * HD
    w = jnp.concatenate([prm["x_wk"], prm["x_wv"]], axis=1)
    return _mm([mem2], [_tile_gain(prm["x_gk"])], w, _lhs_plain,
               [(0, xw, 0, None, [(0, False)]), (xw, xw, None, None, [(xw, False)])], 2 * xw,
               tm=min(256, mem2.shape[0]), gains_idx=0)


def _run_group(x, q_off, t_real, mem_kvs, pasts, layers, cfg):
    b, t, d = x.shape
    x2 = x.reshape(b * t, d)
    pos = q_off + jnp.arange(t, dtype=jnp.int32)
    cos, sin = _rope_tables(pos)
    cs = (jnp.tile(cos, (b, 1)), jnp.tile(sin, (b, 1)))
    outs = {}
    for li, prm in enumerate(layers):
        if li % 2 == 0:
            x2, nk, nv = _even_mixer(x2, b, t, q_off, cs, pasts[li], prm, cfg)
            outs["ek"], outs["ev"] = nk, nv
        else:
            x2, nn, nd, nw = _odd_mixer(x2, b, t, t_real, q_off, cs, pasts[li], prm, cfg)
            outs["on"], outs["od"], outs["ow"] = nn, nd, nw
        x2 = _cross(x2, b, t, mem_kvs[li][0], mem_kvs[li][1], prm, cfg)
        if li % 2 == 0:
            x2 = _ffn(x2, prm["g2"], prm["router"], prm["w1"], prm["w3"], prm["w2"], tm=cfg["tm_ffn"],
                      tf=cfg["tf"], routed=False)
        elif "moe_cap" in cfg and x2.shape[0] % cfg["tm_moe"] == 0:
            x2 = _moe(x2, prm["g2"], prm["router"], prm["w1"], prm["w3"], prm["w2"], tm=cfg["tm_moe"],
                      tf=cfg["tf"], cap=cfg["moe_cap"])
        else:
            x2 = _ffn(x2, prm["g2"], prm["router"], prm["w1"], prm["w3"], prm["w2"], tm=cfg["tm_ffn"],
                      tf=cfg["tf"], routed=True)
    return x2.reshape(b, t, d), outs


def kernel(x_prompt, x_sample, mem_prompt, cache_even_k, cache_even_v, cache_odd_nsa, cache_odd_dsa, state_odd_win, cache_mem, page_table, norm_gain, ev_w_in, ev_w_out, df_qk_gain, df_lambda, df_subln_gain, ffn_w1, ffn_w3, ffn_w2, od_w_in, od_w_out, nsa_qk_gain, cmp_pe, cmp_w1, cmp_w2, dsa_qk_gain, moe_router, moe_w1, moe_w3, moe_w2, x_wq, x_wk, x_wv, x_wo, x_qk_gain):
    depth = norm_gain.shape[0]
    bp, tp, d = x_prompt.shape
    bs, ts, _ = x_sample.shape
    n_mem = mem_prompt.shape[1]
    xw = N_XH * HD

    layers = []
    for l in range(depth):
        i = l // 2
        prm = {
            "layer": l,
            "g0": norm_gain[l, 0].reshape(1, d), "g1": norm_gain[l, 1].reshape(1, d), "g2": norm_gain[l, 2].reshape(1, d),
            "x_wq": x_wq[l].astype(BF16), "x_wk": x_wk[l].astype(BF16), "x_wv": x_wv[l].astype(BF16),
            "x_wo": x_wo[l].astype(BF16), "x_gq": x_qk_gain[l, 0], "x_gk": x_qk_gain[l, 1],
        }
        if l % 2 == 0:
            prm.update({
                "ev_w_in": ev_w_in[i].astype(BF16), "ev_w_out": ev_w_out[i].astype(BF16),
                "df_qk_gain": df_qk_gain[i], "df_lambda": df_lambda[i], "df_subln_gain": df_subln_gain[i],
                "router": jnp.zeros((SUBLANES, LANES), F32),
                "w1": ffn_w1[i][None].astype(BF16), "w3": ffn_w3[i][None].astype(BF16), "w2": ffn_w2[i][None].astype(BF16),
            })
        else:
            prm.update({
                "od_w_in": od_w_in[i].astype(BF16), "od_w_out": od_w_out[i].astype(BF16),
                "nsa_qk_gain": nsa_qk_gain[i], "dsa_qk_gain": dsa_qk_gain[i],
                "cmp_pe": cmp_pe[i], "cmp_w1": cmp_w1[i].astype(BF16), "cmp_w2": cmp_w2[i].astype(BF16),
                "router": jnp.pad(moe_router[i], ((0, 0), (0, LANES - N_EXPERTS))),
                "w1": moe_w1[i].astype(BF16), "w3": moe_w3[i].astype(BF16), "w2": moe_w2[i].astype(BF16),
            })
        layers.append(prm)

    def mem_heads(kv, b):
        k = kv[:, :, 0].transpose(0, 2, 1, 3).astype(BF16)
        v = kv[:, :, 1].transpose(0, 2, 1, 3).astype(BF16)
        return k, v

    mem2 = mem_prompt.reshape(bp * n_mem, d)
    mem_p = [_memory_kv(mem2, layers[l]).reshape(bp, n_mem, 2, N_XH, HD) for l in range(depth)]
    ff = ffn_w1.shape[2]
    tf = ff // 2 if (ff // 2) % LANES == 0 else ff
    cfg_p = {"tm": 256, "tq": 128, "tk": 256, "t_even": 256, "tk_win": 256, "tm_ffn": 512, "tf": tf,
             "tm_moe": 1024, "moe_cap": 320}
    cfg_p["tq"] = min(cfg_p["tq"], tp)
    y_prompt, op = _run_group(x_prompt, 0, tp, [mem_heads(m, bp) for m in mem_p], [None] * depth, layers, cfg_p)
    p_mem = jnp.stack(mem_p)

    n_past = page_table.shape[1] * cache_even_k.shape[2]
    ts_pad = -(-ts // SUBLANES) * SUBLANES
    xs = _pad_rows(x_sample, ts_pad)
    pasts = []
    for l in range(depth):
        i = l // 2
        if l % 2 == 0:
            pasts.append((cache_even_k[i], cache_even_v[i], page_table))
        else:
            pn = cache_odd_nsa[i]
            pd = cache_odd_dsa[i]
            pasts.append((pn, pd, state_odd_win[i], page_table))
    cfg_s = {"tm": bs * ts_pad, "tq": ts_pad, "tk": 512, "pp": 4, "tk_win": 128, "tm_ffn": bs * ts_pad,
             "tf": tf, "new_rows": 512}
    y_s, os_ = _run_group(xs, n_past, ts, [mem_heads(cache_mem[l], bs) for l in range(depth)], pasts, layers, cfg_s)
    y_sample = y_s[:, :ts]

    g = N_NSA_KV
    return (
        y_prompt, y_sample,
        op["ek"][None], op["ev"][None],
        op["on"].reshape(1, bp, tp, 4, g, HD), op["od"].reshape(1, bp, tp, 3, HD),
        op["ow"][None], p_mem,
        os_["ek"][:, :ts][None], os_["ev"][:, :ts][None],
        os_["on"][:, :ts].reshape(1, bs, ts, 4, g, HD), os_["od"][:, :ts].reshape(1, bs, ts, 3, HD),
        os_["ow"][None],
    )
```

```python
import functools
import math

import jax
import jax.numpy as jnp
import numpy as np
from jax import lax
from jax.experimental import pallas as pl
from jax.experimental.pallas import tpu as pltpu

F32 = jnp.float32
BF16 = jnp.bfloat16

HD = 64
N_SB = 8
N_DF = 4
N_NSA = 8
N_NSA_KV = 2
NSA_GROUP = N_NSA // N_NSA_KV
N_DSA = 8
N_IDX = 8
N_XH = 4
N_EXPERTS = 8
ROPE_THETA = 10000.0
CMP_LEN = 32
CMP_STRIDE = 16
SLC_BLOCK = 64
CMP_PER_SLC = SLC_BLOCK // CMP_STRIDE
N_SLC = 16
N_LOCAL = 2
WINDOW = 512
DSA_TOPK_MAX = 256
EPS = 1e-6
NEG = -1e30
FORCE = 1e9
SCALE = HD ** -0.5
IDX_SCALE = HD ** -0.5
SB_W = N_SB * HD
HD_SHIFT = 6
SLC_SHIFT = 6
DF_W = N_DF * 2 * HD

LANES = 128
SUBLANES = 8
PAGE = 128
VMEM_LIMIT = 52 * 1024 * 1024
INT_MIN = -2 ** 31
ROW_SPLIT = 2
SB_CUT = 120.0
SEL_CHUNK_ROWS = 16
MIN_CHAIN_ROWS = 64
KEY_UNROLL = 4

_NT = (((1,), (1,)), ((), ()))


def _cparams(sem):
    return pltpu.CompilerParams(dimension_semantics=sem, vmem_limit_bytes=VMEM_LIMIT)


def _dot(a, b):
    return jnp.dot(a, b, preferred_element_type=F32)


def _dot_nt(a, b):
    return lax.dot_general(a, b, _NT, preferred_element_type=F32)


def _split_dot(x, m_bf16):
    hi = x.astype(BF16)
    lo = (x - hi.astype(F32)).astype(BF16)
    return _dot(hi, m_bf16) + _dot(lo, m_bf16)


def _split_dot_rhs(m_bf16, x):
    hi = x.astype(BF16)
    lo = (x - hi.astype(F32)).astype(BF16)
    return _dot(m_bf16, hi) + _dot(m_bf16, lo)


def _iota(shape, dim):
    return lax.broadcasted_iota(jnp.int32, shape, dim)


def _rms_rows(x, g):
    return x * lax.rsqrt(jnp.mean(x * x, axis=-1, keepdims=True) + EPS) * g


def _group_mean_matrix():
    r = _iota((LANES, LANES), 0) >> HD_SHIFT
    c = _iota((LANES, LANES), 1) >> HD_SHIFT
    return jnp.where(r == c, 1.0 / HD, 0.0).astype(BF16)


def _head_norm(y, g, gm):
    ms = _split_dot(y * y, gm)
    return y * lax.rsqrt(ms + EPS) * g


def _rope_slab(y, cos, sin):
    lane = _iota(y.shape, 1)
    first = (lane & (HD - 1)) < (HD // 2)
    swapped = jnp.where(first, pltpu.roll(y, LANES - HD // 2, 1), pltpu.roll(y, HD // 2, 1))
    return y * cos + swapped * sin


def _mm_kernel(*refs, n_rows, n_consts, lhs_fn, program, rope_idx, gains_idx, res_idx):
    rows = refs[:n_rows]
    consts = refs[n_rows:n_rows + n_consts]
    w_ref = refs[n_rows + n_consts]
    o_ref = refs[-1]
    lhs = lhs_fn(rows, consts).astype(BF16)
    gm = _group_mean_matrix() if gains_idx is not None else None
    for (src, width, gain_row, act, outs) in program:
        y_full = _dot(lhs, w_ref[:, src:src + width])
        for s in range(width // LANES):
            y = y_full[:, s * LANES:(s + 1) * LANES]
            if gain_row is not None:
                y = _head_norm(y, consts[gains_idx][gain_row:gain_row + 1, :], gm)
            if act == "sigmoid":
                y = 1.0 / (1.0 + jnp.exp(-y))
            for (dst, rope) in outs:
                z = y
                if rope:
                    z = _rope_slab(y, rows[rope_idx[0]][...], rows[rope_idx[1]][...])
                d0 = dst + s * LANES
                if res_idx is not None:
                    z = z + rows[res_idx][:, d0:d0 + LANES]
                o_ref[:, d0:d0 + LANES] = z


def _mm(rows, consts, w, lhs_fn, program, out_cols, *, tm, rope_idx=None, gains_idx=None, res_idx=None, name="mm"):
    n = rows[0].shape[0]
    assert n % tm == 0
    in_specs = [pl.BlockSpec((tm, r.shape[1]), lambda i: (i, 0)) for r in rows]
    in_specs += [pl.BlockSpec(c.shape, lambda i: (0, 0)) for c in consts]
    in_specs += [pl.BlockSpec(w.shape, lambda i: (0, 0))]
    kern = functools.partial(_mm_kernel, n_rows=len(rows), n_consts=len(consts), lhs_fn=lhs_fn,
                             program=program, rope_idx=rope_idx, gains_idx=gains_idx, res_idx=res_idx)
    return pl.pallas_call(
        kern,
        grid=(n // tm,),
        in_specs=in_specs,
        out_specs=pl.BlockSpec((tm, out_cols), lambda i: (i, 0)),
        out_shape=jax.ShapeDtypeStruct((n, out_cols), F32),
        compiler_params=_cparams(("parallel",)),
        name=name,
    )(*rows, *consts, w)


def _lhs_norm(rows, consts):
    return _rms_rows(rows[0][...], consts[0][...])


def _lhs_plain(rows, consts):
    return rows[0][...]


def _lhs_cat2(rows, consts):
    return jnp.concatenate([rows[0][...], rows[1][...]], axis=1)


def _lhs_odd(rows, consts):
    oc, os_, ow, od = rows[0][...], rows[1][...], rows[2][...], rows[3][...]
    g0, g1, g2 = rows[4][...], rows[5][...], rows[6][...]
    return jnp.concatenate([g0 * oc + g1 * os_ + g2 * ow, od], axis=1)


def _ffn_kernel(x_ref, g_ref, r_ref, w1_ref, w3_ref, w2_ref, o_ref, h_ref, acc_ref, gate_ref, *, routed):
    e = pl.program_id(1)
    f = pl.program_id(2)
    first = jnp.logical_and(e == 0, f == 0)
    last = jnp.logical_and(e == pl.num_programs(1) - 1, f == pl.num_programs(2) - 1)

    @pl.when(first)
    def _():
        x = x_ref[...]
        h = _rms_rows(x, g_ref[...])
        h_ref[...] = h.astype(BF16)
        acc_ref[...] = x
        if routed:
            logits = jnp.dot(h, r_ref[...], preferred_element_type=F32, precision=lax.Precision.HIGHEST)
            col = _iota(logits.shape, 1).astype(F32)
            logits = jnp.where(col < N_EXPERTS, logits, -jnp.inf)
            m1 = jnp.max(logits, axis=1, keepdims=True)
            i1 = jnp.min(jnp.where(logits == m1, col, 1e9), axis=1, keepdims=True)
            rest = jnp.where(col == i1, -jnp.inf, logits)
            m2 = jnp.max(rest, axis=1, keepdims=True)
            i2 = jnp.min(jnp.where(rest == m2, col, 1e9), axis=1, keepdims=True)
            e2 = jnp.exp(m2 - m1)
            g1 = 1.0 / (1.0 + e2)
            g2 = e2 / (1.0 + e2)
            gate_ref[...] = jnp.where(col == i1, g1, 0.0) + jnp.where(col == i2, g2, 0.0)

    def compute(gcol):
        h = h_ref[...]
        u = _dot(h, w1_ref[0])
        v = _dot(h, w3_ref[0])
        a = (u * (1.0 / (1.0 + jnp.exp(-u)))) * v
        y = _dot(a.astype(BF16), w2_ref[0])
        if gcol is not None:
            y = gcol * y
        acc_ref[...] += y

    if routed:
        col = _iota(gate_ref.shape, 1)
        gcol = jnp.sum(jnp.where(col == e, gate_ref[...], 0.0), axis=1, keepdims=True)
        active = jnp.max(gcol) > 0.0

        @pl.when(active)
        def _():
            compute(gcol)
    else:
        compute(None)

    @pl.when(last)
    def _():
        o_ref[...] = acc_ref[...]


def _moe_kernel(x_ref, g_ref, r_ref, w1_ref, w3_ref, w2_ref, o_ref, h_ref, acc_ref, gate_ref, slot_ref,
                slot_t_ref, he_ref, ye_ref, *, cap):
    e = pl.program_id(1)
    f = pl.program_id(2)
    nf = pl.num_programs(2)
    tm = x_ref.shape[0]
    first = jnp.logical_and(e == 0, f == 0)
    last = jnp.logical_and(e == pl.num_programs(1) - 1, f == nf - 1)

    @pl.when(first)
    def _():
        x = x_ref[...]
        h = _rms_rows(x, g_ref[...])
        h_ref[...] = h.astype(BF16)
        acc_ref[...] = x
        logits = jnp.dot(h, r_ref[...], preferred_element_type=F32, precision=lax.Precision.HIGHEST)
        col = _iota(logits.shape, 1).astype(F32)
        logits = jnp.where(col < N_EXPERTS, logits, -jnp.inf)
        m1 = jnp.max(logits, axis=1, keepdims=True)
        i1 = jnp.min(jnp.where(logits == m1, col, 1e9), axis=1, keepdims=True)
        rest = jnp.where(col == i1, -jnp.inf, logits)
        m2 = jnp.max(rest, axis=1, keepdims=True)
        i2 = jnp.min(jnp.where(rest == m2, col, 1e9), axis=1, keepdims=True)
        e2 = jnp.exp(m2 - m1)
        gate_ref[...] = jnp.where(col == i1, 1.0 / (1.0 + e2), 0.0) + jnp.where(col == i2, e2 / (1.0 + e2), 0.0)
        chosen = jnp.where(jnp.logical_or(col == i1, col == i2), 1.0, 0.0)
        earlier = jnp.where(_iota((tm, tm), 1) < _iota((tm, tm), 0), 1.0, 0.0).astype(BF16)
        slot = jnp.where(chosen > 0.5, _dot(earlier, chosen.astype(BF16)), -1.0)
        slot_ref[...] = slot
        slot_t_ref[...] = jnp.transpose(slot)

    lane = _iota((tm, LANES), 1)
    mine = lane == e
    gcol = jnp.sum(jnp.where(mine, gate_ref[...], 0.0), axis=1, keepdims=True)
    slot_col = jnp.sum(jnp.where(mine, slot_ref[...], 0.0), axis=1, keepdims=True)
    slot_row = slot_t_ref[pl.ds(e, 1), :]
    count = jnp.sum(jnp.where(slot_row >= 0.0, 1.0, 0.0)).astype(jnp.int32)
    n_chunk = lax.div(count + (cap - 1), cap)

    def pack(c):
        want = (c * cap + _iota((cap, tm), 0)).astype(F32)
        return jnp.where(slot_row == want, 1.0, 0.0).astype(BF16)

    def unpack(c):
        want = (c * cap + _iota((tm, cap), 1)).astype(F32)
        return jnp.where(slot_col == want, 1.0, 0.0).astype(BF16)

    def expert(rows):
        u = _dot(rows, w1_ref[0])
        v = _dot(rows, w3_ref[0])
        a = (u * (1.0 / (1.0 + jnp.exp(-u)))) * v
        return _dot(a.astype(BF16), w2_ref[0])

    @pl.when(n_chunk > 0)
    def _():
        @pl.when(f == 0)
        def _():
            he_ref[...] = _dot(pack(0), h_ref[...]).astype(BF16)
            ye_ref[...] = expert(he_ref[...])

        @pl.when(f > 0)
        def _():
            ye_ref[...] += expert(he_ref[...])

        @pl.when(f == nf - 1)
        def _():
            acc_ref[...] += gcol * _split_dot_rhs(unpack(0), ye_ref[...])

    def extra(c, _):
        part = expert(_dot(pack(c), h_ref[...]).astype(BF16))
        acc_ref[...] += gcol * _split_dot_rhs(unpack(c), part)
        return 0

    lax.fori_loop(1, n_chunk, extra, 0)

    @pl.when(last)
    def _():
        o_ref[...] = acc_ref[...]


def _moe(x, g, router, w1, w3, w2, *, tm, tf, cap):
    n, d = x.shape
    ne, _, ff = w1.shape
    assert n % tm == 0 and ff % tf == 0
    kern = functools.partial(_moe_kernel, cap=cap)
    return pl.pallas_call(
        kern,
        grid=(n // tm, ne, ff // tf),
        in_specs=[
            pl.BlockSpec((tm, d), lambda i, e, f: (i, 0), pipeline_mode=pl.Buffered(1)),
            pl.BlockSpec((1, d), lambda i, e, f: (0, 0)),
            pl.BlockSpec(router.shape, lambda i, e, f: (0, 0)),
            pl.BlockSpec((1, d, tf), lambda i, e, f: (e, 0, f)),
            pl.BlockSpec((1, d, tf), lambda i, e, f: (e, 0, f)),
            pl.BlockSpec((1, tf, d), lambda i, e, f: (e, f, 0)),
        ],
        out_specs=pl.BlockSpec((tm, d), lambda i, e, f: (i, 0), pipeline_mode=pl.Buffered(1)),
        out_shape=jax.ShapeDtypeStruct((n, d), F32),
        scratch_shapes=[pltpu.VMEM((tm, d), BF16), pltpu.VMEM((tm, d), F32), pltpu.VMEM((tm, LANES), F32),
                        pltpu.VMEM((tm, LANES), F32), pltpu.VMEM((LANES, tm), F32),
                        pltpu.VMEM((cap, d), BF16), pltpu.VMEM((cap, d), F32)],
        compiler_params=_cparams(("parallel", "arbitrary", "arbitrary")),
        name="moe_routed",
    )(x, g, router, w1, w3, w2)


def _ffn(x, g, router, w1, w3, w2, *, tm, tf, routed):
    n, d = x.shape
    ne, _, ff = w1.shape
    assert n % tm == 0 and ff % tf == 0
    kern = functools.partial(_ffn_kernel, routed=routed)
    return pl.pallas_call(
        kern,
        grid=(n // tm, ne, ff // tf),
        in_specs=[
            pl.BlockSpec((tm, d), lambda i, e, f: (i, 0)),
            pl.BlockSpec((1, d), lambda i, e, f: (0, 0)),
            pl.BlockSpec(router.shape, lambda i, e, f: (0, 0)),
            pl.BlockSpec((1, d, tf), lambda i, e, f: (e, 0, f)),
            pl.BlockSpec((1, d, tf), lambda i, e, f: (e, 0, f)),
            pl.BlockSpec((1, tf, d), lambda i, e, f: (e, f, 0)),
        ],
        out_specs=pl.BlockSpec((tm, d), lambda i, e, f: (i, 0)),
        out_shape=jax.ShapeDtypeStruct((n, d), F32),
        scratch_shapes=[pltpu.VMEM((tm, d), BF16), pltpu.VMEM((tm, d), F32), pltpu.VMEM((tm, LANES), F32)],
        compiler_params=_cparams(("parallel", "arbitrary", "arbitrary")),
        name="moe" if routed else "ffn",
    )(x, g, router, w1, w3, w2)


def _tile_rows(m, reps):
    return m if reps == 1 else jnp.concatenate([m] * reps, axis=0)


def _online_step(s, v, m, l, acc):
    m_new = jnp.maximum(m, jnp.max(s, axis=1, keepdims=True))
    p = jnp.exp(s - m_new)
    alpha = jnp.exp(m - m_new)
    return m_new, alpha * l + jnp.sum(p, axis=1, keepdims=True), alpha * acc + _dot(p.astype(BF16), v)


def _with_ones(v):
    return jnp.concatenate([v, jnp.ones_like(v)], axis=1)


def _online_step_ones(s, v1, m, acc):
    m_new = jnp.maximum(m, jnp.max(s, axis=1, keepdims=True))
    p = jnp.exp(s - m_new)
    return m_new, jnp.exp(m - m_new) * acc + _dot(p.astype(BF16), v1)


def _online_init_ones(rows):
    return (jnp.full((rows, 1), NEG, F32), jnp.zeros((rows, 2 * HD), F32))


def _online_init(rows, dv):
    return (jnp.full((rows, 1), NEG, F32), jnp.zeros((rows, 1), F32), jnp.zeros((rows, dv), F32))


def _unrolled_loop(lo, hi, body, init, unroll=KEY_UNROLL):
    shift = unroll.bit_length() - 1
    n_group = (hi - lo) >> shift

    def group(p, st):
        for u in range(unroll):
            st = body(lo + unroll * p + u, st)
        return st

    st = lax.fori_loop(0, n_group, group, init)
    return lax.fori_loop(lo + n_group * unroll, hi, body, st)


def _flash_kernel(*refs, reps, tq, tk, q_off, k_off, mode, has_bm, slab):
    refs = list(refs)
    q_ref = refs.pop(0)
    k_ref = refs.pop(0)
    v_ref = None if slab else refs.pop(0)
    bm_ref = refs.pop(0) if has_bm else None
    o_ref = refs.pop(0)
    i = pl.program_id(2)
    q = q_ref[0, 0, 0]
    dv = HD if slab else v_ref.shape[-1]
    n_kt = k_ref.shape[-2] // tk
    q_lo = q_off + i * tq

    def kv_tiles(k0):
        if not slab:
            return k_ref[0, 0, pl.ds(k0, tk), :], v_ref[0, 0, pl.ds(k0, tk), :]
        t = k_ref[0, pl.ds(k0, tk), :]
        first = pl.program_id(1) == 0
        return (jnp.where(first, t[:, :HD], t[:, HD:2 * HD]).astype(BF16),
                jnp.where(first, t[:, 2 * HD:3 * HD], t[:, 3 * HD:]).astype(BF16))

    if mode == "full":
        lo, hi = 0, n_kt
    else:
        hi = jnp.minimum(lax.div(q_lo + tq - 1 - k_off, tk) + 1, n_kt)
        lo = 0
        if mode == "window":
            if (tq + WINDOW - 2) // tk + 2 <= KEY_UNROLL:
                lo = jnp.maximum(hi - KEY_UNROLL, 0)
            else:
                lo = lax.div(jnp.maximum(q_lo - (WINDOW - 1) - k_off, 0), tk)
    qpos = q_lo + _iota((tq, tk), 0)
    if has_bm:
        bm = bm_ref[0, 0].astype(BF16)
        nbp = bm.shape[1]

    def body(j, state):
        k0 = pl.multiple_of(j * tk, tk)
        k, v = kv_tiles(k0)
        bias = None
        if mode != "full":
            kpos = k_off + j * tk + _iota((tq, tk), 1)
            ok = kpos <= qpos
            if mode == "window":
                ok = jnp.logical_and(ok, qpos - kpos < WINDOW)
                ok = jnp.logical_and(ok, kpos >= 0)
            bias = jnp.where(ok, 0.0, NEG)
        if has_bm:
            blk = _iota((nbp, tk), 0)
            tok = (j * tk + _iota((nbp, tk), 1)) >> SLC_SHIFT
            expand = jnp.where(blk == tok, 1.0, 0.0).astype(BF16)
            bias = jnp.where(_dot(bm, expand) > 0.5, bias, NEG)
        if bias is not None:
            bias = _tile_rows(bias, cr // tq)
        if slab:
            v = _with_ones(v)
        new = []
        for c in range(n_chain):
            s = _dot_nt(q[c * cr:(c + 1) * cr].astype(BF16), k)
            if bias is not None:
                s = s + bias
            new.append(_online_step_ones(s, v, *state[c]) if slab else _online_step(s, v, *state[c]))
        return tuple(new)

    n_chain = reps if tq >= MIN_CHAIN_ROWS else 1
    cr = reps * tq // n_chain
    init = tuple((_online_init_ones(cr) if slab else _online_init(cr, dv)) for _ in range(n_chain))
    state = _unrolled_loop(lo, hi, body, init)
    for c in range(n_chain):
        if slab:
            acc = state[c][1]
            o_ref[0, 0, 0, c * cr:(c + 1) * cr, :] = acc[:, :HD] / acc[:, HD:HD + 1]
        else:
            o_ref[0, 0, 0, c * cr:(c + 1) * cr, :] = state[c][2] / state[c][1]


def _flash(q, k, v=None, *, reps, tq, tk, q_off, k_off, mode, bm=None, kv_blk=None):
    b, hk, nq, rows, _ = q.shape
    lk = k.shape[-2]
    assert rows == reps * tq and lk % tk == 0
    if kv_blk is None:
        dv = v.shape[3]
        kv_specs = [pl.BlockSpec((1, 1, lk, HD), lambda b_, h, i: (b_, h, 0, 0)),
                    pl.BlockSpec((1, 1, lk, dv), lambda b_, h, i: (b_, h, 0, 0))]
        args = [q, k, v]
    else:
        assert hk == 2 and v is None
        dv = HD
        kv_specs = [pl.BlockSpec((1, lk, 2 * LANES), lambda b_, h, i: (b_, 0, kv_blk))]
        args = [q, k]
    in_specs = [pl.BlockSpec((1, 1, 1, rows, HD), lambda b_, h, i: (b_, h, i, 0, 0))] + kv_specs
    if bm is not None:
        in_specs.append(pl.BlockSpec((1, 1, tq, bm.shape[3]), lambda b_, h, i: (b_, h, i, 0)))
        args.append(bm)
    kern = functools.partial(_flash_kernel, reps=reps, tq=tq, tk=tk, q_off=q_off, k_off=k_off, mode=mode,
                             has_bm=bm is not None, slab=kv_blk is not None)
    return pl.pallas_call(
        kern,
        grid=(b, hk, nq),
        in_specs=in_specs,
        out_specs=pl.BlockSpec((1, 1, 1, rows, dv), lambda b_, h, i: (b_, h, i, 0, 0)),
        out_shape=jax.ShapeDtypeStruct((b, hk, nq, rows, dv), F32),
        compiler_params=_cparams(("parallel", "parallel", "arbitrary")),
        name="flash_" + mode + ("_blockmask" if bm is not None else ""),
    )(*args)


def _softplus(z):
    return jnp.maximum(z, 0.0) + jnp.log(1.0 + jnp.exp(-jnp.abs(z)))


def _later_matrix(n):
    return jnp.where(_iota((n, n), 0) > _iota((n, n), 1), 1.0, 0.0).astype(BF16)


def _sb_kernel(q_ref, k_ref, v_ref, o_ref, *, t):
    i = pl.program_id(2)
    th = t // ROW_SPLIT
    lane = _iota((th, LANES), 1)
    later = _later_matrix(t)
    qs = []
    for part in range(ROW_SPLIT):
        qf = q_ref[0, part * th:(part + 1) * th, :] * SCALE
        qs.append((jnp.where(lane < HD, qf, 0.0).astype(BF16), jnp.where(lane >= HD, qf, 0.0).astype(BF16)))

    def tile(j, carries, outs, masked):
        k0 = pl.multiple_of(j * t, t)
        k = k_ref[0, pl.ds(k0, t), :].astype(BF16)
        v = v_ref[0, pl.ds(k0, t), :].astype(BF16)
        new_carries, new_outs = [], []
        for part in range(ROW_SPLIT):
            heads = []
            for hd in range(2):
                carry = carries[2 * part + hd]
                z = _dot_nt(qs[part][hd], k)
                sp = _softplus(z)
                log_sig = z - sp
                if masked:
                    vis = _iota((th, t), 1) < part * th + _iota((th, t), 0)
                    sp = jnp.where(vis, sp, 0.0)
                a = jnp.exp(log_sig - _dot(sp.astype(BF16), later) - carry)
                if masked:
                    a = jnp.where(vis, a, 0.0)
                heads.append(_dot(a.astype(BF16), v))
                new_carries.append(carry + jnp.sum(sp, axis=1, keepdims=True))
            new_outs.append(outs[part] + jnp.where(lane < HD, heads[0], heads[1]))
        return tuple(new_carries), tuple(new_outs)

    def min_carry(carries):
        m = jnp.min(carries[0])
        for c in carries[1:]:
            m = jnp.minimum(m, jnp.min(c))
        return m

    zero = jnp.zeros((th, 1), F32)
    carries, outs = tile(i, (zero,) * (2 * ROW_SPLIT), (jnp.zeros((th, LANES), F32),) * ROW_SPLIT, True)

    def cond(st):
        return jnp.logical_and(st[0] < i, st[1] < SB_CUT)

    def body(st):
        carries, outs = tile(i - 1 - st[0], st[2], st[3], False)
        return st[0] + 1, min_carry(carries), carries, outs

    _, _, carries, outs = lax.while_loop(cond, body, (jnp.int32(0), min_carry(carries), carries, outs))
    for part in range(ROW_SPLIT):
        o_ref[0, part * th:(part + 1) * th, :] = outs[part]


def _sb(proj3, *, t, q_slab, k_slab, v_slab, n_slabs):
    b, tt, _ = proj3.shape
    assert tt % t == 0
    kern = functools.partial(_sb_kernel, t=t)
    return pl.pallas_call(
        kern,
        grid=(b, n_slabs, tt // t),
        in_specs=[
            pl.BlockSpec((1, t, LANES), lambda b_, p, i: (b_, i, q_slab + p)),
            pl.BlockSpec((1, tt, LANES), lambda b_, p, i: (b_, 0, k_slab + p)),
            pl.BlockSpec((1, tt, LANES), lambda b_, p, i: (b_, 0, v_slab + p)),
        ],
        out_specs=pl.BlockSpec((1, t, LANES), lambda b_, p, i: (b_, i, p)),
        out_shape=jax.ShapeDtypeStruct((b, tt, n_slabs * LANES), F32),
        compiler_params=_cparams(("parallel", "parallel", "arbitrary")),
        name="sb_prompt",
    )(proj3, proj3, proj3)


def _lam(lv, lam_init):
    a = jnp.sum(jnp.sum(lv[0:1] * lv[1:2], axis=1, keepdims=True), axis=0, keepdims=True)
    b = jnp.sum(jnp.sum(lv[2:3] * lv[3:4], axis=1, keepdims=True), axis=0, keepdims=True)
    return jnp.exp(a) - jnp.exp(b) + lam_init


def _df_kernel(lv_ref, g_ref, q_ref, k_ref, v_ref, o_ref, *, t, lam_init):
    i = pl.program_id(2)
    th = t // ROW_SPLIT
    lane = _iota((th, LANES), 1)
    qs = []
    for part in range(ROW_SPLIT):
        qf = q_ref[0, part * th:(part + 1) * th, :] * SCALE
        qs.append((jnp.where(lane < HD, qf, 0.0).astype(BF16), jnp.where(lane >= HD, qf, 0.0).astype(BF16)))

    def tile(j, state, masked):
        k0 = pl.multiple_of(j * t, t)
        k = k_ref[0, pl.ds(k0, t), :].astype(BF16)
        v = v_ref[0, pl.ds(k0, t), :].astype(BF16)
        new = []
        for part in range(ROW_SPLIT):
            for mp in range(2):
                s = _dot_nt(qs[part][mp], k)
                if masked:
                    s = jnp.where(_iota((th, t), 1) <= part * th + _iota((th, t), 0), s, NEG)
                new.append(_online_step(s, v, *state[2 * part + mp]))
        return tuple(new)

    init = tuple(_online_init(th, LANES) for _ in range(2 * ROW_SPLIT))
    state = _unrolled_loop(0, i, lambda j, st: tile(j, st, False), init)
    state = tile(i, state, True)
    lam = _lam(lv_ref[...], lam_init)
    for part in range(ROW_SPLIT):
        (_, l0, a0), (_, l1, a1) = state[2 * part], state[2 * part + 1]
        d = a0 / l0 - lam * (a1 / l1)
        o_ref[0, part * th:(part + 1) * th, :] = _rms_rows(d, g_ref[...]) * (1.0 - lam_init)


def _df(proj3, lv, subln, *, t, q_slab, k_slab, v_slab, n_slabs, lam_init):
    b, tt, _ = proj3.shape
    assert tt % t == 0
    kern = functools.partial(_df_kernel, t=t, lam_init=lam_init)
    return pl.pallas_call(
        kern,
        grid=(b, n_slabs, tt // t),
        in_specs=[
            pl.BlockSpec(lv.shape, lambda b_, p, i: (0, 0)),
            pl.BlockSpec(subln.shape, lambda b_, p, i: (0, 0)),
            pl.BlockSpec((1, t, LANES), lambda b_, p, i: (b_, i, q_slab + p)),
            pl.BlockSpec((1, tt, LANES), lambda b_, p, i: (b_, 0, k_slab + p)),
            pl.BlockSpec((1, tt, LANES), lambda b_, p, i: (b_, 0, v_slab + p)),
        ],
        out_specs=pl.BlockSpec((1, t, LANES), lambda b_, p, i: (b_, i, p)),
        out_shape=jax.ShapeDtypeStruct((b, tt, n_slabs * LANES), F32),
        compiler_params=_cparams(("parallel", "parallel", "arbitrary")),
        name="df_prompt",
    )(lv, subln, proj3, proj3, proj3)


def _dec_even_kernel(*refs, pp, ts, lam_init):
    tbl_ref, qt_ref = refs[0], refs[1]
    k_refs = refs[2:2 + pp]
    v_refs = refs[2 + pp:2 + 2 * pp]
    kn_ref, vn_ref, lv_ref, g_ref, o_ref, later_ref, st_ref, asb_ref, adf_ref = refs[2 + 2 * pp:]
    del tbl_ref
    s_id = pl.program_id(1)
    nk = pp * PAGE
    half = LANES // 2
    qt = qt_ref[0]

    def col_of(row):
        return jnp.transpose(jnp.broadcast_to(row, (SUBLANES, LANES)))[half:, 0:1]

    def tile(kt, vt, later, vis_sb, vis_df):
        carry, m, l = st_ref[0:1, :], st_ref[1:2, :], st_ref[2:3, :]
        zt = _dot(kt, qt)
        lane = _iota(zt.shape, 1)
        sp = _softplus(zt)
        log_sig = zt - sp
        s = zt
        if vis_sb is not None:
            sp = jnp.where(vis_sb, sp, 0.0)
            s = jnp.where(vis_df, s, NEG)
        a = jnp.exp(log_sig - _dot(later, sp.astype(BF16)) - carry)
        if vis_sb is not None:
            a = jnp.where(vis_sb, a, 0.0)
        m_new = jnp.maximum(m, jnp.max(s, axis=0, keepdims=True))
        p = jnp.exp(s - m_new)
        alpha = jnp.exp(m - m_new)
        st_ref[0:1, :] = carry + jnp.sum(sp, axis=0, keepdims=True)
        st_ref[1:2, :] = m_new
        st_ref[2:3, :] = alpha * l + jnp.sum(p, axis=0, keepdims=True)
        w = jnp.transpose(jnp.where(lane < half, a, p)).astype(BF16)
        asb_ref[...] += _dot(w[:half], vt[:, :SB_W])
        adf_ref[...] = col_of(alpha) * adf_ref[...] + _dot(w[half:], vt[:, SB_W:])

    @pl.when(s_id == 0)
    def _():
        later_ref[...] = jnp.where(_iota((nk, nk), 1) > _iota((nk, nk), 0), 1.0, 0.0).astype(BF16)
        st_ref[...] = jnp.where(_iota(st_ref.shape, 0) == 1, NEG, 0.0)
        asb_ref[...] = jnp.zeros_like(asb_ref)
        adf_ref[...] = jnp.zeros_like(adf_ref)
        key = _iota((PAGE, LANES), 0)
        tok = _iota((PAGE, LANES), 1) & (ts - 1)
        tile(kn_ref[0].astype(BF16), vn_ref[0].astype(BF16), later_ref[0:PAGE, 0:PAGE], key < tok, key <= tok)

    kt = jnp.concatenate([r[0].astype(BF16) for r in k_refs], axis=0)
    vt = jnp.concatenate([r[0].astype(BF16) for r in v_refs], axis=0)
    tile(kt, vt, later_ref[...], None, None)

    @pl.when(s_id == pl.num_programs(1) - 1)
    def _():
        row = _iota((half, SB_W), 0)
        lane = _iota((half, SB_W), 1)

        def fold(x):
            out = x[0:ts]
            for u in range(1, half // ts):
                out = out + x[u * ts:(u + 1) * ts]
            return out

        o_sb = fold(jnp.where((row >> 3) == (lane >> HD_SHIFT), asb_ref[...], 0.0))
        pn = adf_ref[...] / col_of(st_ref[2:3, :])
        same_head = (row >> 4) == (lane >> 7)
        o0 = fold(jnp.where(jnp.logical_and(same_head, ((row >> 3) & 1) == 0), pn, 0.0))
        o1 = fold(jnp.where(jnp.logical_and(same_head, ((row >> 3) & 1) == 1), pn, 0.0))
        d = o0 - _lam(lv_ref[...], lam_init) * o1
        parts = [o_sb]
        for h in range(N_DF):
            parts.append(_rms_rows(d[:, h * LANES:(h + 1) * LANES], g_ref[...]) * (1.0 - lam_init))
        o_ref[0] = jnp.concatenate(parts, axis=1)


def _dec_even(qt, pool_k, pool_v, table, k_new, v_new, lv, subln, *, pp, ts, lam_init):
    b, n_pages = table.shape
    width = pool_k.shape[2]
    assert n_pages % pp == 0 and ts == SUBLANES and k_new.shape[1] == PAGE
    n_steps = n_pages // pp

    def page_map(u):
        return lambda b_, s, t: (t[b_, n_pages - (s + 1) * pp + u], 0, 0)

    page_specs = [pl.BlockSpec((1, PAGE, width), page_map(u)) for u in range(pp)]
    new_spec = pl.BlockSpec((1, PAGE, width), lambda b_, s, t: (b_, 0, 0))
    grid_spec = pltpu.PrefetchScalarGridSpec(
        num_scalar_prefetch=1,
        grid=(b, n_steps),
        in_specs=[pl.BlockSpec((1,) + qt.shape[1:], lambda b_, s, t: (b_, 0, 0))] + page_specs + page_specs
        + [new_spec, new_spec, pl.BlockSpec(lv.shape, lambda b_, s, t: (0, 0)),
           pl.BlockSpec(subln.shape, lambda b_, s, t: (0, 0))],
        out_specs=pl.BlockSpec((1, ts, width), lambda b_, s, t: (b_, 0, 0)),
        scratch_shapes=[
            pltpu.VMEM((pp * PAGE, pp * PAGE), BF16),
            pltpu.VMEM((SUBLANES, LANES), F32),
            pltpu.VMEM((LANES // 2, SB_W), F32),
            pltpu.VMEM((LANES // 2, DF_W), F32),
        ],
    )
    kern = functools.partial(_dec_even_kernel, pp=pp, ts=ts, lam_init=lam_init)
    return pl.pallas_call(
        kern,
        grid_spec=grid_spec,
        out_shape=jax.ShapeDtypeStruct((b, ts, width), F32),
        compiler_params=_cparams(("parallel", "arbitrary")),
        name="dec_even",
    )(table, qt, *([pool_k] * pp), *([pool_v] * pp), k_new, v_new, lv, subln)


def _compress_kernel(t_ref, w1_ref, w1g_ref, pe_ref, w2_ref, g_ref, o_ref):
    kind = pl.program_id(1)
    n = t_ref.shape[1] // CMP_STRIDE
    a1 = [None] * N_NSA_KV
    a2 = [None] * N_NSA_KV
    for p in range(CMP_STRIDE):
        rows = t_ref[0, pl.ds(p, n, stride=CMP_STRIDE), :].astype(BF16)
        for g in range(N_NSA_KV):
            d1 = _dot(rows, w1g_ref[0, g, p])
            d2 = _dot(rows, w1g_ref[0, g, CMP_STRIDE + p])
            a1[g] = d1 if a1[g] is None else a1[g] + d1
            a2[g] = d2 if a2[g] is None else a2[g] + d2
    pe = _dot(jnp.broadcast_to(pe_ref[0], (SUBLANES, CMP_LEN * HD)).astype(BF16), w1_ref[0])[0:1]
    for g in range(N_NSA_KV):
        nxt = jnp.where(_iota(a2[g].shape, 0) < n - 1, pltpu.roll(a2[g], n - 1, 0), 0.0)
        hid = a1[g] + nxt + pe
        hid = hid * (1.0 / (1.0 + jnp.exp(-hid)))
        out = _dot(hid.astype(BF16), w2_ref[0])
        normed = _rms_rows(out, g_ref[...])
        o_ref[0, 0, g] = jnp.where(kind == 0, normed, out)


def _compress(t, slab0, w1, pe_flat, w2, gain):
    b, lk, _ = t.shape
    g = N_NSA_KV
    n = lk // CMP_STRIDE
    pieces = w1.reshape(2, CMP_LEN, HD, w1.shape[2])
    w1g = jnp.stack([jnp.pad(pieces, ((0, 0), (0, 0), (HD * gi, HD * (g - 1 - gi)), (0, 0))) for gi in range(g)],
                    axis=1)
    return pl.pallas_call(
        _compress_kernel,
        grid=(b, 2),
        in_specs=[
            pl.BlockSpec((1, lk, LANES), lambda b_, k: (b_, 0, slab0 + k)),
            pl.BlockSpec((1,) + w1.shape[1:], lambda b_, k: (k, 0, 0)),
            pl.BlockSpec((1,) + w1g.shape[1:], lambda b_, k: (k, 0, 0, 0, 0)),
            pl.BlockSpec((1,) + pe_flat.shape[1:], lambda b_, k: (k, 0, 0)),
            pl.BlockSpec((1,) + w2.shape[1:], lambda b_, k: (k, 0, 0)),
            pl.BlockSpec(gain.shape, lambda b_, k: (0, 0)),
        ],
        out_specs=pl.BlockSpec((1, 1, g, n, HD), lambda b_, k: (b_, k, 0, 0, 0)),
        out_shape=jax.ShapeDtypeStruct((b, 2, g, n, HD), F32),
        compiler_params=_cparams(("parallel", "arbitrary")),
        name="compress",
    )(t, w1, w1g, pe_flat, w2, gain)


def _nsa_cmp_kernel(q_ref, kc_ref, vc_ref, o_ref, sel_ref, *, tq, q_off, n_sel):
    i = pl.program_id(2)
    q = q_ref[0, 0, 0].astype(BF16)
    kc = kc_ref[0, 0]
    vc = vc_ref[0, 0]
    ncol = kc.shape[0]
    nbp = ncol // CMP_PER_SLC
    q_lo = q_off + i * tq
    qpos = q_lo + _iota((tq, ncol), 0)
    col = _iota((tq, ncol), 1)
    jj = jnp.zeros_like(col)
    for u in range(1, CMP_PER_SLC):
        jj = jj + jnp.where(col >= u * nbp, 1, 0)
    c_end = (col - jj * nbp) * SLC_BLOCK + jj * CMP_STRIDE + (CMP_LEN - 1)
    maskf = _tile_rows(jnp.where(c_end <= qpos, 1.0, 0.0), NSA_GROUP)
    keep = maskf > 0.5
    s = jnp.where(keep, _dot_nt(q, kc), NEG)
    m = jnp.max(s, axis=1, keepdims=True)
    p = jnp.where(keep, jnp.exp(s - m), 0.0)
    p = p / jnp.maximum(jnp.sum(p, axis=1, keepdims=True), 1e-30)
    o_ref[0, 0, 0] = _dot(p.astype(BF16), vc)
    pg = p[0:tq]
    for r in range(1, NSA_GROUP):
        pg = pg + p[r * tq:(r + 1) * tq]
    imp = pg[:, 0:nbp]
    for j in range(1, CMP_PER_SLC):
        imp = imp + pg[:, j * nbp:(j + 1) * nbp]
    blk = _iota((tq, nbp), 1)
    qp = q_lo + _iota((tq, nbp), 0)
    q_blk = qp >> SLC_SHIFT
    visible = blk * SLC_BLOCK <= qp
    forced = jnp.logical_or(blk == 0, jnp.logical_and(blk <= q_blk, blk > q_blk - N_LOCAL))
    score = jnp.where(visible, jnp.where(forced, FORCE, imp), NEG)
    sel0 = jnp.zeros((tq, nbp), F32)
    rounds = n_sel
    if n_sel > N_LOCAL + 1:
        pre = jnp.logical_and(forced, visible)
        sel0 = jnp.where(pre, 1.0, 0.0)
        score = jnp.where(pre, -jnp.inf, score)
        rounds = n_sel - (N_LOCAL + 1)
    rc = min(tq, SEL_CHUNK_ROWS)
    blkf = _iota((rc, nbp), 1).astype(F32)
    scores = [score[c * rc:(c + 1) * rc] for c in range(tq // rc)]
    sels = [sel0[c * rc:(c + 1) * rc] for c in range(tq // rc)]
    for _ in range(rounds):
        for c in range(len(scores)):
            top = jnp.max(scores[c], axis=1, keepdims=True)
            idx = jnp.min(jnp.where(scores[c] == top, blkf, 1e9), axis=1, keepdims=True)
            pick = blkf == idx
            sels[c] = jnp.where(pick, 1.0, sels[c])
            scores[c] = jnp.where(pick, -jnp.inf, scores[c])
    for c in range(len(scores)):
        sel_ref[0, 0, c * rc:(c + 1) * rc, :] = sels[c]


def _nsa_cmp(q, kc, vc, *, tq, q_off, n_sel):
    b, g, nq, rows, _ = q.shape
    ncol = kc.shape[2]
    nbp = ncol // CMP_PER_SLC
    kern = functools.partial(_nsa_cmp_kernel, tq=tq, q_off=q_off, n_sel=n_sel)
    return pl.pallas_call(
        kern,
        grid=(b, g, nq),
        in_specs=[
            pl.BlockSpec((1, 1, 1, rows, HD), lambda b_, g_, i: (b_, g_, i, 0, 0)),
            pl.BlockSpec((1, 1, ncol, HD), lambda b_, g_, i: (b_, g_, 0, 0)),
            pl.BlockSpec((1, 1, ncol, HD), lambda b_, g_, i: (b_, g_, 0, 0)),
        ],
        out_specs=[
            pl.BlockSpec((1, 1, 1, rows, HD), lambda b_, g_, i: (b_, g_, i, 0, 0)),
            pl.BlockSpec((1, 1, tq, nbp), lambda b_, g_, i: (b_, g_, i, 0)),
        ],
        out_shape=[
            jax.ShapeDtypeStruct((b, g, nq, rows, HD), F32),
            jax.ShapeDtypeStruct((b, g, nq * tq, nbp), F32),
        ],
        compiler_params=_cparams(("parallel", "parallel", "arbitrary")),
        name="nsa_cmp",
    )(q, kc, vc)


def _dsa_kernel(qi_ref, wi_ref, qd_ref, kk_ref, o_ref, key_ref, *, tq, tk, q_off, n_top, offs):
    i = pl.program_id(1)
    qi = qi_ref[0, 0]
    qd = qd_ref[0, 0]
    w = wi_ref[0] * (N_IDX ** -0.5)
    n_kt = kk_ref.shape[1] // tk
    kd_off, ki_off, vd_off = offs

    def cols(ref, k0, off):
        return ref[0, pl.ds(k0, tk), :][:, off:off + HD].astype(BF16)

    q_lo = q_off + i * tq
    hi = jnp.minimum(lax.div(q_lo + tq - 1, tk) + 1, n_kt)
    qpos = q_lo + _iota((tq, tk), 0)
    n_chain = N_DSA if tq >= MIN_CHAIN_ROWS else 1
    cr = N_DSA * tq // n_chain

    def visible(j):
        return (j * tk + _iota((tq, tk), 1)) <= qpos

    def score_tile(j, _):
        k0 = pl.multiple_of(j * tk, tk)
        k = cols(kk_ref, k0, ki_off)
        tot = None
        for c in range(n_chain):
            sc = jnp.maximum(_dot_nt(qi[c * cr:(c + 1) * cr].astype(BF16), k), 0.0)
            for u in range(cr // tq):
                h = c * (cr // tq) + u
                term = w[:, h:h + 1] * sc[u * tq:(u + 1) * tq]
                tot = term if tot is None else tot + term
        tot = jnp.where(visible(j), tot, NEG)
        bits = pltpu.bitcast(tot, jnp.int32)
        key = jnp.where(bits < 0, bits ^ jnp.int32(0x7FFFFFFF), bits)
        key_ref[:, pl.ds(k0, tk)] = jnp.where(tot == 0.0, 0, key)
        return 0

    _unrolled_loop(0, hi, score_tile, 0)

    def count_ge(c):
        def body(j, acc):
            blk = key_ref[:, pl.ds(pl.multiple_of(j * tk, tk), tk)]
            hit = jnp.where(blk >= c, 1.0, 0.0)
            part = hit[:, 0:LANES]
            for u in range(1, tk // LANES):
                part = part + hit[:, u * LANES:(u + 1) * LANES]
            return acc + part
        acc = _unrolled_loop(0, hi, body, jnp.zeros((tq, LANES), F32))
        return jnp.sum(acc, axis=1, keepdims=True)

    kf = float(n_top)
    tau = jnp.where(count_ge(jnp.zeros((tq, 1), jnp.int32)) >= kf, 0, INT_MIN).astype(jnp.int32)

    def bit_body(t, tau):
        cand = tau + jnp.left_shift(jnp.int32(1), 30 - t)
        return jnp.where(count_ge(cand) >= kf, cand, tau)

    tau = lax.fori_loop(0, 31, bit_body, tau)
    need = kf - count_ge(tau + 1)
    before = jnp.where(_iota((tk, tk), 0) < _iota((tk, tk), 1), 1.0, 0.0).astype(BF16)

    def attend(j, carry):
        state, n_eq = carry
        k0 = pl.multiple_of(j * tk, tk)
        key = key_ref[:, pl.ds(k0, tk)]
        eqf = jnp.where(key == tau, 1.0, 0.0)
        rank = n_eq + _dot(eqf.astype(BF16), before)
        kept = jnp.logical_or(key > tau, jnp.logical_and(key == tau, rank < need))
        bias = jnp.where(jnp.logical_and(kept, visible(j)), 0.0, NEG)
        k = cols(kk_ref, k0, kd_off)
        v1 = _with_ones(cols(kk_ref, k0, vd_off))
        bias = _tile_rows(bias, cr // tq)
        new = []
        for c in range(n_chain):
            s = _dot_nt(qd[c * cr:(c + 1) * cr].astype(BF16), k) + bias
            new.append(_online_step_ones(s, v1, *state[c]))
        return tuple(new), n_eq + jnp.sum(eqf, axis=1, keepdims=True)

    init = (tuple(_online_init_ones(cr) for _ in range(n_chain)), jnp.zeros((tq, 1), F32))
    state, _ = _unrolled_loop(0, hi, attend, init)
    for c in range(n_chain):
        acc = state[c][1]
        o_ref[0, 0, c * cr:(c + 1) * cr, :] = acc[:, :HD] / acc[:, HD:HD + 1]


def _dsa(qi, wi, qd, kk, *, width, blk, offs, tq, tk, q_off, n_top):
    b, nq, rows, _ = qi.shape
    lk = kk.shape[1]
    assert lk % tk == 0
    kern = functools.partial(_dsa_kernel, tq=tq, tk=tk, q_off=q_off, n_top=n_top, offs=offs)
    qspec = pl.BlockSpec((1, 1, rows, HD), lambda b_, i: (b_, i, 0, 0))
    return pl.pallas_call(
        kern,
        grid=(b, nq),
        in_specs=[qspec, pl.BlockSpec((1, tq, N_IDX), lambda b_, i: (b_, i, 0)), qspec,
                  pl.BlockSpec((1, lk, width), lambda b_, i: (b_, 0, blk))],
        out_specs=pl.BlockSpec((1, 1, rows, HD), lambda b_, i: (b_, i, 0, 0)),
        out_shape=jax.ShapeDtypeStruct((b, nq, rows, HD), F32),
        scratch_shapes=[pltpu.VMEM((tq, lk), jnp.int32)],
        compiler_params=_cparams(("parallel", "arbitrary")),
        name="dsa",
    )(qi, wi, qd, kk)


def _gather_kernel(*refs, pp, n_steps):
    pool_refs = refs[1:1 + pp]
    new_ref, o_ref = refs[1 + pp], refs[2 + pp]
    s = pl.program_id(1)

    def page(ref):
        mid = ref.shape[2:-1]
        if not mid:
            return ref[0]
        pieces = [ref[(0, slice(None)) + idx + (slice(None),)] for idx in np.ndindex(*mid)]
        return jnp.concatenate(pieces, axis=1)

    @pl.when(s < n_steps)
    def _():
        for u in range(pp):
            o_ref[0, u * PAGE:(u + 1) * PAGE, :] = page(pool_refs[u])

    @pl.when(s >= n_steps)
    def _():
        o_ref[...] = new_ref[...]


def _page_gather(pool, table, new, *, pp):
    b, n_pages = table.shape
    tail = pool.shape[2:]
    width = int(np.prod(tail))
    assert n_pages % pp == 0 and new.shape[1:] == (pp * PAGE, width)
    n_steps = n_pages // pp

    def page_map(u):
        return lambda b_, s, t: (t[b_, jnp.minimum(s, n_steps - 1) * pp + u], 0) + (0,) * len(tail)

    kern = functools.partial(_gather_kernel, pp=pp, n_steps=n_steps)
    grid_spec = pltpu.PrefetchScalarGridSpec(
        num_scalar_prefetch=1,
        grid=(b, n_steps + 1),
        in_specs=[pl.BlockSpec((1, PAGE) + tail, page_map(u)) for u in range(pp)]
        + [pl.BlockSpec((1, pp * PAGE, width), lambda b_, s, t: (b_, 0, 0))],
        out_specs=pl.BlockSpec((1, pp * PAGE, width), lambda b_, s, t: (b_, s, 0)),
    )
    return pl.pallas_call(
        kern,
        grid_spec=grid_spec,
        out_shape=jax.ShapeDtypeStruct((b, (n_pages + pp) * PAGE, width), pool.dtype),
        compiler_params=_cparams(("parallel", "arbitrary")),
        name="page_gather",
    )(table, *([pool] * pp), new)


def _rope_tables(pos):
    half = HD // 2
    inv = ROPE_THETA ** (-jnp.arange(half, dtype=F32) / half)
    ang = pos.astype(F32)[:, None] * inv[None, :]
    cos, sin = jnp.cos(ang), jnp.sin(ang)
    cos128 = jnp.tile(jnp.concatenate([cos, cos], axis=1), (1, LANES // HD))
    sin128 = jnp.tile(jnp.concatenate([-sin, sin], axis=1), (1, LANES // HD))
    return cos128, sin128


def _tile_gain(g):
    return jnp.tile(g.reshape(1, HD), (1, LANES // HD))


def _heads(a, b, t, h, d, scale=None):
    a = a.reshape(b, t, h, d)
    if scale is not None:
        a = a * scale
    return a.transpose(0, 2, 1, 3)


def _stack_q(a, tq):
    b, hk, r, t, d = a.shape
    return a.reshape(b, hk, r, t // tq, tq, d).transpose(0, 1, 3, 2, 4, 5).reshape(b, hk, t // tq, r * tq, d)


def _unstack_q(a, r, tq):
    b, hk, nq, _, d = a.shape
    a = a.reshape(b, hk, nq, r, tq, d).transpose(0, 2, 4, 1, 3, 5)
    return a.reshape(b * nq * tq, hk * r * d)


def _pad_rows(a, rows):
    return jnp.pad(a, ((0, 0), (0, rows - a.shape[1])) + ((0, 0),) * (a.ndim - 2))


def _even_mixer(x2, b, t, q_off, cs, past, prm, cfg):
    n = b * t
    program = [
        (0, SB_W, None, None, [(0, False)]),
        (3 * SB_W, DF_W, 0, None, [(SB_W, True)]),
        (SB_W, SB_W, None, None, [(SB_W + DF_W, False)]),
        (3 * SB_W + DF_W, DF_W, 1, None, [(2 * SB_W + DF_W, True)]),
        (2 * SB_W, SB_W, None, None, [(2 * (SB_W + DF_W), False)]),
        (3 * SB_W + 2 * DF_W, DF_W, None, None, [(3 * SB_W + 2 * DF_W, False)]),
    ]
    gains = jnp.concatenate([_tile_gain(prm["df_qk_gain"][0]), _tile_gain(prm["df_qk_gain"][1])], axis=0)
    proj = _mm([x2, cs[0], cs[1]], [prm["g0"], gains], prm["ev_w_in"], _lhs_norm, program, 3 * (SB_W + DF_W),
               tm=cfg["tm"], rope_idx=(1, 2), gains_idx=1)
    mw = SB_W + DF_W
    new_k = proj[:, mw:2 * mw].reshape(b, t, mw)
    new_v = proj[:, 2 * mw:3 * mw].reshape(b, t, mw)
    lam_init = 0.8 - 0.6 * math.exp(-0.3 * prm["layer"])
    lv, subln = prm["df_lambda"], prm["df_subln_gain"].reshape(1, 2 * HD)
    d = x2.shape[1]
    if past is None:
        proj3 = proj.reshape(b, t, 3 * mw)
        ns = SB_W // LANES
        o_sb = _sb(proj3, t=cfg["t_even"], q_slab=0, k_slab=2 * ns, v_slab=4 * ns, n_slabs=ns)
        o_df = _df(proj3, lv, subln, t=cfg["t_even"], q_slab=ns, k_slab=3 * ns, v_slab=5 * ns, n_slabs=ns,
                   lam_init=lam_init)
        return _mm([o_sb.reshape(n, SB_W), o_df.reshape(n, DF_W), x2], [], prm["ev_w_out"], _lhs_cat2,
                   [(0, d, None, None, [(0, False)])], d, tm=cfg["tm"], res_idx=2), new_k, new_v
    pool_k, pool_v, table = past
    qcat = proj[:, :mw].reshape(b, t, mw).transpose(0, 2, 1) * SCALE
    qt = (jnp.tile(qcat, (1, 1, LANES // t)) * _dec_even_mask(t)).astype(BF16)
    o = _dec_even(qt, pool_k, pool_v, table, _pad_rows(new_k, PAGE), _pad_rows(new_v, PAGE), lv, subln,
                  pp=cfg["pp"], ts=t, lam_init=lam_init)
    return _mm([o.reshape(n, mw), x2], [], prm["ev_w_out"], _lhs_plain,
               [(0, d, None, None, [(0, False)])], d, tm=cfg["tm"], res_idx=1), new_k, new_v


def _dec_even_mask(ts):
    f = np.arange(SB_W + DF_W)[:, None]
    c = np.arange(LANES)[None, :]
    half = LANES // 2
    sb = (f < SB_W) & (c < half) & (f // HD == c // ts)
    df = (f >= SB_W) & (c >= half) & ((f - SB_W) // HD == (c - half) // ts)
    return jnp.asarray((sb | df).astype(np.float32))


def _odd_layout():
    widths = (N_NSA * HD, 128, 128, 128, 128, 128, 128, N_NSA * 3, N_DSA * HD, HD, HD, N_IDX * HD, HD, N_IDX)
    offs = np.concatenate([[0], np.cumsum(widths)])
    (q_n, k_c, v_c, k_s, v_s, k_w, v_w, gate, q_d, k_d, v_d, q_i, k_i, w_i) = [
        (int(offs[j]), int(offs[j + 1])) for j in range(len(widths))]
    pieces = [q_n, (k_c[0], v_w[1]), q_d, q_i, k_d, k_i, v_d, w_i, ("pad", HD - N_IDX), gate,
              ("pad", LANES - N_NSA * 3)]
    program = [
        (0, 512, 0, None, [(0, False), (512, True)]),
        (512, 128, None, None, [(1024, False)]),
        (640, 128, None, None, [(1152, False)]),
        (768, 128, 1, None, [(1280, True)]),
        (896, 128, None, None, [(1408, False)]),
        (1024, 128, 2, None, [(1536, True)]),
        (1152, 128, None, None, [(1664, False)]),
        (1280, 512, 3, None, [(1792, True)]),
        (1792, 512, None, None, [(2304, True)]),
        (2304, 128, 4, None, [(2816, True)]),
        (2432, 128, None, None, [(2944, False)]),
        (2560, 128, None, "sigmoid", [(3072, False)]),
    ]
    return pieces, program, 3200


def _permute_cols(w, pieces):
    cols = []
    for p in pieces:
        if p[0] == "pad":
            cols.append(jnp.zeros((w.shape[0], p[1]), w.dtype))
        else:
            cols.append(w[:, p[0]:p[1]])
    return jnp.concatenate(cols, axis=1)


def _odd_mixer(x2, b, t, t_real, q_off, cs, past, prm, cfg):
    n = b * t
    pieces, program, out_cols = _odd_layout()
    ng, dg = prm["nsa_qk_gain"], prm["dsa_qk_gain"]
    gains = jnp.concatenate([
        _tile_gain(ng[0]), _tile_gain(ng[2]), _tile_gain(ng[3]), _tile_gain(dg[0]),
        jnp.concatenate([dg[1], dg[2]]).reshape(1, LANES)], axis=0)
    w_in = _permute_cols(prm["od_w_in"], pieces)
    proj = _mm([x2, cs[0], cs[1]], [prm["g0"], gains], w_in, _lhs_norm, program, out_cols,
               tm=cfg["tm"], rope_idx=(1, 2), gains_idx=1)
    g = N_NSA_KV
    new_nsa = proj[:, 1024:1536].reshape(b, t, 4 * g * HD)
    new_win = proj[:, 1536:1792].reshape(b, t, 2, g, HD)
    new_dsa = jnp.concatenate([proj[:, 2816:2880], proj[:, 2944:3008], proj[:, 2880:2944]], axis=1).reshape(b, t, 3 * HD)
    w_i = proj[:, 3008:3008 + N_IDX].reshape(b, t, N_IDX)
    gate = proj[:, 3072:3072 + N_NSA * 3].reshape(n, N_NSA, 3)
    if past is None:
        proj3 = proj.reshape(b, t, out_cols)
        nsa_src, win_src = (proj3, 8), (proj3, 12)
        dsa_src = dict(kk=proj3, width=2 * LANES, blk=11, offs=(0, HD, 2 * HD))
        win_off = 0
        new_state = new_win[:, -min(WINDOW, t):]
        l_real = t
    else:
        pool_nsa, pool_dsa, state, table = past
        nsa_buf = _page_gather(pool_nsa, table, _pad_rows(new_nsa, cfg["pp"] * PAGE), pp=cfg["pp"])
        dsa_buf = _page_gather(pool_dsa, table, _pad_rows(new_dsa, cfg["pp"] * PAGE), pp=cfg["pp"])
        wb = state.shape[1]
        win_buf = _pad_rows(jnp.concatenate([state, new_win], axis=1), wb + cfg["tk_win"])
        nsa_src, win_src = (nsa_buf, 0), (win_buf.reshape(b, wb + cfg["tk_win"], 2 * g * HD), 0)
        dsa_src = dict(kk=dsa_buf, width=3 * HD, blk=0, offs=(0, 2 * HD, HD))
        win_off = q_off - wb
        new_state = jnp.concatenate([state, new_win[:, :t_real]], axis=1)[:, -wb:]
        l_real = table.shape[1] * PAGE + t_real
    lk = nsa_src[0].shape[1]
    tq = cfg["tq"]

    n_cmp = lk // CMP_STRIDE
    pe_flat = prm["cmp_pe"].reshape(2, 1, CMP_LEN * HD)
    cmp = _compress(nsa_src[0], nsa_src[1], prm["cmp_w1"], pe_flat, prm["cmp_w2"], ng[1].reshape(1, HD))
    nb = n_cmp // CMP_PER_SLC
    nbp = -(-nb // LANES) * LANES
    cmp = cmp.reshape(b, 2, g, nb, CMP_PER_SLC, HD).transpose(0, 1, 2, 4, 3, 5)
    cmp = jnp.pad(cmp, ((0, 0),) * 4 + ((0, nbp - nb), (0, 0))).reshape(b, 2, g, CMP_PER_SLC * nbp, HD).astype(BF16)

    def group_q(cols):
        a = _heads(cols, b, t, N_NSA, HD, SCALE).reshape(b, g, NSA_GROUP, t, HD)
        return _stack_q(a, tq)

    q_n = group_q(proj[:, 0:512])
    q_r = group_q(proj[:, 512:1024])
    n_blk = -(-l_real // SLC_BLOCK)
    o_c, sel = _nsa_cmp(q_n, cmp[:, 0], cmp[:, 1], tq=tq, q_off=q_off, n_sel=min(N_SLC, n_blk))

    o_s = _flash(q_r, nsa_src[0], kv_blk=(nsa_src[1] + 2) // 2, reps=NSA_GROUP, tq=tq,
                 tk=cfg["tk"], q_off=q_off, k_off=0, mode="causal", bm=sel)
    o_w = _flash(q_r, win_src[0], kv_blk=win_src[1] // 2, reps=NSA_GROUP, tq=tq,
                 tk=cfg["tk_win"], q_off=q_off, k_off=win_off, mode="window")

    q_d = _stack_q(_heads(proj[:, 1792:2304], b, t, N_DSA, HD, SCALE)[:, None], tq)[:, 0]
    q_i = _stack_q(_heads(proj[:, 2304:2816], b, t, N_IDX, HD, IDX_SCALE)[:, None], tq)[:, 0]
    o_d = _dsa(q_i, w_i, q_d, tq=tq, tk=cfg["tk"], q_off=q_off, n_top=min(DSA_TOPK_MAX, l_real // 4), **dsa_src)

    o_c, o_s, o_w = (_unstack_q(o, NSA_GROUP, tq) for o in (o_c, o_s, o_w))
    o_d = _unstack_q(o_d[:, None], N_DSA, tq)
    gfull = [jnp.repeat(gate[:, :, j], HD, axis=1) for j in range(3)]
    d = x2.shape[1]
    out = _mm([o_c, o_s, o_w, o_d] + gfull + [x2], [], prm["od_w_out"], _lhs_odd,
              [(0, d, None, None, [(0, False)])], d, tm=cfg["tm"], res_idx=7)
    return out, new_nsa, new_dsa, new_state


def _cross(x2, b, t, mem_k, mem_v, prm, cfg):
    d = x2.shape[1]
    xw = N_XH * HD
    q = _mm([x2], [prm["g1"], _tile_gain(prm["x_gq"])], prm["x_wq"], _lhs_norm,
            [(0, xw, 0, None, [(0, False)])], xw, tm=cfg["tm"], gains_idx=1)
    tq = cfg["tq_cross"]
    qh = _stack_q(_heads(q, b, t, N_XH, HD, SCALE)[:, :, None], tq)
    o = _flash(qh, mem_k, mem_v, reps=1, tq=tq, tk=mem_k.shape[2], q_off=0, k_off=0, mode="full")
    o = _unstack_q(o, 1, tq)
    return _mm([o, x2], [], prm["x_wo"], _lhs_plain, [(0, d, None, None, [(0, False)])], d, tm=cfg["tm"], res_idx=1)


def _memory_kv(mem2, prm):
    xw = N_XH * HD
    w = jnp.concatenate([prm["x_wk"], prm["x_wv"]], axis=1)
    return _mm([mem2], [_tile_gain(prm["x_gk"])], w, _lhs_plain,
               [(0, xw, 0, None, [(0, False)]), (xw, xw, None, None, [(xw, False)])], 2 * xw,
               tm=min(256, mem2.shape[0]), gains_idx=0)


def _run_group(x, q_off, t_real, mem_kvs, pasts, layers, cfg):
    b, t, d = x.shape
    x2 = x.reshape(b * t, d)
    pos = q_off + jnp.arange(t, dtype=jnp.int32)
    cos, sin = _rope_tables(pos)
    cs = (jnp.tile(cos, (b, 1)), jnp.tile(sin, (b, 1)))
    outs = {}
    for li, prm in enumerate(layers):
        if li % 2 == 0:
            x2, nk, nv = _even_mixer(x2, b, t, q_off, cs, pasts[li], prm, cfg)
            outs["ek"], outs["ev"] = nk, nv
        else:
            x2, nn, nd, nw = _odd_mixer(x2, b, t, t_real, q_off, cs, pasts[li], prm, cfg)
            outs["on"], outs["od"], outs["ow"] = nn, nd, nw
        x2 = _cross(x2, b, t, mem_kvs[li][0], mem_kvs[li][1], prm, cfg)
        if li % 2 == 0:
            x2 = _ffn(x2, prm["g2"], prm["router"], prm["w1"], prm["w3"], prm["w2"], tm=cfg["tm_ffn"],
                      tf=cfg["tf"], routed=False)
        elif "moe_cap" in cfg and x2.shape[0] % cfg["tm_moe"] == 0:
            x2 = _moe(x2, prm["g2"], prm["router"], prm["w1"], prm["w3"], prm["w2"], tm=cfg["tm_moe"],
                      tf=cfg["tf"], cap=cfg["moe_cap"])
        else:
            x2 = _ffn(x2, prm["g2"], prm["router"], prm["w1"], prm["w3"], prm["w2"], tm=cfg["tm_ffn"],
                      tf=cfg["tf"], routed=True)
    return x2.reshape(b, t, d), outs


def kernel(x_prompt, x_sample, mem_prompt, cache_even_k, cache_even_v, cache_odd_nsa, cache_odd_dsa, state_odd_win, cache_mem, page_table, norm_gain, ev_w_in, ev_w_out, df_qk_gain, df_lambda, df_subln_gain, ffn_w1, ffn_w3, ffn_w2, od_w_in, od_w_out, nsa_qk_gain, cmp_pe, cmp_w1, cmp_w2, dsa_qk_gain, moe_router, moe_w1, moe_w3, moe_w2, x_wq, x_wk, x_wv, x_wo, x_qk_gain):
    depth = norm_gain.shape[0]
    bp, tp, d = x_prompt.shape
    bs, ts, _ = x_sample.shape
    n_mem = mem_prompt.shape[1]
    xw = N_XH * HD

    layers = []
    for l in range(depth):
        i = l // 2
        prm = {
            "layer": l,
            "g0": norm_gain[l, 0].reshape(1, d), "g1": norm_gain[l, 1].reshape(1, d), "g2": norm_gain[l, 2].reshape(1, d),
            "x_wq": x_wq[l].astype(BF16), "x_wk": x_wk[l].astype(BF16), "x_wv": x_wv[l].astype(BF16),
            "x_wo": x_wo[l].astype(BF16), "x_gq": x_qk_gain[l, 0], "x_gk": x_qk_gain[l, 1],
        }
        if l % 2 == 0:
            prm.update({
                "ev_w_in": ev_w_in[i].astype(BF16), "ev_w_out": ev_w_out[i].astype(BF16),
                "df_qk_gain": df_qk_gain[i], "df_lambda": df_lambda[i], "df_subln_gain": df_subln_gain[i],
                "router": jnp.zeros((SUBLANES, LANES), F32),
                "w1": ffn_w1[i][None].astype(BF16), "w3": ffn_w3[i][None].astype(BF16), "w2": ffn_w2[i][None].astype(BF16),
            })
        else:
            prm.update({
                "od_w_in": od_w_in[i].astype(BF16), "od_w_out": od_w_out[i].astype(BF16),
                "nsa_qk_gain": nsa_qk_gain[i], "dsa_qk_gain": dsa_qk_gain[i],
                "cmp_pe": cmp_pe[i], "cmp_w1": cmp_w1[i].astype(BF16), "cmp_w2": cmp_w2[i].astype(BF16),
                "router": jnp.pad(moe_router[i], ((0, 0), (0, LANES - N_EXPERTS))),
                "w1": moe_w1[i].astype(BF16), "w3": moe_w3[i].astype(BF16), "w2": moe_w2[i].astype(BF16),
            })
        layers.append(prm)

    def mem_heads(kv, b):
        k = kv[:, :, 0].transpose(0, 2, 1, 3).astype(BF16)
        v = kv[:, :, 1].transpose(0, 2, 1, 3).astype(BF16)
        return k, v

    mem2 = mem_prompt.reshape(bp * n_mem, d)
    mem_p = [_memory_kv(mem2, layers[l]).reshape(bp, n_mem, 2, N_XH, HD) for l in range(depth)]
    ff = ffn_w1.shape[2]
    tf = ff // 2 if (ff // 2) % LANES == 0 else ff
    cfg_p = {"tm": 256, "tq": 128, "tk": 256, "t_even": 256, "tk_win": 256, "tm_ffn": 512, "tf": tf,
             "tm_moe": 1024, "moe_cap": 320}
    cfg_p["tq"] = min(cfg_p["tq"], tp)
    cfg_p["tq_cross"] = 1024 if tp % 1024 == 0 else cfg_p["tq"]
    y_prompt, op = _run_group(x_prompt, 0, tp, [mem_heads(m, bp) for m in mem_p], [None] * depth, layers, cfg_p)
    p_mem = jnp.stack(mem_p)

    n_past = page_table.shape[1] * cache_even_k.shape[2]
    ts_pad = -(-ts // SUBLANES) * SUBLANES
    xs = _pad_rows(x_sample, ts_pad)
    pasts = []
    for l in range(depth):
        i = l // 2
        if l % 2 == 0:
            pasts.append((cache_even_k[i], cache_even_v[i], page_table))
        else:
            pn = cache_odd_nsa[i]
            pd = cache_odd_dsa[i]
            pasts.append((pn.reshape(pn.shape[0], pn.shape[1], -1), pd.reshape(pd.shape[0], pd.shape[1], -1),
                          state_odd_win[i], page_table))
    cfg_s = {"tm": bs * ts_pad, "tq": ts_pad, "tq_cross": ts_pad, "tk": 512, "pp": 4, "tk_win": 128,
             "tm_ffn": bs * ts_pad, "tf": tf}
    y_s, os_ = _run_group(xs, n_past, ts, [mem_heads(cache_mem[l], bs) for l in range(depth)], pasts, layers, cfg_s)
    y_sample = y_s[:, :ts]

    g = N_NSA_KV
    return (
        y_prompt, y_sample,
        op["ek"][None], op["ev"][None],
        op["on"].reshape(1, bp, tp, 4, g, HD), op["od"].reshape(1, bp, tp, 3, HD),
        op["ow"][None], p_mem,
        os_["ek"][:, :ts][None], os_["ev"][:, :ts][None],
        os_["on"][:, :ts].reshape(1, bs, ts, 4, g, HD), os_["od"][:, :ts].reshape(1, bs, ts, 3, HD),
        os_["ow"][None],
    )
```

```python
import functools
import math

import jax
import jax.numpy as jnp
import numpy as np
from jax import lax
from jax.experimental import pallas as pl
from jax.experimental.pallas import tpu as pltpu

F32 = jnp.float32
BF16 = jnp.bfloat16

HD = 64
N_SB = 8
N_DF = 4
N_NSA = 8
N_NSA_KV = 2
NSA_GROUP = N_NSA // N_NSA_KV
N_DSA = 8
N_IDX = 8
N_XH = 4
N_EXPERTS = 8
ROPE_THETA = 10000.0
CMP_LEN = 32
CMP_STRIDE = 16
SLC_BLOCK = 64
CMP_PER_SLC = SLC_BLOCK // CMP_STRIDE
N_SLC = 16
N_LOCAL = 2
WINDOW = 512
DSA_TOPK_MAX = 256
EPS = 1e-6
NEG = -1e30
FORCE = 1e9
SCALE = HD ** -0.5
IDX_SCALE = HD ** -0.5
SB_W = N_SB * HD
HD_SHIFT = 6
SLC_SHIFT = 6
DF_W = N_DF * 2 * HD

LANES = 128
SUBLANES = 8
PAGE = 128
VMEM_LIMIT = 52 * 1024 * 1024
INT_MIN = -2 ** 31
ROW_SPLIT = 2
SB_CUT = 120.0
SEL_CHUNK_ROWS = 16
MIN_CHAIN_ROWS = 64
KEY_UNROLL = 4

_NT = (((1,), (1,)), ((), ()))


def _cparams(sem):
    return pltpu.CompilerParams(dimension_semantics=sem, vmem_limit_bytes=VMEM_LIMIT)


def _dot(a, b):
    return jnp.dot(a, b, preferred_element_type=F32)


def _dot_nt(a, b):
    return lax.dot_general(a, b, _NT, preferred_element_type=F32)


def _split_dot(x, m_bf16):
    hi = x.astype(BF16)
    lo = (x - hi.astype(F32)).astype(BF16)
    return _dot(hi, m_bf16) + _dot(lo, m_bf16)


def _split_dot_rhs(m_bf16, x):
    hi = x.astype(BF16)
    lo = (x - hi.astype(F32)).astype(BF16)
    return _dot(m_bf16, hi) + _dot(m_bf16, lo)


def _iota(shape, dim):
    return lax.broadcasted_iota(jnp.int32, shape, dim)


def _rms_rows(x, g):
    return x * lax.rsqrt(jnp.mean(x * x, axis=-1, keepdims=True) + EPS) * g


def _group_mean_matrix():
    r = _iota((LANES, LANES), 0) >> HD_SHIFT
    c = _iota((LANES, LANES), 1) >> HD_SHIFT
    return jnp.where(r == c, 1.0 / HD, 0.0).astype(BF16)


def _head_norm(y, g, gm):
    ms = _split_dot(y * y, gm)
    return y * lax.rsqrt(ms + EPS) * g


def _rope_slab(y, cos, sin):
    lane = _iota(y.shape, 1)
    first = (lane & (HD - 1)) < (HD // 2)
    swapped = jnp.where(first, pltpu.roll(y, LANES - HD // 2, 1), pltpu.roll(y, HD // 2, 1))
    return y * cos + swapped * sin


def _mm_kernel(*refs, n_rows, n_consts, lhs_fn, program, rope_idx, gains_idx, res_idx):
    rows = refs[:n_rows]
    consts = refs[n_rows:n_rows + n_consts]
    w_ref = refs[n_rows + n_consts]
    o_ref = refs[-1]
    lhs = lhs_fn(rows, consts).astype(BF16)
    gm = _group_mean_matrix() if gains_idx is not None else None
    for (src, width, gain_row, act, outs) in program:
        y_full = _dot(lhs, w_ref[:, src:src + width])
        for s in range(width // LANES):
            y = y_full[:, s * LANES:(s + 1) * LANES]
            if gain_row is not None:
                y = _head_norm(y, consts[gains_idx][gain_row:gain_row + 1, :], gm)
            if act == "sigmoid":
                y = 1.0 / (1.0 + jnp.exp(-y))
            for (dst, rope) in outs:
                z = y
                if rope:
                    z = _rope_slab(y, rows[rope_idx[0]][...], rows[rope_idx[1]][...])
                d0 = dst + s * LANES
                if res_idx is not None:
                    z = z + rows[res_idx][:, d0:d0 + LANES]
                o_ref[:, d0:d0 + LANES] = z


def _mm(rows, consts, w, lhs_fn, program, out_cols, *, tm, rope_idx=None, gains_idx=None, res_idx=None, name="mm"):
    n = rows[0].shape[0]
    assert n % tm == 0
    in_specs = [pl.BlockSpec((tm, r.shape[1]), lambda i: (i, 0)) for r in rows]
    in_specs += [pl.BlockSpec(c.shape, lambda i: (0, 0)) for c in consts]
    in_specs += [pl.BlockSpec(w.shape, lambda i: (0, 0))]
    kern = functools.partial(_mm_kernel, n_rows=len(rows), n_consts=len(consts), lhs_fn=lhs_fn,
                             program=program, rope_idx=rope_idx, gains_idx=gains_idx, res_idx=res_idx)
    return pl.pallas_call(
        kern,
        grid=(n // tm,),
        in_specs=in_specs,
        out_specs=pl.BlockSpec((tm, out_cols), lambda i: (i, 0)),
        out_shape=jax.ShapeDtypeStruct((n, out_cols), F32),
        compiler_params=_cparams(("parallel",)),
        name=name,
    )(*rows, *consts, w)


def _lhs_norm(rows, consts):
    return _rms_rows(rows[0][...], consts[0][...])


def _lhs_plain(rows, consts):
    return rows[0][...]


def _lhs_cat2(rows, consts):
    return jnp.concatenate([rows[0][...], rows[1][...]], axis=1)


def _lhs_odd(rows, consts):
    oc, os_, ow, od = rows[0][...], rows[1][...], rows[2][...], rows[3][...]
    g0, g1, g2 = rows[4][...], rows[5][...], rows[6][...]
    return jnp.concatenate([g0 * oc + g1 * os_ + g2 * ow, od], axis=1)


def _ffn_kernel(x_ref, g_ref, r_ref, w1_ref, w3_ref, w2_ref, o_ref, h_ref, acc_ref, gate_ref, *, routed):
    e = pl.program_id(1)
    f = pl.program_id(2)
    first = jnp.logical_and(e == 0, f == 0)
    last = jnp.logical_and(e == pl.num_programs(1) - 1, f == pl.num_programs(2) - 1)

    @pl.when(first)
    def _():
        x = x_ref[...]
        h = _rms_rows(x, g_ref[...])
        h_ref[...] = h.astype(BF16)
        acc_ref[...] = x
        if routed:
            logits = jnp.dot(h, r_ref[...], preferred_element_type=F32, precision=lax.Precision.HIGHEST)
            col = _iota(logits.shape, 1).astype(F32)
            logits = jnp.where(col < N_EXPERTS, logits, -jnp.inf)
            m1 = jnp.max(logits, axis=1, keepdims=True)
            i1 = jnp.min(jnp.where(logits == m1, col, 1e9), axis=1, keepdims=True)
            rest = jnp.where(col == i1, -jnp.inf, logits)
            m2 = jnp.max(rest, axis=1, keepdims=True)
            i2 = jnp.min(jnp.where(rest == m2, col, 1e9), axis=1, keepdims=True)
            e2 = jnp.exp(m2 - m1)
            g1 = 1.0 / (1.0 + e2)
            g2 = e2 / (1.0 + e2)
            gate_ref[...] = jnp.where(col == i1, g1, 0.0) + jnp.where(col == i2, g2, 0.0)

    def compute(gcol):
        h = h_ref[...]
        u = _dot(h, w1_ref[0])
        v = _dot(h, w3_ref[0])
        a = (u * (1.0 / (1.0 + jnp.exp(-u)))) * v
        y = _dot(a.astype(BF16), w2_ref[0])
        if gcol is not None:
            y = gcol * y
        acc_ref[...] += y

    if routed:
        col = _iota(gate_ref.shape, 1)
        gcol = jnp.sum(jnp.where(col == e, gate_ref[...], 0.0), axis=1, keepdims=True)
        active = jnp.max(gcol) > 0.0

        @pl.when(active)
        def _():
            compute(gcol)
    else:
        compute(None)

    @pl.when(last)
    def _():
        o_ref[...] = acc_ref[...]


def _moe_kernel(x_ref, g_ref, r_ref, w1_ref, w3_ref, w2_ref, o_ref, h_ref, acc_ref, gate_ref, slot_ref,
                slot_t_ref, he_ref, ye_ref, *, cap):
    e = pl.program_id(1)
    f = pl.program_id(2)
    nf = pl.num_programs(2)
    tm = x_ref.shape[0]
    first = jnp.logical_and(e == 0, f == 0)
    last = jnp.logical_and(e == pl.num_programs(1) - 1, f == nf - 1)

    @pl.when(first)
    def _():
        x = x_ref[...]
        h = _rms_rows(x, g_ref[...])
        h_ref[...] = h.astype(BF16)
        acc_ref[...] = x
        logits = jnp.dot(h, r_ref[...], preferred_element_type=F32, precision=lax.Precision.HIGHEST)
        col = _iota(logits.shape, 1).astype(F32)
        logits = jnp.where(col < N_EXPERTS, logits, -jnp.inf)
        m1 = jnp.max(logits, axis=1, keepdims=True)
        i1 = jnp.min(jnp.where(logits == m1, col, 1e9), axis=1, keepdims=True)
        rest = jnp.where(col == i1, -jnp.inf, logits)
        m2 = jnp.max(rest, axis=1, keepdims=True)
        i2 = jnp.min(jnp.where(rest == m2, col, 1e9), axis=1, keepdims=True)
        e2 = jnp.exp(m2 - m1)
        gate_ref[...] = jnp.where(col == i1, 1.0 / (1.0 + e2), 0.0) + jnp.where(col == i2, e2 / (1.0 + e2), 0.0)
        chosen = jnp.where(jnp.logical_or(col == i1, col == i2), 1.0, 0.0)
        earlier = jnp.where(_iota((tm, tm), 1) < _iota((tm, tm), 0), 1.0, 0.0).astype(BF16)
        slot = jnp.where(chosen > 0.5, _dot(earlier, chosen.astype(BF16)), -1.0)
        slot_ref[...] = slot
        slot_t_ref[...] = jnp.transpose(slot)

    lane = _iota((tm, LANES), 1)
    mine = lane == e
    gcol = jnp.sum(jnp.where(mine, gate_ref[...], 0.0), axis=1, keepdims=True)
    slot_col = jnp.sum(jnp.where(mine, slot_ref[...], 0.0), axis=1, keepdims=True)
    slot_row = slot_t_ref[pl.ds(e, 1), :]
    count = jnp.sum(jnp.where(slot_row >= 0.0, 1.0, 0.0)).astype(jnp.int32)
    n_chunk = lax.div(count + (cap - 1), cap)

    def pack(c):
        want = (c * cap + _iota((cap, tm), 0)).astype(F32)
        return jnp.where(slot_row == want, 1.0, 0.0).astype(BF16)

    def unpack(c):
        want = (c * cap + _iota((tm, cap), 1)).astype(F32)
        return jnp.where(slot_col == want, 1.0, 0.0).astype(BF16)

    def expert(rows):
        u = _dot(rows, w1_ref[0])
        v = _dot(rows, w3_ref[0])
        a = (u * (1.0 / (1.0 + jnp.exp(-u)))) * v
        return _dot(a.astype(BF16), w2_ref[0])

    @pl.when(n_chunk > 0)
    def _():
        @pl.when(f == 0)
        def _():
            he_ref[...] = _dot(pack(0), h_ref[...]).astype(BF16)
            ye_ref[...] = expert(he_ref[...])

        @pl.when(f > 0)
        def _():
            ye_ref[...] += expert(he_ref[...])

        @pl.when(f == nf - 1)
        def _():
            acc_ref[...] += gcol * _split_dot_rhs(unpack(0), ye_ref[...])

    def extra(c, _):
        part = expert(_dot(pack(c), h_ref[...]).astype(BF16))
        acc_ref[...] += gcol * _split_dot_rhs(unpack(c), part)
        return 0

    lax.fori_loop(1, n_chunk, extra, 0)

    @pl.when(last)
    def _():
        o_ref[...] = acc_ref[...]


def _moe(x, g, router, w1, w3, w2, *, tm, tf, cap):
    n, d = x.shape
    ne, _, ff = w1.shape
    assert n % tm == 0 and ff % tf == 0
    kern = functools.partial(_moe_kernel, cap=cap)
    return pl.pallas_call(
        kern,
        grid=(n // tm, ne, ff // tf),
        in_specs=[
            pl.BlockSpec((tm, d), lambda i, e, f: (i, 0), pipeline_mode=pl.Buffered(1)),
            pl.BlockSpec((1, d), lambda i, e, f: (0, 0)),
            pl.BlockSpec(router.shape, lambda i, e, f: (0, 0)),
            pl.BlockSpec((1, d, tf), lambda i, e, f: (e, 0, f)),
            pl.BlockSpec((1, d, tf), lambda i, e, f: (e, 0, f)),
            pl.BlockSpec((1, tf, d), lambda i, e, f: (e, f, 0)),
        ],
        out_specs=pl.BlockSpec((tm, d), lambda i, e, f: (i, 0), pipeline_mode=pl.Buffered(1)),
        out_shape=jax.ShapeDtypeStruct((n, d), F32),
        scratch_shapes=[pltpu.VMEM((tm, d), BF16), pltpu.VMEM((tm, d), F32), pltpu.VMEM((tm, LANES), F32),
                        pltpu.VMEM((tm, LANES), F32), pltpu.VMEM((LANES, tm), F32),
                        pltpu.VMEM((cap, d), BF16), pltpu.VMEM((cap, d), F32)],
        compiler_params=_cparams(("parallel", "arbitrary", "arbitrary")),
        name="moe_routed",
    )(x, g, router, w1, w3, w2)


def _ffn(x, g, router, w1, w3, w2, *, tm, tf, routed):
    n, d = x.shape
    ne, _, ff = w1.shape
    assert n % tm == 0 and ff % tf == 0
    kern = functools.partial(_ffn_kernel, routed=routed)
    return pl.pallas_call(
        kern,
        grid=(n // tm, ne, ff // tf),
        in_specs=[
            pl.BlockSpec((tm, d), lambda i, e, f: (i, 0)),
            pl.BlockSpec((1, d), lambda i, e, f: (0, 0)),
            pl.BlockSpec(router.shape, lambda i, e, f: (0, 0)),
            pl.BlockSpec((1, d, tf), lambda i, e, f: (e, 0, f)),
            pl.BlockSpec((1, d, tf), lambda i, e, f: (e, 0, f)),
            pl.BlockSpec((1, tf, d), lambda i, e, f: (e, f, 0)),
        ],
        out_specs=pl.BlockSpec((tm, d), lambda i, e, f: (i, 0)),
        out_shape=jax.ShapeDtypeStruct((n, d), F32),
        scratch_shapes=[pltpu.VMEM((tm, d), BF16), pltpu.VMEM((tm, d), F32), pltpu.VMEM((tm, LANES), F32)],
        compiler_params=_cparams(("parallel", "arbitrary", "arbitrary")),
        name="moe" if routed else "ffn",
    )(x, g, router, w1, w3, w2)


def _tile_rows(m, reps):
    return m if reps == 1 else jnp.concatenate([m] * reps, axis=0)


def _online_step(s, v, m, l, acc):
    m_new = jnp.maximum(m, jnp.max(s, axis=1, keepdims=True))
    p = jnp.exp(s - m_new)
    alpha = jnp.exp(m - m_new)
    return m_new, alpha * l + jnp.sum(p, axis=1, keepdims=True), alpha * acc + _dot(p.astype(BF16), v)


def _with_ones(v):
    return jnp.concatenate([v, jnp.ones_like(v)], axis=1)


def _online_step_ones(s, v1, m, acc):
    m_new = jnp.maximum(m, jnp.max(s, axis=1, keepdims=True))
    p = jnp.exp(s - m_new)
    return m_new, jnp.exp(m - m_new) * acc + _dot(p.astype(BF16), v1)


def _online_init_ones(rows):
    return (jnp.full((rows, 1), NEG, F32), jnp.zeros((rows, 2 * HD), F32))


def _online_init(rows, dv):
    return (jnp.full((rows, 1), NEG, F32), jnp.zeros((rows, 1), F32), jnp.zeros((rows, dv), F32))


def _unrolled_loop(lo, hi, body, init, unroll=KEY_UNROLL):
    shift = unroll.bit_length() - 1
    n_group = (hi - lo) >> shift

    def group(p, st):
        for u in range(unroll):
            st = body(lo + unroll * p + u, st)
        return st

    st = lax.fori_loop(0, n_group, group, init)
    return lax.fori_loop(lo + n_group * unroll, hi, body, st)


def _flash_kernel(*refs, reps, tq, tk, q_off, k_off, mode, has_bm, slab):
    refs = list(refs)
    q_ref = refs.pop(0)
    k_ref = refs.pop(0)
    v_ref = None if slab else refs.pop(0)
    bm_ref = refs.pop(0) if has_bm else None
    o_ref = refs.pop(0)
    i = pl.program_id(2)
    q = q_ref[0, 0, 0]
    dv = HD if slab else v_ref.shape[-1]
    n_kt = k_ref.shape[-2] // tk
    q_lo = q_off + i * tq

    def kv_tiles(k0):
        if not slab:
            return k_ref[0, 0, pl.ds(k0, tk), :], v_ref[0, 0, pl.ds(k0, tk), :]
        t = k_ref[0, pl.ds(k0, tk), :]
        first = pl.program_id(1) == 0
        return (jnp.where(first, t[:, :HD], t[:, HD:2 * HD]).astype(BF16),
                jnp.where(first, t[:, 2 * HD:3 * HD], t[:, 3 * HD:]).astype(BF16))

    if mode == "full":
        lo, hi = 0, n_kt
    else:
        hi = jnp.minimum(lax.div(q_lo + tq - 1 - k_off, tk) + 1, n_kt)
        lo = 0
        if mode == "window":
            if (tq + WINDOW - 2) // tk + 2 <= KEY_UNROLL:
                lo = jnp.maximum(hi - KEY_UNROLL, 0)
            else:
                lo = lax.div(jnp.maximum(q_lo - (WINDOW - 1) - k_off, 0), tk)
    qpos = q_lo + _iota((tq, tk), 0)
    if has_bm:
        bm = bm_ref[0, 0].astype(BF16)
        nbp = bm.shape[1]

    def body(j, state):
        k0 = pl.multiple_of(j * tk, tk)
        k, v = kv_tiles(k0)
        bias = None
        if mode != "full":
            kpos = k_off + j * tk + _iota((tq, tk), 1)
            ok = kpos <= qpos
            if mode == "window":
                ok = jnp.logical_and(ok, qpos - kpos < WINDOW)
                ok = jnp.logical_and(ok, kpos >= 0)
            bias = jnp.where(ok, 0.0, NEG)
        if has_bm:
            blk = _iota((nbp, tk), 0)
            tok = (j * tk + _iota((nbp, tk), 1)) >> SLC_SHIFT
            expand = jnp.where(blk == tok, 1.0, 0.0).astype(BF16)
            bias = jnp.where(_dot(bm, expand) > 0.5, bias, NEG)
        if bias is not None:
            bias = _tile_rows(bias, cr // tq)
        if slab:
            v = _with_ones(v)
        new = []
        for c in range(n_chain):
            s = _dot_nt(q[c * cr:(c + 1) * cr].astype(BF16), k)
            if bias is not None:
                s = s + bias
            new.append(_online_step_ones(s, v, *state[c]) if slab else _online_step(s, v, *state[c]))
        return tuple(new)

    n_chain = reps if tq >= MIN_CHAIN_ROWS else 1
    cr = reps * tq // n_chain
    init = tuple((_online_init_ones(cr) if slab else _online_init(cr, dv)) for _ in range(n_chain))
    state = _unrolled_loop(lo, hi, body, init)
    for c in range(n_chain):
        if slab:
            acc = state[c][1]
            o_ref[0, 0, 0, c * cr:(c + 1) * cr, :] = acc[:, :HD] / acc[:, HD:HD + 1]
        else:
            o_ref[0, 0, 0, c * cr:(c + 1) * cr, :] = state[c][2] / state[c][1]


def _flash(q, k, v=None, *, reps, tq, tk, q_off, k_off, mode, bm=None, kv_blk=None):
    b, hk, nq, rows, _ = q.shape
    lk = k.shape[-2]
    assert rows == reps * tq and lk % tk == 0
    if kv_blk is None:
        dv = v.shape[3]
        kv_specs = [pl.BlockSpec((1, 1, lk, HD), lambda b_, h, i: (b_, h, 0, 0)),
                    pl.BlockSpec((1, 1, lk, dv), lambda b_, h, i: (b_, h, 0, 0))]
        args = [q, k, v]
    else:
        assert hk == 2 and v is None
        dv = HD
        kv_specs = [pl.BlockSpec((1, lk, 2 * LANES), lambda b_, h, i: (b_, 0, kv_blk))]
        args = [q, k]
    in_specs = [pl.BlockSpec((1, 1, 1, rows, HD), lambda b_, h, i: (b_, h, i, 0, 0))] + kv_specs
    if bm is not None:
        in_specs.append(pl.BlockSpec((1, 1, tq, bm.shape[3]), lambda b_, h, i: (b_, h, i, 0)))
        args.append(bm)
    kern = functools.partial(_flash_kernel, reps=reps, tq=tq, tk=tk, q_off=q_off, k_off=k_off, mode=mode,
                             has_bm=bm is not None, slab=kv_blk is not None)
    return pl.pallas_call(
        kern,
        grid=(b, hk, nq),
        in_specs=in_specs,
        out_specs=pl.BlockSpec((1, 1, 1, rows, dv), lambda b_, h, i: (b_, h, i, 0, 0)),
        out_shape=jax.ShapeDtypeStruct((b, hk, nq, rows, dv), F32),
        compiler_params=_cparams(("parallel", "parallel", "arbitrary")),
        name="flash_" + mode + ("_blockmask" if bm is not None else ""),
    )(*args)


def _softplus(z):
    return jnp.maximum(z, 0.0) + jnp.log(1.0 + jnp.exp(-jnp.abs(z)))


def _later_matrix(n):
    return jnp.where(_iota((n, n), 0) > _iota((n, n), 1), 1.0, 0.0).astype(BF16)


def _sb_kernel(q_ref, k_ref, v_ref, o_ref, *, t):
    i = pl.program_id(2)
    th = t // ROW_SPLIT
    lane = _iota((th, LANES), 1)
    later = _later_matrix(t)
    qs = []
    for part in range(ROW_SPLIT):
        qf = q_ref[0, part * th:(part + 1) * th, :] * SCALE
        qs.append((jnp.where(lane < HD, qf, 0.0).astype(BF16), jnp.where(lane >= HD, qf, 0.0).astype(BF16)))

    def tile(j, carries, outs, masked):
        k0 = pl.multiple_of(j * t, t)
        k = k_ref[0, pl.ds(k0, t), :].astype(BF16)
        v = v_ref[0, pl.ds(k0, t), :].astype(BF16)
        new_carries, new_outs = [], []
        for part in range(ROW_SPLIT):
            heads = []
            for hd in range(2):
                carry = carries[2 * part + hd]
                z = _dot_nt(qs[part][hd], k)
                sp = _softplus(z)
                log_sig = z - sp
                if masked:
                    vis = _iota((th, t), 1) < part * th + _iota((th, t), 0)
                    sp = jnp.where(vis, sp, 0.0)
                a = jnp.exp(log_sig - _dot(sp.astype(BF16), later) - carry)
                if masked:
                    a = jnp.where(vis, a, 0.0)
                heads.append(_dot(a.astype(BF16), v))
                new_carries.append(carry + jnp.sum(sp, axis=1, keepdims=True))
            new_outs.append(outs[part] + jnp.where(lane < HD, heads[0], heads[1]))
        return tuple(new_carries), tuple(new_outs)

    def min_carry(carries):
        m = jnp.min(carries[0])
        for c in carries[1:]:
            m = jnp.minimum(m, jnp.min(c))
        return m

    zero = jnp.zeros((th, 1), F32)
    carries, outs = tile(i, (zero,) * (2 * ROW_SPLIT), (jnp.zeros((th, LANES), F32),) * ROW_SPLIT, True)

    def cond(st):
        return jnp.logical_and(st[0] < i, st[1] < SB_CUT)

    def body(st):
        carries, outs = tile(i - 1 - st[0], st[2], st[3], False)
        return st[0] + 1, min_carry(carries), carries, outs

    _, _, carries, outs = lax.while_loop(cond, body, (jnp.int32(0), min_carry(carries), carries, outs))
    for part in range(ROW_SPLIT):
        o_ref[0, part * th:(part + 1) * th, :] = outs[part]


def _sb(proj3, *, t, q_slab, k_slab, v_slab, n_slabs):
    b, tt, _ = proj3.shape
    assert tt % t == 0
    kern = functools.partial(_sb_kernel, t=t)
    return pl.pallas_call(
        kern,
        grid=(b, n_slabs, tt // t),
        in_specs=[
            pl.BlockSpec((1, t, LANES), lambda b_, p, i: (b_, i, q_slab + p)),
            pl.BlockSpec((1, tt, LANES), lambda b_, p, i: (b_, 0, k_slab + p)),
            pl.BlockSpec((1, tt, LANES), lambda b_, p, i: (b_, 0, v_slab + p)),
        ],
        out_specs=pl.BlockSpec((1, t, LANES), lambda b_, p, i: (b_, i, p)),
        out_shape=jax.ShapeDtypeStruct((b, tt, n_slabs * LANES), F32),
        compiler_params=_cparams(("parallel", "parallel", "arbitrary")),
        name="sb_prompt",
    )(proj3, proj3, proj3)


def _lam(lv, lam_init):
    a = jnp.sum(jnp.sum(lv[0:1] * lv[1:2], axis=1, keepdims=True), axis=0, keepdims=True)
    b = jnp.sum(jnp.sum(lv[2:3] * lv[3:4], axis=1, keepdims=True), axis=0, keepdims=True)
    return jnp.exp(a) - jnp.exp(b) + lam_init


def _df_kernel(lv_ref, g_ref, q_ref, k_ref, v_ref, o_ref, *, t, lam_init):
    i = pl.program_id(2)
    th = t // ROW_SPLIT
    lane = _iota((th, LANES), 1)
    qs = []
    for part in range(ROW_SPLIT):
        qf = q_ref[0, part * th:(part + 1) * th, :] * SCALE
        qs.append((jnp.where(lane < HD, qf, 0.0).astype(BF16), jnp.where(lane >= HD, qf, 0.0).astype(BF16)))

    def tile(j, state, masked):
        k0 = pl.multiple_of(j * t, t)
        k = k_ref[0, pl.ds(k0, t), :].astype(BF16)
        v = v_ref[0, pl.ds(k0, t), :].astype(BF16)
        new = []
        for part in range(ROW_SPLIT):
            for mp in range(2):
                s = _dot_nt(qs[part][mp], k)
                if masked:
                    s = jnp.where(_iota((th, t), 1) <= part * th + _iota((th, t), 0), s, NEG)
                new.append(_online_step(s, v, *state[2 * part + mp]))
        return tuple(new)

    init = tuple(_online_init(th, LANES) for _ in range(2 * ROW_SPLIT))
    state = _unrolled_loop(0, i, lambda j, st: tile(j, st, False), init)
    state = tile(i, state, True)
    lam = _lam(lv_ref[...], lam_init)
    for part in range(ROW_SPLIT):
        (_, l0, a0), (_, l1, a1) = state[2 * part], state[2 * part + 1]
        d = a0 / l0 - lam * (a1 / l1)
        o_ref[0, part * th:(part + 1) * th, :] = _rms_rows(d, g_ref[...]) * (1.0 - lam_init)


def _df(proj3, lv, subln, *, t, q_slab, k_slab, v_slab, n_slabs, lam_init):
    b, tt, _ = proj3.shape
    assert tt % t == 0
    kern = functools.partial(_df_kernel, t=t, lam_init=lam_init)
    return pl.pallas_call(
        kern,
        grid=(b, n_slabs, tt // t),
        in_specs=[
            pl.BlockSpec(lv.shape, lambda b_, p, i: (0, 0)),
            pl.BlockSpec(subln.shape, lambda b_, p, i: (0, 0)),
            pl.BlockSpec((1, t, LANES), lambda b_, p, i: (b_, i, q_slab + p)),
            pl.BlockSpec((1, tt, LANES), lambda b_, p, i: (b_, 0, k_slab + p)),
            pl.BlockSpec((1, tt, LANES), lambda b_, p, i: (b_, 0, v_slab + p)),
        ],
        out_specs=pl.BlockSpec((1, t, LANES), lambda b_, p, i: (b_, i, p)),
        out_shape=jax.ShapeDtypeStruct((b, tt, n_slabs * LANES), F32),
        compiler_params=_cparams(("parallel", "parallel", "arbitrary")),
        name="df_prompt",
    )(lv, subln, proj3, proj3, proj3)


def _dec_even_kernel(*refs, pp, ts, lam_init):
    tbl_ref, qt_ref = refs[0], refs[1]
    k_refs = refs[2:2 + pp]
    v_refs = refs[2 + pp:2 + 2 * pp]
    kn_ref, vn_ref, lv_ref, g_ref, o_ref, later_ref, st_ref, asb_ref, adf_ref = refs[2 + 2 * pp:]
    del tbl_ref
    s_id = pl.program_id(1)
    nk = pp * PAGE
    half = LANES // 2
    qt = qt_ref[0]

    def col_of(row):
        return jnp.transpose(jnp.broadcast_to(row, (SUBLANES, LANES)))[half:, 0:1]

    def tile(kt, vt, later, vis_sb, vis_df):
        carry, m, l = st_ref[0:1, :], st_ref[1:2, :], st_ref[2:3, :]
        zt = _dot(kt, qt)
        lane = _iota(zt.shape, 1)
        sp = _softplus(zt)
        log_sig = zt - sp
        s = zt
        if vis_sb is not None:
            sp = jnp.where(vis_sb, sp, 0.0)
            s = jnp.where(vis_df, s, NEG)
        a = jnp.exp(log_sig - _dot(later, sp.astype(BF16)) - carry)
        if vis_sb is not None:
            a = jnp.where(vis_sb, a, 0.0)
        m_new = jnp.maximum(m, jnp.max(s, axis=0, keepdims=True))
        p = jnp.exp(s - m_new)
        alpha = jnp.exp(m - m_new)
        st_ref[0:1, :] = carry + jnp.sum(sp, axis=0, keepdims=True)
        st_ref[1:2, :] = m_new
        st_ref[2:3, :] = alpha * l + jnp.sum(p, axis=0, keepdims=True)
        w = jnp.transpose(jnp.where(lane < half, a, p)).astype(BF16)
        asb_ref[...] += _dot(w[:half], vt[:, :SB_W])
        adf_ref[...] = col_of(alpha) * adf_ref[...] + _dot(w[half:], vt[:, SB_W:])

    @pl.when(s_id == 0)
    def _():
        later_ref[...] = jnp.where(_iota((nk, nk), 1) > _iota((nk, nk), 0), 1.0, 0.0).astype(BF16)
        st_ref[...] = jnp.where(_iota(st_ref.shape, 0) == 1, NEG, 0.0)
        asb_ref[...] = jnp.zeros_like(asb_ref)
        adf_ref[...] = jnp.zeros_like(adf_ref)
        key = _iota((PAGE, LANES), 0)
        tok = _iota((PAGE, LANES), 1) & (ts - 1)
        tile(kn_ref[0].astype(BF16), vn_ref[0].astype(BF16), later_ref[0:PAGE, 0:PAGE], key < tok, key <= tok)

    kt = jnp.concatenate([r[0].astype(BF16) for r in k_refs], axis=0)
    vt = jnp.concatenate([r[0].astype(BF16) for r in v_refs], axis=0)
    tile(kt, vt, later_ref[...], None, None)

    @pl.when(s_id == pl.num_programs(1) - 1)
    def _():
        row = _iota((half, SB_W), 0)
        lane = _iota((half, SB_W), 1)

        def fold(x):
            out = x[0:ts]
            for u in range(1, half // ts):
                out = out + x[u * ts:(u + 1) * ts]
            return out

        o_sb = fold(jnp.where((row >> 3) == (lane >> HD_SHIFT), asb_ref[...], 0.0))
        pn = adf_ref[...] / col_of(st_ref[2:3, :])
        same_head = (row >> 4) == (lane >> 7)
        o0 = fold(jnp.where(jnp.logical_and(same_head, ((row >> 3) & 1) == 0), pn, 0.0))
        o1 = fold(jnp.where(jnp.logical_and(same_head, ((row >> 3) & 1) == 1), pn, 0.0))
        d = o0 - _lam(lv_ref[...], lam_init) * o1
        parts = [o_sb]
        for h in range(N_DF):
            parts.append(_rms_rows(d[:, h * LANES:(h + 1) * LANES], g_ref[...]) * (1.0 - lam_init))
        o_ref[0] = jnp.concatenate(parts, axis=1)


def _dec_even(qt, pool_k, pool_v, table, k_new, v_new, lv, subln, *, pp, ts, lam_init):
    b, n_pages = table.shape
    width = pool_k.shape[2]
    assert n_pages % pp == 0 and ts == SUBLANES and k_new.shape[1] == PAGE
    n_steps = n_pages // pp

    def page_map(u):
        return lambda b_, s, t: (t[b_, n_pages - (s + 1) * pp + u], 0, 0)

    page_specs = [pl.BlockSpec((1, PAGE, width), page_map(u)) for u in range(pp)]
    new_spec = pl.BlockSpec((1, PAGE, width), lambda b_, s, t: (b_, 0, 0))
    grid_spec = pltpu.PrefetchScalarGridSpec(
        num_scalar_prefetch=1,
        grid=(b, n_steps),
        in_specs=[pl.BlockSpec((1,) + qt.shape[1:], lambda b_, s, t: (b_, 0, 0))] + page_specs + page_specs
        + [new_spec, new_spec, pl.BlockSpec(lv.shape, lambda b_, s, t: (0, 0)),
           pl.BlockSpec(subln.shape, lambda b_, s, t: (0, 0))],
        out_specs=pl.BlockSpec((1, ts, width), lambda b_, s, t: (b_, 0, 0)),
        scratch_shapes=[
            pltpu.VMEM((pp * PAGE, pp * PAGE), BF16),
            pltpu.VMEM((SUBLANES, LANES), F32),
            pltpu.VMEM((LANES // 2, SB_W), F32),
            pltpu.VMEM((LANES // 2, DF_W), F32),
        ],
    )
    kern = functools.partial(_dec_even_kernel, pp=pp, ts=ts, lam_init=lam_init)
    return pl.pallas_call(
        kern,
        grid_spec=grid_spec,
        out_shape=jax.ShapeDtypeStruct((b, ts, width), F32),
        compiler_params=_cparams(("parallel", "arbitrary")),
        name="dec_even",
    )(table, qt, *([pool_k] * pp), *([pool_v] * pp), k_new, v_new, lv, subln)


def _compress_kernel(t_ref, w1_ref, w1g_ref, pe_ref, w2_ref, g_ref, o_ref):
    kind = pl.program_id(1)
    n = t_ref.shape[1] // CMP_STRIDE
    a1 = [None] * N_NSA_KV
    a2 = [None] * N_NSA_KV
    for p in range(CMP_STRIDE):
        rows = t_ref[0, pl.ds(p, n, stride=CMP_STRIDE), :].astype(BF16)
        for g in range(N_NSA_KV):
            d1 = _dot(rows, w1g_ref[0, g, p])
            d2 = _dot(rows, w1g_ref[0, g, CMP_STRIDE + p])
            a1[g] = d1 if a1[g] is None else a1[g] + d1
            a2[g] = d2 if a2[g] is None else a2[g] + d2
    pe = _dot(jnp.broadcast_to(pe_ref[0], (SUBLANES, CMP_LEN * HD)).astype(BF16), w1_ref[0])[0:1]
    for g in range(N_NSA_KV):
        nxt = jnp.where(_iota(a2[g].shape, 0) < n - 1, pltpu.roll(a2[g], n - 1, 0), 0.0)
        hid = a1[g] + nxt + pe
        hid = hid * (1.0 / (1.0 + jnp.exp(-hid)))
        out = _dot(hid.astype(BF16), w2_ref[0])
        normed = _rms_rows(out, g_ref[...])
        o_ref[0, 0, g] = jnp.where(kind == 0, normed, out)


def _compress(t, slab0, w1, pe_flat, w2, gain):
    b, lk, _ = t.shape
    g = N_NSA_KV
    n = lk // CMP_STRIDE
    pieces = w1.reshape(2, CMP_LEN, HD, w1.shape[2])
    w1g = jnp.stack([jnp.pad(pieces, ((0, 0), (0, 0), (HD * gi, HD * (g - 1 - gi)), (0, 0))) for gi in range(g)],
                    axis=1)
    return pl.pallas_call(
        _compress_kernel,
        grid=(b, 2),
        in_specs=[
            pl.BlockSpec((1, lk, LANES), lambda b_, k: (b_, 0, slab0 + k)),
            pl.BlockSpec((1,) + w1.shape[1:], lambda b_, k: (k, 0, 0)),
            pl.BlockSpec((1,) + w1g.shape[1:], lambda b_, k: (k, 0, 0, 0, 0)),
            pl.BlockSpec((1,) + pe_flat.shape[1:], lambda b_, k: (k, 0, 0)),
            pl.BlockSpec((1,) + w2.shape[1:], lambda b_, k: (k, 0, 0)),
            pl.BlockSpec(gain.shape, lambda b_, k: (0, 0)),
        ],
        out_specs=pl.BlockSpec((1, 1, g, n, HD), lambda b_, k: (b_, k, 0, 0, 0)),
        out_shape=jax.ShapeDtypeStruct((b, 2, g, n, HD), F32),
        compiler_params=_cparams(("parallel", "arbitrary")),
        name="compress",
    )(t, w1, w1g, pe_flat, w2, gain)


def _nsa_cmp_kernel(q_ref, kc_ref, vc_ref, o_ref, sel_ref, *, tq, q_off, n_sel):
    i = pl.program_id(2)
    q = q_ref[0, 0, 0].astype(BF16)
    kc = kc_ref[0, 0]
    vc = vc_ref[0, 0]
    ncol = kc.shape[0]
    nbp = ncol // CMP_PER_SLC
    q_lo = q_off + i * tq
    qpos = q_lo + _iota((tq, ncol), 0)
    col = _iota((tq, ncol), 1)
    jj = jnp.zeros_like(col)
    for u in range(1, CMP_PER_SLC):
        jj = jj + jnp.where(col >= u * nbp, 1, 0)
    c_end = (col - jj * nbp) * SLC_BLOCK + jj * CMP_STRIDE + (CMP_LEN - 1)
    maskf = _tile_rows(jnp.where(c_end <= qpos, 1.0, 0.0), NSA_GROUP)
    keep = maskf > 0.5
    s = jnp.where(keep, _dot_nt(q, kc), NEG)
    m = jnp.max(s, axis=1, keepdims=True)
    p = jnp.where(keep, jnp.exp(s - m), 0.0)
    p = p / jnp.maximum(jnp.sum(p, axis=1, keepdims=True), 1e-30)
    o_ref[0, 0, 0] = _dot(p.astype(BF16), vc)
    pg = p[0:tq]
    for r in range(1, NSA_GROUP):
        pg = pg + p[r * tq:(r + 1) * tq]
    imp = pg[:, 0:nbp]
    for j in range(1, CMP_PER_SLC):
        imp = imp + pg[:, j * nbp:(j + 1) * nbp]
    blk = _iota((tq, nbp), 1)
    qp = q_lo + _iota((tq, nbp), 0)
    q_blk = qp >> SLC_SHIFT
    visible = blk * SLC_BLOCK <= qp
    forced = jnp.logical_or(blk == 0, jnp.logical_and(blk <= q_blk, blk > q_blk - N_LOCAL))
    score = jnp.where(visible, jnp.where(forced, FORCE, imp), NEG)
    sel0 = jnp.zeros((tq, nbp), F32)
    rounds = n_sel
    if n_sel > N_LOCAL + 1:
        pre = jnp.logical_and(forced, visible)
        sel0 = jnp.where(pre, 1.0, 0.0)
        score = jnp.where(pre, -jnp.inf, score)
        rounds = n_sel - (N_LOCAL + 1)
    rc = min(tq, SEL_CHUNK_ROWS)
    blkf = _iota((rc, nbp), 1).astype(F32)
    scores = [score[c * rc:(c + 1) * rc] for c in range(tq // rc)]
    sels = [sel0[c * rc:(c + 1) * rc] for c in range(tq // rc)]
    for _ in range(rounds):
        for c in range(len(scores)):
            top = jnp.max(scores[c], axis=1, keepdims=True)
            idx = jnp.min(jnp.where(scores[c] == top, blkf, 1e9), axis=1, keepdims=True)
            pick = blkf == idx
            sels[c] = jnp.where(pick, 1.0, sels[c])
            scores[c] = jnp.where(pick, -jnp.inf, scores[c])
    for c in range(len(scores)):
        sel_ref[0, 0, c * rc:(c + 1) * rc, :] = sels[c]


def _nsa_cmp(q, kc, vc, *, tq, q_off, n_sel):
    b, g, nq, rows, _ = q.shape
    ncol = kc.shape[2]
    nbp = ncol // CMP_PER_SLC
    kern = functools.partial(_nsa_cmp_kernel, tq=tq, q_off=q_off, n_sel=n_sel)
    return pl.pallas_call(
        kern,
        grid=(b, g, nq),
        in_specs=[
            pl.BlockSpec((1, 1, 1, rows, HD), lambda b_, g_, i: (b_, g_, i, 0, 0)),
            pl.BlockSpec((1, 1, ncol, HD), lambda b_, g_, i: (b_, g_, 0, 0)),
            pl.BlockSpec((1, 1, ncol, HD), lambda b_, g_, i: (b_, g_, 0, 0)),
        ],
        out_specs=[
            pl.BlockSpec((1, 1, 1, rows, HD), lambda b_, g_, i: (b_, g_, i, 0, 0)),
            pl.BlockSpec((1, 1, tq, nbp), lambda b_, g_, i: (b_, g_, i, 0)),
        ],
        out_shape=[
            jax.ShapeDtypeStruct((b, g, nq, rows, HD), F32),
            jax.ShapeDtypeStruct((b, g, nq * tq, nbp), F32),
        ],
        compiler_params=_cparams(("parallel", "parallel", "arbitrary")),
        name="nsa_cmp",
    )(q, kc, vc)


def _dsa_kernel(qi_ref, wi_ref, qd_ref, kk_ref, o_ref, key_ref, *, tq, tk, q_off, n_top, offs):
    i = pl.program_id(1)
    qi = qi_ref[0, 0]
    qd = qd_ref[0, 0]
    w = wi_ref[0] * (N_IDX ** -0.5)
    n_kt = kk_ref.shape[1] // tk
    kd_off, ki_off, vd_off = offs

    def cols(ref, k0, off):
        return ref[0, pl.ds(k0, tk), :][:, off:off + HD].astype(BF16)

    q_lo = q_off + i * tq
    hi = jnp.minimum(lax.div(q_lo + tq - 1, tk) + 1, n_kt)
    qpos = q_lo + _iota((tq, tk), 0)
    n_chain = N_DSA if tq >= MIN_CHAIN_ROWS else 1
    cr = N_DSA * tq // n_chain

    def visible(j):
        return (j * tk + _iota((tq, tk), 1)) <= qpos

    def score_tile(j, _):
        k0 = pl.multiple_of(j * tk, tk)
        k = cols(kk_ref, k0, ki_off)
        tot = None
        for c in range(n_chain):
            sc = jnp.maximum(_dot_nt(qi[c * cr:(c + 1) * cr].astype(BF16), k), 0.0)
            for u in range(cr // tq):
                h = c * (cr // tq) + u
                term = w[:, h:h + 1] * sc[u * tq:(u + 1) * tq]
                tot = term if tot is None else tot + term
        tot = jnp.where(visible(j), tot, NEG)
        bits = pltpu.bitcast(tot, jnp.int32)
        key = jnp.where(bits < 0, bits ^ jnp.int32(0x7FFFFFFF), bits)
        key_ref[:, pl.ds(k0, tk)] = jnp.where(tot == 0.0, 0, key)
        return 0

    _unrolled_loop(0, hi, score_tile, 0)

    def count_ge(c):
        def body(j, acc):
            blk = key_ref[:, pl.ds(pl.multiple_of(j * tk, tk), tk)]
            hit = jnp.where(blk >= c, 1.0, 0.0)
            part = hit[:, 0:LANES]
            for u in range(1, tk // LANES):
                part = part + hit[:, u * LANES:(u + 1) * LANES]
            return acc + part
        acc = _unrolled_loop(0, hi, body, jnp.zeros((tq, LANES), F32))
        return jnp.sum(acc, axis=1, keepdims=True)

    kf = float(n_top)
    tau = jnp.where(count_ge(jnp.zeros((tq, 1), jnp.int32)) >= kf, 0, INT_MIN).astype(jnp.int32)

    def bit_body(t, tau):
        cand = tau + jnp.left_shift(jnp.int32(1), 30 - t)
        return jnp.where(count_ge(cand) >= kf, cand, tau)

    tau = lax.fori_loop(0, 31, bit_body, tau)
    need = kf - count_ge(tau + 1)
    before = jnp.where(_iota((tk, tk), 0) < _iota((tk, tk), 1), 1.0, 0.0).astype(BF16)

    def attend(j, carry):
        state, n_eq = carry
        k0 = pl.multiple_of(j * tk, tk)
        key = key_ref[:, pl.ds(k0, tk)]
        eqf = jnp.where(key == tau, 1.0, 0.0)
        rank = n_eq + _dot(eqf.astype(BF16), before)
        kept = jnp.logical_or(key > tau, jnp.logical_and(key == tau, rank < need))
        bias = jnp.where(jnp.logical_and(kept, visible(j)), 0.0, NEG)
        k = cols(kk_ref, k0, kd_off)
        v1 = _with_ones(cols(kk_ref, k0, vd_off))
        bias = _tile_rows(bias, cr // tq)
        new = []
        for c in range(n_chain):
            s = _dot_nt(qd[c * cr:(c + 1) * cr].astype(BF16), k) + bias
            new.append(_online_step_ones(s, v1, *state[c]))
        return tuple(new), n_eq + jnp.sum(eqf, axis=1, keepdims=True)

    init = (tuple(_online_init_ones(cr) for _ in range(n_chain)), jnp.zeros((tq, 1), F32))
    state, _ = _unrolled_loop(0, hi, attend, init)
    for c in range(n_chain):
        acc = state[c][1]
        o_ref[0, 0, c * cr:(c + 1) * cr, :] = acc[:, :HD] / acc[:, HD:HD + 1]


def _dsa(qi, wi, qd, kk, *, width, blk, offs, tq, tk, q_off, n_top):
    b, nq, rows, _ = qi.shape
    lk = kk.shape[1]
    assert lk % tk == 0
    kern = functools.partial(_dsa_kernel, tq=tq, tk=tk, q_off=q_off, n_top=n_top, offs=offs)
    qspec = pl.BlockSpec((1, 1, rows, HD), lambda b_, i: (b_, i, 0, 0))
    return pl.pallas_call(
        kern,
        grid=(b, nq),
        in_specs=[qspec, pl.BlockSpec((1, tq, N_IDX), lambda b_, i: (b_, i, 0)), qspec,
                  pl.BlockSpec((1, lk, width), lambda b_, i: (b_, 0, blk))],
        out_specs=pl.BlockSpec((1, 1, rows, HD), lambda b_, i: (b_, i, 0, 0)),
        out_shape=jax.ShapeDtypeStruct((b, nq, rows, HD), F32),
        scratch_shapes=[pltpu.VMEM((tq, lk), jnp.int32)],
        compiler_params=_cparams(("parallel", "arbitrary")),
        name="dsa",
    )(qi, wi, qd, kk)


def _gather_kernel(*refs, pp, n_steps):
    pool_refs = refs[1:1 + pp]
    new_ref, o_ref = refs[1 + pp], refs[2 + pp]
    s = pl.program_id(1)

    def page(ref):
        mid = ref.shape[2:-1]
        if not mid:
            return ref[0]
        pieces = [ref[(0, slice(None)) + idx + (slice(None),)] for idx in np.ndindex(*mid)]
        return jnp.concatenate(pieces, axis=1)

    @pl.when(s < n_steps)
    def _():
        for u in range(pp):
            o_ref[0, u * PAGE:(u + 1) * PAGE, :] = page(pool_refs[u])

    @pl.when(s >= n_steps)
    def _():
        o_ref[...] = new_ref[...]


def _page_gather(pool, table, new, *, pp):
    b, n_pages = table.shape
    tail = pool.shape[2:]
    width = int(np.prod(tail))
    assert n_pages % pp == 0 and new.shape[1:] == (pp * PAGE, width)
    n_steps = n_pages // pp

    def page_map(u):
        return lambda b_, s, t: (t[b_, jnp.minimum(s, n_steps - 1) * pp + u], 0) + (0,) * len(tail)

    kern = functools.partial(_gather_kernel, pp=pp, n_steps=n_steps)
    grid_spec = pltpu.PrefetchScalarGridSpec(
        num_scalar_prefetch=1,
        grid=(b, n_steps + 1),
        in_specs=[pl.BlockSpec((1, PAGE) + tail, page_map(u)) for u in range(pp)]
        + [pl.BlockSpec((1, pp * PAGE, width), lambda b_, s, t: (b_, 0, 0))],
        out_specs=pl.BlockSpec((1, pp * PAGE, width), lambda b_, s, t: (b_, s, 0)),
    )
    return pl.pallas_call(
        kern,
        grid_spec=grid_spec,
        out_shape=jax.ShapeDtypeStruct((b, (n_pages + pp) * PAGE, width), pool.dtype),
        compiler_params=_cparams(("parallel", "arbitrary")),
        name="page_gather",
    )(table, *([pool] * pp), new)


def _rope_tables(pos):
    half = HD // 2
    inv = ROPE_THETA ** (-jnp.arange(half, dtype=F32) / half)
    ang = pos.astype(F32)[:, None] * inv[None, :]
    cos, sin = jnp.cos(ang), jnp.sin(ang)
    cos128 = jnp.tile(jnp.concatenate([cos, cos], axis=1), (1, LANES // HD))
    sin128 = jnp.tile(jnp.concatenate([-sin, sin], axis=1), (1, LANES // HD))
    return cos128, sin128


def _tile_gain(g):
    return jnp.tile(g.reshape(1, HD), (1, LANES // HD))


def _heads(a, b, t, h, d, scale=None):
    a = a.reshape(b, t, h, d)
    if scale is not None:
        a = a * scale
    return a.transpose(0, 2, 1, 3)


def _stack_q(a, tq):
    b, hk, r, t, d = a.shape
    return a.reshape(b, hk, r, t // tq, tq, d).transpose(0, 1, 3, 2, 4, 5).reshape(b, hk, t // tq, r * tq, d)


def _unstack_q(a, r, tq):
    b, hk, nq, _, d = a.shape
    a = a.reshape(b, hk, nq, r, tq, d).transpose(0, 2, 4, 1, 3, 5)
    return a.reshape(b * nq * tq, hk * r * d)


def _pad_rows(a, rows):
    return jnp.pad(a, ((0, 0), (0, rows - a.shape[1])) + ((0, 0),) * (a.ndim - 2))


def _even_mixer(x2, b, t, q_off, cs, past, prm, cfg):
    n = b * t
    program = [
        (0, SB_W, None, None, [(0, False)]),
        (3 * SB_W, DF_W, 0, None, [(SB_W, True)]),
        (SB_W, SB_W, None, None, [(SB_W + DF_W, False)]),
        (3 * SB_W + DF_W, DF_W, 1, None, [(2 * SB_W + DF_W, True)]),
        (2 * SB_W, SB_W, None, None, [(2 * (SB_W + DF_W), False)]),
        (3 * SB_W + 2 * DF_W, DF_W, None, None, [(3 * SB_W + 2 * DF_W, False)]),
    ]
    gains = jnp.concatenate([_tile_gain(prm["df_qk_gain"][0]), _tile_gain(prm["df_qk_gain"][1])], axis=0)
    proj = _mm([x2, cs[0], cs[1]], [prm["g0"], gains], prm["ev_w_in"], _lhs_norm, program, 3 * (SB_W + DF_W),
               tm=cfg["tm"], rope_idx=(1, 2), gains_idx=1)
    mw = SB_W + DF_W
    new_k = proj[:, mw:2 * mw].reshape(b, t, mw)
    new_v = proj[:, 2 * mw:3 * mw].reshape(b, t, mw)
    lam_init = 0.8 - 0.6 * math.exp(-0.3 * prm["layer"])
    lv, subln = prm["df_lambda"], prm["df_subln_gain"].reshape(1, 2 * HD)
    d = x2.shape[1]
    if past is None:
        proj3 = proj.reshape(b, t, 3 * mw)
        ns = SB_W // LANES
        o_sb = _sb(proj3, t=cfg["t_even"], q_slab=0, k_slab=2 * ns, v_slab=4 * ns, n_slabs=ns)
        o_df = _df(proj3, lv, subln, t=cfg["t_even"], q_slab=ns, k_slab=3 * ns, v_slab=5 * ns, n_slabs=ns,
                   lam_init=lam_init)
        return _mm([o_sb.reshape(n, SB_W), o_df.reshape(n, DF_W), x2], [], prm["ev_w_out"], _lhs_cat2,
                   [(0, d, None, None, [(0, False)])], d, tm=cfg["tm"], res_idx=2), new_k, new_v
    pool_k, pool_v, table = past
    qcat = proj[:, :mw].reshape(b, t, mw).transpose(0, 2, 1) * SCALE
    qt = (jnp.tile(qcat, (1, 1, LANES // t)) * _dec_even_mask(t)).astype(BF16)
    o = _dec_even(qt, pool_k, pool_v, table, _pad_rows(new_k, PAGE), _pad_rows(new_v, PAGE), lv, subln,
                  pp=cfg["pp"], ts=t, lam_init=lam_init)
    return _mm([o.reshape(n, mw), x2], [], prm["ev_w_out"], _lhs_plain,
               [(0, d, None, None, [(0, False)])], d, tm=cfg["tm"], res_idx=1), new_k, new_v


def _dec_even_mask(ts):
    f = np.arange(SB_W + DF_W)[:, None]
    c = np.arange(LANES)[None, :]
    half = LANES // 2
    sb = (f < SB_W) & (c < half) & (f // HD == c // ts)
    df = (f >= SB_W) & (c >= half) & ((f - SB_W) // HD == (c - half) // ts)
    return jnp.asarray((sb | df).astype(np.float32))


def _odd_layout():
    widths = (N_NSA * HD, 128, 128, 128, 128, 128, 128, N_NSA * 3, N_DSA * HD, HD, HD, N_IDX * HD, HD, N_IDX)
    offs = np.concatenate([[0], np.cumsum(widths)])
    (q_n, k_c, v_c, k_s, v_s, k_w, v_w, gate, q_d, k_d, v_d, q_i, k_i, w_i) = [
        (int(offs[j]), int(offs[j + 1])) for j in range(len(widths))]
    pieces = [q_n, (k_c[0], v_w[1]), q_d, q_i, k_d, k_i, v_d, w_i, ("pad", HD - N_IDX), gate,
              ("pad", LANES - N_NSA * 3)]
    program = [
        (0, 512, 0, None, [(0, False), (512, True)]),
        (512, 128, None, None, [(1024, False)]),
        (640, 128, None, None, [(1152, False)]),
        (768, 128, 1, None, [(1280, True)]),
        (896, 128, None, None, [(1408, False)]),
        (1024, 128, 2, None, [(1536, True)]),
        (1152, 128, None, None, [(1664, False)]),
        (1280, 512, 3, None, [(1792, True)]),
        (1792, 512, None, None, [(2304, True)]),
        (2304, 128, 4, None, [(2816, True)]),
        (2432, 128, None, None, [(2944, False)]),
        (2560, 128, None, "sigmoid", [(3072, False)]),
    ]
    return pieces, program, 3200


def _permute_cols(w, pieces):
    cols = []
    for p in pieces:
        if p[0] == "pad":
            cols.append(jnp.zeros((w.shape[0], p[1]), w.dtype))
        else:
            cols.append(w[:, p[0]:p[1]])
    return jnp.concatenate(cols, axis=1)


def _odd_mixer(x2, b, t, t_real, q_off, cs, past, prm, cfg):
    n = b * t
    pieces, program, out_cols = _odd_layout()
    ng, dg = prm["nsa_qk_gain"], prm["dsa_qk_gain"]
    gains = jnp.concatenate([
        _tile_gain(ng[0]), _tile_gain(ng[2]), _tile_gain(ng[3]), _tile_gain(dg[0]),
        jnp.concatenate([dg[1], dg[2]]).reshape(1, LANES)], axis=0)
    w_in = _permute_cols(prm["od_w_in"], pieces)
    proj = _mm([x2, cs[0], cs[1]], [prm["g0"], gains], w_in, _lhs_norm, program, out_cols,
               tm=cfg["tm"], rope_idx=(1, 2), gains_idx=1)
    g = N_NSA_KV
    new_nsa = proj[:, 1024:1536].reshape(b, t, 4 * g * HD)
    new_win = proj[:, 1536:1792].reshape(b, t, 2, g, HD)
    new_dsa = jnp.concatenate([proj[:, 2816:2880], proj[:, 2944:3008], proj[:, 2880:2944]], axis=1).reshape(b, t, 3 * HD)
    w_i = proj[:, 3008:3008 + N_IDX].reshape(b, t, N_IDX)
    gate = proj[:, 3072:3072 + N_NSA * 3].reshape(n, N_NSA, 3)
    if past is None:
        proj3 = proj.reshape(b, t, out_cols)
        nsa_src, win_src = (proj3, 8), (proj3, 12)
        dsa_src = dict(kk=proj3, width=2 * LANES, blk=11, offs=(0, HD, 2 * HD))
        win_off = 0
        new_state = new_win[:, -min(WINDOW, t):]
        l_real = t
    else:
        pool_nsa, pool_dsa, state, table = past
        nsa_buf = _page_gather(pool_nsa, table, _pad_rows(new_nsa, cfg["pp"] * PAGE), pp=cfg["pp"])
        dsa_buf = _page_gather(pool_dsa, table, _pad_rows(new_dsa, cfg["pp"] * PAGE), pp=cfg["pp"])
        wb = state.shape[1]
        win_buf = _pad_rows(jnp.concatenate([state, new_win], axis=1), wb + cfg["tk_win"])
        nsa_src, win_src = (nsa_buf, 0), (win_buf.reshape(b, wb + cfg["tk_win"], 2 * g * HD), 0)
        dsa_src = dict(kk=dsa_buf, width=3 * HD, blk=0, offs=(0, 2 * HD, HD))
        win_off = q_off - wb
        new_state = jnp.concatenate([state, new_win[:, :t_real]], axis=1)[:, -wb:]
        l_real = table.shape[1] * PAGE + t_real
    lk = nsa_src[0].shape[1]
    tq = cfg["tq"]

    n_cmp = lk // CMP_STRIDE
    pe_flat = prm["cmp_pe"].reshape(2, 1, CMP_LEN * HD)
    cmp = _compress(nsa_src[0], nsa_src[1], prm["cmp_w1"], pe_flat, prm["cmp_w2"], ng[1].reshape(1, HD))
    nb = n_cmp // CMP_PER_SLC
    nbp = -(-nb // LANES) * LANES
    cmp = cmp.reshape(b, 2, g, nb, CMP_PER_SLC, HD).transpose(0, 1, 2, 4, 3, 5)
    cmp = jnp.pad(cmp, ((0, 0),) * 4 + ((0, nbp - nb), (0, 0))).reshape(b, 2, g, CMP_PER_SLC * nbp, HD).astype(BF16)

    def group_q(cols, rows):
        a = _heads(cols, b, t, N_NSA, HD, SCALE).reshape(b, g, NSA_GROUP, t, HD)
        return _stack_q(a, rows)

    q_n = group_q(proj[:, 0:512], tq)
    q_r = group_q(proj[:, 512:1024], cfg["tq_slc"])
    n_blk = -(-l_real // SLC_BLOCK)
    o_c, sel = _nsa_cmp(q_n, cmp[:, 0], cmp[:, 1], tq=tq, q_off=q_off, n_sel=min(N_SLC, n_blk))

    tq_s = cfg["tq_slc"]
    o_s = _flash(q_r, nsa_src[0], kv_blk=(nsa_src[1] + 2) // 2, reps=NSA_GROUP, tq=tq_s,
                 tk=cfg["tk"], q_off=q_off, k_off=0, mode="causal", bm=sel)
    o_w = _flash(q_r, win_src[0], kv_blk=win_src[1] // 2, reps=NSA_GROUP, tq=tq_s,
                 tk=cfg["tk_win"], q_off=q_off, k_off=win_off, mode="window")

    tq_d = cfg["tq_dsa"]
    q_d = _stack_q(_heads(proj[:, 1792:2304], b, t, N_DSA, HD, SCALE)[:, None], tq_d)[:, 0]
    q_i = _stack_q(_heads(proj[:, 2304:2816], b, t, N_IDX, HD, IDX_SCALE)[:, None], tq_d)[:, 0]
    o_d = _dsa(q_i, w_i, q_d, tq=tq_d, tk=cfg["tk"], q_off=q_off, n_top=min(DSA_TOPK_MAX, l_real // 4), **dsa_src)

    o_c = _unstack_q(o_c, NSA_GROUP, tq)
    o_s, o_w = (_unstack_q(o, NSA_GROUP, tq_s) for o in (o_s, o_w))
    o_d = _unstack_q(o_d[:, None], N_DSA, tq_d)
    gfull = [jnp.repeat(gate[:, :, j], HD, axis=1) for j in range(3)]
    d = x2.shape[1]
    out = _mm([o_c, o_s, o_w, o_d] + gfull + [x2], [], prm["od_w_out"], _lhs_odd,
              [(0, d, None, None, [(0, False)])], d, tm=cfg["tm"], res_idx=7)
    return out, new_nsa, new_dsa, new_state


def _cross(x2, b, t, mem_k, mem_v, prm, cfg):
    d = x2.shape[1]
    xw = N_XH * HD
    q = _mm([x2], [prm["g1"], _tile_gain(prm["x_gq"])], prm["x_wq"], _lhs_norm,
            [(0, xw, 0, None, [(0, False)])], xw, tm=cfg["tm"], gains_idx=1)
    tq = cfg["tq_cross"]
    qh = _stack_q(_heads(q, b, t, N_XH, HD, SCALE)[:, :, None], tq)
    o = _flash(qh, mem_k, mem_v, reps=1, tq=tq, tk=mem_k.shape[2], q_off=0, k_off=0, mode="full")
    o = _unstack_q(o, 1, tq)
    return _mm([o, x2], [], prm["x_wo"], _lhs_plain, [(0, d, None, None, [(0, False)])], d, tm=cfg["tm"], res_idx=1)


def _memory_kv(mem2, prm):
    xw = N_XH * HD
    w = jnp.concatenate([prm["x_wk"], prm["x_wv"]], axis=1)
    return _mm([mem2], [_tile_gain(prm["x_gk"])], w, _lhs_plain,
               [(0, xw, 0, None, [(0, False)]), (xw, xw, None, None, [(xw, False)])], 2 * xw,
               tm=min(256, mem2.shape[0]), gains_idx=0)


def _run_group(x, q_off, t_real, mem_kvs, pasts, layers, cfg):
    b, t, d = x.shape
    x2 = x.reshape(b * t, d)
    pos = q_off + jnp.arange(t, dtype=jnp.int32)
    cos, sin = _rope_tables(pos)
    cs = (jnp.tile(cos, (b, 1)), jnp.tile(sin, (b, 1)))
    outs = {}
    for li, prm in enumerate(layers):
        if li % 2 == 0:
            x2, nk, nv = _even_mixer(x2, b, t, q_off, cs, pasts[li], prm, cfg)
            outs["ek"], outs["ev"] = nk, nv
        else:
            x2, nn, nd, nw = _odd_mixer(x2, b, t, t_real, q_off, cs, pasts[li], prm, cfg)
            outs["on"], outs["od"], outs["ow"] = nn, nd, nw
        x2 = _cross(x2, b, t, mem_kvs[li][0], mem_kvs[li][1], prm, cfg)
        if li % 2 == 0:
            x2 = _ffn(x2, prm["g2"], prm["router"], prm["w1"], prm["w3"], prm["w2"], tm=cfg["tm_ffn"],
                      tf=cfg["tf"], routed=False)
        elif "moe_cap" in cfg and x2.shape[0] % cfg["tm_moe"] == 0:
            x2 = _moe(x2, prm["g2"], prm["router"], prm["w1"], prm["w3"], prm["w2"], tm=cfg["tm_moe"],
                      tf=cfg["tf"], cap=cfg["moe_cap"])
        else:
            x2 = _ffn(x2, prm["g2"], prm["router"], prm["w1"], prm["w3"], prm["w2"], tm=cfg["tm_ffn"],
                      tf=cfg["tf"], routed=True)
    return x2.reshape(b, t, d), outs


def kernel(x_prompt, x_sample, mem_prompt, cache_even_k, cache_even_v, cache_odd_nsa, cache_odd_dsa, state_odd_win, cache_mem, page_table, norm_gain, ev_w_in, ev_w_out, df_qk_gain, df_lambda, df_subln_gain, ffn_w1, ffn_w3, ffn_w2, od_w_in, od_w_out, nsa_qk_gain, cmp_pe, cmp_w1, cmp_w2, dsa_qk_gain, moe_router, moe_w1, moe_w3, moe_w2, x_wq, x_wk, x_wv, x_wo, x_qk_gain):
    depth = norm_gain.shape[0]
    bp, tp, d = x_prompt.shape
    bs, ts, _ = x_sample.shape
    n_mem = mem_prompt.shape[1]
    xw = N_XH * HD

    layers = []
    for l in range(depth):
        i = l // 2
        prm = {
            "layer": l,
            "g0": norm_gain[l, 0].reshape(1, d), "g1": norm_gain[l, 1].reshape(1, d), "g2": norm_gain[l, 2].reshape(1, d),
            "x_wq": x_wq[l].astype(BF16), "x_wk": x_wk[l].astype(BF16), "x_wv": x_wv[l].astype(BF16),
            "x_wo": x_wo[l].astype(BF16), "x_gq": x_qk_gain[l, 0], "x_gk": x_qk_gain[l, 1],
        }
        if l % 2 == 0:
            prm.update({
                "ev_w_in": ev_w_in[i].astype(BF16), "ev_w_out": ev_w_out[i].astype(BF16),
                "df_qk_gain": df_qk_gain[i], "df_lambda": df_lambda[i], "df_subln_gain": df_subln_gain[i],
                "router": jnp.zeros((SUBLANES, LANES), F32),
                "w1": ffn_w1[i][None].astype(BF16), "w3": ffn_w3[i][None].astype(BF16), "w2": ffn_w2[i][None].astype(BF16),
            })
        else:
            prm.update({
                "od_w_in": od_w_in[i].astype(BF16), "od_w_out": od_w_out[i].astype(BF16),
                "nsa_qk_gain": nsa_qk_gain[i], "dsa_qk_gain": dsa_qk_gain[i],
                "cmp_pe": cmp_pe[i], "cmp_w1": cmp_w1[i].astype(BF16), "cmp_w2": cmp_w2[i].astype(BF16),
                "router": jnp.pad(moe_router[i], ((0, 0), (0, LANES - N_EXPERTS))),
                "w1": moe_w1[i].astype(BF16), "w3": moe_w3[i].astype(BF16), "w2": moe_w2[i].astype(BF16),
            })
        layers.append(prm)

    def mem_heads(kv, b):
        k = kv[:, :, 0].transpose(0, 2, 1, 3).astype(BF16)
        v = kv[:, :, 1].transpose(0, 2, 1, 3).astype(BF16)
        return k, v

    mem2 = mem_prompt.reshape(bp * n_mem, d)
    mem_p = [_memory_kv(mem2, layers[l]).reshape(bp, n_mem, 2, N_XH, HD) for l in range(depth)]
    ff = ffn_w1.shape[2]
    tf = ff // 2 if (ff // 2) % LANES == 0 else ff
    cfg_p = {"tm": 256, "tq": 128, "tk": 256, "t_even": 256, "tk_win": 256, "tm_ffn": 512, "tf": tf,
             "tm_moe": 1024, "moe_cap": 320}
    cfg_p["tq"] = min(cfg_p["tq"], tp)
    cfg_p["tq_cross"] = 1024 if tp % 1024 == 0 else cfg_p["tq"]
    cfg_p["tq_slc"] = 256 if tp % 256 == 0 else cfg_p["tq"]
    cfg_p["tq_dsa"] = cfg_p["tq"]
    y_prompt, op = _run_group(x_prompt, 0, tp, [mem_heads(m, bp) for m in mem_p], [None] * depth, layers, cfg_p)
    p_mem = jnp.stack(mem_p)

    n_past = page_table.shape[1] * cache_even_k.shape[2]
    ts_pad = -(-ts // SUBLANES) * SUBLANES
    xs = _pad_rows(x_sample, ts_pad)
    pasts = []
    for l in range(depth):
        i = l // 2
        if l % 2 == 0:
            pasts.append((cache_even_k[i], cache_even_v[i], page_table))
        else:
            pn = cache_odd_nsa[i]
            pd = cache_odd_dsa[i]
            pasts.append((pn.reshape(pn.shape[0], pn.shape[1], -1), pd.reshape(pd.shape[0], pd.shape[1], -1),
                          state_odd_win[i], page_table))
    cfg_s = {"tm": bs * ts_pad, "tq": ts_pad, "tq_cross": ts_pad, "tq_slc": ts_pad, "tq_dsa": ts_pad, "tk": 512,
             "pp": 4, "tk_win": 128,
             "tm_ffn": bs * ts_pad, "tf": tf}
    y_s, os_ = _run_group(xs, n_past, ts, [mem_heads(cache_mem[l], bs) for l in range(depth)], pasts, layers, cfg_s)
    y_sample = y_s[:, :ts]

    g = N_NSA_KV
    return (
        y_prompt, y_sample,
        op["ek"][None], op["ev"][None],
        op["on"].reshape(1, bp, tp, 4, g, HD), op["od"].reshape(1, bp, tp, 3, HD),
        op["ow"][None], p_mem,
        os_["ek"][:, :ts][None], os_["ev"][:, :ts][None],
        os_["on"][:, :ts].reshape(1, bs, ts, 4, g, HD), os_["od"][:, :ts].reshape(1, bs, ts, 3, HD),
        os_["ow"][None],
    )
```

```python
import functools
import math

import jax
import jax.numpy as jnp
import numpy as np
from jax import lax
from jax.experimental import pallas as pl
from jax.experimental.pallas import tpu as pltpu

F32 = jnp.float32
BF16 = jnp.bfloat16

HD = 64
N_SB = 8
N_DF = 4
N_NSA = 8
N_NSA_KV = 2
NSA_GROUP = N_NSA // N_NSA_KV
N_DSA = 8
N_IDX = 8
N_XH = 4
N_EXPERTS = 8
ROPE_THETA = 10000.0
CMP_LEN = 32
CMP_STRIDE = 16
SLC_BLOCK = 64
CMP_PER_SLC = SLC_BLOCK // CMP_STRIDE
N_SLC = 16
N_LOCAL = 2
WINDOW = 512
DSA_TOPK_MAX = 256
EPS = 1e-6
NEG = -1e30
FORCE = 1e9
SCALE = HD ** -0.5
IDX_SCALE = HD ** -0.5
SB_W = N_SB * HD
HD_SHIFT = 6
SLC_SHIFT = 6
DF_W = N_DF * 2 * HD

LANES = 128
SUBLANES = 8
PAGE = 128
VMEM_LIMIT = 52 * 1024 * 1024
INT_MIN = -2 ** 31
ROW_SPLIT = 2
SB_CUT = 120.0
SEL_CHUNK_ROWS = 16
MIN_CHAIN_ROWS = 64
KEY_UNROLL = 4

_NT = (((1,), (1,)), ((), ()))


def _cparams(sem):
    return pltpu.CompilerParams(dimension_semantics=sem, vmem_limit_bytes=VMEM_LIMIT)


def _dot(a, b):
    return jnp.dot(a, b, preferred_element_type=F32)


def _dot_nt(a, b):
    return lax.dot_general(a, b, _NT, preferred_element_type=F32)


def _split_dot(x, m_bf16):
    hi = x.astype(BF16)
    lo = (x - hi.astype(F32)).astype(BF16)
    return _dot(hi, m_bf16) + _dot(lo, m_bf16)


def _split_dot_rhs(m_bf16, x):
    hi = x.astype(BF16)
    lo = (x - hi.astype(F32)).astype(BF16)
    return _dot(m_bf16, hi) + _dot(m_bf16, lo)


def _iota(shape, dim):
    return lax.broadcasted_iota(jnp.int32, shape, dim)


def _rms_rows(x, g):
    return x * lax.rsqrt(jnp.mean(x * x, axis=-1, keepdims=True) + EPS) * g


def _group_mean_matrix():
    r = _iota((LANES, LANES), 0) >> HD_SHIFT
    c = _iota((LANES, LANES), 1) >> HD_SHIFT
    return jnp.where(r == c, 1.0 / HD, 0.0).astype(BF16)


def _head_norm(y, g, gm):
    ms = _split_dot(y * y, gm)
    return y * lax.rsqrt(ms + EPS) * g


def _rope_slab(y, cos, sin):
    lane = _iota(y.shape, 1)
    first = (lane & (HD - 1)) < (HD // 2)
    swapped = jnp.where(first, pltpu.roll(y, LANES - HD // 2, 1), pltpu.roll(y, HD // 2, 1))
    return y * cos + swapped * sin


def _mm_kernel(*refs, n_rows, n_consts, lhs_fn, program, rope_idx, gains_idx, res_idx):
    rows = refs[:n_rows]
    consts = refs[n_rows:n_rows + n_consts]
    w_ref = refs[n_rows + n_consts]
    o_ref = refs[-1]
    lhs = lhs_fn(rows, consts).astype(BF16)
    gm = _group_mean_matrix() if gains_idx is not None else None
    for (src, width, gain_row, act, outs) in program:
        y_full = _dot(lhs, w_ref[:, src:src + width])
        for s in range(width // LANES):
            y = y_full[:, s * LANES:(s + 1) * LANES]
            if gain_row is not None:
                y = _head_norm(y, consts[gains_idx][gain_row:gain_row + 1, :], gm)
            if act == "sigmoid":
                y = 1.0 / (1.0 + jnp.exp(-y))
            for (dst, rope) in outs:
                z = y
                if rope:
                    z = _rope_slab(y, rows[rope_idx[0]][...], rows[rope_idx[1]][...])
                d0 = dst + s * LANES
                if res_idx is not None:
                    z = z + rows[res_idx][:, d0:d0 + LANES]
                o_ref[:, d0:d0 + LANES] = z


def _mm(rows, consts, w, lhs_fn, program, out_cols, *, tm, rope_idx=None, gains_idx=None, res_idx=None, name="mm"):
    n = rows[0].shape[0]
    assert n % tm == 0
    in_specs = [pl.BlockSpec((tm, r.shape[1]), lambda i: (i, 0)) for r in rows]
    in_specs += [pl.BlockSpec(c.shape, lambda i: (0, 0)) for c in consts]
    in_specs += [pl.BlockSpec(w.shape, lambda i: (0, 0))]
    kern = functools.partial(_mm_kernel, n_rows=len(rows), n_consts=len(consts), lhs_fn=lhs_fn,
                             program=program, rope_idx=rope_idx, gains_idx=gains_idx, res_idx=res_idx)
    return pl.pallas_call(
        kern,
        grid=(n // tm,),
        in_specs=in_specs,
        out_specs=pl.BlockSpec((tm, out_cols), lambda i: (i, 0)),
        out_shape=jax.ShapeDtypeStruct((n, out_cols), F32),
        compiler_params=_cparams(("parallel",)),
        name=name,
    )(*rows, *consts, w)


def _lhs_norm(rows, consts):
    return _rms_rows(rows[0][...], consts[0][...])


def _lhs_plain(rows, consts):
    return rows[0][...]


def _lhs_cat2(rows, consts):
    return jnp.concatenate([rows[0][...], rows[1][...]], axis=1)


def _lhs_odd(rows, consts):
    oc, os_, ow, od = rows[0][...], rows[1][...], rows[2][...], rows[3][...]
    g0, g1, g2 = rows[4][...], rows[5][...], rows[6][...]
    return jnp.concatenate([g0 * oc + g1 * os_ + g2 * ow, od], axis=1)


def _ffn_kernel(x_ref, g_ref, r_ref, w1_ref, w3_ref, w2_ref, o_ref, h_ref, acc_ref, gate_ref, *, routed):
    e = pl.program_id(1)
    f = pl.program_id(2)
    first = jnp.logical_and(e == 0, f == 0)
    last = jnp.logical_and(e == pl.num_programs(1) - 1, f == pl.num_programs(2) - 1)

    @pl.when(first)
    def _():
        x = x_ref[...]
        h = _rms_rows(x, g_ref[...])
        h_ref[...] = h.astype(BF16)
        acc_ref[...] = x
        if routed:
            logits = jnp.dot(h, r_ref[...], preferred_element_type=F32, precision=lax.Precision.HIGHEST)
            col = _iota(logits.shape, 1).astype(F32)
            logits = jnp.where(col < N_EXPERTS, logits, -jnp.inf)
            m1 = jnp.max(logits, axis=1, keepdims=True)
            i1 = jnp.min(jnp.where(logits == m1, col, 1e9), axis=1, keepdims=True)
            rest = jnp.where(col == i1, -jnp.inf, logits)
            m2 = jnp.max(rest, axis=1, keepdims=True)
            i2 = jnp.min(jnp.where(rest == m2, col, 1e9), axis=1, keepdims=True)
            e2 = jnp.exp(m2 - m1)
            g1 = 1.0 / (1.0 + e2)
            g2 = e2 / (1.0 + e2)
            gate_ref[...] = jnp.where(col == i1, g1, 0.0) + jnp.where(col == i2, g2, 0.0)

    def compute(gcol):
        h = h_ref[...]
        u = _dot(h, w1_ref[0])
        v = _dot(h, w3_ref[0])
        a = (u * (1.0 / (1.0 + jnp.exp(-u)))) * v
        y = _dot(a.astype(BF16), w2_ref[0])
        if gcol is not None:
            y = gcol * y
        acc_ref[...] += y

    if routed:
        col = _iota(gate_ref.shape, 1)
        gcol = jnp.sum(jnp.where(col == e, gate_ref[...], 0.0), axis=1, keepdims=True)
        active = jnp.max(gcol) > 0.0

        @pl.when(active)
        def _():
            compute(gcol)
    else:
        compute(None)

    @pl.when(last)
    def _():
        o_ref[...] = acc_ref[...]


def _moe_kernel(x_ref, g_ref, r_ref, w1_ref, w3_ref, w2_ref, o_ref, h_ref, acc_ref, gate_ref, slot_ref,
                slot_t_ref, he_ref, ye_ref, *, cap):
    e = pl.program_id(1)
    f = pl.program_id(2)
    nf = pl.num_programs(2)
    tm = x_ref.shape[0]
    first = jnp.logical_and(e == 0, f == 0)
    last = jnp.logical_and(e == pl.num_programs(1) - 1, f == nf - 1)

    @pl.when(first)
    def _():
        x = x_ref[...]
        h = _rms_rows(x, g_ref[...])
        h_ref[...] = h.astype(BF16)
        acc_ref[...] = x
        logits = jnp.dot(h, r_ref[...], preferred_element_type=F32, precision=lax.Precision.HIGHEST)
        col = _iota(logits.shape, 1).astype(F32)
        logits = jnp.where(col < N_EXPERTS, logits, -jnp.inf)
        m1 = jnp.max(logits, axis=1, keepdims=True)
        i1 = jnp.min(jnp.where(logits == m1, col, 1e9), axis=1, keepdims=True)
        rest = jnp.where(col == i1, -jnp.inf, logits)
        m2 = jnp.max(rest, axis=1, keepdims=True)
        i2 = jnp.min(jnp.where(rest == m2, col, 1e9), axis=1, keepdims=True)
        e2 = jnp.exp(m2 - m1)
        gate_ref[...] = jnp.where(col == i1, 1.0 / (1.0 + e2), 0.0) + jnp.where(col == i2, e2 / (1.0 + e2), 0.0)
        chosen = jnp.where(jnp.logical_or(col == i1, col == i2), 1.0, 0.0)
        earlier = jnp.where(_iota((tm, tm), 1) < _iota((tm, tm), 0), 1.0, 0.0).astype(BF16)
        slot = jnp.where(chosen > 0.5, _dot(earlier, chosen.astype(BF16)), -1.0)
        slot_ref[...] = slot
        slot_t_ref[...] = jnp.transpose(slot)

    lane = _iota((tm, LANES), 1)
    mine = lane == e
    gcol = jnp.sum(jnp.where(mine, gate_ref[...], 0.0), axis=1, keepdims=True)
    slot_col = jnp.sum(jnp.where(mine, slot_ref[...], 0.0), axis=1, keepdims=True)
    slot_row = slot_t_ref[pl.ds(e, 1), :]
    count = jnp.sum(jnp.where(slot_row >= 0.0, 1.0, 0.0)).astype(jnp.int32)
    n_chunk = lax.div(count + (cap - 1), cap)

    def pack(c):
        want = (c * cap + _iota((cap, tm), 0)).astype(F32)
        return jnp.where(slot_row == want, 1.0, 0.0).astype(BF16)

    def unpack(c):
        want = (c * cap + _iota((tm, cap), 1)).astype(F32)
        return jnp.where(slot_col == want, 1.0, 0.0).astype(BF16)

    def expert(rows):
        u = _dot(rows, w1_ref[0])
        v = _dot(rows, w3_ref[0])
        a = (u * (1.0 / (1.0 + jnp.exp(-u)))) * v
        return _dot(a.astype(BF16), w2_ref[0])

    @pl.when(n_chunk > 0)
    def _():
        @pl.when(f == 0)
        def _():
            he_ref[...] = _dot(pack(0), h_ref[...]).astype(BF16)
            ye_ref[...] = expert(he_ref[...])

        @pl.when(f > 0)
        def _():
            ye_ref[...] += expert(he_ref[...])

        @pl.when(f == nf - 1)
        def _():
            acc_ref[...] += gcol * _split_dot_rhs(unpack(0), ye_ref[...])

    def extra(c, _):
        part = expert(_dot(pack(c), h_ref[...]).astype(BF16))
        acc_ref[...] += gcol * _split_dot_rhs(unpack(c), part)
        return 0

    lax.fori_loop(1, n_chunk, extra, 0)

    @pl.when(last)
    def _():
        o_ref[...] = acc_ref[...]


def _moe(x, g, router, w1, w3, w2, *, tm, tf, cap):
    n, d = x.shape
    ne, _, ff = w1.shape
    assert n % tm == 0 and ff % tf == 0
    kern = functools.partial(_moe_kernel, cap=cap)
    return pl.pallas_call(
        kern,
        grid=(n // tm, ne, ff // tf),
        in_specs=[
            pl.BlockSpec((tm, d), lambda i, e, f: (i, 0), pipeline_mode=pl.Buffered(1)),
            pl.BlockSpec((1, d), lambda i, e, f: (0, 0)),
            pl.BlockSpec(router.shape, lambda i, e, f: (0, 0)),
            pl.BlockSpec((1, d, tf), lambda i, e, f: (e, 0, f)),
            pl.BlockSpec((1, d, tf), lambda i, e, f: (e, 0, f)),
            pl.BlockSpec((1, tf, d), lambda i, e, f: (e, f, 0)),
        ],
        out_specs=pl.BlockSpec((tm, d), lambda i, e, f: (i, 0), pipeline_mode=pl.Buffered(1)),
        out_shape=jax.ShapeDtypeStruct((n, d), F32),
        scratch_shapes=[pltpu.VMEM((tm, d), BF16), pltpu.VMEM((tm, d), F32), pltpu.VMEM((tm, LANES), F32),
                        pltpu.VMEM((tm, LANES), F32), pltpu.VMEM((LANES, tm), F32),
                        pltpu.VMEM((cap, d), BF16), pltpu.VMEM((cap, d), F32)],
        compiler_params=_cparams(("parallel", "arbitrary", "arbitrary")),
        name="moe_routed",
    )(x, g, router, w1, w3, w2)


def _ffn(x, g, router, w1, w3, w2, *, tm, tf, routed):
    n, d = x.shape
    ne, _, ff = w1.shape
    assert n % tm == 0 and ff % tf == 0
    kern = functools.partial(_ffn_kernel, routed=routed)
    return pl.pallas_call(
        kern,
        grid=(n // tm, ne, ff // tf),
        in_specs=[
            pl.BlockSpec((tm, d), lambda i, e, f: (i, 0)),
            pl.BlockSpec((1, d), lambda i, e, f: (0, 0)),
            pl.BlockSpec(router.shape, lambda i, e, f: (0, 0)),
            pl.BlockSpec((1, d, tf), lambda i, e, f: (e, 0, f)),
            pl.BlockSpec((1, d, tf), lambda i, e, f: (e, 0, f)),
            pl.BlockSpec((1, tf, d), lambda i, e, f: (e, f, 0)),
        ],
        out_specs=pl.BlockSpec((tm, d), lambda i, e, f: (i, 0)),
        out_shape=jax.ShapeDtypeStruct((n, d), F32),
        scratch_shapes=[pltpu.VMEM((tm, d), BF16), pltpu.VMEM((tm, d), F32), pltpu.VMEM((tm, LANES), F32)],
        compiler_params=_cparams(("parallel", "arbitrary", "arbitrary")),
        name="moe" if routed else "ffn",
    )(x, g, router, w1, w3, w2)


def _tile_rows(m, reps):
    return m if reps == 1 else jnp.concatenate([m] * reps, axis=0)


def _online_step(s, v, m, l, acc):
    m_new = jnp.maximum(m, jnp.max(s, axis=1, keepdims=True))
    p = jnp.exp(s - m_new)
    alpha = jnp.exp(m - m_new)
    return m_new, alpha * l + jnp.sum(p, axis=1, keepdims=True), alpha * acc + _dot(p.astype(BF16), v)


def _with_ones(v):
    return jnp.concatenate([v, jnp.ones_like(v)], axis=1)


def _online_step_ones(s, v1, m, acc):
    m_new = jnp.maximum(m, jnp.max(s, axis=1, keepdims=True))
    p = jnp.exp(s - m_new)
    return m_new, jnp.exp(m - m_new) * acc + _dot(p.astype(BF16), v1)


def _online_init_ones(rows):
    return (jnp.full((rows, 1), NEG, F32), jnp.zeros((rows, 2 * HD), F32))


def _online_init(rows, dv):
    return (jnp.full((rows, 1), NEG, F32), jnp.zeros((rows, 1), F32), jnp.zeros((rows, dv), F32))


def _unrolled_loop(lo, hi, body, init, unroll=KEY_UNROLL):
    shift = unroll.bit_length() - 1
    n_group = (hi - lo) >> shift

    def group(p, st):
        for u in range(unroll):
            st = body(lo + unroll * p + u, st)
        return st

    st = lax.fori_loop(0, n_group, group, init)
    return lax.fori_loop(lo + n_group * unroll, hi, body, st)


def _flash_kernel(*refs, reps, tq, tk, q_off, k_off, mode, has_bm, slab):
    refs = list(refs)
    q_ref = refs.pop(0)
    k_ref = refs.pop(0)
    v_ref = None if slab else refs.pop(0)
    bm_ref = refs.pop(0) if has_bm else None
    o_ref = refs.pop(0)
    i = pl.program_id(2)
    q = q_ref[0, 0, 0]
    dv = HD if slab else v_ref.shape[-1]
    n_kt = k_ref.shape[-2] // tk
    q_lo = q_off + i * tq

    def kv_tiles(k0):
        if not slab:
            return k_ref[0, 0, pl.ds(k0, tk), :], v_ref[0, 0, pl.ds(k0, tk), :]
        t = k_ref[0, pl.ds(k0, tk), :]
        first = pl.program_id(1) == 0
        return (jnp.where(first, t[:, :HD], t[:, HD:2 * HD]).astype(BF16),
                jnp.where(first, t[:, 2 * HD:3 * HD], t[:, 3 * HD:]).astype(BF16))

    if mode == "full":
        lo, hi = 0, n_kt
    else:
        hi = jnp.minimum(lax.div(q_lo + tq - 1 - k_off, tk) + 1, n_kt)
        lo = 0
        if mode == "window":
            if (tq + WINDOW - 2) // tk + 2 <= KEY_UNROLL:
                lo = jnp.maximum(hi - KEY_UNROLL, 0)
            else:
                lo = lax.div(jnp.maximum(q_lo - (WINDOW - 1) - k_off, 0), tk)
    qpos = q_lo + _iota((tq, tk), 0)
    if has_bm:
        bm = bm_ref[0, 0].astype(BF16)
        nbp = bm.shape[1]

    def body(j, state):
        k0 = pl.multiple_of(j * tk, tk)
        k, v = kv_tiles(k0)
        bias = None
        if mode != "full":
            kpos = k_off + j * tk + _iota((tq, tk), 1)
            ok = kpos <= qpos
            if mode == "window":
                ok = jnp.logical_and(ok, qpos - kpos < WINDOW)
                ok = jnp.logical_and(ok, kpos >= 0)
            bias = jnp.where(ok, 0.0, NEG)
        if has_bm:
            blk = _iota((nbp, tk), 0)
            tok = (j * tk + _iota((nbp, tk), 1)) >> SLC_SHIFT
            expand = jnp.where(blk == tok, 1.0, 0.0).astype(BF16)
            bias = jnp.where(_dot(bm, expand) > 0.5, bias, NEG)
        if bias is not None:
            bias = _tile_rows(bias, cr // tq)
        if slab:
            v = _with_ones(v)
        new = []
        for c in range(n_chain):
            s = _dot_nt(q[c * cr:(c + 1) * cr].astype(BF16), k)
            if bias is not None:
                s = s + bias
            new.append(_online_step_ones(s, v, *state[c]) if slab else _online_step(s, v, *state[c]))
        return tuple(new)

    n_chain = reps if tq >= MIN_CHAIN_ROWS else 1
    cr = reps * tq // n_chain
    init = tuple((_online_init_ones(cr) if slab else _online_init(cr, dv)) for _ in range(n_chain))
    state = _unrolled_loop(lo, hi, body, init)
    for c in range(n_chain):
        if slab:
            acc = state[c][1]
            o_ref[0, 0, 0, c * cr:(c + 1) * cr, :] = acc[:, :HD] / acc[:, HD:HD + 1]
        else:
            o_ref[0, 0, 0, c * cr:(c + 1) * cr, :] = state[c][2] / state[c][1]


def _flash(q, k, v=None, *, reps, tq, tk, q_off, k_off, mode, bm=None, kv_blk=None):
    b, hk, nq, rows, _ = q.shape
    lk = k.shape[-2]
    assert rows == reps * tq and lk % tk == 0
    if kv_blk is None:
        dv = v.shape[3]
        kv_specs = [pl.BlockSpec((1, 1, lk, HD), lambda b_, h, i: (b_, h, 0, 0)),
                    pl.BlockSpec((1, 1, lk, dv), lambda b_, h, i: (b_, h, 0, 0))]
        args = [q, k, v]
    else:
        assert hk == 2 and v is None
        dv = HD
        kv_specs = [pl.BlockSpec((1, lk, 2 * LANES), lambda b_, h, i: (b_, 0, kv_blk))]
        args = [q, k]
    in_specs = [pl.BlockSpec((1, 1, 1, rows, HD), lambda b_, h, i: (b_, h, i, 0, 0))] + kv_specs
    if bm is not None:
        in_specs.append(pl.BlockSpec((1, 1, tq, bm.shape[3]), lambda b_, h, i: (b_, h, i, 0)))
        args.append(bm)
    kern = functools.partial(_flash_kernel, reps=reps, tq=tq, tk=tk, q_off=q_off, k_off=k_off, mode=mode,
                             has_bm=bm is not None, slab=kv_blk is not None)
    return pl.pallas_call(
        kern,
        grid=(b, hk, nq),
        in_specs=in_specs,
        out_specs=pl.BlockSpec((1, 1, 1, rows, dv), lambda b_, h, i: (b_, h, i, 0, 0)),
        out_shape=jax.ShapeDtypeStruct((b, hk, nq, rows, dv), F32),
        compiler_params=_cparams(("parallel", "parallel", "arbitrary")),
        name="flash_" + mode + ("_blockmask" if bm is not None else ""),
    )(*args)


def _softplus(z):
    return jnp.maximum(z, 0.0) + jnp.log(1.0 + jnp.exp(-jnp.abs(z)))


def _later_matrix(n):
    return jnp.where(_iota((n, n), 0) > _iota((n, n), 1), 1.0, 0.0).astype(BF16)


def _sb_kernel(q_ref, k_ref, v_ref, o_ref, *, t):
    i = pl.program_id(2)
    th = t // ROW_SPLIT
    lane = _iota((th, LANES), 1)
    later = _later_matrix(t)
    qs = []
    for part in range(ROW_SPLIT):
        qf = q_ref[0, part * th:(part + 1) * th, :] * SCALE
        qs.append((jnp.where(lane < HD, qf, 0.0).astype(BF16), jnp.where(lane >= HD, qf, 0.0).astype(BF16)))

    def tile(j, carries, outs, masked):
        k0 = pl.multiple_of(j * t, t)
        k = k_ref[0, pl.ds(k0, t), :].astype(BF16)
        v = v_ref[0, pl.ds(k0, t), :].astype(BF16)
        new_carries, new_outs = [], []
        for part in range(ROW_SPLIT):
            heads = []
            for hd in range(2):
                carry = carries[2 * part + hd]
                z = _dot_nt(qs[part][hd], k)
                sp = _softplus(z)
                log_sig = z - sp
                if masked:
                    vis = _iota((th, t), 1) < part * th + _iota((th, t), 0)
                    sp = jnp.where(vis, sp, 0.0)
                a = jnp.exp(log_sig - _dot(sp.astype(BF16), later) - carry)
                if masked:
                    a = jnp.where(vis, a, 0.0)
                heads.append(_dot(a.astype(BF16), v))
                new_carries.append(carry + jnp.sum(sp, axis=1, keepdims=True))
            new_outs.append(outs[part] + jnp.where(lane < HD, heads[0], heads[1]))
        return tuple(new_carries), tuple(new_outs)

    def min_carry(carries):
        m = jnp.min(carries[0])
        for c in carries[1:]:
            m = jnp.minimum(m, jnp.min(c))
        return m

    zero = jnp.zeros((th, 1), F32)
    carries, outs = tile(i, (zero,) * (2 * ROW_SPLIT), (jnp.zeros((th, LANES), F32),) * ROW_SPLIT, True)

    def cond(st):
        return jnp.logical_and(st[0] < i, st[1] < SB_CUT)

    def body(st):
        carries, outs = tile(i - 1 - st[0], st[2], st[3], False)
        return st[0] + 1, min_carry(carries), carries, outs

    _, _, carries, outs = lax.while_loop(cond, body, (jnp.int32(0), min_carry(carries), carries, outs))
    for part in range(ROW_SPLIT):
        o_ref[0, part * th:(part + 1) * th, :] = outs[part]


def _sb(proj3, *, t, q_slab, k_slab, v_slab, n_slabs):
    b, tt, _ = proj3.shape
    assert tt % t == 0
    kern = functools.partial(_sb_kernel, t=t)
    return pl.pallas_call(
        kern,
        grid=(b, n_slabs, tt // t),
        in_specs=[
            pl.BlockSpec((1, t, LANES), lambda b_, p, i: (b_, i, q_slab + p)),
            pl.BlockSpec((1, tt, LANES), lambda b_, p, i: (b_, 0, k_slab + p)),
            pl.BlockSpec((1, tt, LANES), lambda b_, p, i: (b_, 0, v_slab + p)),
        ],
        out_specs=pl.BlockSpec((1, t, LANES), lambda b_, p, i: (b_, i, p)),
        out_shape=jax.ShapeDtypeStruct((b, tt, n_slabs * LANES), F32),
        compiler_params=_cparams(("parallel", "parallel", "arbitrary")),
        name="sb_prompt",
    )(proj3, proj3, proj3)


def _lam(lv, lam_init):
    a = jnp.sum(jnp.sum(lv[0:1] * lv[1:2], axis=1, keepdims=True), axis=0, keepdims=True)
    b = jnp.sum(jnp.sum(lv[2:3] * lv[3:4], axis=1, keepdims=True), axis=0, keepdims=True)
    return jnp.exp(a) - jnp.exp(b) + lam_init


def _df_kernel(lv_ref, g_ref, q_ref, k_ref, v_ref, o_ref, *, t, lam_init):
    i = pl.program_id(2)
    th = t // ROW_SPLIT
    lane = _iota((th, LANES), 1)
    qs = []
    for part in range(ROW_SPLIT):
        qf = q_ref[0, part * th:(part + 1) * th, :] * SCALE
        qs.append((jnp.where(lane < HD, qf, 0.0).astype(BF16), jnp.where(lane >= HD, qf, 0.0).astype(BF16)))

    def tile(j, state, masked):
        k0 = pl.multiple_of(j * t, t)
        k = k_ref[0, pl.ds(k0, t), :].astype(BF16)
        v = _with_ones(v_ref[0, pl.ds(k0, t), :].astype(BF16))
        new = []
        for part in range(ROW_SPLIT):
            for mp in range(2):
                s = _dot_nt(qs[part][mp], k)
                if masked:
                    s = jnp.where(_iota((th, t), 1) <= part * th + _iota((th, t), 0), s, NEG)
                new.append(_online_step_ones(s, v, *state[2 * part + mp]))
        return tuple(new)

    init = tuple((jnp.full((th, 1), NEG, F32), jnp.zeros((th, 2 * LANES), F32)) for _ in range(2 * ROW_SPLIT))
    state = _unrolled_loop(0, i, lambda j, st: tile(j, st, False), init)
    state = tile(i, state, True)
    lam = _lam(lv_ref[...], lam_init)
    for part in range(ROW_SPLIT):
        acc0, acc1 = state[2 * part][1], state[2 * part + 1][1]
        d = acc0[:, :LANES] / acc0[:, LANES:LANES + 1] - lam * (acc1[:, :LANES] / acc1[:, LANES:LANES + 1])
        o_ref[0, part * th:(part + 1) * th, :] = _rms_rows(d, g_ref[...]) * (1.0 - lam_init)


def _df(proj3, lv, subln, *, t, q_slab, k_slab, v_slab, n_slabs, lam_init):
    b, tt, _ = proj3.shape
    assert tt % t == 0
    kern = functools.partial(_df_kernel, t=t, lam_init=lam_init)
    return pl.pallas_call(
        kern,
        grid=(b, n_slabs, tt // t),
        in_specs=[
            pl.BlockSpec(lv.shape, lambda b_, p, i: (0, 0)),
            pl.BlockSpec(subln.shape, lambda b_, p, i: (0, 0)),
            pl.BlockSpec((1, t, LANES), lambda b_, p, i: (b_, i, q_slab + p)),
            pl.BlockSpec((1, tt, LANES), lambda b_, p, i: (b_, 0, k_slab + p)),
            pl.BlockSpec((1, tt, LANES), lambda b_, p, i: (b_, 0, v_slab + p)),
        ],
        out_specs=pl.BlockSpec((1, t, LANES), lambda b_, p, i: (b_, i, p)),
        out_shape=jax.ShapeDtypeStruct((b, tt, n_slabs * LANES), F32),
        compiler_params=_cparams(("parallel", "parallel", "arbitrary")),
        name="df_prompt",
    )(lv, subln, proj3, proj3, proj3)


def _dec_even_kernel(*refs, pp, ts, lam_init):
    tbl_ref, qt_ref = refs[0], refs[1]
    k_refs = refs[2:2 + pp]
    v_refs = refs[2 + pp:2 + 2 * pp]
    kn_ref, vn_ref, lv_ref, g_ref, o_ref, later_ref, st_ref, asb_ref, adf_ref = refs[2 + 2 * pp:]
    del tbl_ref
    s_id = pl.program_id(1)
    nk = pp * PAGE
    half = LANES // 2
    qt = qt_ref[0]

    def col_of(row):
        return jnp.transpose(jnp.broadcast_to(row, (SUBLANES, LANES)))[half:, 0:1]

    def tile(kt, vt, later, vis_sb, vis_df):
        carry, m, l = st_ref[0:1, :], st_ref[1:2, :], st_ref[2:3, :]
        zt = _dot(kt, qt)
        lane = _iota(zt.shape, 1)
        sp = _softplus(zt)
        log_sig = zt - sp
        s = zt
        if vis_sb is not None:
            sp = jnp.where(vis_sb, sp, 0.0)
            s = jnp.where(vis_df, s, NEG)
        a = jnp.exp(log_sig - _dot(later, sp.astype(BF16)) - carry)
        if vis_sb is not None:
            a = jnp.where(vis_sb, a, 0.0)
        m_new = jnp.maximum(m, jnp.max(s, axis=0, keepdims=True))
        p = jnp.exp(s - m_new)
        alpha = jnp.exp(m - m_new)
        st_ref[0:1, :] = carry + jnp.sum(sp, axis=0, keepdims=True)
        st_ref[1:2, :] = m_new
        st_ref[2:3, :] = alpha * l + jnp.sum(p, axis=0, keepdims=True)
        w = jnp.transpose(jnp.where(lane < half, a, p)).astype(BF16)
        asb_ref[...] += _dot(w[:half], vt[:, :SB_W])
        adf_ref[...] = col_of(alpha) * adf_ref[...] + _dot(w[half:], vt[:, SB_W:])

    @pl.when(s_id == 0)
    def _():
        later_ref[...] = jnp.where(_iota((nk, nk), 1) > _iota((nk, nk), 0), 1.0, 0.0).astype(BF16)
        st_ref[...] = jnp.where(_iota(st_ref.shape, 0) == 1, NEG, 0.0)
        asb_ref[...] = jnp.zeros_like(asb_ref)
        adf_ref[...] = jnp.zeros_like(adf_ref)
        key = _iota((PAGE, LANES), 0)
        tok = _iota((PAGE, LANES), 1) & (ts - 1)
        tile(kn_ref[0].astype(BF16), vn_ref[0].astype(BF16), later_ref[0:PAGE, 0:PAGE], key < tok, key <= tok)

    kt = jnp.concatenate([r[0].astype(BF16) for r in k_refs], axis=0)
    vt = jnp.concatenate([r[0].astype(BF16) for r in v_refs], axis=0)
    tile(kt, vt, later_ref[...], None, None)

    @pl.when(s_id == pl.num_programs(1) - 1)
    def _():
        row = _iota((half, SB_W), 0)
        lane = _iota((half, SB_W), 1)

        def fold(x):
            out = x[0:ts]
            for u in range(1, half // ts):
                out = out + x[u * ts:(u + 1) * ts]
            return out

        o_sb = fold(jnp.where((row >> 3) == (lane >> HD_SHIFT), asb_ref[...], 0.0))
        pn = adf_ref[...] / col_of(st_ref[2:3, :])
        same_head = (row >> 4) == (lane >> 7)
        o0 = fold(jnp.where(jnp.logical_and(same_head, ((row >> 3) & 1) == 0), pn, 0.0))
        o1 = fold(jnp.where(jnp.logical_and(same_head, ((row >> 3) & 1) == 1), pn, 0.0))
        d = o0 - _lam(lv_ref[...], lam_init) * o1
        parts = [o_sb]
        for h in range(N_DF):
            parts.append(_rms_rows(d[:, h * LANES:(h + 1) * LANES], g_ref[...]) * (1.0 - lam_init))
        o_ref[0] = jnp.concatenate(parts, axis=1)


def _dec_even(qt, pool_k, pool_v, table, k_new, v_new, lv, subln, *, pp, ts, lam_init):
    b, n_pages = table.shape
    width = pool_k.shape[2]
    assert n_pages % pp == 0 and ts == SUBLANES and k_new.shape[1] == PAGE
    n_steps = n_pages // pp

    def page_map(u):
        return lambda b_, s, t: (t[b_, n_pages - (s + 1) * pp + u], 0, 0)

    page_specs = [pl.BlockSpec((1, PAGE, width), page_map(u)) for u in range(pp)]
    new_spec = pl.BlockSpec((1, PAGE, width), lambda b_, s, t: (b_, 0, 0))
    grid_spec = pltpu.PrefetchScalarGridSpec(
        num_scalar_prefetch=1,
        grid=(b, n_steps),
        in_specs=[pl.BlockSpec((1,) + qt.shape[1:], lambda b_, s, t: (b_, 0, 0))] + page_specs + page_specs
        + [new_spec, new_spec, pl.BlockSpec(lv.shape, lambda b_, s, t: (0, 0)),
           pl.BlockSpec(subln.shape, lambda b_, s, t: (0, 0))],
        out_specs=pl.BlockSpec((1, ts, width), lambda b_, s, t: (b_, 0, 0)),
        scratch_shapes=[
            pltpu.VMEM((pp * PAGE, pp * PAGE), BF16),
            pltpu.VMEM((SUBLANES, LANES), F32),
            pltpu.VMEM((LANES // 2, SB_W), F32),
            pltpu.VMEM((LANES // 2, DF_W), F32),
        ],
    )
    kern = functools.partial(_dec_even_kernel, pp=pp, ts=ts, lam_init=lam_init)
    return pl.pallas_call(
        kern,
        grid_spec=grid_spec,
        out_shape=jax.ShapeDtypeStruct((b, ts, width), F32),
        compiler_params=_cparams(("parallel", "arbitrary")),
        name="dec_even",
    )(table, qt, *([pool_k] * pp), *([pool_v] * pp), k_new, v_new, lv, subln)


def _compress_kernel(t_ref, w1_ref, w1g_ref, pe_ref, w2_ref, g_ref, o_ref):
    kind = pl.program_id(1)
    n = t_ref.shape[1] // CMP_STRIDE
    a1 = [None] * N_NSA_KV
    a2 = [None] * N_NSA_KV
    for p in range(CMP_STRIDE):
        rows = t_ref[0, pl.ds(p, n, stride=CMP_STRIDE), :].astype(BF16)
        for g in range(N_NSA_KV):
            d1 = _dot(rows, w1g_ref[0, g, p])
            d2 = _dot(rows, w1g_ref[0, g, CMP_STRIDE + p])
            a1[g] = d1 if a1[g] is None else a1[g] + d1
            a2[g] = d2 if a2[g] is None else a2[g] + d2
    pe = _dot(jnp.broadcast_to(pe_ref[0], (SUBLANES, CMP_LEN * HD)).astype(BF16), w1_ref[0])[0:1]
    for g in range(N_NSA_KV):
        nxt = jnp.where(_iota(a2[g].shape, 0) < n - 1, pltpu.roll(a2[g], n - 1, 0), 0.0)
        hid = a1[g] + nxt + pe
        hid = hid * (1.0 / (1.0 + jnp.exp(-hid)))
        out = _dot(hid.astype(BF16), w2_ref[0])
        normed = _rms_rows(out, g_ref[...])
        o_ref[0, 0, g] = jnp.where(kind == 0, normed, out)


def _compress(t, slab0, w1, pe_flat, w2, gain):
    b, lk, _ = t.shape
    g = N_NSA_KV
    n = lk // CMP_STRIDE
    pieces = w1.reshape(2, CMP_LEN, HD, w1.shape[2])
    w1g = jnp.stack([jnp.pad(pieces, ((0, 0), (0, 0), (HD * gi, HD * (g - 1 - gi)), (0, 0))) for gi in range(g)],
                    axis=1)
    return pl.pallas_call(
        _compress_kernel,
        grid=(b, 2),
        in_specs=[
            pl.BlockSpec((1, lk, LANES), lambda b_, k: (b_, 0, slab0 + k)),
            pl.BlockSpec((1,) + w1.shape[1:], lambda b_, k: (k, 0, 0)),
            pl.BlockSpec((1,) + w1g.shape[1:], lambda b_, k: (k, 0, 0, 0, 0)),
            pl.BlockSpec((1,) + pe_flat.shape[1:], lambda b_, k: (k, 0, 0)),
            pl.BlockSpec((1,) + w2.shape[1:], lambda b_, k: (k, 0, 0)),
            pl.BlockSpec(gain.shape, lambda b_, k: (0, 0)),
        ],
        out_specs=pl.BlockSpec((1, 1, g, n, HD), lambda b_, k: (b_, k, 0, 0, 0)),
        out_shape=jax.ShapeDtypeStruct((b, 2, g, n, HD), F32),
        compiler_params=_cparams(("parallel", "arbitrary")),
        name="compress",
    )(t, w1, w1g, pe_flat, w2, gain)


def _nsa_cmp_kernel(q_ref, kc_ref, vc_ref, o_ref, sel_ref, *, tq, q_off, n_sel):
    i = pl.program_id(2)
    q = q_ref[0, 0, 0].astype(BF16)
    kc = kc_ref[0, 0]
    vc = vc_ref[0, 0]
    ncol = kc.shape[0]
    nbp = ncol // CMP_PER_SLC
    q_lo = q_off + i * tq
    qpos = q_lo + _iota((tq, ncol), 0)
    col = _iota((tq, ncol), 1)
    jj = jnp.zeros_like(col)
    for u in range(1, CMP_PER_SLC):
        jj = jj + jnp.where(col >= u * nbp, 1, 0)
    c_end = (col - jj * nbp) * SLC_BLOCK + jj * CMP_STRIDE + (CMP_LEN - 1)
    maskf = _tile_rows(jnp.where(c_end <= qpos, 1.0, 0.0), NSA_GROUP)
    keep = maskf > 0.5
    s = jnp.where(keep, _dot_nt(q, kc), NEG)
    m = jnp.max(s, axis=1, keepdims=True)
    p = jnp.where(keep, jnp.exp(s - m), 0.0)
    p = p / jnp.maximum(jnp.sum(p, axis=1, keepdims=True), 1e-30)
    o_ref[0, 0, 0] = _dot(p.astype(BF16), vc)
    pg = p[0:tq]
    for r in range(1, NSA_GROUP):
        pg = pg + p[r * tq:(r + 1) * tq]
    imp = pg[:, 0:nbp]
    for j in range(1, CMP_PER_SLC):
        imp = imp + pg[:, j * nbp:(j + 1) * nbp]
    blk = _iota((tq, nbp), 1)
    qp = q_lo + _iota((tq, nbp), 0)
    q_blk = qp >> SLC_SHIFT
    visible = blk * SLC_BLOCK <= qp
    forced = jnp.logical_or(blk == 0, jnp.logical_and(blk <= q_blk, blk > q_blk - N_LOCAL))
    score = jnp.where(visible, jnp.where(forced, FORCE, imp), NEG)
    sel0 = jnp.zeros((tq, nbp), F32)
    rounds = n_sel
    if n_sel > N_LOCAL + 1:
        pre = jnp.logical_and(forced, visible)
        sel0 = jnp.where(pre, 1.0, 0.0)
        score = jnp.where(pre, -jnp.inf, score)
        rounds = n_sel - (N_LOCAL + 1)
    rc = min(tq, SEL_CHUNK_ROWS)
    blkf = _iota((rc, nbp), 1).astype(F32)
    scores = [score[c * rc:(c + 1) * rc] for c in range(tq // rc)]
    sels = [sel0[c * rc:(c + 1) * rc] for c in range(tq // rc)]
    for _ in range(rounds):
        for c in range(len(scores)):
            top = jnp.max(scores[c], axis=1, keepdims=True)
            idx = jnp.min(jnp.where(scores[c] == top, blkf, 1e9), axis=1, keepdims=True)
            pick = blkf == idx
            sels[c] = jnp.where(pick, 1.0, sels[c])
            scores[c] = jnp.where(pick, -jnp.inf, scores[c])
    for c in range(len(scores)):
        sel_ref[0, 0, c * rc:(c + 1) * rc, :] = sels[c]


def _nsa_cmp(q, kc, vc, *, tq, q_off, n_sel):
    b, g, nq, rows, _ = q.shape
    ncol = kc.shape[2]
    nbp = ncol // CMP_PER_SLC
    kern = functools.partial(_nsa_cmp_kernel, tq=tq, q_off=q_off, n_sel=n_sel)
    return pl.pallas_call(
        kern,
        grid=(b, g, nq),
        in_specs=[
            pl.BlockSpec((1, 1, 1, rows, HD), lambda b_, g_, i: (b_, g_, i, 0, 0)),
            pl.BlockSpec((1, 1, ncol, HD), lambda b_, g_, i: (b_, g_, 0, 0)),
            pl.BlockSpec((1, 1, ncol, HD), lambda b_, g_, i: (b_, g_, 0, 0)),
        ],
        out_specs=[
            pl.BlockSpec((1, 1, 1, rows, HD), lambda b_, g_, i: (b_, g_, i, 0, 0)),
            pl.BlockSpec((1, 1, tq, nbp), lambda b_, g_, i: (b_, g_, i, 0)),
        ],
        out_shape=[
            jax.ShapeDtypeStruct((b, g, nq, rows, HD), F32),
            jax.ShapeDtypeStruct((b, g, nq * tq, nbp), F32),
        ],
        compiler_params=_cparams(("parallel", "parallel", "arbitrary")),
        name="nsa_cmp",
    )(q, kc, vc)


def _dsa_kernel(qi_ref, wi_ref, qd_ref, kk_ref, o_ref, key_ref, *, tq, tk, q_off, n_top, offs):
    i = pl.program_id(1)
    qi = qi_ref[0, 0]
    qd = qd_ref[0, 0]
    w = wi_ref[0] * (N_IDX ** -0.5)
    n_kt = kk_ref.shape[1] // tk
    kd_off, ki_off, vd_off = offs

    def cols(ref, k0, off):
        return ref[0, pl.ds(k0, tk), :][:, off:off + HD].astype(BF16)

    q_lo = q_off + i * tq
    hi = jnp.minimum(lax.div(q_lo + tq - 1, tk) + 1, n_kt)
    qpos = q_lo + _iota((tq, tk), 0)
    n_chain = N_DSA if tq >= MIN_CHAIN_ROWS else 1
    cr = N_DSA * tq // n_chain

    def visible(j):
        return (j * tk + _iota((tq, tk), 1)) <= qpos

    def score_tile(j, _):
        k0 = pl.multiple_of(j * tk, tk)
        k = cols(kk_ref, k0, ki_off)
        tot = None
        for c in range(n_chain):
            sc = jnp.maximum(_dot_nt(qi[c * cr:(c + 1) * cr].astype(BF16), k), 0.0)
            for u in range(cr // tq):
                h = c * (cr // tq) + u
                term = w[:, h:h + 1] * sc[u * tq:(u + 1) * tq]
                tot = term if tot is None else tot + term
        tot = jnp.where(visible(j), tot, NEG)
        bits = pltpu.bitcast(tot, jnp.int32)
        key = jnp.where(bits < 0, bits ^ jnp.int32(0x7FFFFFFF), bits)
        key_ref[:, pl.ds(k0, tk)] = jnp.where(tot == 0.0, 0, key)
        return 0

    _unrolled_loop(0, hi, score_tile, 0)

    def count_ge(c):
        def body(j, acc):
            blk = key_ref[:, pl.ds(pl.multiple_of(j * tk, tk), tk)]
            hit = jnp.where(blk >= c, 1.0, 0.0)
            part = hit[:, 0:LANES]
            for u in range(1, tk // LANES):
                part = part + hit[:, u * LANES:(u + 1) * LANES]
            return acc + part
        acc = _unrolled_loop(0, hi, body, jnp.zeros((tq, LANES), F32))
        return jnp.sum(acc, axis=1, keepdims=True)

    kf = float(n_top)
    tau = jnp.where(count_ge(jnp.zeros((tq, 1), jnp.int32)) >= kf, 0, INT_MIN).astype(jnp.int32)

    def bit_body(t, tau):
        cand = tau + jnp.left_shift(jnp.int32(1), 30 - t)
        return jnp.where(count_ge(cand) >= kf, cand, tau)

    tau = lax.fori_loop(0, 31, bit_body, tau)
    need = kf - count_ge(tau + 1)
    before = jnp.where(_iota((tk, tk), 0) < _iota((tk, tk), 1), 1.0, 0.0).astype(BF16)

    def attend(j, carry):
        state, n_eq = carry
        k0 = pl.multiple_of(j * tk, tk)
        key = key_ref[:, pl.ds(k0, tk)]
        eqf = jnp.where(key == tau, 1.0, 0.0)
        rank = n_eq + _dot(eqf.astype(BF16), before)
        kept = jnp.logical_or(key > tau, jnp.logical_and(key == tau, rank < need))
        bias = jnp.where(jnp.logical_and(kept, visible(j)), 0.0, NEG)
        k = cols(kk_ref, k0, kd_off)
        v1 = _with_ones(cols(kk_ref, k0, vd_off))
        bias = _tile_rows(bias, cr // tq)
        new = []
        for c in range(n_chain):
            s = _dot_nt(qd[c * cr:(c + 1) * cr].astype(BF16), k) + bias
            new.append(_online_step_ones(s, v1, *state[c]))
        return tuple(new), n_eq + jnp.sum(eqf, axis=1, keepdims=True)

    init = (tuple(_online_init_ones(cr) for _ in range(n_chain)), jnp.zeros((tq, 1), F32))
    state, _ = _unrolled_loop(0, hi, attend, init)
    for c in range(n_chain):
        acc = state[c][1]
        o_ref[0, 0, c * cr:(c + 1) * cr, :] = acc[:, :HD] / acc[:, HD:HD + 1]


def _dsa(qi, wi, qd, kk, *, width, blk, offs, tq, tk, q_off, n_top):
    b, nq, rows, _ = qi.shape
    lk = kk.shape[1]
    assert lk % tk == 0
    kern = functools.partial(_dsa_kernel, tq=tq, tk=tk, q_off=q_off, n_top=n_top, offs=offs)
    qspec = pl.BlockSpec((1, 1, rows, HD), lambda b_, i: (b_, i, 0, 0))
    return pl.pallas_call(
        kern,
        grid=(b, nq),
        in_specs=[qspec, pl.BlockSpec((1, tq, N_IDX), lambda b_, i: (b_, i, 0)), qspec,
                  pl.BlockSpec((1, lk, width), lambda b_, i: (b_, 0, blk))],
        out_specs=pl.BlockSpec((1, 1, rows, HD), lambda b_, i: (b_, i, 0, 0)),
        out_shape=jax.ShapeDtypeStruct((b, nq, rows, HD), F32),
        scratch_shapes=[pltpu.VMEM((tq, lk), jnp.int32)],
        compiler_params=_cparams(("parallel", "arbitrary")),
        name="dsa",
    )(qi, wi, qd, kk)


def _gather_kernel(*refs, pp, n_steps):
    pool_refs = refs[1:1 + pp]
    new_ref, o_ref = refs[1 + pp], refs[2 + pp]
    s = pl.program_id(1)

    def page(ref):
        mid = ref.shape[2:-1]
        if not mid:
            return ref[0]
        pieces = [ref[(0, slice(None)) + idx + (slice(None),)] for idx in np.ndindex(*mid)]
        return jnp.concatenate(pieces, axis=1)

    @pl.when(s < n_steps)
    def _():
        for u in range(pp):
            o_ref[0, u * PAGE:(u + 1) * PAGE, :] = page(pool_refs[u])

    @pl.when(s >= n_steps)
    def _():
        o_ref[...] = new_ref[...]


def _page_gather(pool, table, new, *, pp):
    b, n_pages = table.shape
    tail = pool.shape[2:]
    width = int(np.prod(tail))
    assert n_pages % pp == 0 and new.shape[1:] == (pp * PAGE, width)
    n_steps = n_pages // pp

    def page_map(u):
        return lambda b_, s, t: (t[b_, jnp.minimum(s, n_steps - 1) * pp + u], 0) + (0,) * len(tail)

    kern = functools.partial(_gather_kernel, pp=pp, n_steps=n_steps)
    grid_spec = pltpu.PrefetchScalarGridSpec(
        num_scalar_prefetch=1,
        grid=(b, n_steps + 1),
        in_specs=[pl.BlockSpec((1, PAGE) + tail, page_map(u)) for u in range(pp)]
        + [pl.BlockSpec((1, pp * PAGE, width), lambda b_, s, t: (b_, 0, 0))],
        out_specs=pl.BlockSpec((1, pp * PAGE, width), lambda b_, s, t: (b_, s, 0)),
    )
    return pl.pallas_call(
        kern,
        grid_spec=grid_spec,
        out_shape=jax.ShapeDtypeStruct((b, (n_pages + pp) * PAGE, width), pool.dtype),
        compiler_params=_cparams(("parallel", "arbitrary")),
        name="page_gather",
    )(table, *([pool] * pp), new)


def _rope_tables(pos):
    half = HD // 2
    inv = ROPE_THETA ** (-jnp.arange(half, dtype=F32) / half)
    ang = pos.astype(F32)[:, None] * inv[None, :]
    cos, sin = jnp.cos(ang), jnp.sin(ang)
    cos128 = jnp.tile(jnp.concatenate([cos, cos], axis=1), (1, LANES // HD))
    sin128 = jnp.tile(jnp.concatenate([-sin, sin], axis=1), (1, LANES // HD))
    return cos128, sin128


def _tile_gain(g):
    return jnp.tile(g.reshape(1, HD), (1, LANES // HD))


def _heads(a, b, t, h, d, scale=None):
    a = a.reshape(b, t, h, d)
    if scale is not None:
        a = a * scale
    return a.transpose(0, 2, 1, 3)


def _stack_q(a, tq):
    b, hk, r, t, d = a.shape
    return a.reshape(b, hk, r, t // tq, tq, d).transpose(0, 1, 3, 2, 4, 5).reshape(b, hk, t // tq, r * tq, d)


def _unstack_q(a, r, tq):
    b, hk, nq, _, d = a.shape
    a = a.reshape(b, hk, nq, r, tq, d).transpose(0, 2, 4, 1, 3, 5)
    return a.reshape(b * nq * tq, hk * r * d)


def _pad_rows(a, rows):
    return jnp.pad(a, ((0, 0), (0, rows - a.shape[1])) + ((0, 0),) * (a.ndim - 2))


def _even_mixer(x2, b, t, q_off, cs, past, prm, cfg):
    n = b * t
    program = [
        (0, SB_W, None, None, [(0, False)]),
        (3 * SB_W, DF_W, 0, None, [(SB_W, True)]),
        (SB_W, SB_W, None, None, [(SB_W + DF_W, False)]),
        (3 * SB_W + DF_W, DF_W, 1, None, [(2 * SB_W + DF_W, True)]),
        (2 * SB_W, SB_W, None, None, [(2 * (SB_W + DF_W), False)]),
        (3 * SB_W + 2 * DF_W, DF_W, None, None, [(3 * SB_W + 2 * DF_W, False)]),
    ]
    gains = jnp.concatenate([_tile_gain(prm["df_qk_gain"][0]), _tile_gain(prm["df_qk_gain"][1])], axis=0)
    proj = _mm([x2, cs[0], cs[1]], [prm["g0"], gains], prm["ev_w_in"], _lhs_norm, program, 3 * (SB_W + DF_W),
               tm=cfg["tm"], rope_idx=(1, 2), gains_idx=1)
    mw = SB_W + DF_W
    new_k = proj[:, mw:2 * mw].reshape(b, t, mw)
    new_v = proj[:, 2 * mw:3 * mw].reshape(b, t, mw)
    lam_init = 0.8 - 0.6 * math.exp(-0.3 * prm["layer"])
    lv, subln = prm["df_lambda"], prm["df_subln_gain"].reshape(1, 2 * HD)
    d = x2.shape[1]
    if past is None:
        proj3 = proj.reshape(b, t, 3 * mw)
        ns = SB_W // LANES
        o_sb = _sb(proj3, t=cfg["t_even"], q_slab=0, k_slab=2 * ns, v_slab=4 * ns, n_slabs=ns)
        o_df = _df(proj3, lv, subln, t=cfg["t_even"], q_slab=ns, k_slab=3 * ns, v_slab=5 * ns, n_slabs=ns,
                   lam_init=lam_init)
        return _mm([o_sb.reshape(n, SB_W), o_df.reshape(n, DF_W), x2], [], prm["ev_w_out"], _lhs_cat2,
                   [(0, d, None, None, [(0, False)])], d, tm=cfg["tm"], res_idx=2), new_k, new_v
    pool_k, pool_v, table = past
    qcat = proj[:, :mw].reshape(b, t, mw).transpose(0, 2, 1) * SCALE
    qt = (jnp.tile(qcat, (1, 1, LANES // t)) * _dec_even_mask(t)).astype(BF16)
    o = _dec_even(qt, pool_k, pool_v, table, _pad_rows(new_k, PAGE), _pad_rows(new_v, PAGE), lv, subln,
                  pp=cfg["pp"], ts=t, lam_init=lam_init)
    return _mm([o.reshape(n, mw), x2], [], prm["ev_w_out"], _lhs_plain,
               [(0, d, None, None, [(0, False)])], d, tm=cfg["tm"], res_idx=1), new_k, new_v


def _dec_even_mask(ts):
    f = np.arange(SB_W + DF_W)[:, None]
    c = np.arange(LANES)[None, :]
    half = LANES // 2
    sb = (f < SB_W) & (c < half) & (f // HD == c // ts)
    df = (f >= SB_W) & (c >= half) & ((f - SB_W) // HD == (c - half) // ts)
    return jnp.asarray((sb | df).astype(np.float32))


def _odd_layout():
    widths = (N_NSA * HD, 128, 128, 128, 128, 128, 128, N_NSA * 3, N_DSA * HD, HD, HD, N_IDX * HD, HD, N_IDX)
    offs = np.concatenate([[0], np.cumsum(widths)])
    (q_n, k_c, v_c, k_s, v_s, k_w, v_w, gate, q_d, k_d, v_d, q_i, k_i, w_i) = [
        (int(offs[j]), int(offs[j + 1])) for j in range(len(widths))]
    pieces = [q_n, (k_c[0], v_w[1]), q_d, q_i, k_d, k_i, v_d, w_i, ("pad", HD - N_IDX), gate,
              ("pad", LANES - N_NSA * 3)]
    program = [
        (0, 512, 0, None, [(0, False), (512, True)]),
        (512, 128, None, None, [(1024, False)]),
        (640, 128, None, None, [(1152, False)]),
        (768, 128, 1, None, [(1280, True)]),
        (896, 128, None, None, [(1408, False)]),
        (1024, 128, 2, None, [(1536, True)]),
        (1152, 128, None, None, [(1664, False)]),
        (1280, 512, 3, None, [(1792, True)]),
        (1792, 512, None, None, [(2304, True)]),
        (2304, 128, 4, None, [(2816, True)]),
        (2432, 128, None, None, [(2944, False)]),
        (2560, 128, None, "sigmoid", [(3072, False)]),
    ]
    return pieces, program, 3200


def _permute_cols(w, pieces):
    cols = []
    for p in pieces:
        if p[0] == "pad":
            cols.append(jnp.zeros((w.shape[0], p[1]), w.dtype))
        else:
            cols.append(w[:, p[0]:p[1]])
    return jnp.concatenate(cols, axis=1)


def _odd_mixer(x2, b, t, t_real, q_off, cs, past, prm, cfg):
    n = b * t
    pieces, program, out_cols = _odd_layout()
    ng, dg = prm["nsa_qk_gain"], prm["dsa_qk_gain"]
    gains = jnp.concatenate([
        _tile_gain(ng[0]), _tile_gain(ng[2]), _tile_gain(ng[3]), _tile_gain(dg[0]),
        jnp.concatenate([dg[1], dg[2]]).reshape(1, LANES)], axis=0)
    w_in = _permute_cols(prm["od_w_in"], pieces)
    proj = _mm([x2, cs[0], cs[1]], [prm["g0"], gains], w_in, _lhs_norm, program, out_cols,
               tm=cfg["tm"], rope_idx=(1, 2), gains_idx=1)
    g = N_NSA_KV
    new_nsa = proj[:, 1024:1536].reshape(b, t, 4 * g * HD)
    new_win = proj[:, 1536:1792].reshape(b, t, 2, g, HD)
    new_dsa = jnp.concatenate([proj[:, 2816:2880], proj[:, 2944:3008], proj[:, 2880:2944]], axis=1).reshape(b, t, 3 * HD)
    w_i = proj[:, 3008:3008 + N_IDX].reshape(b, t, N_IDX)
    gate = proj[:, 3072:3072 + N_NSA * 3].reshape(n, N_NSA, 3)
    if past is None:
        proj3 = proj.reshape(b, t, out_cols)
        nsa_src, win_src = (proj3, 8), (proj3, 12)
        dsa_src = dict(kk=proj3, width=2 * LANES, blk=11, offs=(0, HD, 2 * HD))
        win_off = 0
        new_state = new_win[:, -min(WINDOW, t):]
        l_real = t
    else:
        pool_nsa, pool_dsa, state, table = past
        nsa_buf = _page_gather(pool_nsa, table, _pad_rows(new_nsa, cfg["pp"] * PAGE), pp=cfg["pp"])
        dsa_buf = _page_gather(pool_dsa, table, _pad_rows(new_dsa, cfg["pp"] * PAGE), pp=cfg["pp"])
        wb = state.shape[1]
        win_buf = _pad_rows(jnp.concatenate([state, new_win], axis=1), wb + cfg["tk_win"])
        nsa_src, win_src = (nsa_buf, 0), (win_buf.reshape(b, wb + cfg["tk_win"], 2 * g * HD), 0)
        dsa_src = dict(kk=dsa_buf, width=3 * HD, blk=0, offs=(0, 2 * HD, HD))
        win_off = q_off - wb
        new_state = jnp.concatenate([state, new_win[:, :t_real]], axis=1)[:, -wb:]
        l_real = table.shape[1] * PAGE + t_real
    lk = nsa_src[0].shape[1]
    tq = cfg["tq"]

    n_cmp = lk // CMP_STRIDE
    pe_flat = prm["cmp_pe"].reshape(2, 1, CMP_LEN * HD)
    cmp = _compress(nsa_src[0], nsa_src[1], prm["cmp_w1"], pe_flat, prm["cmp_w2"], ng[1].reshape(1, HD))
    nb = n_cmp // CMP_PER_SLC
    nbp = -(-nb // LANES) * LANES
    cmp = cmp.reshape(b, 2, g, nb, CMP_PER_SLC, HD).transpose(0, 1, 2, 4, 3, 5)
    cmp = jnp.pad(cmp, ((0, 0),) * 4 + ((0, nbp - nb), (0, 0))).reshape(b, 2, g, CMP_PER_SLC * nbp, HD).astype(BF16)

    def group_q(cols, rows):
        a = _heads(cols, b, t, N_NSA, HD, SCALE).reshape(b, g, NSA_GROUP, t, HD)
        return _stack_q(a, rows)

    q_n = group_q(proj[:, 0:512], tq)
    q_r = group_q(proj[:, 512:1024], cfg["tq_slc"])
    n_blk = -(-l_real // SLC_BLOCK)
    o_c, sel = _nsa_cmp(q_n, cmp[:, 0], cmp[:, 1], tq=tq, q_off=q_off, n_sel=min(N_SLC, n_blk))

    tq_s = cfg["tq_slc"]
    o_s = _flash(q_r, nsa_src[0], kv_blk=(nsa_src[1] + 2) // 2, reps=NSA_GROUP, tq=tq_s,
                 tk=cfg["tk"], q_off=q_off, k_off=0, mode="causal", bm=sel)
    o_w = _flash(q_r, win_src[0], kv_blk=win_src[1] // 2, reps=NSA_GROUP, tq=tq_s,
                 tk=cfg["tk_win"], q_off=q_off, k_off=win_off, mode="window")

    tq_d = cfg["tq_dsa"]
    q_d = _stack_q(_heads(proj[:, 1792:2304], b, t, N_DSA, HD, SCALE)[:, None], tq_d)[:, 0]
    q_i = _stack_q(_heads(proj[:, 2304:2816], b, t, N_IDX, HD, IDX_SCALE)[:, None], tq_d)[:, 0]
    o_d = _dsa(q_i, w_i, q_d, tq=tq_d, tk=cfg["tk"], q_off=q_off, n_top=min(DSA_TOPK_MAX, l_real // 4), **dsa_src)

    o_c = _unstack_q(o_c, NSA_GROUP, tq)
    o_s, o_w = (_unstack_q(o, NSA_GROUP, tq_s) for o in (o_s, o_w))
    o_d = _unstack_q(o_d[:, None], N_DSA, tq_d)
    gfull = [jnp.repeat(gate[:, :, j], HD, axis=1) for j in range(3)]
    d = x2.shape[1]
    out = _mm([o_c, o_s, o_w, o_d] + gfull + [x2], [], prm["od_w_out"], _lhs_odd,
              [(0, d, None, None, [(0, False)])], d, tm=cfg["tm"], res_idx=7)
    return out, new_nsa, new_dsa, new_state


def _cross(x2, b, t, mem_k, mem_v, prm, cfg):
    d = x2.shape[1]
    xw = N_XH * HD
    q = _mm([x2], [prm["g1"], _tile_gain(prm["x_gq"])], prm["x_wq"], _lhs_norm,
            [(0, xw, 0, None, [(0, False)])], xw, tm=cfg["tm"], gains_idx=1)
    tq = cfg["tq_cross"]
    qh = _stack_q(_heads(q, b, t, N_XH, HD, SCALE)[:, :, None], tq)
    o = _flash(qh, mem_k, mem_v, reps=1, tq=tq, tk=mem_k.shape[2], q_off=0, k_off=0, mode="full")
    o = _unstack_q(o, 1, tq)
    return _mm([o, x2], [], prm["x_wo"], _lhs_plain, [(0, d, None, None, [(0, False)])], d, tm=cfg["tm"], res_idx=1)


def _memory_kv(mem2, prm):
    xw = N_XH * HD
    w = jnp.concatenate([prm["x_wk"], prm["x_wv"]], axis=1)
    return _mm([mem2], [_tile_gain(prm["x_gk"])], w, _lhs_plain,
               [(0, xw, 0, None, [(0, False)]), (xw, xw, None, None, [(xw, False)])], 2 * xw,
               tm=min(256, mem2.shape[0]), gains_idx=0)


def _run_group(x, q_off, t_real, mem_kvs, pasts, layers, cfg):
    b, t, d = x.shape
    x2 = x.reshape(b * t, d)
    pos = q_off + jnp.arange(t, dtype=jnp.int32)
    cos, sin = _rope_tables(pos)
    cs = (jnp.tile(cos, (b, 1)), jnp.tile(sin, (b, 1)))
    outs = {}
    for li, prm in enumerate(layers):
        if li % 2 == 0:
            x2, nk, nv = _even_mixer(x2, b, t, q_off, cs, pasts[li], prm, cfg)
            outs["ek"], outs["ev"] = nk, nv
        else:
            x2, nn, nd, nw = _odd_mixer(x2, b, t, t_real, q_off, cs, pasts[li], prm, cfg)
            outs["on"], outs["od"], outs["ow"] = nn, nd, nw
        x2 = _cross(x2, b, t, mem_kvs[li][0], mem_kvs[li][1], prm, cfg)
        if li % 2 == 0:
            x2 = _ffn(x2, prm["g2"], prm["router"], prm["w1"], prm["w3"], prm["w2"], tm=cfg["tm_ffn"],
                      tf=cfg["tf"], routed=False)
        elif "moe_cap" in cfg and x2.shape[0] % cfg["tm_moe"] == 0:
            x2 = _moe(x2, prm["g2"], prm["router"], prm["w1"], prm["w3"], prm["w2"], tm=cfg["tm_moe"],
                      tf=cfg["tf"], cap=cfg["moe_cap"])
        else:
            x2 = _ffn(x2, prm["g2"], prm["router"], prm["w1"], prm["w3"], prm["w2"], tm=cfg["tm_ffn"],
                      tf=cfg["tf"], routed=True)
    return x2.reshape(b, t, d), outs


def kernel(x_prompt, x_sample, mem_prompt, cache_even_k, cache_even_v, cache_odd_nsa, cache_odd_dsa, state_odd_win, cache_mem, page_table, norm_gain, ev_w_in, ev_w_out, df_qk_gain, df_lambda, df_subln_gain, ffn_w1, ffn_w3, ffn_w2, od_w_in, od_w_out, nsa_qk_gain, cmp_pe, cmp_w1, cmp_w2, dsa_qk_gain, moe_router, moe_w1, moe_w3, moe_w2, x_wq, x_wk, x_wv, x_wo, x_qk_gain):
    depth = norm_gain.shape[0]
    bp, tp, d = x_prompt.shape
    bs, ts, _ = x_sample.shape
    n_mem = mem_prompt.shape[1]
    xw = N_XH * HD

    layers = []
    for l in range(depth):
        i = l // 2
        prm = {
            "layer": l,
            "g0": norm_gain[l, 0].reshape(1, d), "g1": norm_gain[l, 1].reshape(1, d), "g2": norm_gain[l, 2].reshape(1, d),
            "x_wq": x_wq[l].astype(BF16), "x_wk": x_wk[l].astype(BF16), "x_wv": x_wv[l].astype(BF16),
            "x_wo": x_wo[l].astype(BF16), "x_gq": x_qk_gain[l, 0], "x_gk": x_qk_gain[l, 1],
        }
        if l % 2 == 0:
            prm.update({
                "ev_w_in": ev_w_in[i].astype(BF16), "ev_w_out": ev_w_out[i].astype(BF16),
                "df_qk_gain": df_qk_gain[i], "df_lambda": df_lambda[i], "df_subln_gain": df_subln_gain[i],
                "router": jnp.zeros((SUBLANES, LANES), F32),
                "w1": ffn_w1[i][None].astype(BF16), "w3": ffn_w3[i][None].astype(BF16), "w2": ffn_w2[i][None].astype(BF16),
            })
        else:
            prm.update({
                "od_w_in": od_w_in[i].astype(BF16), "od_w_out": od_w_out[i].astype(BF16),
                "nsa_qk_gain": nsa_qk_gain[i], "dsa_qk_gain": dsa_qk_gain[i],
                "cmp_pe": cmp_pe[i], "cmp_w1": cmp_w1[i].astype(BF16), "cmp_w2": cmp_w2[i].astype(BF16),
                "router": jnp.pad(moe_router[i], ((0, 0), (0, LANES - N_EXPERTS))),
                "w1": moe_w1[i].astype(BF16), "w3": moe_w3[i].astype(BF16), "w2": moe_w2[i].astype(BF16),
            })
        layers.append(prm)

    def mem_heads(kv, b):
        k = kv[:, :, 0].transpose(0, 2, 1, 3).astype(BF16)
        v = kv[:, :, 1].transpose(0, 2, 1, 3).astype(BF16)
        return k, v

    mem2 = mem_prompt.reshape(bp * n_mem, d)
    mem_p = [_memory_kv(mem2, layers[l]).reshape(bp, n_mem, 2, N_XH, HD) for l in range(depth)]
    ff = ffn_w1.shape[2]
    tf = ff // 2 if (ff // 2) % LANES == 0 else ff
    cfg_p = {"tm": 256, "tq": 128, "tk": 256, "t_even": 256, "tk_win": 256, "tm_ffn": 512, "tf": tf,
             "tm_moe": 1024, "moe_cap": 288}
    cfg_p["tq"] = min(cfg_p["tq"], tp)
    cfg_p["tq_cross"] = 1024 if tp % 1024 == 0 else cfg_p["tq"]
    cfg_p["tq_slc"] = 256 if tp % 256 == 0 else cfg_p["tq"]
    cfg_p["tq_dsa"] = cfg_p["tq"]
    y_prompt, op = _run_group(x_prompt, 0, tp, [mem_heads(m, bp) for m in mem_p], [None] * depth, layers, cfg_p)
    p_mem = jnp.stack(mem_p)

    n_past = page_table.shape[1] * cache_even_k.shape[2]
    ts_pad = -(-ts // SUBLANES) * SUBLANES
    xs = _pad_rows(x_sample, ts_pad)
    pasts = []
    for l in range(depth):
        i = l // 2
        if l % 2 == 0:
            pasts.append((cache_even_k[i], cache_even_v[i], page_table))
        else:
            pn = cache_odd_nsa[i]
            pd = cache_odd_dsa[i]
            pasts.append((pn.reshape(pn.shape[0], pn.shape[1], -1), pd.reshape(pd.shape[0], pd.shape[1], -1),
                          state_odd_win[i], page_table))
    cfg_s = {"tm": bs * ts_pad, "tq": ts_pad, "tq_cross": ts_pad, "tq_slc": ts_pad, "tq_dsa": ts_pad, "tk": 512,
             "pp": 4, "tk_win": 128,
             "tm_ffn": bs * ts_pad, "tf": tf}
    y_s, os_ = _run_group(xs, n_past, ts, [mem_heads(cache_mem[l], bs) for l in range(depth)], pasts, layers, cfg_s)
    y_sample = y_s[:, :ts]

    g = N_NSA_KV
    return (
        y_prompt, y_sample,
        op["ek"][None], op["ev"][None],
        op["on"].reshape(1, bp, tp, 4, g, HD), op["od"].reshape(1, bp, tp, 3, HD),
        op["ow"][None], p_mem,
        os_["ek"][:, :ts][None], os_["ev"][:, :ts][None],
        os_["on"][:, :ts].reshape(1, bs, ts, 4, g, HD), os_["od"][:, :ts].reshape(1, bs, ts, 3, HD),
        os_["ow"][None],
    )
```

```python
import functools
import math

import jax
import jax.numpy as jnp
import numpy as np
from jax import lax
from jax.experimental import pallas as pl
from jax.experimental.pallas import tpu as pltpu

F32 = jnp.float32
BF16 = jnp.bfloat16

HD = 64
N_SB = 8
N_DF = 4
N_NSA = 8
N_NSA_KV = 2
NSA_GROUP = N_NSA // N_NSA_KV
N_DSA = 8
N_IDX = 8
N_XH = 4
N_EXPERTS = 8
ROPE_THETA = 10000.0
CMP_LEN = 32
CMP_STRIDE = 16
SLC_BLOCK = 64
CMP_PER_SLC = SLC_BLOCK // CMP_STRIDE
N_SLC = 16
N_LOCAL = 2
WINDOW = 512
DSA_TOPK_MAX = 256
EPS = 1e-6
NEG = -1e30
FORCE = 1e9
SCALE = HD ** -0.5
IDX_SCALE = HD ** -0.5
SB_W = N_SB * HD
HD_SHIFT = 6
SLC_SHIFT = 6
DF_W = N_DF * 2 * HD

LANES = 128
SUBLANES = 8
PAGE = 128
VMEM_LIMIT = 52 * 1024 * 1024
INT_MIN = -2 ** 31
ROW_SPLIT = 2
SB_CUT = 120.0
SEL_CHUNK_ROWS = 16
MIN_CHAIN_ROWS = 64
KEY_UNROLL = 4

_NT = (((1,), (1,)), ((), ()))


def _cparams(sem):
    return pltpu.CompilerParams(dimension_semantics=sem, vmem_limit_bytes=VMEM_LIMIT)


def _dot(a, b):
    return jnp.dot(a, b, preferred_element_type=F32)


def _dot_nt(a, b):
    return lax.dot_general(a, b, _NT, preferred_element_type=F32)


def _split_dot(x, m_bf16):
    hi = x.astype(BF16)
    lo = (x - hi.astype(F32)).astype(BF16)
    return _dot(hi, m_bf16) + _dot(lo, m_bf16)


def _split_dot_rhs(m_bf16, x):
    hi = x.astype(BF16)
    lo = (x - hi.astype(F32)).astype(BF16)
    return _dot(m_bf16, hi) + _dot(m_bf16, lo)


def _iota(shape, dim):
    return lax.broadcasted_iota(jnp.int32, shape, dim)


def _rms_rows(x, g):
    return x * lax.rsqrt(jnp.mean(x * x, axis=-1, keepdims=True) + EPS) * g


def _group_mean_matrix():
    r = _iota((LANES, LANES), 0) >> HD_SHIFT
    c = _iota((LANES, LANES), 1) >> HD_SHIFT
    return jnp.where(r == c, 1.0 / HD, 0.0).astype(BF16)


def _head_norm(y, g, gm):
    ms = _split_dot(y * y, gm)
    return y * lax.rsqrt(ms + EPS) * g


def _rope_slab(y, cos, sin):
    lane = _iota(y.shape, 1)
    first = (lane & (HD - 1)) < (HD // 2)
    swapped = jnp.where(first, pltpu.roll(y, LANES - HD // 2, 1), pltpu.roll(y, HD // 2, 1))
    return y * cos + swapped * sin


def _mm_kernel(*refs, n_rows, n_consts, lhs_fn, program, rope_idx, gains_idx, res_idx):
    rows = refs[:n_rows]
    consts = refs[n_rows:n_rows + n_consts]
    w_ref = refs[n_rows + n_consts]
    o_ref = refs[-1]
    lhs = lhs_fn(rows, consts).astype(BF16)
    gm = _group_mean_matrix() if gains_idx is not None else None
    for (src, width, gain_row, act, outs) in program:
        y_full = _dot(lhs, w_ref[:, src:src + width])
        for s in range(width // LANES):
            y = y_full[:, s * LANES:(s + 1) * LANES]
            if gain_row is not None:
                y = _head_norm(y, consts[gains_idx][gain_row:gain_row + 1, :], gm)
            if act == "sigmoid":
                y = 1.0 / (1.0 + jnp.exp(-y))
            for (dst, rope) in outs:
                z = y
                if rope:
                    z = _rope_slab(y, rows[rope_idx[0]][...], rows[rope_idx[1]][...])
                d0 = dst + s * LANES
                if res_idx is not None:
                    z = z + rows[res_idx][:, d0:d0 + LANES]
                o_ref[:, d0:d0 + LANES] = z


def _mm(rows, consts, w, lhs_fn, program, out_cols, *, tm, rope_idx=None, gains_idx=None, res_idx=None, name="mm"):
    n = rows[0].shape[0]
    assert n % tm == 0
    in_specs = [pl.BlockSpec((tm, r.shape[1]), lambda i: (i, 0)) for r in rows]
    in_specs += [pl.BlockSpec(c.shape, lambda i: (0, 0)) for c in consts]
    in_specs += [pl.BlockSpec(w.shape, lambda i: (0, 0))]
    kern = functools.partial(_mm_kernel, n_rows=len(rows), n_consts=len(consts), lhs_fn=lhs_fn,
                             program=program, rope_idx=rope_idx, gains_idx=gains_idx, res_idx=res_idx)
    return pl.pallas_call(
        kern,
        grid=(n // tm,),
        in_specs=in_specs,
        out_specs=pl.BlockSpec((tm, out_cols), lambda i: (i, 0)),
        out_shape=jax.ShapeDtypeStruct((n, out_cols), F32),
        compiler_params=_cparams(("parallel",)),
        name=name,
    )(*rows, *consts, w)


def _lhs_norm(rows, consts):
    return _rms_rows(rows[0][...], consts[0][...])


def _lhs_plain(rows, consts):
    return rows[0][...]


def _lhs_cat2(rows, consts):
    return jnp.concatenate([rows[0][...], rows[1][...]], axis=1)


def _lhs_odd(rows, consts):
    oc, os_, ow, od = rows[0][...], rows[1][...], rows[2][...], rows[3][...]
    g0, g1, g2 = rows[4][...], rows[5][...], rows[6][...]
    return jnp.concatenate([g0 * oc + g1 * os_ + g2 * ow, od], axis=1)


def _ffn_kernel(x_ref, g_ref, r_ref, w1_ref, w3_ref, w2_ref, o_ref, h_ref, acc_ref, gate_ref, *, routed):
    e = pl.program_id(1)
    f = pl.program_id(2)
    first = jnp.logical_and(e == 0, f == 0)
    last = jnp.logical_and(e == pl.num_programs(1) - 1, f == pl.num_programs(2) - 1)

    @pl.when(first)
    def _():
        x = x_ref[...]
        h = _rms_rows(x, g_ref[...])
        h_ref[...] = h.astype(BF16)
        acc_ref[...] = x
        if routed:
            logits = jnp.dot(h, r_ref[...], preferred_element_type=F32, precision=lax.Precision.HIGHEST)
            col = _iota(logits.shape, 1).astype(F32)
            logits = jnp.where(col < N_EXPERTS, logits, -jnp.inf)
            m1 = jnp.max(logits, axis=1, keepdims=True)
            i1 = jnp.min(jnp.where(logits == m1, col, 1e9), axis=1, keepdims=True)
            rest = jnp.where(col == i1, -jnp.inf, logits)
            m2 = jnp.max(rest, axis=1, keepdims=True)
            i2 = jnp.min(jnp.where(rest == m2, col, 1e9), axis=1, keepdims=True)
            e2 = jnp.exp(m2 - m1)
            g1 = 1.0 / (1.0 + e2)
            g2 = e2 / (1.0 + e2)
            gate_ref[...] = jnp.where(col == i1, g1, 0.0) + jnp.where(col == i2, g2, 0.0)

    def compute(gcol):
        h = h_ref[...]
        u = _dot(h, w1_ref[0])
        v = _dot(h, w3_ref[0])
        a = (u * (1.0 / (1.0 + jnp.exp(-u)))) * v
        y = _dot(a.astype(BF16), w2_ref[0])
        if gcol is not None:
            y = gcol * y
        acc_ref[...] += y

    if routed:
        col = _iota(gate_ref.shape, 1)
        gcol = jnp.sum(jnp.where(col == e, gate_ref[...], 0.0), axis=1, keepdims=True)
        active = jnp.max(gcol) > 0.0

        @pl.when(active)
        def _():
            compute(gcol)
    else:
        compute(None)

    @pl.when(last)
    def _():
        o_ref[...] = acc_ref[...]


def _moe_kernel(x_ref, g_ref, r_ref, w1_ref, w3_ref, w2_ref, o_ref, h_ref, acc_ref, gate_ref, slot_ref,
                slot_t_ref, he_ref, ye_ref, *, cap):
    e = pl.program_id(1)
    f = pl.program_id(2)
    nf = pl.num_programs(2)
    tm = x_ref.shape[0]
    first = jnp.logical_and(e == 0, f == 0)
    last = jnp.logical_and(e == pl.num_programs(1) - 1, f == nf - 1)

    @pl.when(first)
    def _():
        x = x_ref[...]
        h = _rms_rows(x, g_ref[...])
        h_ref[...] = h.astype(BF16)
        acc_ref[...] = x
        logits = jnp.dot(h, r_ref[...], preferred_element_type=F32, precision=lax.Precision.HIGHEST)
        col = _iota(logits.shape, 1).astype(F32)
        logits = jnp.where(col < N_EXPERTS, logits, -jnp.inf)
        m1 = jnp.max(logits, axis=1, keepdims=True)
        i1 = jnp.min(jnp.where(logits == m1, col, 1e9), axis=1, keepdims=True)
        rest = jnp.where(col == i1, -jnp.inf, logits)
        m2 = jnp.max(rest, axis=1, keepdims=True)
        i2 = jnp.min(jnp.where(rest == m2, col, 1e9), axis=1, keepdims=True)
        e2 = jnp.exp(m2 - m1)
        gate_ref[...] = jnp.where(col == i1, 1.0 / (1.0 + e2), 0.0) + jnp.where(col == i2, e2 / (1.0 + e2), 0.0)
        chosen = jnp.where(jnp.logical_or(col == i1, col == i2), 1.0, 0.0)
        earlier = jnp.where(_iota((tm, tm), 1) < _iota((tm, tm), 0), 1.0, 0.0).astype(BF16)
        slot = jnp.where(chosen > 0.5, _dot(earlier, chosen.astype(BF16)), -1.0)
        slot_ref[...] = slot
        slot_t_ref[...] = jnp.transpose(slot)

    lane = _iota((tm, LANES), 1)
    mine = lane == e
    gcol = jnp.sum(jnp.where(mine, gate_ref[...], 0.0), axis=1, keepdims=True)
    slot_col = jnp.sum(jnp.where(mine, slot_ref[...], 0.0), axis=1, keepdims=True)
    slot_row = slot_t_ref[pl.ds(e, 1), :]
    count = jnp.sum(jnp.where(slot_row >= 0.0, 1.0, 0.0)).astype(jnp.int32)
    n_chunk = lax.div(count + (cap - 1), cap)

    def pack(c):
        want = (c * cap + _iota((cap, tm), 0)).astype(F32)
        return jnp.where(slot_row == want, 1.0, 0.0).astype(BF16)

    def unpack(c):
        want = (c * cap + _iota((tm, cap), 1)).astype(F32)
        return jnp.where(slot_col == want, 1.0, 0.0).astype(BF16)

    def expert(rows):
        u = _dot(rows, w1_ref[0])
        v = _dot(rows, w3_ref[0])
        a = (u * (1.0 / (1.0 + jnp.exp(-u)))) * v
        return _dot(a.astype(BF16), w2_ref[0])

    @pl.when(n_chunk > 0)
    def _():
        @pl.when(f == 0)
        def _():
            he_ref[...] = _dot(pack(0), h_ref[...]).astype(BF16)
            ye_ref[...] = expert(he_ref[...])

        @pl.when(f > 0)
        def _():
            ye_ref[...] += expert(he_ref[...])

        @pl.when(f == nf - 1)
        def _():
            acc_ref[...] += gcol * _split_dot_rhs(unpack(0), ye_ref[...])

    def extra(c, _):
        part = expert(_dot(pack(c), h_ref[...]).astype(BF16))
        acc_ref[...] += gcol * _split_dot_rhs(unpack(c), part)
        return 0

    lax.fori_loop(1, n_chunk, extra, 0)

    @pl.when(last)
    def _():
        o_ref[...] = acc_ref[...]


def _moe(x, g, router, w1, w3, w2, *, tm, tf, cap):
    n, d = x.shape
    ne, _, ff = w1.shape
    assert n % tm == 0 and ff % tf == 0
    kern = functools.partial(_moe_kernel, cap=cap)
    return pl.pallas_call(
        kern,
        grid=(n // tm, ne, ff // tf),
        in_specs=[
            pl.BlockSpec((tm, d), lambda i, e, f: (i, 0), pipeline_mode=pl.Buffered(1)),
            pl.BlockSpec((1, d), lambda i, e, f: (0, 0)),
            pl.BlockSpec(router.shape, lambda i, e, f: (0, 0)),
            pl.BlockSpec((1, d, tf), lambda i, e, f: (e, 0, f)),
            pl.BlockSpec((1, d, tf), lambda i, e, f: (e, 0, f)),
            pl.BlockSpec((1, tf, d), lambda i, e, f: (e, f, 0)),
        ],
        out_specs=pl.BlockSpec((tm, d), lambda i, e, f: (i, 0), pipeline_mode=pl.Buffered(1)),
        out_shape=jax.ShapeDtypeStruct((n, d), F32),
        scratch_shapes=[pltpu.VMEM((tm, d), BF16), pltpu.VMEM((tm, d), F32), pltpu.VMEM((tm, LANES), F32),
                        pltpu.VMEM((tm, LANES), F32), pltpu.VMEM((LANES, tm), F32),
                        pltpu.VMEM((cap, d), BF16), pltpu.VMEM((cap, d), F32)],
        compiler_params=_cparams(("parallel", "arbitrary", "arbitrary")),
        name="moe_routed",
    )(x, g, router, w1, w3, w2)


def _ffn(x, g, router, w1, w3, w2, *, tm, tf, routed):
    n, d = x.shape
    ne, _, ff = w1.shape
    assert n % tm == 0 and ff % tf == 0
    kern = functools.partial(_ffn_kernel, routed=routed)
    return pl.pallas_call(
        kern,
        grid=(n // tm, ne, ff // tf),
        in_specs=[
            pl.BlockSpec((tm, d), lambda i, e, f: (i, 0)),
            pl.BlockSpec((1, d), lambda i, e, f: (0, 0)),
            pl.BlockSpec(router.shape, lambda i, e, f: (0, 0)),
            pl.BlockSpec((1, d, tf), lambda i, e, f: (e, 0, f)),
            pl.BlockSpec((1, d, tf), lambda i, e, f: (e, 0, f)),
            pl.BlockSpec((1, tf, d), lambda i, e, f: (e, f, 0)),
        ],
        out_specs=pl.BlockSpec((tm, d), lambda i, e, f: (i, 0)),
        out_shape=jax.ShapeDtypeStruct((n, d), F32),
        scratch_shapes=[pltpu.VMEM((tm, d), BF16), pltpu.VMEM((tm, d), F32), pltpu.VMEM((tm, LANES), F32)],
        compiler_params=_cparams(("parallel", "arbitrary", "arbitrary")),
        name="moe" if routed else "ffn",
    )(x, g, router, w1, w3, w2)


def _tile_rows(m, reps):
    return m if reps == 1 else jnp.concatenate([m] * reps, axis=0)


def _online_step(s, v, m, l, acc):
    m_new = jnp.maximum(m, jnp.max(s, axis=1, keepdims=True))
    p = jnp.exp(s - m_new)
    alpha = jnp.exp(m - m_new)
    return m_new, alpha * l + jnp.sum(p, axis=1, keepdims=True), alpha * acc + _dot(p.astype(BF16), v)


def _with_ones(v):
    return jnp.concatenate([v, jnp.ones_like(v)], axis=1)


def _online_step_ones(s, v1, m, acc):
    m_new = jnp.maximum(m, jnp.max(s, axis=1, keepdims=True))
    p = jnp.exp(s - m_new)
    return m_new, jnp.exp(m - m_new) * acc + _dot(p.astype(BF16), v1)


def _online_init_ones(rows):
    return (jnp.full((rows, 1), NEG, F32), jnp.zeros((rows, 2 * HD), F32))


def _online_init(rows, dv):
    return (jnp.full((rows, 1), NEG, F32), jnp.zeros((rows, 1), F32), jnp.zeros((rows, dv), F32))


def _unrolled_loop(lo, hi, body, init, unroll=KEY_UNROLL):
    shift = unroll.bit_length() - 1
    n_group = (hi - lo) >> shift

    def group(p, st):
        for u in range(unroll):
            st = body(lo + unroll * p + u, st)
        return st

    st = lax.fori_loop(0, n_group, group, init)
    return lax.fori_loop(lo + n_group * unroll, hi, body, st)


def _flash_kernel(*refs, reps, tq, tk, q_off, k_off, mode, has_bm, slab):
    refs = list(refs)
    q_ref = refs.pop(0)
    k_ref = refs.pop(0)
    v_ref = None if slab else refs.pop(0)
    bm_ref = refs.pop(0) if has_bm else None
    o_ref = refs.pop(0)
    i = pl.program_id(2)
    q = q_ref[0, 0, 0]
    dv = HD if slab else v_ref.shape[-1]
    n_kt = k_ref.shape[-2] // tk
    q_lo = q_off + i * tq

    def kv_tiles(k0):
        if not slab:
            return k_ref[0, 0, pl.ds(k0, tk), :], v_ref[0, 0, pl.ds(k0, tk), :]
        t = k_ref[0, pl.ds(k0, tk), :]
        first = pl.program_id(1) == 0
        return (jnp.where(first, t[:, :HD], t[:, HD:2 * HD]).astype(BF16),
                jnp.where(first, t[:, 2 * HD:3 * HD], t[:, 3 * HD:]).astype(BF16))

    if mode == "full":
        lo, hi = 0, n_kt
    else:
        hi = jnp.minimum(lax.div(q_lo + tq - 1 - k_off, tk) + 1, n_kt)
        lo = 0
        if mode == "window":
            if (tq + WINDOW - 2) // tk + 2 <= KEY_UNROLL:
                lo = jnp.maximum(hi - KEY_UNROLL, 0)
            else:
                lo = lax.div(jnp.maximum(q_lo - (WINDOW - 1) - k_off, 0), tk)
    qpos = q_lo + _iota((tq, tk), 0)
    if has_bm:
        bm = bm_ref[0, 0].astype(BF16)
        nbp = bm.shape[1]

    def body(j, state):
        k0 = pl.multiple_of(j * tk, tk)
        k, v = kv_tiles(k0)
        bias = None
        if mode != "full":
            kpos = k_off + j * tk + _iota((tq, tk), 1)
            ok = kpos <= qpos
            if mode == "window":
                ok = jnp.logical_and(ok, qpos - kpos < WINDOW)
                ok = jnp.logical_and(ok, kpos >= 0)
            bias = jnp.where(ok, 0.0, NEG)
        if has_bm:
            blk = _iota((nbp, tk), 0)
            tok = (j * tk + _iota((nbp, tk), 1)) >> SLC_SHIFT
            expand = jnp.where(blk == tok, 1.0, 0.0).astype(BF16)
            bias = jnp.where(_dot(bm, expand) > 0.5, bias, NEG)
        if bias is not None:
            bias = _tile_rows(bias, cr // tq)
        if slab:
            v = _with_ones(v)
        new = []
        for c in range(n_chain):
            s = _dot_nt(q[c * cr:(c + 1) * cr].astype(BF16), k)
            if bias is not None:
                s = s + bias
            new.append(_online_step_ones(s, v, *state[c]) if slab else _online_step(s, v, *state[c]))
        return tuple(new)

    n_chain = reps if tq >= MIN_CHAIN_ROWS else 1
    cr = reps * tq // n_chain
    init = tuple((_online_init_ones(cr) if slab else _online_init(cr, dv)) for _ in range(n_chain))
    state = _unrolled_loop(lo, hi, body, init)
    for c in range(n_chain):
        if slab:
            acc = state[c][1]
            o_ref[0, 0, 0, c * cr:(c + 1) * cr, :] = acc[:, :HD] / acc[:, HD:HD + 1]
        else:
            o_ref[0, 0, 0, c * cr:(c + 1) * cr, :] = state[c][2] / state[c][1]


def _flash(q, k, v=None, *, reps, tq, tk, q_off, k_off, mode, bm=None, kv_blk=None):
    b, hk, nq, rows, _ = q.shape
    lk = k.shape[-2]
    assert rows == reps * tq and lk % tk == 0
    if kv_blk is None:
        dv = v.shape[3]
        kv_specs = [pl.BlockSpec((1, 1, lk, HD), lambda b_, h, i: (b_, h, 0, 0)),
                    pl.BlockSpec((1, 1, lk, dv), lambda b_, h, i: (b_, h, 0, 0))]
        args = [q, k, v]
    else:
        assert hk == 2 and v is None
        dv = HD
        kv_specs = [pl.BlockSpec((1, lk, 2 * LANES), lambda b_, h, i: (b_, 0, kv_blk))]
        args = [q, k]
    in_specs = [pl.BlockSpec((1, 1, 1, rows, HD), lambda b_, h, i: (b_, h, i, 0, 0))] + kv_specs
    if bm is not None:
        in_specs.append(pl.BlockSpec((1, 1, tq, bm.shape[3]), lambda b_, h, i: (b_, h, i, 0)))
        args.append(bm)
    kern = functools.partial(_flash_kernel, reps=reps, tq=tq, tk=tk, q_off=q_off, k_off=k_off, mode=mode,
                             has_bm=bm is not None, slab=kv_blk is not None)
    return pl.pallas_call(
        kern,
        grid=(b, hk, nq),
        in_specs=in_specs,
        out_specs=pl.BlockSpec((1, 1, 1, rows, dv), lambda b_, h, i: (b_, h, i, 0, 0)),
        out_shape=jax.ShapeDtypeStruct((b, hk, nq, rows, dv), F32),
        compiler_params=_cparams(("parallel", "parallel", "arbitrary")),
        name="flash_" + mode + ("_blockmask" if bm is not None else ""),
    )(*args)


def _softplus(z):
    return jnp.maximum(z, 0.0) + jnp.log(1.0 + jnp.exp(-jnp.abs(z)))


def _later_matrix(n):
    return jnp.where(_iota((n, n), 0) > _iota((n, n), 1), 1.0, 0.0).astype(BF16)


def _sb_kernel(q_ref, k_ref, v_ref, o_ref, *, t):
    i = pl.program_id(2)
    th = t // ROW_SPLIT
    lane = _iota((th, LANES), 1)
    later = _later_matrix(t)
    qs = []
    for part in range(ROW_SPLIT):
        qf = q_ref[0, part * th:(part + 1) * th, :] * SCALE
        qs.append((jnp.where(lane < HD, qf, 0.0).astype(BF16), jnp.where(lane >= HD, qf, 0.0).astype(BF16)))

    def tile(j, carries, outs, masked):
        k0 = pl.multiple_of(j * t, t)
        k = k_ref[0, pl.ds(k0, t), :].astype(BF16)
        v = v_ref[0, pl.ds(k0, t), :].astype(BF16)
        new_carries, new_outs = [], []
        for part in range(ROW_SPLIT):
            heads = []
            for hd in range(2):
                carry = carries[2 * part + hd]
                z = _dot_nt(qs[part][hd], k)
                sp = _softplus(z)
                log_sig = z - sp
                if masked:
                    vis = _iota((th, t), 1) < part * th + _iota((th, t), 0)
                    sp = jnp.where(vis, sp, 0.0)
                a = jnp.exp(log_sig - _dot(sp.astype(BF16), later) - carry)
                if masked:
                    a = jnp.where(vis, a, 0.0)
                heads.append(_dot(a.astype(BF16), v))
                new_carries.append(carry + jnp.sum(sp, axis=1, keepdims=True))
            new_outs.append(outs[part] + jnp.where(lane < HD, heads[0], heads[1]))
        return tuple(new_carries), tuple(new_outs)

    def min_carry(carries):
        m = jnp.min(carries[0])
        for c in carries[1:]:
            m = jnp.minimum(m, jnp.min(c))
        return m

    zero = jnp.zeros((th, 1), F32)
    carries, outs = tile(i, (zero,) * (2 * ROW_SPLIT), (jnp.zeros((th, LANES), F32),) * ROW_SPLIT, True)

    def cond(st):
        return jnp.logical_and(st[0] < i, st[1] < SB_CUT)

    def body(st):
        carries, outs = tile(i - 1 - st[0], st[2], st[3], False)
        return st[0] + 1, min_carry(carries), carries, outs

    _, _, carries, outs = lax.while_loop(cond, body, (jnp.int32(0), min_carry(carries), carries, outs))
    for part in range(ROW_SPLIT):
        o_ref[0, part * th:(part + 1) * th, :] = outs[part]


def _sb(proj3, *, t, q_slab, k_slab, v_slab, n_slabs):
    b, tt, _ = proj3.shape
    assert tt % t == 0
    kern = functools.partial(_sb_kernel, t=t)
    return pl.pallas_call(
        kern,
        grid=(b, n_slabs, tt // t),
        in_specs=[
            pl.BlockSpec((1, t, LANES), lambda b_, p, i: (b_, i, q_slab + p)),
            pl.BlockSpec((1, tt, LANES), lambda b_, p, i: (b_, 0, k_slab + p)),
            pl.BlockSpec((1, tt, LANES), lambda b_, p, i: (b_, 0, v_slab + p)),
        ],
        out_specs=pl.BlockSpec((1, t, LANES), lambda b_, p, i: (b_, i, p)),
        out_shape=jax.ShapeDtypeStruct((b, tt, n_slabs * LANES), F32),
        compiler_params=_cparams(("parallel", "parallel", "arbitrary")),
        name="sb_prompt",
    )(proj3, proj3, proj3)


def _lam(lv, lam_init):
    a = jnp.sum(jnp.sum(lv[0:1] * lv[1:2], axis=1, keepdims=True), axis=0, keepdims=True)
    b = jnp.sum(jnp.sum(lv[2:3] * lv[3:4], axis=1, keepdims=True), axis=0, keepdims=True)
    return jnp.exp(a) - jnp.exp(b) + lam_init


def _df_kernel(lv_ref, g_ref, q_ref, k_ref, v_ref, o_ref, *, t, lam_init):
    i = pl.program_id(2)
    th = t // ROW_SPLIT
    lane = _iota((th, LANES), 1)
    qs = []
    for part in range(ROW_SPLIT):
        qf = q_ref[0, part * th:(part + 1) * th, :] * SCALE
        qs.append((jnp.where(lane < HD, qf, 0.0).astype(BF16), jnp.where(lane >= HD, qf, 0.0).astype(BF16)))

    def tile(j, state, masked):
        k0 = pl.multiple_of(j * t, t)
        k = k_ref[0, pl.ds(k0, t), :].astype(BF16)
        v = _with_ones(v_ref[0, pl.ds(k0, t), :].astype(BF16))
        new = []
        for part in range(ROW_SPLIT):
            for mp in range(2):
                s = _dot_nt(qs[part][mp], k)
                if masked:
                    s = jnp.where(_iota((th, t), 1) <= part * th + _iota((th, t), 0), s, NEG)
                new.append(_online_step_ones(s, v, *state[2 * part + mp]))
        return tuple(new)

    init = tuple((jnp.full((th, 1), NEG, F32), jnp.zeros((th, 2 * LANES), F32)) for _ in range(2 * ROW_SPLIT))
    state = _unrolled_loop(0, i, lambda j, st: tile(j, st, False), init)
    state = tile(i, state, True)
    lam = _lam(lv_ref[...], lam_init)
    for part in range(ROW_SPLIT):
        acc0, acc1 = state[2 * part][1], state[2 * part + 1][1]
        d = acc0[:, :LANES] / acc0[:, LANES:LANES + 1] - lam * (acc1[:, :LANES] / acc1[:, LANES:LANES + 1])
        o_ref[0, part * th:(part + 1) * th, :] = _rms_rows(d, g_ref[...]) * (1.0 - lam_init)


def _df(proj3, lv, subln, *, t, q_slab, k_slab, v_slab, n_slabs, lam_init):
    b, tt, _ = proj3.shape
    assert tt % t == 0
    kern = functools.partial(_df_kernel, t=t, lam_init=lam_init)
    return pl.pallas_call(
        kern,
        grid=(b, n_slabs, tt // t),
        in_specs=[
            pl.BlockSpec(lv.shape, lambda b_, p, i: (0, 0)),
            pl.BlockSpec(subln.shape, lambda b_, p, i: (0, 0)),
            pl.BlockSpec((1, t, LANES), lambda b_, p, i: (b_, i, q_slab + p)),
            pl.BlockSpec((1, tt, LANES), lambda b_, p, i: (b_, 0, k_slab + p)),
            pl.BlockSpec((1, tt, LANES), lambda b_, p, i: (b_, 0, v_slab + p)),
        ],
        out_specs=pl.BlockSpec((1, t, LANES), lambda b_, p, i: (b_, i, p)),
        out_shape=jax.ShapeDtypeStruct((b, tt, n_slabs * LANES), F32),
        compiler_params=_cparams(("parallel", "parallel", "arbitrary")),
        name="df_prompt",
    )(lv, subln, proj3, proj3, proj3)


def _dec_even_kernel(*refs, pp, ts, lam_init):
    tbl_ref, qt_ref = refs[0], refs[1]
    k_refs = refs[2:2 + pp]
    v_refs = refs[2 + pp:2 + 2 * pp]
    kn_ref, vn_ref, lv_ref, g_ref, o_ref, later_ref, st_ref, asb_ref, adf_ref = refs[2 + 2 * pp:]
    del tbl_ref
    s_id = pl.program_id(1)
    nk = pp * PAGE
    half = LANES // 2
    qt = qt_ref[0]

    def col_of(row):
        return jnp.transpose(jnp.broadcast_to(row, (SUBLANES, LANES)))[half:, 0:1]

    def tile(kt, vt, later, vis_sb, vis_df):
        carry, m, l = st_ref[0:1, :], st_ref[1:2, :], st_ref[2:3, :]
        zt = _dot(kt, qt)
        lane = _iota(zt.shape, 1)
        sp = _softplus(zt)
        log_sig = zt - sp
        s = zt
        if vis_sb is not None:
            sp = jnp.where(vis_sb, sp, 0.0)
            s = jnp.where(vis_df, s, NEG)
        a = jnp.exp(log_sig - _dot(later, sp.astype(BF16)) - carry)
        if vis_sb is not None:
            a = jnp.where(vis_sb, a, 0.0)
        m_new = jnp.maximum(m, jnp.max(s, axis=0, keepdims=True))
        p = jnp.exp(s - m_new)
        alpha = jnp.exp(m - m_new)
        st_ref[0:1, :] = carry + jnp.sum(sp, axis=0, keepdims=True)
        st_ref[1:2, :] = m_new
        st_ref[2:3, :] = alpha * l + jnp.sum(p, axis=0, keepdims=True)
        w = jnp.transpose(jnp.where(lane < half, a, p)).astype(BF16)
        asb_ref[...] += _dot(w[:half], vt[:, :SB_W])
        adf_ref[...] = col_of(alpha) * adf_ref[...] + _dot(w[half:], vt[:, SB_W:])

    @pl.when(s_id == 0)
    def _():
        later_ref[...] = jnp.where(_iota((nk, nk), 1) > _iota((nk, nk), 0), 1.0, 0.0).astype(BF16)
        st_ref[...] = jnp.where(_iota(st_ref.shape, 0) == 1, NEG, 0.0)
        asb_ref[...] = jnp.zeros_like(asb_ref)
        adf_ref[...] = jnp.zeros_like(adf_ref)
        key = _iota((PAGE, LANES), 0)
        tok = _iota((PAGE, LANES), 1) & (ts - 1)
        tile(kn_ref[0].astype(BF16), vn_ref[0].astype(BF16), later_ref[0:PAGE, 0:PAGE], key < tok, key <= tok)

    kt = jnp.concatenate([r[0].astype(BF16) for r in k_refs], axis=0)
    vt = jnp.concatenate([r[0].astype(BF16) for r in v_refs], axis=0)
    tile(kt, vt, later_ref[...], None, None)

    @pl.when(s_id == pl.num_programs(1) - 1)
    def _():
        row = _iota((half, SB_W), 0)
        lane = _iota((half, SB_W), 1)

        def fold(x):
            out = x[0:ts]
            for u in range(1, half // ts):
                out = out + x[u * ts:(u + 1) * ts]
            return out

        o_sb = fold(jnp.where((row >> 3) == (lane >> HD_SHIFT), asb_ref[...], 0.0))
        pn = adf_ref[...] / col_of(st_ref[2:3, :])
        same_head = (row >> 4) == (lane >> 7)
        o0 = fold(jnp.where(jnp.logical_and(same_head, ((row >> 3) & 1) == 0), pn, 0.0))
        o1 = fold(jnp.where(jnp.logical_and(same_head, ((row >> 3) & 1) == 1), pn, 0.0))
        d = o0 - _lam(lv_ref[...], lam_init) * o1
        parts = [o_sb]
        for h in range(N_DF):
            parts.append(_rms_rows(d[:, h * LANES:(h + 1) * LANES], g_ref[...]) * (1.0 - lam_init))
        o_ref[0] = jnp.concatenate(parts, axis=1)


def _dec_even(qt, pool_k, pool_v, table, k_new, v_new, lv, subln, *, pp, ts, lam_init):
    b, n_pages = table.shape
    width = pool_k.shape[2]
    assert n_pages % pp == 0 and ts == SUBLANES and k_new.shape[1] == PAGE
    n_steps = n_pages // pp

    def page_map(u):
        return lambda b_, s, t: (t[b_, n_pages - (s + 1) * pp + u], 0, 0)

    page_specs = [pl.BlockSpec((1, PAGE, width), page_map(u)) for u in range(pp)]
    new_spec = pl.BlockSpec((1, PAGE, width), lambda b_, s, t: (b_, 0, 0))
    grid_spec = pltpu.PrefetchScalarGridSpec(
        num_scalar_prefetch=1,
        grid=(b, n_steps),
        in_specs=[pl.BlockSpec((1,) + qt.shape[1:], lambda b_, s, t: (b_, 0, 0))] + page_specs + page_specs
        + [new_spec, new_spec, pl.BlockSpec(lv.shape, lambda b_, s, t: (0, 0)),
           pl.BlockSpec(subln.shape, lambda b_, s, t: (0, 0))],
        out_specs=pl.BlockSpec((1, ts, width), lambda b_, s, t: (b_, 0, 0)),
        scratch_shapes=[
            pltpu.VMEM((pp * PAGE, pp * PAGE), BF16),
            pltpu.VMEM((SUBLANES, LANES), F32),
            pltpu.VMEM((LANES // 2, SB_W), F32),
            pltpu.VMEM((LANES // 2, DF_W), F32),
        ],
    )
    kern = functools.partial(_dec_even_kernel, pp=pp, ts=ts, lam_init=lam_init)
    return pl.pallas_call(
        kern,
        grid_spec=grid_spec,
        out_shape=jax.ShapeDtypeStruct((b, ts, width), F32),
        compiler_params=_cparams(("parallel", "arbitrary")),
        name="dec_even",
    )(table, qt, *([pool_k] * pp), *([pool_v] * pp), k_new, v_new, lv, subln)


def _compress_kernel(t_ref, w1_ref, w1g_ref, pe_ref, w2_ref, g_ref, o_ref):
    kind = pl.program_id(1)
    n = t_ref.shape[1] // CMP_STRIDE
    a1 = [None] * N_NSA_KV
    a2 = [None] * N_NSA_KV
    for p in range(CMP_STRIDE):
        rows = t_ref[0, pl.ds(p, n, stride=CMP_STRIDE), :].astype(BF16)
        for g in range(N_NSA_KV):
            d1 = _dot(rows, w1g_ref[0, g, p])
            d2 = _dot(rows, w1g_ref[0, g, CMP_STRIDE + p])
            a1[g] = d1 if a1[g] is None else a1[g] + d1
            a2[g] = d2 if a2[g] is None else a2[g] + d2
    pe = _dot(jnp.broadcast_to(pe_ref[0], (SUBLANES, CMP_LEN * HD)).astype(BF16), w1_ref[0])[0:1]
    for g in range(N_NSA_KV):
        nxt = jnp.where(_iota(a2[g].shape, 0) < n - 1, pltpu.roll(a2[g], n - 1, 0), 0.0)
        hid = a1[g] + nxt + pe
        hid = hid * (1.0 / (1.0 + jnp.exp(-hid)))
        out = _dot(hid.astype(BF16), w2_ref[0])
        normed = _rms_rows(out, g_ref[...])
        o_ref[0, 0, g] = jnp.where(kind == 0, normed, out)


def _compress(t, slab0, w1, pe_flat, w2, gain):
    b, lk, _ = t.shape
    g = N_NSA_KV
    n = lk // CMP_STRIDE
    pieces = w1.reshape(2, CMP_LEN, HD, w1.shape[2])
    w1g = jnp.stack([jnp.pad(pieces, ((0, 0), (0, 0), (HD * gi, HD * (g - 1 - gi)), (0, 0))) for gi in range(g)],
                    axis=1)
    return pl.pallas_call(
        _compress_kernel,
        grid=(b, 2),
        in_specs=[
            pl.BlockSpec((1, lk, LANES), lambda b_, k: (b_, 0, slab0 + k)),
            pl.BlockSpec((1,) + w1.shape[1:], lambda b_, k: (k, 0, 0)),
            pl.BlockSpec((1,) + w1g.shape[1:], lambda b_, k: (k, 0, 0, 0, 0)),
            pl.BlockSpec((1,) + pe_flat.shape[1:], lambda b_, k: (k, 0, 0)),
            pl.BlockSpec((1,) + w2.shape[1:], lambda b_, k: (k, 0, 0)),
            pl.BlockSpec(gain.shape, lambda b_, k: (0, 0)),
        ],
        out_specs=pl.BlockSpec((1, 1, g, n, HD), lambda b_, k: (b_, k, 0, 0, 0)),
        out_shape=jax.ShapeDtypeStruct((b, 2, g, n, HD), F32),
        compiler_params=_cparams(("parallel", "arbitrary")),
        name="compress",
    )(t, w1, w1g, pe_flat, w2, gain)


def _nsa_cmp_kernel(q_ref, kc_ref, vc_ref, o_ref, sel_ref, *, tq, q_off, n_sel):
    i = pl.program_id(2)
    q = q_ref[0, 0, 0].astype(BF16)
    kc = kc_ref[0, 0]
    vc = vc_ref[0, 0]
    ncol = kc.shape[0]
    nbp = ncol // CMP_PER_SLC
    q_lo = q_off + i * tq
    qpos = q_lo + _iota((tq, ncol), 0)
    col = _iota((tq, ncol), 1)
    jj = jnp.zeros_like(col)
    for u in range(1, CMP_PER_SLC):
        jj = jj + jnp.where(col >= u * nbp, 1, 0)
    c_end = (col - jj * nbp) * SLC_BLOCK + jj * CMP_STRIDE + (CMP_LEN - 1)
    maskf = _tile_rows(jnp.where(c_end <= qpos, 1.0, 0.0), NSA_GROUP)
    keep = maskf > 0.5
    s = jnp.where(keep, _dot_nt(q, kc), NEG)
    m = jnp.max(s, axis=1, keepdims=True)
    p = jnp.where(keep, jnp.exp(s - m), 0.0)
    p = p / jnp.maximum(jnp.sum(p, axis=1, keepdims=True), 1e-30)
    o_ref[0, 0, 0] = _dot(p.astype(BF16), vc)
    pg = p[0:tq]
    for r in range(1, NSA_GROUP):
        pg = pg + p[r * tq:(r + 1) * tq]
    imp = pg[:, 0:nbp]
    for j in range(1, CMP_PER_SLC):
        imp = imp + pg[:, j * nbp:(j + 1) * nbp]
    blk = _iota((tq, nbp), 1)
    qp = q_lo + _iota((tq, nbp), 0)
    q_blk = qp >> SLC_SHIFT
    visible = blk * SLC_BLOCK <= qp
    forced = jnp.logical_or(blk == 0, jnp.logical_and(blk <= q_blk, blk > q_blk - N_LOCAL))
    score = jnp.where(visible, jnp.where(forced, FORCE, imp), NEG)
    sel0 = jnp.zeros((tq, nbp), F32)
    rounds = n_sel
    if n_sel > N_LOCAL + 1:
        pre = jnp.logical_and(forced, visible)
        sel0 = jnp.where(pre, 1.0, 0.0)
        score = jnp.where(pre, -jnp.inf, score)
        rounds = n_sel - (N_LOCAL + 1)
    rc = min(tq, SEL_CHUNK_ROWS)
    blkf = _iota((rc, nbp), 1).astype(F32)
    scores = [score[c * rc:(c + 1) * rc] for c in range(tq // rc)]
    sels = [sel0[c * rc:(c + 1) * rc] for c in range(tq // rc)]
    for _ in range(rounds):
        for c in range(len(scores)):
            top = jnp.max(scores[c], axis=1, keepdims=True)
            idx = jnp.min(jnp.where(scores[c] == top, blkf, 1e9), axis=1, keepdims=True)
            pick = blkf == idx
            sels[c] = jnp.where(pick, 1.0, sels[c])
            scores[c] = jnp.where(pick, -jnp.inf, scores[c])
    for c in range(len(scores)):
        sel_ref[0, 0, c * rc:(c + 1) * rc, :] = sels[c]


def _nsa_cmp(q, kc, vc, *, tq, q_off, n_sel):
    b, g, nq, rows, _ = q.shape
    ncol = kc.shape[2]
    nbp = ncol // CMP_PER_SLC
    kern = functools.partial(_nsa_cmp_kernel, tq=tq, q_off=q_off, n_sel=n_sel)
    return pl.pallas_call(
        kern,
        grid=(b, g, nq),
        in_specs=[
            pl.BlockSpec((1, 1, 1, rows, HD), lambda b_, g_, i: (b_, g_, i, 0, 0)),
            pl.BlockSpec((1, 1, ncol, HD), lambda b_, g_, i: (b_, g_, 0, 0)),
            pl.BlockSpec((1, 1, ncol, HD), lambda b_, g_, i: (b_, g_, 0, 0)),
        ],
        out_specs=[
            pl.BlockSpec((1, 1, 1, rows, HD), lambda b_, g_, i: (b_, g_, i, 0, 0)),
            pl.BlockSpec((1, 1, tq, nbp), lambda b_, g_, i: (b_, g_, i, 0)),
        ],
        out_shape=[
            jax.ShapeDtypeStruct((b, g, nq, rows, HD), F32),
            jax.ShapeDtypeStruct((b, g, nq * tq, nbp), F32),
        ],
        compiler_params=_cparams(("parallel", "parallel", "arbitrary")),
        name="nsa_cmp",
    )(q, kc, vc)


def _dsa_kernel(qi_ref, wi_ref, qd_ref, kk_ref, o_ref, key_ref, *, tq, tk, q_off, n_top, offs):
    i = pl.program_id(1)
    qi = qi_ref[0, 0]
    qd = qd_ref[0, 0]
    w = wi_ref[0] * (N_IDX ** -0.5)
    n_kt = kk_ref.shape[1] // tk
    kd_off, ki_off, vd_off = offs

    def cols(ref, k0, off):
        return ref[0, pl.ds(k0, tk), :][:, off:off + HD].astype(BF16)

    q_lo = q_off + i * tq
    hi = jnp.minimum(lax.div(q_lo + tq - 1, tk) + 1, n_kt)
    qpos = q_lo + _iota((tq, tk), 0)
    n_chain = N_DSA if tq >= MIN_CHAIN_ROWS else 1
    cr = N_DSA * tq // n_chain

    def visible(j):
        return (j * tk + _iota((tq, tk), 1)) <= qpos

    def score_tile(j, _):
        k0 = pl.multiple_of(j * tk, tk)
        k = cols(kk_ref, k0, ki_off)
        tot = None
        for c in range(n_chain):
            sc = jnp.maximum(_dot_nt(qi[c * cr:(c + 1) * cr].astype(BF16), k), 0.0)
            for u in range(cr // tq):
                h = c * (cr // tq) + u
                term = w[:, h:h + 1] * sc[u * tq:(u + 1) * tq]
                tot = term if tot is None else tot + term
        tot = jnp.where(visible(j), tot, NEG)
        bits = pltpu.bitcast(tot, jnp.int32)
        key = jnp.where(bits < 0, bits ^ jnp.int32(0x7FFFFFFF), bits)
        key_ref[:, pl.ds(k0, tk)] = jnp.where(tot == 0.0, 0, key)
        return 0

    _unrolled_loop(0, hi, score_tile, 0)

    def count_ge(c):
        def body(j, acc):
            blk = key_ref[:, pl.ds(pl.multiple_of(j * tk, tk), tk)]
            hit = jnp.where(blk >= c, 1.0, 0.0)
            part = hit[:, 0:LANES]
            for u in range(1, tk // LANES):
                part = part + hit[:, u * LANES:(u + 1) * LANES]
            return acc + part
        acc = _unrolled_loop(0, hi, body, jnp.zeros((tq, LANES), F32))
        return jnp.sum(acc, axis=1, keepdims=True)

    kf = float(n_top)
    tau = jnp.where(count_ge(jnp.zeros((tq, 1), jnp.int32)) >= kf, 0, INT_MIN).astype(jnp.int32)

    def bit_body(t, tau):
        cand = tau + jnp.left_shift(jnp.int32(1), 30 - t)
        return jnp.where(count_ge(cand) >= kf, cand, tau)

    tau = lax.fori_loop(0, 31, bit_body, tau)
    need = kf - count_ge(tau + 1)
    before = jnp.where(_iota((tk, tk), 0) < _iota((tk, tk), 1), 1.0, 0.0).astype(BF16)

    def attend(j, carry):
        state, n_eq = carry
        k0 = pl.multiple_of(j * tk, tk)
        key = key_ref[:, pl.ds(k0, tk)]
        eqf = jnp.where(key == tau, 1.0, 0.0)
        rank = n_eq + _dot(eqf.astype(BF16), before)
        kept = jnp.logical_or(key > tau, jnp.logical_and(key == tau, rank < need))
        bias = jnp.where(jnp.logical_and(kept, visible(j)), 0.0, NEG)
        k = cols(kk_ref, k0, kd_off)
        v1 = _with_ones(cols(kk_ref, k0, vd_off))
        bias = _tile_rows(bias, cr // tq)
        new = []
        for c in range(n_chain):
            s = _dot_nt(qd[c * cr:(c + 1) * cr].astype(BF16), k) + bias
            new.append(_online_step_ones(s, v1, *state[c]))
        return tuple(new), n_eq + jnp.sum(eqf, axis=1, keepdims=True)

    init = (tuple(_online_init_ones(cr) for _ in range(n_chain)), jnp.zeros((tq, 1), F32))
    state, _ = _unrolled_loop(0, hi, attend, init)
    for c in range(n_chain):
        acc = state[c][1]
        o_ref[0, 0, c * cr:(c + 1) * cr, :] = acc[:, :HD] / acc[:, HD:HD + 1]


def _dsa(qi, wi, qd, kk, *, width, blk, offs, tq, tk, q_off, n_top):
    b, nq, rows, _ = qi.shape
    lk = kk.shape[1]
    assert lk % tk == 0
    kern = functools.partial(_dsa_kernel, tq=tq, tk=tk, q_off=q_off, n_top=n_top, offs=offs)
    qspec = pl.BlockSpec((1, 1, rows, HD), lambda b_, i: (b_, i, 0, 0))
    return pl.pallas_call(
        kern,
        grid=(b, nq),
        in_specs=[qspec, pl.BlockSpec((1, tq, N_IDX), lambda b_, i: (b_, i, 0)), qspec,
                  pl.BlockSpec((1, lk, width), lambda b_, i: (b_, 0, blk))],
        out_specs=pl.BlockSpec((1, 1, rows, HD), lambda b_, i: (b_, i, 0, 0)),
        out_shape=jax.ShapeDtypeStruct((b, nq, rows, HD), F32),
        scratch_shapes=[pltpu.VMEM((tq, lk), jnp.int32)],
        compiler_params=_cparams(("parallel", "arbitrary")),
        name="dsa",
    )(qi, wi, qd, kk)


def _gather_kernel(*refs, pp, n_steps):
    pool_refs = refs[1:1 + pp]
    new_ref, o_ref = refs[1 + pp], refs[2 + pp]
    s = pl.program_id(1)

    def page(ref):
        mid = ref.shape[2:-1]
        if not mid:
            return ref[0]
        pieces = [ref[(0, slice(None)) + idx + (slice(None),)] for idx in np.ndindex(*mid)]
        return jnp.concatenate(pieces, axis=1)

    @pl.when(s < n_steps)
    def _():
        for u in range(pp):
            o_ref[0, u * PAGE:(u + 1) * PAGE, :] = page(pool_refs[u])

    @pl.when(s >= n_steps)
    def _():
        o_ref[...] = new_ref[...]


def _page_gather(pool, table, new, *, pp):
    b, n_pages = table.shape
    tail = pool.shape[2:]
    width = int(np.prod(tail))
    assert n_pages % pp == 0 and new.shape[1:] == (pp * PAGE, width)
    n_steps = n_pages // pp

    def page_map(u):
        return lambda b_, s, t: (t[b_, jnp.minimum(s, n_steps - 1) * pp + u], 0) + (0,) * len(tail)

    kern = functools.partial(_gather_kernel, pp=pp, n_steps=n_steps)
    grid_spec = pltpu.PrefetchScalarGridSpec(
        num_scalar_prefetch=1,
        grid=(b, n_steps + 1),
        in_specs=[pl.BlockSpec((1, PAGE) + tail, page_map(u)) for u in range(pp)]
        + [pl.BlockSpec((1, pp * PAGE, width), lambda b_, s, t: (b_, 0, 0))],
        out_specs=pl.BlockSpec((1, pp * PAGE, width), lambda b_, s, t: (b_, s, 0)),
    )
    return pl.pallas_call(
        kern,
        grid_spec=grid_spec,
        out_shape=jax.ShapeDtypeStruct((b, (n_pages + pp) * PAGE, width), pool.dtype),
        compiler_params=_cparams(("parallel", "arbitrary")),
        name="page_gather",
    )(table, *([pool] * pp), new)


def _rope_tables(pos):
    half = HD // 2
    inv = ROPE_THETA ** (-jnp.arange(half, dtype=F32) / half)
    ang = pos.astype(F32)[:, None] * inv[None, :]
    cos, sin = jnp.cos(ang), jnp.sin(ang)
    cos128 = jnp.tile(jnp.concatenate([cos, cos], axis=1), (1, LANES // HD))
    sin128 = jnp.tile(jnp.concatenate([-sin, sin], axis=1), (1, LANES // HD))
    return cos128, sin128


def _tile_gain(g):
    return jnp.tile(g.reshape(1, HD), (1, LANES // HD))


def _heads(a, b, t, h, d, scale=None):
    a = a.reshape(b, t, h, d)
    if scale is not None:
        a = a * scale
    return a.transpose(0, 2, 1, 3)


def _stack_q(a, tq):
    b, hk, r, t, d = a.shape
    return a.reshape(b, hk, r, t // tq, tq, d).transpose(0, 1, 3, 2, 4, 5).reshape(b, hk, t // tq, r * tq, d)


def _unstack_q(a, r, tq):
    b, hk, nq, _, d = a.shape
    a = a.reshape(b, hk, nq, r, tq, d).transpose(0, 2, 4, 1, 3, 5)
    return a.reshape(b * nq * tq, hk * r * d)


def _pad_rows(a, rows):
    return jnp.pad(a, ((0, 0), (0, rows - a.shape[1])) + ((0, 0),) * (a.ndim - 2))


def _even_mixer(x2, b, t, q_off, cs, past, prm, cfg):
    n = b * t
    program = [
        (0, SB_W, None, None, [(0, False)]),
        (3 * SB_W, DF_W, 0, None, [(SB_W, True)]),
        (SB_W, SB_W, None, None, [(SB_W + DF_W, False)]),
        (3 * SB_W + DF_W, DF_W, 1, None, [(2 * SB_W + DF_W, True)]),
        (2 * SB_W, SB_W, None, None, [(2 * (SB_W + DF_W), False)]),
        (3 * SB_W + 2 * DF_W, DF_W, None, None, [(3 * SB_W + 2 * DF_W, False)]),
    ]
    gains = jnp.concatenate([_tile_gain(prm["df_qk_gain"][0]), _tile_gain(prm["df_qk_gain"][1])], axis=0)
    proj = _mm([x2, cs[0], cs[1]], [prm["g0"], gains], prm["ev_w_in"], _lhs_norm, program, 3 * (SB_W + DF_W),
               tm=cfg["tm"], rope_idx=(1, 2), gains_idx=1)
    mw = SB_W + DF_W
    new_k = proj[:, mw:2 * mw].reshape(b, t, mw)
    new_v = proj[:, 2 * mw:3 * mw].reshape(b, t, mw)
    lam_init = 0.8 - 0.6 * math.exp(-0.3 * prm["layer"])
    lv, subln = prm["df_lambda"], prm["df_subln_gain"].reshape(1, 2 * HD)
    d = x2.shape[1]
    if past is None:
        proj3 = proj.reshape(b, t, 3 * mw)
        ns = SB_W // LANES
        o_sb = _sb(proj3, t=cfg["t_even"], q_slab=0, k_slab=2 * ns, v_slab=4 * ns, n_slabs=ns)
        o_df = _df(proj3, lv, subln, t=cfg["t_even"], q_slab=ns, k_slab=3 * ns, v_slab=5 * ns, n_slabs=ns,
                   lam_init=lam_init)
        return _mm([o_sb.reshape(n, SB_W), o_df.reshape(n, DF_W), x2], [], prm["ev_w_out"], _lhs_cat2,
                   [(0, d, None, None, [(0, False)])], d, tm=cfg["tm"], res_idx=2), new_k, new_v
    pool_k, pool_v, table = past
    qcat = proj[:, :mw].reshape(b, t, mw).transpose(0, 2, 1) * SCALE
    qt = (jnp.tile(qcat, (1, 1, LANES // t)) * _dec_even_mask(t)).astype(BF16)
    o = _dec_even(qt, pool_k, pool_v, table, _pad_rows(new_k, PAGE), _pad_rows(new_v, PAGE), lv, subln,
                  pp=cfg["pp"], ts=t, lam_init=lam_init)
    return _mm([o.reshape(n, mw), x2], [], prm["ev_w_out"], _lhs_plain,
               [(0, d, None, None, [(0, False)])], d, tm=cfg["tm"], res_idx=1), new_k, new_v


def _dec_even_mask(ts):
    f = np.arange(SB_W + DF_W)[:, None]
    c = np.arange(LANES)[None, :]
    half = LANES // 2
    sb = (f < SB_W) & (c < half) & (f // HD == c // ts)
    df = (f >= SB_W) & (c >= half) & ((f - SB_W) // HD == (c - half) // ts)
    return jnp.asarray((sb | df).astype(np.float32))


def _odd_layout():
    widths = (N_NSA * HD, 128, 128, 128, 128, 128, 128, N_NSA * 3, N_DSA * HD, HD, HD, N_IDX * HD, HD, N_IDX)
    offs = np.concatenate([[0], np.cumsum(widths)])
    (q_n, k_c, v_c, k_s, v_s, k_w, v_w, gate, q_d, k_d, v_d, q_i, k_i, w_i) = [
        (int(offs[j]), int(offs[j + 1])) for j in range(len(widths))]
    pieces = [q_n, (k_c[0], v_w[1]), q_d, q_i, k_d, k_i, v_d, w_i, ("pad", HD - N_IDX), gate,
              ("pad", LANES - N_NSA * 3)]
    program = [
        (0, 512, 0, None, [(0, False), (512, True)]),
        (512, 128, None, None, [(1024, False)]),
        (640, 128, None, None, [(1152, False)]),
        (768, 128, 1, None, [(1280, True)]),
        (896, 128, None, None, [(1408, False)]),
        (1024, 128, 2, None, [(1536, True)]),
        (1152, 128, None, None, [(1664, False)]),
        (1280, 512, 3, None, [(1792, True)]),
        (1792, 512, None, None, [(2304, True)]),
        (2304, 128, 4, None, [(2816, True)]),
        (2432, 128, None, None, [(2944, False)]),
        (2560, 128, None, "sigmoid", [(3072, False)]),
    ]
    return pieces, program, 3200


def _permute_cols(w, pieces):
    cols = []
    for p in pieces:
        if p[0] == "pad":
            cols.append(jnp.zeros((w.shape[0], p[1]), w.dtype))
        else:
            cols.append(w[:, p[0]:p[1]])
    return jnp.concatenate(cols, axis=1)


def _odd_mixer(x2, b, t, t_real, q_off, cs, past, prm, cfg):
    n = b * t
    pieces, program, out_cols = _odd_layout()
    ng, dg = prm["nsa_qk_gain"], prm["dsa_qk_gain"]
    gains = jnp.concatenate([
        _tile_gain(ng[0]), _tile_gain(ng[2]), _tile_gain(ng[3]), _tile_gain(dg[0]),
        jnp.concatenate([dg[1], dg[2]]).reshape(1, LANES)], axis=0)
    w_in = _permute_cols(prm["od_w_in"], pieces)
    proj = _mm([x2, cs[0], cs[1]], [prm["g0"], gains], w_in, _lhs_norm, program, out_cols,
               tm=cfg["tm"], rope_idx=(1, 2), gains_idx=1)
    g = N_NSA_KV
    new_nsa = proj[:, 1024:1536].reshape(b, t, 4 * g * HD)
    new_win = proj[:, 1536:1792].reshape(b, t, 2, g, HD)
    new_dsa = jnp.concatenate([proj[:, 2816:2880], proj[:, 2944:3008], proj[:, 2880:2944]], axis=1).reshape(b, t, 3 * HD)
    w_i = proj[:, 3008:3008 + N_IDX].reshape(b, t, N_IDX)
    gate = proj[:, 3072:3072 + N_NSA * 3].reshape(n, N_NSA, 3)
    if past is None:
        proj3 = proj.reshape(b, t, out_cols)
        nsa_src, win_src = (proj3, 8), (proj3, 12)
        dsa_src = dict(kk=proj3, width=2 * LANES, blk=11, offs=(0, HD, 2 * HD))
        win_off = 0
        new_state = new_win[:, -min(WINDOW, t):]
        l_real = t
    else:
        pool_nsa, pool_dsa, state, table = past
        nsa_buf = _page_gather(pool_nsa, table, _pad_rows(new_nsa, cfg["pp"] * PAGE), pp=cfg["pp"])
        dsa_buf = _page_gather(pool_dsa, table, _pad_rows(new_dsa, cfg["pp"] * PAGE), pp=cfg["pp"])
        wb = state.shape[1]
        win_buf = _pad_rows(jnp.concatenate([state, new_win], axis=1), wb + cfg["tk_win"])
        nsa_src, win_src = (nsa_buf, 0), (win_buf.reshape(b, wb + cfg["tk_win"], 2 * g * HD), 0)
        dsa_src = dict(kk=dsa_buf, width=3 * HD, blk=0, offs=(0, 2 * HD, HD))
        win_off = q_off - wb
        new_state = jnp.concatenate([state, new_win[:, :t_real]], axis=1)[:, -wb:]
        l_real = table.shape[1] * PAGE + t_real
    lk = nsa_src[0].shape[1]
    tq = cfg["tq"]

    n_cmp = lk // CMP_STRIDE
    pe_flat = prm["cmp_pe"].reshape(2, 1, CMP_LEN * HD)
    cmp = _compress(nsa_src[0], nsa_src[1], prm["cmp_w1"], pe_flat, prm["cmp_w2"], ng[1].reshape(1, HD))
    nb = n_cmp // CMP_PER_SLC
    nbp = -(-nb // LANES) * LANES
    cmp = cmp.reshape(b, 2, g, nb, CMP_PER_SLC, HD).transpose(0, 1, 2, 4, 3, 5)
    cmp = jnp.pad(cmp, ((0, 0),) * 4 + ((0, nbp - nb), (0, 0))).reshape(b, 2, g, CMP_PER_SLC * nbp, HD).astype(BF16)

    def group_q(cols, rows):
        a = _heads(cols, b, t, N_NSA, HD, SCALE).reshape(b, g, NSA_GROUP, t, HD)
        return _stack_q(a, rows)

    q_n = group_q(proj[:, 0:512], tq)
    q_r = group_q(proj[:, 512:1024], cfg["tq_slc"])
    n_blk = -(-l_real // SLC_BLOCK)
    o_c, sel = _nsa_cmp(q_n, cmp[:, 0], cmp[:, 1], tq=tq, q_off=q_off, n_sel=min(N_SLC, n_blk))

    tq_s = cfg["tq_slc"]
    o_s = _flash(q_r, nsa_src[0], kv_blk=(nsa_src[1] + 2) // 2, reps=NSA_GROUP, tq=tq_s,
                 tk=cfg["tk"], q_off=q_off, k_off=0, mode="causal", bm=sel)
    o_w = _flash(q_r, win_src[0], kv_blk=win_src[1] // 2, reps=NSA_GROUP, tq=tq_s,
                 tk=cfg["tk_win"], q_off=q_off, k_off=win_off, mode="window")

    tq_d = cfg["tq_dsa"]
    q_d = _stack_q(_heads(proj[:, 1792:2304], b, t, N_DSA, HD, SCALE)[:, None], tq_d)[:, 0]
    q_i = _stack_q(_heads(proj[:, 2304:2816], b, t, N_IDX, HD, IDX_SCALE)[:, None], tq_d)[:, 0]
    o_d = _dsa(q_i, w_i, q_d, tq=tq_d, tk=cfg["tk"], q_off=q_off, n_top=min(DSA_TOPK_MAX, l_real // 4), **dsa_src)

    o_c = _unstack_q(o_c, NSA_GROUP, tq)
    o_s, o_w = (_unstack_q(o, NSA_GROUP, tq_s) for o in (o_s, o_w))
    o_d = _unstack_q(o_d[:, None], N_DSA, tq_d)
    gfull = [jnp.repeat(gate[:, :, j], HD, axis=1) for j in range(3)]
    d = x2.shape[1]
    out = _mm([o_c, o_s, o_w, o_d] + gfull + [x2], [], prm["od_w_out"], _lhs_odd,
              [(0, d, None, None, [(0, False)])], d, tm=cfg["tm"], res_idx=7)
    return out, new_nsa, new_dsa, new_state


def _cross(x2, b, t, mem_k, mem_v, prm, cfg):
    d = x2.shape[1]
    xw = N_XH * HD
    q = _mm([x2], [prm["g1"], _tile_gain(prm["x_gq"])], prm["x_wq"], _lhs_norm,
            [(0, xw, 0, None, [(0, False)])], xw, tm=cfg["tm"], gains_idx=1)
    tq = cfg["tq_cross"]
    qh = _stack_q(_heads(q, b, t, N_XH, HD, SCALE)[:, :, None], tq)
    o = _flash(qh, mem_k, mem_v, reps=1, tq=tq, tk=mem_k.shape[2], q_off=0, k_off=0, mode="full")
    o = _unstack_q(o, 1, tq)
    return _mm([o, x2], [], prm["x_wo"], _lhs_plain, [(0, d, None, None, [(0, False)])], d, tm=cfg["tm"], res_idx=1)


def _memory_kv(mem2, prm):
    xw = N_XH * HD
    w = jnp.concatenate([prm["x_wk"], prm["x_wv"]], axis=1)
    return _mm([mem2], [_tile_gain(prm["x_gk"])], w, _lhs_plain,
               [(0, xw, 0, None, [(0, False)]), (xw, xw, None, None, [(xw, False)])], 2 * xw,
               tm=min(256, mem2.shape[0]), gains_idx=0)


def _run_group(x, q_off, t_real, mem_kvs, pasts, layers, cfg):
    b, t, d = x.shape
    x2 = x.reshape(b * t, d)
    pos = q_off + jnp.arange(t, dtype=jnp.int32)
    cos, sin = _rope_tables(pos)
    cs = (jnp.tile(cos, (b, 1)), jnp.tile(sin, (b, 1)))
    outs = {}
    for li, prm in enumerate(layers):
        if li % 2 == 0:
            x2, nk, nv = _even_mixer(x2, b, t, q_off, cs, pasts[li], prm, cfg)
            outs["ek"], outs["ev"] = nk, nv
        else:
            x2, nn, nd, nw = _odd_mixer(x2, b, t, t_real, q_off, cs, pasts[li], prm, cfg)
            outs["on"], outs["od"], outs["ow"] = nn, nd, nw
        x2 = _cross(x2, b, t, mem_kvs[li][0], mem_kvs[li][1], prm, cfg)
        if li % 2 == 0:
            x2 = _ffn(x2, prm["g2"], prm["router"], prm["w1"], prm["w3"], prm["w2"], tm=cfg["tm_ffn"],
                      tf=cfg["tf"], routed=False)
        elif "moe_cap" in cfg and x2.shape[0] % cfg["tm_moe"] == 0:
            x2 = _moe(x2, prm["g2"], prm["router"], prm["w1"], prm["w3"], prm["w2"], tm=cfg["tm_moe"],
                      tf=cfg["tf"], cap=cfg["moe_cap"])
        else:
            x2 = _ffn(x2, prm["g2"], prm["router"], prm["w1"], prm["w3"], prm["w2"], tm=cfg["tm_ffn"],
                      tf=cfg["tf"], routed=True)
    return x2.reshape(b, t, d), outs


def kernel(x_prompt, x_sample, mem_prompt, cache_even_k, cache_even_v, cache_odd_nsa, cache_odd_dsa, state_odd_win, cache_mem, page_table, norm_gain, ev_w_in, ev_w_out, df_qk_gain, df_lambda, df_subln_gain, ffn_w1, ffn_w3, ffn_w2, od_w_in, od_w_out, nsa_qk_gain, cmp_pe, cmp_w1, cmp_w2, dsa_qk_gain, moe_router, moe_w1, moe_w3, moe_w2, x_wq, x_wk, x_wv, x_wo, x_qk_gain):
    depth = norm_gain.shape[0]
    bp, tp, d = x_prompt.shape
    bs, ts, _ = x_sample.shape
    n_mem = mem_prompt.shape[1]
    xw = N_XH * HD

    layers = []
    for l in range(depth):
        i = l // 2
        prm = {
            "layer": l,
            "g0": norm_gain[l, 0].reshape(1, d), "g1": norm_gain[l, 1].reshape(1, d), "g2": norm_gain[l, 2].reshape(1, d),
            "x_wq": x_wq[l].astype(BF16), "x_wk": x_wk[l].astype(BF16), "x_wv": x_wv[l].astype(BF16),
            "x_wo": x_wo[l].astype(BF16), "x_gq": x_qk_gain[l, 0], "x_gk": x_qk_gain[l, 1],
        }
        if l % 2 == 0:
            prm.update({
                "ev_w_in": ev_w_in[i].astype(BF16), "ev_w_out": ev_w_out[i].astype(BF16),
                "df_qk_gain": df_qk_gain[i], "df_lambda": df_lambda[i], "df_subln_gain": df_subln_gain[i],
                "router": jnp.zeros((SUBLANES, LANES), F32),
                "w1": ffn_w1[i][None].astype(BF16), "w3": ffn_w3[i][None].astype(BF16), "w2": ffn_w2[i][None].astype(BF16),
            })
        else:
            prm.update({
                "od_w_in": od_w_in[i].astype(BF16), "od_w_out": od_w_out[i].astype(BF16),
                "nsa_qk_gain": nsa_qk_gain[i], "dsa_qk_gain": dsa_qk_gain[i],
                "cmp_pe": cmp_pe[i], "cmp_w1": cmp_w1[i].astype(BF16), "cmp_w2": cmp_w2[i].astype(BF16),
                "router": jnp.pad(moe_router[i], ((0, 0), (0, LANES - N_EXPERTS))),
                "w1": moe_w1[i].astype(BF16), "w3": moe_w3[i].astype(BF16), "w2": moe_w2[i].astype(BF16),
            })
        layers.append(prm)

    def mem_heads(kv, b):
        k = kv[:, :, 0].transpose(0, 2, 1, 3).astype(BF16)
        v = kv[:, :, 1].transpose(0, 2, 1, 3).astype(BF16)
        return k, v

    mem2 = mem_prompt.reshape(bp * n_mem, d)
    mem_p = [_memory_kv(mem2, layers[l]).reshape(bp, n_mem, 2, N_XH, HD) for l in range(depth)]
    ff = ffn_w1.shape[2]
    tf = ff // 2 if (ff // 2) % LANES == 0 else ff
    cfg_p = {"tm": 256, "tq": 128, "tk": 256, "t_even": 256, "tk_win": 256, "tm_ffn": 512, "tf": tf,
             "tm_moe": 1024, "moe_cap": 320}
    cfg_p["tq"] = min(cfg_p["tq"], tp)
    cfg_p["tq_cross"] = 1024 if tp % 1024 == 0 else cfg_p["tq"]
    cfg_p["tq_slc"] = 256 if tp % 256 == 0 else cfg_p["tq"]
    cfg_p["tq_dsa"] = cfg_p["tq"]
    y_prompt, op = _run_group(x_prompt, 0, tp, [mem_heads(m, bp) for m in mem_p], [None] * depth, layers, cfg_p)
    p_mem = jnp.stack(mem_p)

    n_past = page_table.shape[1] * cache_even_k.shape[2]
    ts_pad = -(-ts // SUBLANES) * SUBLANES
    xs = _pad_rows(x_sample, ts_pad)
    pasts = []
    for l in range(depth):
        i = l // 2
        if l % 2 == 0:
            pasts.append((cache_even_k[i], cache_even_v[i], page_table))
        else:
            pn = cache_odd_nsa[i]
            pd = cache_odd_dsa[i]
            pasts.append((pn.reshape(pn.shape[0], pn.shape[1], -1), pd.reshape(pd.shape[0], pd.shape[1], -1),
                          state_odd_win[i], page_table))
    cfg_s = {"tm": bs * ts_pad, "tq": ts_pad, "tq_cross": ts_pad, "tq_slc": ts_pad, "tq_dsa": ts_pad, "tk": 512,
             "pp": 4, "tk_win": 128,
             "tm_ffn": bs * ts_pad, "tf": tf}
    y_s, os_ = _run_group(xs, n_past, ts, [mem_heads(cache_mem[l], bs) for l in range(depth)], pasts, layers, cfg_s)
    y_sample = y_s[:, :ts]

    g = N_NSA_KV
    return (
        y_prompt, y_sample,
        op["ek"][None], op["ev"][None],
        op["on"].reshape(1, bp, tp, 4, g, HD), op["od"].reshape(1, bp, tp, 3, HD),
        op["ow"][None], p_mem,
        os_["ek"][:, :ts][None], os_["ev"][:, :ts][None],
        os_["on"][:, :ts].reshape(1, bs, ts, 4, g, HD), os_["od"][:, :ts].reshape(1, bs, ts, 3, HD),
        os_["ow"][None],
    )
```
